```python
import math
import jax, jax.numpy as jnp
from jax import lax
import numpy as np

D_MODEL = 1024
BATCH = 8
SEQ = 2048
DEPTH = 1
DEC_BATCH = 128
DEC_SEQ = 1
PAST_LEN = 16384
PAGE_SIZE = 128

M_HEADS = 4
M_DH = 128
M_WIDTH = M_HEADS * M_DH
G_HEADS = 4
G_DK = 64
G_DV = 128
G_KW = G_HEADS * G_DK
G_VW = G_HEADS * G_DV
G_RANK = 16
G_TAU = 16.0
MIX_WIDTH = M_WIDTH + G_VW
CONV_W = 4
QK_CONV = 2 * M_WIDTH
D_FF = 4 * D_MODEL
D_PLE = 256
CHUNK = 64
EPS = 1e-6
F_BIAS = 3.0
IN_SIZES = (QK_CONV, M_WIDTH, M_WIDTH, M_HEADS, M_HEADS, G_KW, G_KW, G_VW, G_VW, G_RANK)
D_IN = sum(IN_SIZES)

kernel_name = "hymba_mlstm_gla_step"


def rmsnorm(x, g):
    xf = x.astype(jnp.float32)
    y = xf * lax.rsqrt(jnp.mean(xf * xf, axis=-1, keepdims=True) + EPS)
    return (y * g.astype(jnp.float32)).astype(x.dtype)


def headnorm(h, g):
    B, T, H, d = h.shape
    y = h * lax.rsqrt(jnp.mean(h * h, axis=-1, keepdims=True) + EPS)
    return y.reshape(B, T, H * d) * g.astype(jnp.float32)


def to_heads(a, H):
    B, T, _ = a.shape
    return a.reshape(B, T, H, -1).transpose(0, 2, 1, 3).astype(jnp.float32)


def causal_conv(x, buf, w, b):
    T = x.shape[1]
    xpad = jnp.concatenate([buf.astype(x.dtype), x], axis=1)
    y = b
    for j in range(CONV_W):
        y = y + xpad[:, j:j + T] * w[j]
    return jax.nn.silu(y), xpad[:, -(CONV_W - 1):]


def split_chunks(a, nc, L):
    return jnp.moveaxis(a.reshape(a.shape[:2] + (nc, L) + a.shape[3:]), 2, 0)


def mlstm_chunked(q, k, v, ig, lf, C0, n0, m0):
    B, H, T, d = q.shape
    L = math.gcd(T, CHUNK)
    nc = T // L
    causal = jnp.tril(jnp.ones((L, L), dtype=bool))

    def step(carry, inp):
        C, n, m = carry
        qc, kc, vc, ic, fc = inp
        b = jnp.cumsum(fc, axis=-1)
        D = jnp.where(causal, b[..., :, None] - b[..., None, :] + ic[..., None, :], -jnp.inf)
        inter = b + m[..., None]
        m_out = jnp.maximum(inter, jnp.max(D, axis=-1))
        s = jnp.einsum('bhtd,bhsd->bhts', qc, kc) * jnp.exp(D - m_out[..., None])
        scale = jnp.exp(inter - m_out)
        num = scale[..., None] * jnp.einsum('bhtd,bhde->bhte', qc, C) + jnp.einsum('bhts,bhse->bhte', s, vc)
        den = scale * jnp.einsum('bhtd,bhd->bht', qc, n) + jnp.sum(s, axis=-1)
        h = num / jnp.maximum(jnp.abs(den), jnp.exp(-m_out))[..., None]
        bL = b[..., -1]
        dec = bL[..., None] - b + ic
        m_new = jnp.maximum(bL + m, jnp.max(dec, axis=-1))
        wk = jnp.exp(dec - m_new[..., None])
        cs = jnp.exp(bL + m - m_new)
        C_new = cs[..., None, None] * C + jnp.einsum('bhs,bhsd,bhse->bhde', wk, kc, vc)
        n_new = cs[..., None] * n + jnp.einsum('bhs,bhsd->bhd', wk, kc)
        return (C_new, n_new, m_new), h

    xs = (split_chunks(q, nc, L), split_chunks(k, nc, L), split_chunks(v, nc, L),
          split_chunks(ig, nc, L), split_chunks(lf, nc, L))
    (C, n, m), hs = lax.scan(step, (C0, n0, m0), xs)
    h = jnp.moveaxis(hs, 0, 2).reshape(B, H, T, v.shape[-1])
    return h, C, n, m


def gla_chunked(q, k, v, la, S0):
    B, H, T, dk = q.shape
    L = math.gcd(T, CHUNK)
    nc = T // L
    causal = jnp.tril(jnp.ones((L, L), dtype=bool))

    def step(S, inp):
        qc, kc, vc, lc = inp
        bc = jnp.cumsum(lc, axis=-2)
        rel = jnp.where(causal[..., None], bc[..., :, None, :] - bc[..., None, :, :], -jnp.inf)
        A = jnp.einsum('bhtd,bhsd,bhtsd->bhts', qc, kc, jnp.exp(rel))
        o = jnp.einsum('bhtd,bhde->bhte', qc * jnp.exp(bc), S) + jnp.einsum('bhts,bhse->bhte', A, vc)
        bL = bc[..., -1:, :]
        S_new = jnp.exp(bL[..., 0, :])[..., None] * S + jnp.einsum('bhsd,bhse->bhde', kc * jnp.exp(bL - bc), vc)
        return S_new, o

    xs = (split_chunks(q, nc, L), split_chunks(k, nc, L), split_chunks(v, nc, L), split_chunks(la, nc, L))
    S, os_ = lax.scan(step, S0, xs)
    o = jnp.moveaxis(os_, 0, 2).reshape(B, H, T, v.shape[-1])
    return o, S


def hybrid_mixer(h, buf, C0, n0, m0, S0, w_in, conv_w, conv_b, b_gate, w_a2, b_a, g_mhead, g_ghead, w_out):
    B, T, _ = h.shape
    proj = h @ w_in
    splits = [int(s) for s in np.cumsum(IN_SIZES)[:-1]]
    qk_raw, mv, mo, mi, mf, gq, gk, gv, gr, ga = jnp.split(proj, splits, axis=-1)
    qk, new_buf = causal_conv(qk_raw, buf, conv_w, conv_b)
    mq, mk = jnp.split(qk, 2, axis=-1)
    gates = jnp.concatenate([mi, mf], axis=-1).astype(jnp.float32) + b_gate.astype(jnp.float32)
    ig = gates[..., :M_HEADS].transpose(0, 2, 1)
    lf = jax.nn.log_sigmoid(gates[..., M_HEADS:]).transpose(0, 2, 1)
    hm, C, n, m = mlstm_chunked(to_heads(mq, M_HEADS), to_heads(mk, M_HEADS) * (M_DH ** -0.5),
                                to_heads(mv, M_HEADS), ig, lf,
                                C0.astype(jnp.float32), n0.astype(jnp.float32), m0.astype(jnp.float32))
    hm = headnorm(hm.transpose(0, 2, 1, 3), g_mhead) * jax.nn.sigmoid(mo.astype(jnp.float32))
    la = jax.nn.log_sigmoid(ga.astype(jnp.float32) @ w_a2.astype(jnp.float32) + b_a.astype(jnp.float32)) / G_TAU
    hg, S = gla_chunked(to_heads(gq, G_HEADS) * (G_DK ** -0.5), to_heads(gk, G_HEADS), to_heads(gv, G_HEADS),
                        to_heads(la, G_HEADS), S0.astype(jnp.float32))
    hg = headnorm(hg.transpose(0, 2, 1, 3), g_ghead) * jax.nn.silu(gr.astype(jnp.float32))
    out = jnp.concatenate([hm, hg], axis=-1).astype(h.dtype) @ w_out
    return out, new_buf, C, n, m, S


def layer(x, p, buf, C0, n0, m0, S0, w_in, conv_w, conv_b, b_gate, w_a2, b_a, g_mhead, g_ghead, w_out,
          g_mix, g_mlp, w1, w2, g_ple, w_ple, w_pg):
    a, new_buf, C, n, m, S = hybrid_mixer(rmsnorm(x, g_mix), buf, C0, n0, m0, S0, w_in, conv_w, conv_b,
                                          b_gate, w_a2, b_a, g_mhead, g_ghead, w_out)
    x = x + a
    u = rmsnorm(x, g_mlp) @ w1
    x = x + jnp.square(jax.nn.relu(u)) @ w2
    x = x + (p @ w_ple) * jax.nn.sigmoid(rmsnorm(x, g_ple) @ w_pg)
    return x, new_buf, C, n, m, S


def setup_inputs(seed: int = 0) -> dict:
    key = jax.random.key(seed)
    ks = jax.random.split(key, 32)
    f32 = jnp.float32
    nrm = lambda k, shape, s=1.0: (jax.random.normal(k, shape, f32) * s)
    Dp = DEPTH
    b_gate = jnp.concatenate([nrm(ks[0], (Dp, M_HEADS), 0.1),
                              F_BIAS + nrm(ks[1], (Dp, M_HEADS), 0.1)], axis=-1)
    return {
        "x_prompt": nrm(ks[2], (BATCH, SEQ, D_MODEL)),
        "x_sample": nrm(ks[3], (DEC_BATCH, DEC_SEQ, D_MODEL)),
        "p_prompt": nrm(ks[4], (Dp, BATCH, SEQ, D_PLE)),
        "p_sample": nrm(ks[5], (Dp, DEC_BATCH, DEC_SEQ, D_PLE)),
        "state_mlstm_C": nrm(ks[6], (Dp, DEC_BATCH, M_HEADS, M_DH, M_DH), 0.05),
        "state_mlstm_n": nrm(ks[7], (Dp, DEC_BATCH, M_HEADS, M_DH), 0.1),
        "state_mlstm_m": nrm(ks[8], (Dp, DEC_BATCH, M_HEADS)),
        "state_conv": nrm(ks[9], (Dp, DEC_BATCH, CONV_W - 1, QK_CONV)),
        "state_gla_S": nrm(ks[10], (Dp, DEC_BATCH, G_HEADS, G_DK, G_DV), 0.1),
        "w_in": nrm(ks[11], (Dp, D_MODEL, D_IN), D_MODEL ** -0.5),
        "conv_w": nrm(ks[12], (Dp, CONV_W, QK_CONV), CONV_W ** -0.5),
        "conv_b": nrm(ks[13], (Dp, QK_CONV), 0.02),
        "b_gate": b_gate,
        "w_a2": nrm(ks[14], (Dp, G_RANK, G_KW), G_RANK ** -0.5),
        "b_a": nrm(ks[15], (Dp, G_KW), 0.1),
        "g_mhead": 1.0 + nrm(ks[16], (Dp, M_WIDTH), 0.02),
        "g_ghead": 1.0 + nrm(ks[17], (Dp, G_VW), 0.02),
        "w_out": nrm(ks[18], (Dp, MIX_WIDTH, D_MODEL), MIX_WIDTH ** -0.5),
        "g_mix": 1.0 + nrm(ks[19], (Dp, D_MODEL), 0.02),
        "g_mlp": 1.0 + nrm(ks[20], (Dp, D_MODEL), 0.02),
        "w1": nrm(ks[21], (Dp, D_MODEL, D_FF), D_MODEL ** -0.5),
        "w2": nrm(ks[22], (Dp, D_FF, D_MODEL), D_FF ** -0.5),
        "g_ple": 1.0 + nrm(ks[23], (Dp, D_MODEL), 0.02),
        "w_ple": nrm(ks[24], (Dp, D_PLE, D_MODEL), D_PLE ** -0.5),
        "w_pg": nrm(ks[25], (Dp, D_MODEL, D_MODEL), D_MODEL ** -0.5),
        "g_final": 1.0 + nrm(ks[26], (D_MODEL,), 0.02),
    }


def reference(x_prompt, x_sample, p_prompt, p_sample, state_mlstm_C, state_mlstm_n, state_mlstm_m,
              state_conv, state_gla_S, w_in, conv_w, conv_b, b_gate, w_a2, b_a, g_mhead, g_ghead, w_out,
              g_mix, g_mlp, w1, w2, g_ple, w_ple, w_pg, g_final):
    Bp = x_prompt.shape[0]
    f32 = jnp.float32
    xp, xs = x_prompt, x_sample
    new_p = ([], [], [], [], [])
    new_s = ([], [], [], [], [])
    for i in range(DEPTH):
        w = (w_in[i], conv_w[i], conv_b[i], b_gate[i], w_a2[i], b_a[i], g_mhead[i], g_ghead[i], w_out[i],
             g_mix[i], g_mlp[i], w1[i], w2[i], g_ple[i], w_ple[i], w_pg[i])
        xp, bp, Cp, np_, mp, Sp = layer(xp, p_prompt[i],
                                        jnp.zeros((Bp, CONV_W - 1, QK_CONV), x_prompt.dtype),
                                        jnp.zeros((Bp, M_HEADS, M_DH, M_DH), f32),
                                        jnp.zeros((Bp, M_HEADS, M_DH), f32),
                                        jnp.zeros((Bp, M_HEADS), f32),
                                        jnp.zeros((Bp, G_HEADS, G_DK, G_DV), f32), *w)
        xs, bs, Cs, ns, ms, Ss = layer(xs, p_sample[i], state_conv[i], state_mlstm_C[i], state_mlstm_n[i],
                                       state_mlstm_m[i], state_gla_S[i], *w)
        for lst, val in zip(new_p, (bp, Cp, np_, mp, Sp)):
            lst.append(val)
        for lst, val in zip(new_s, (bs, Cs, ns, ms, Ss)):
            lst.append(val)
    y_prompt = rmsnorm(xp, g_final)
    y_sample = rmsnorm(xs, g_final)
    conv_p, C_p, n_p, m_p, S_p = [jnp.stack(l, axis=0) for l in new_p]
    conv_s, C_s, n_s, m_s, S_s = [jnp.stack(l, axis=0) for l in new_s]
    return (y_prompt, y_sample, C_p, n_p, m_p, conv_p, S_p, C_s, n_s, m_s, conv_s, S_s)
```

```python
import functools

import jax
import jax.numpy as jnp
from jax import lax
from jax.experimental import pallas as pl
from jax.experimental.pallas import tpu as pltpu

D_MODEL = 1024
M_HEADS = 4
M_DH = 128
M_WIDTH = M_HEADS * M_DH
G_HEADS = 4
G_DK = 64
G_DV = 128
G_KW = G_HEADS * G_DK
G_VW = G_HEADS * G_DV
G_RANK = 16
G_TAU = 16.0
CONV_W = 4
QK_CONV = 2 * M_WIDTH
D_FF = 4 * D_MODEL
D_PLE = 256
CHUNK = 64
SUB = 16
EPS = 1e-6
IN_SIZES = (QK_CONV, M_WIDTH, M_WIDTH, M_HEADS, M_HEADS, G_KW, G_KW, G_VW, G_VW, G_RANK)

LANES = 128
SUBLANES = 8
VMEM_LIMIT = 56 * 1024 * 1024

F32 = jnp.float32
BF16 = jnp.bfloat16
NEG = -1e30


def _rms(x, g):
    return x * lax.rsqrt(jnp.mean(x * x, axis=-1, keepdims=True) + EPS) * g


def _log_sigmoid(x):
    return jnp.minimum(x, 0.0) - jnp.log1p(jnp.exp(-jnp.abs(x)))


def _sigmoid(x):
    return 1.0 / (1.0 + jnp.exp(-x))


def _dot(a, b):
    return jnp.dot(a, b, preferred_element_type=F32)


def _dot_nt(a, b):
    return lax.dot_general(a, b, (((1,), (1,)), ((), ())), preferred_element_type=F32)


def _cumsum_dot(tri, x):
    hi = x.astype(BF16)
    r1 = x - hi.astype(F32)
    mid = r1.astype(BF16)
    lo = (r1 - mid.astype(F32)).astype(BF16)
    return _dot(tri, hi) + _dot(tri, mid) + _dot(tri, lo)


def _proj_kernel(*refs, steps_per_seq, per_token_conv):
    if per_token_conv:
        (x_ref, b0_ref, b1_ref, b2_ref, gmix_ref, wbig_ref, wsmall_ref, bsmall_ref, wa2_ref, ba_ref,
         convw_ref, convb_ref,
         mq_ref, mk_ref, mv_ref, mo_ref, gq_ref, gk_ref, gv_ref, gr_ref, gates_ref, la_ref, raw_ref) = refs
    else:
        (x_ref, gmix_ref, wbig_ref, wsmall_ref, bsmall_ref, wa2_ref, ba_ref, convw_ref, convb_ref,
         mq_ref, mk_ref, mv_ref, mo_ref, gq_ref, gk_ref, gv_ref, gr_ref, gates_ref, la_ref, tail_ref,
         cbuf_ref) = refs
    tb = x_ref.shape[0]
    h = _rms(x_ref[...], gmix_ref[...]).astype(BF16)

    qk_raw = _dot(h, wbig_ref[:, 0:QK_CONV])
    cw = convw_ref[...]
    if per_token_conv:
        y = (convb_ref[...] + b0_ref[...] * cw[0:1] + b1_ref[...] * cw[1:2] + b2_ref[...] * cw[2:3]
             + qk_raw * cw[3:4])
        raw_ref[...] = qk_raw
    else:
        i = pl.program_id(0)

        @pl.when(i % steps_per_seq == 0)
        def _():
            cbuf_ref[0:SUBLANES, :] = jnp.zeros((SUBLANES, QK_CONV), F32)

        cbuf_ref[SUBLANES:SUBLANES + tb, :] = qk_raw
        y = convb_ref[...] + qk_raw * cw[3:4]
        for j in range(CONV_W - 1):
            off = SUBLANES - (CONV_W - 1) + j
            y = y + cbuf_ref[off:off + tb, :] * cw[j:j + 1]
        last = cbuf_ref[tb:tb + SUBLANES, :]
        tail_ref[0] = last
        cbuf_ref[0:SUBLANES, :] = last
    qk = y * _sigmoid(y)
    mq_ref[...] = qk[:, :M_WIDTH].astype(mq_ref.dtype)
    mk_ref[...] = qk[:, M_WIDTH:] * (M_DH ** -0.5)

    o = QK_CONV
    mv_ref[...] = _dot(h, wbig_ref[:, o:o + M_WIDTH]).astype(mv_ref.dtype)
    o += M_WIDTH
    mo_ref[...] = _dot(h, wbig_ref[:, o:o + M_WIDTH])
    o += M_WIDTH
    gq_ref[...] = _dot(h, wbig_ref[:, o:o + G_KW]) * (G_DK ** -0.5)
    o += G_KW
    gk_ref[...] = _dot(h, wbig_ref[:, o:o + G_KW])
    o += G_KW
    gv_ref[...] = _dot(h, wbig_ref[:, o:o + G_VW]).astype(gv_ref.dtype)
    o += G_VW
    gr_ref[...] = _dot(h, wbig_ref[:, o:o + G_VW])

    small = _dot(h, wsmall_ref[...])
    g = small + bsmall_ref[...]
    lane = lax.broadcasted_iota(jnp.int32, g.shape, 1)
    gates_ref[...] = jnp.where(lane < M_HEADS, g, _log_sigmoid(g))
    z = _dot(small.astype(BF16), wa2_ref[...]) + ba_ref[...]
    la_ref[...] = _log_sigmoid(z) * (1.0 / G_TAU)


def _proj_call(x2d, conv_rows, wts, *, tb, steps_per_seq, n_seq):
    n = x2d.shape[0]
    per_token_conv = conv_rows is not None
    grid = (n // tb,)
    row = lambda w: pl.BlockSpec((tb, w), lambda i: (i, 0))
    whole = pl.BlockSpec(memory_space=pltpu.VMEM)
    in_specs = [row(D_MODEL)]
    args = [x2d]
    if per_token_conv:
        in_specs += [row(QK_CONV)] * 3
        args += list(conv_rows)
    in_specs += [whole] * 8
    args += [wts["g_mix"], wts["w_big"], wts["w_small"], wts["b_small"], wts["w_a2p"], wts["b_a"],
             wts["conv_w"], wts["conv_b"]]
    out_shape = [
        jax.ShapeDtypeStruct((n, M_WIDTH), BF16),
        jax.ShapeDtypeStruct((n, M_WIDTH), F32),
        jax.ShapeDtypeStruct((n, M_WIDTH), BF16),
        jax.ShapeDtypeStruct((n, M_WIDTH), F32),
        jax.ShapeDtypeStruct((n, G_KW), F32),
        jax.ShapeDtypeStruct((n, G_KW), F32),
        jax.ShapeDtypeStruct((n, G_VW), BF16),
        jax.ShapeDtypeStruct((n, G_VW), F32),
        jax.ShapeDtypeStruct((n, LANES), F32),
        jax.ShapeDtypeStruct((n, G_KW), F32),
    ]
    out_specs = [row(M_WIDTH), row(M_WIDTH), row(M_WIDTH), row(M_WIDTH), row(G_KW), row(G_KW),
                 row(G_VW), row(G_VW), row(LANES), row(G_KW)]
    scratch = []
    if per_token_conv:
        out_shape.append(jax.ShapeDtypeStruct((n, QK_CONV), F32))
        out_specs.append(row(QK_CONV))
    else:
        out_shape.append(jax.ShapeDtypeStruct((n_seq, SUBLANES, QK_CONV), F32))
        out_specs.append(pl.BlockSpec((1, SUBLANES, QK_CONV), lambda i: (i // steps_per_seq, 0, 0)))
        scratch.append(pltpu.VMEM((tb + SUBLANES, QK_CONV), F32))
    return pl.pallas_call(
        functools.partial(_proj_kernel, steps_per_seq=steps_per_seq, per_token_conv=per_token_conv),
        grid=grid, in_specs=in_specs, out_specs=out_specs, out_shape=out_shape, scratch_shapes=scratch,
        compiler_params=pltpu.CompilerParams(dimension_semantics=("arbitrary",), vmem_limit_bytes=VMEM_LIMIT),
        name="proj_tok" if per_token_conv else "proj_seq",
    )(*args)


def _mix_seq_kernel(mq_ref, mk_ref, mv_ref, gates_ref, gq_ref, gk_ref, gv_ref, la_ref, tri_ref, wred_ref,
                    hm_ref, hg_ref, cout_ref, mout_ref, sout_ref,
                    caug_ref, m_ref, s_ref):
    ts = mq_ref.shape[0]
    n_chunks = ts // CHUNK
    r = pl.program_id(1)

    @pl.when(r == 0)
    def _():
        caug_ref[...] = jnp.zeros(caug_ref.shape, F32)
        m_ref[...] = jnp.zeros(m_ref.shape, F32)
        s_ref[...] = jnp.zeros(s_ref.shape, F32)

    tri = tri_ref[...]
    gates = gates_ref[...]
    bcum = _cumsum_dot(tri, gates)
    gates_t = gates.T
    bcum_t = bcum.T

    gq = gq_ref[...]
    gk = gk_ref[...]
    bc = _cumsum_dot(tri, la_ref[...])
    bc_t = bc.T

    nb = ts // SUB
    q3 = gq.reshape(nb, SUB, G_KW)
    k3 = gk.reshape(nb, SUB, G_KW)
    bc3 = bc.reshape(nb, SUB, G_KW)
    tl = lax.broadcasted_iota(jnp.int32, (nb, SUB, G_KW), 1)
    adiag = jnp.zeros((ts, G_KW), F32)
    for j in range(SUB):
        arg = jnp.where(tl >= j, bc3 - bc3[:, j:j + 1, :], NEG)
        e = (q3 * k3[:, j:j + 1, :] * jnp.exp(arg)).reshape(ts, G_KW)
        adiag = adiag + _dot(e.astype(BF16), wred_ref[j])
    sub_of = lambda idx: lax.shift_right_logical(idx & (CHUNK - 1), SUB.bit_length() - 1)
    rowb = sub_of(lax.broadcasted_iota(jnp.int32, (ts, G_KW), 0))
    colb = sub_of(lax.broadcasted_iota(jnp.int32, (ts, G_KW), 1))
    adiag = jnp.where(rowb == colb, adiag, 0.0)

    causal = (lax.broadcasted_iota(jnp.int32, (CHUNK, CHUNK), 0)
              >= lax.broadcasted_iota(jnp.int32, (CHUNK, CHUNK), 1))
    ones_col = (lax.broadcasted_iota(jnp.int32, (CHUNK, LANES), 1) == 0).astype(BF16)
    krow = lax.broadcasted_iota(jnp.int32, (CHUNK, G_KW), 0)

    for c in range(n_chunks):
        lo, hi = c * CHUNK, (c + 1) * CHUNK
        for hd in range(M_HEADS):
            cs_, ce_ = hd * M_DH, (hd + 1) * M_DH
            b_col = bcum[lo:hi, M_HEADS + hd:M_HEADS + hd + 1]
            b_row = bcum_t[M_HEADS + hd:M_HEADS + hd + 1, lo:hi]
            i_col = gates[lo:hi, hd:hd + 1]
            i_row = gates_t[hd:hd + 1, lo:hi]
            m_prev = m_ref[hd:hd + 1, 0:1]
            dmat = jnp.where(causal, b_col - b_row + i_row, -jnp.inf)
            inter = b_col + m_prev
            m_tok = jnp.maximum(inter, jnp.max(dmat, axis=1, keepdims=True))
            q = mq_ref[lo:hi, cs_:ce_]
            k = mk_ref[lo:hi, cs_:ce_]
            vaug = jnp.concatenate([mv_ref[lo:hi, cs_:ce_], ones_col], axis=1)
            p = _dot_nt(q, k.astype(BF16)) * jnp.exp(dmat - m_tok)
            scale = jnp.exp(inter - m_tok)
            caug = caug_ref[hd]
            tot = scale * _dot(q, caug.astype(BF16)) + _dot(p.astype(BF16), vaug)
            den = tot[:, M_DH:M_DH + 1]
            hm_ref[lo:hi, cs_:ce_] = tot[:, :M_DH] / jnp.maximum(jnp.abs(den), jnp.exp(-m_tok))
            b_last = b_col[CHUNK - 1:CHUNK, :]
            dec = b_last - b_col + i_col
            m_new = jnp.maximum(b_last + m_prev, jnp.max(dec, axis=0, keepdims=True))
            wk = jnp.exp(dec - m_new)
            cscale = jnp.exp(b_last + m_prev - m_new)
            kw_t = (k * wk).T.astype(BF16)
            caug_ref[hd] = cscale * caug + _dot(kw_t, vaug)
            m_ref[hd:hd + 1, :] = jnp.broadcast_to(m_new, (1, LANES))
        bc_c = bc[lo:hi]
        q_c = gq[lo:hi]
        k_c = gk[lo:hi]
        b_end = bc_c[CHUNK - 1:CHUNK, :]
        q_in = (q_c * jnp.exp(bc_c)).astype(BF16)
        k_out = k_c * jnp.exp(b_end - bc_c)
        offs = [jnp.zeros((SUB, G_KW), F32)]
        for i in range(1, CHUNK // SUB):
            r_i = bc_c[i * SUB - 1:i * SUB, :]
            qi = (q_c[i * SUB:(i + 1) * SUB] * jnp.exp(bc_c[i * SUB:(i + 1) * SUB] - r_i)).astype(BF16)
            ki = (k_c * jnp.exp(jnp.where(krow < i * SUB, r_i - bc_c, NEG))).astype(BF16)
            offs.append(jnp.concatenate(
                [_dot_nt(qi[:, g * G_DK:(g + 1) * G_DK], ki[:, g * G_DK:(g + 1) * G_DK])
                 for g in range(G_HEADS)], axis=1))
        a_all = (jnp.concatenate(offs, axis=0) + adiag[lo:hi]).astype(BF16)
        for g in range(G_HEADS):
            ks_, ke_ = g * G_DK, (g + 1) * G_DK
            vs_, ve_ = g * G_DV, (g + 1) * G_DV
            v = gv_ref[lo:hi, vs_:ve_]
            s_prev = s_ref[g]
            hg_ref[lo:hi, vs_:ve_] = (_dot(q_in[:, ks_:ke_], s_prev.astype(BF16))
                                      + _dot(a_all[:, ks_:ke_], v))
            dcol = jnp.exp(bc_t[ks_:ke_, hi - 1:hi])
            s_ref[g] = dcol * s_prev + _dot(k_out[:, ks_:ke_].T.astype(BF16), v)

    @pl.when(r == pl.num_programs(1) - 1)
    def _():
        cout_ref[0] = caug_ref[...]
        mout_ref[0] = m_ref[...]
        sout_ref[0] = s_ref[...]


def _mix_seq_call(p, tri, wred, *, n_seq, seq_len, ts):
    steps = seq_len // ts
    row = lambda w: pl.BlockSpec((ts, w), lambda b, r: (b * steps + r, 0))
    whole = pl.BlockSpec(memory_space=pltpu.VMEM)
    n = n_seq * seq_len
    return pl.pallas_call(
        _mix_seq_kernel,
        grid=(n_seq, steps),
        in_specs=[row(M_WIDTH), row(M_WIDTH), row(M_WIDTH), row(LANES), row(G_KW), row(G_KW), row(G_VW),
                  row(G_KW), whole, whole],
        out_specs=[row(M_WIDTH), row(G_VW),
                   pl.BlockSpec((1, M_HEADS, M_DH, 2 * M_DH), lambda b, r: (b, 0, 0, 0)),
                   pl.BlockSpec((1, SUBLANES, LANES), lambda b, r: (b, 0, 0)),
                   pl.BlockSpec((1, G_HEADS, G_DK, G_DV), lambda b, r: (b, 0, 0, 0))],
        out_shape=[jax.ShapeDtypeStruct((n, M_WIDTH), F32), jax.ShapeDtypeStruct((n, G_VW), F32),
                   jax.ShapeDtypeStruct((n_seq, M_HEADS, M_DH, 2 * M_DH), F32),
                   jax.ShapeDtypeStruct((n_seq, SUBLANES, LANES), F32),
                   jax.ShapeDtypeStruct((n_seq, G_HEADS, G_DK, G_DV), F32)],
        scratch_shapes=[pltpu.VMEM((M_HEADS, M_DH, 2 * M_DH), F32), pltpu.VMEM((SUBLANES, LANES), F32),
                        pltpu.VMEM((G_HEADS, G_DK, G_DV), F32)],
        compiler_params=pltpu.CompilerParams(dimension_semantics=("arbitrary", "arbitrary"),
                                             vmem_limit_bytes=VMEM_LIMIT),
        name="mix_seq",
    )(p["mq"], p["mk"], p["mv"], p["gates"], p["gq"], p["gk"], p["gv"], p["la"], tri, wred)


def _mix_tok_kernel(mq_ref, mk_ref, mv_ref, gates_ref, gq_ref, gk_ref, gv_ref, la_ref,
                    c_ref, n_ref, m_ref, s_ref,
                    hm_ref, hg_ref, cn_ref, nn_ref, mn_ref, sn_ref):
    bb = mq_ref.shape[0]
    gates = gates_ref[...]
    m_all = m_ref[...]
    for hd in range(M_HEADS):
        cs_, ce_ = hd * M_DH, (hd + 1) * M_DH
        q = mq_ref[:, cs_:ce_].astype(F32)
        k = mk_ref[:, cs_:ce_]
        v = mv_ref[:, cs_:ce_].astype(F32)
        n_prev = n_ref[:, cs_:ce_]
        ig = gates[:, hd:hd + 1]
        lf = gates[:, M_HEADS + hd:M_HEADS + hd + 1]
        m_prev = m_all[:, hd:hd + 1]
        m_new = jnp.maximum(lf + m_prev, ig)
        scale = jnp.exp(lf + m_prev - m_new)
        wk = jnp.exp(ig - m_new)
        s_qk = jnp.sum(q * k, axis=1, keepdims=True) * wk
        den = scale * jnp.sum(q * n_prev, axis=1, keepdims=True) + s_qk
        inv = 1.0 / jnp.maximum(jnp.abs(den), jnp.exp(-m_new))
        nn_ref[:, cs_:ce_] = scale * n_prev + wk * k
        mn_ref[:, hd:hd + 1] = m_new
        q_t = q.T
        kw_t = (k * wk).T
        for b in range(bb):
            c_prev = c_ref[b, hd]
            q_col = q_t[:, b:b + 1]
            qc = jnp.sum(q_col * c_prev, axis=0, keepdims=True)
            num = scale[b:b + 1] * qc + s_qk[b:b + 1] * v[b:b + 1]
            hm_ref[b:b + 1, cs_:ce_] = num * inv[b:b + 1]
            cn_ref[b, hd] = scale[b:b + 1] * c_prev + kw_t[:, b:b + 1] * v[b:b + 1]
    la = la_ref[...]
    dec = jnp.exp(la)
    gq = gq_ref[...]
    gk = gk_ref[...]
    for g in range(G_HEADS):
        ks_, ke_ = g * G_DK, (g + 1) * G_DK
        vs_, ve_ = g * G_DV, (g + 1) * G_DV
        q = gq[:, ks_:ke_]
        k = gk[:, ks_:ke_]
        v = gv_ref[:, vs_:ve_].astype(F32)
        a = jnp.sum(q * k, axis=1, keepdims=True)
        qd_t = (q * dec[:, ks_:ke_]).T
        k_t = k.T
        d_t = dec[:, ks_:ke_].T
        for b in range(bb):
            s_prev = s_ref[b, g]
            o = jnp.sum(qd_t[:, b:b + 1] * s_prev, axis=0, keepdims=True) + a[b:b + 1] * v[b:b + 1]
            hg_ref[b:b + 1, vs_:ve_] = o
            sn_ref[b, g] = d_t[:, b:b + 1] * s_prev + k_t[:, b:b + 1] * v[b:b + 1]


def _mix_tok_call(p, c0, n0, m0, s0, *, bb):
    n = c0.shape[0]
    row = lambda w: pl.BlockSpec((bb, w), lambda i: (i, 0))
    st4 = lambda a, b_: pl.BlockSpec((bb, M_HEADS, a, b_), lambda i: (i, 0, 0, 0))
    return pl.pallas_call(
        _mix_tok_kernel,
        grid=(n // bb,),
        in_specs=[row(M_WIDTH), row(M_WIDTH), row(M_WIDTH), row(LANES), row(G_KW), row(G_KW), row(G_VW),
                  row(G_KW), st4(M_DH, M_DH), row(M_WIDTH), row(M_HEADS), st4(G_DK, G_DV)],
        out_specs=[row(M_WIDTH), row(G_VW), st4(M_DH, M_DH), row(M_WIDTH), row(M_HEADS), st4(G_DK, G_DV)],
        out_shape=[jax.ShapeDtypeStruct((n, M_WIDTH), F32), jax.ShapeDtypeStruct((n, G_VW), F32),
                   jax.ShapeDtypeStruct(c0.shape, F32), jax.ShapeDtypeStruct((n, M_WIDTH), F32),
                   jax.ShapeDtypeStruct((n, M_HEADS), F32), jax.ShapeDtypeStruct(s0.shape, F32)],
        compiler_params=pltpu.CompilerParams(dimension_semantics=("arbitrary",), vmem_limit_bytes=VMEM_LIMIT),
        name="mix_tok",
    )(p["mq"], p["mk"], p["mv"], p["gates"], p["gq"], p["gk"], p["gv"], p["la"], c0, n0, m0, s0)


def _head_norm(hv, n_heads, width):
    parts = []
    for hd in range(n_heads):
        seg = hv[:, hd * width:(hd + 1) * width]
        parts.append(seg * lax.rsqrt(jnp.mean(seg * seg, axis=-1, keepdims=True) + EPS))
    return jnp.concatenate(parts, axis=1)


def _post_kernel(x_ref, p_ref, hm_ref, hg_ref, mo_ref, gr_ref, gmh_ref, ggh_ref, wout_ref, gmlp_ref,
                 w1_ref, w2_ref, gple_ref, wple_ref, wpg_ref, gfin_ref, y_ref):
    hm = _head_norm(hm_ref[...], M_HEADS, M_DH) * gmh_ref[...] * _sigmoid(mo_ref[...])
    gr = gr_ref[...]
    hg = _head_norm(hg_ref[...], G_HEADS, G_DV) * ggh_ref[...] * (gr * _sigmoid(gr))
    mixed = jnp.concatenate([hm, hg], axis=1).astype(BF16)
    x1 = x_ref[...] + _dot(mixed, wout_ref[...])
    u = _dot(_rms(x1, gmlp_ref[...]).astype(BF16), w1_ref[...])
    act = jnp.square(jnp.maximum(u, 0.0)).astype(BF16)
    x2 = x1 + _dot(act, w2_ref[...])
    ple = _dot(p_ref[...].astype(BF16), wple_ref[...])
    gate = _sigmoid(_dot(_rms(x2, gple_ref[...]).astype(BF16), wpg_ref[...]))
    x3 = x2 + ple * gate
    y_ref[...] = _rms(x3, gfin_ref[...])


def _post_call(x2d, p2d, hm, hg, mo, gr, wts, *, tb):
    n = x2d.shape[0]
    row = lambda w: pl.BlockSpec((tb, w), lambda i: (i, 0))
    whole = pl.BlockSpec(memory_space=pltpu.VMEM)
    return pl.pallas_call(
        _post_kernel,
        grid=(n // tb,),
        in_specs=[row(D_MODEL), row(D_PLE), row(M_WIDTH), row(G_VW), row(M_WIDTH), row(G_VW)] + [whole] * 10,
        out_specs=row(D_MODEL),
        out_shape=jax.ShapeDtypeStruct((n, D_MODEL), F32),
        compiler_params=pltpu.CompilerParams(dimension_semantics=("arbitrary",), vmem_limit_bytes=VMEM_LIMIT),
        name="post",
    )(x2d, p2d, hm, hg, mo, gr, wts["g_mhead"], wts["g_ghead"], wts["w_out"], wts["g_mlp"], wts["w1"],
      wts["w2"], wts["g_ple"], wts["w_ple"], wts["w_pg"], wts["g_final"])


def _prep_weights(w_in, conv_w, conv_b, b_gate, w_a2, b_a, g_mhead, g_ghead, w_out, g_mix, g_mlp, w1, w2,
                  g_ple, w_ple, w_pg, g_final):
    offs = [0]
    for s in IN_SIZES:
        offs.append(offs[-1] + s)
    gates_lo, gates_hi = offs[3], offs[5]
    ga_lo = offs[9]
    w_big = jnp.concatenate([w_in[:, :gates_lo], w_in[:, gates_hi:ga_lo]], axis=1).astype(BF16)
    n_small = 2 * M_HEADS + G_RANK
    w_small = jnp.concatenate([w_in[:, gates_lo:gates_hi], w_in[:, ga_lo:],
                               jnp.zeros((D_MODEL, LANES - n_small), F32)], axis=1).astype(BF16)
    b_small = jnp.concatenate([b_gate, jnp.zeros((LANES - 2 * M_HEADS,), F32)])[None]
    w_a2p = jnp.concatenate([jnp.zeros((2 * M_HEADS, G_KW), F32), w_a2,
                             jnp.zeros((LANES - n_small, G_KW), F32)], axis=0).astype(BF16)
    return dict(
        w_big=w_big, w_small=w_small, b_small=b_small, w_a2p=w_a2p, b_a=b_a[None],
        conv_w=conv_w, conv_b=conv_b[None], g_mix=g_mix[None], g_mhead=g_mhead[None], g_ghead=g_ghead[None],
        w_out=w_out.astype(BF16), g_mlp=g_mlp[None], w1=w1.astype(BF16), w2=w2.astype(BF16),
        g_ple=g_ple[None], w_ple=w_ple.astype(BF16), w_pg=w_pg.astype(BF16), g_final=g_final[None])


def _mix_constants(ts):
    t = jnp.arange(ts)
    tri = ((t[:, None] // CHUNK == t[None, :] // CHUNK) & (t[None, :] <= t[:, None])).astype(BF16)
    rr = jnp.arange(G_KW)
    wred = ((rr[None, :, None] // G_DK == rr[None, None, :] // G_DK)
            & (rr[None, None, :] % SUB == jnp.arange(SUB)[:, None, None])).astype(BF16)
    return tri, wred


_PROJ_NAMES = ("mq", "mk", "mv", "mo", "gq", "gk", "gv", "gr", "gates", "la")


def kernel(x_prompt, x_sample, p_prompt, p_sample, state_mlstm_C, state_mlstm_n, state_mlstm_m, state_conv,
           state_gla_S, w_in, conv_w, conv_b, b_gate, w_a2, b_a, g_mhead, g_ghead, w_out, g_mix, g_mlp, w1,
           w2, g_ple, w_ple, w_pg, g_final):
    assert w_in.shape[0] == 1, "single-layer trunk"
    n_seq, seq_len, _ = x_prompt.shape
    n_tok = x_sample.shape[0]
    assert x_sample.shape[1] == 1
    wts = _prep_weights(w_in[0], conv_w[0], conv_b[0], b_gate[0], w_a2[0], b_a[0], g_mhead[0], g_ghead[0],
                        w_out[0], g_mix[0], g_mlp[0], w1[0], w2[0], g_ple[0], w_ple[0], w_pg[0], g_final)

    xp = x_prompt.reshape(n_seq * seq_len, D_MODEL)
    tb1 = 512
    outs = _proj_call(xp, None, wts, tb=tb1, steps_per_seq=seq_len // tb1, n_seq=n_seq)
    pp = dict(zip(_PROJ_NAMES, outs[:-1]))
    tail_p = outs[-1]
    ts = 256
    tri, wred = _mix_constants(ts)
    hm_p, hg_p, caug_p, m_p, s_p = _mix_seq_call(pp, tri, wred, n_seq=n_seq, seq_len=seq_len, ts=ts)
    y_p = _post_call(xp, p_prompt[0].reshape(n_seq * seq_len, D_PLE), hm_p, hg_p, pp["mo"], pp["gr"], wts,
                     tb=256)

    xs = x_sample.reshape(n_tok, D_MODEL)
    buf = state_conv[0]
    outs = _proj_call(xs, (buf[:, 0], buf[:, 1], buf[:, 2]), wts, tb=n_tok, steps_per_seq=1, n_seq=n_tok)
    ps = dict(zip(_PROJ_NAMES, outs[:-1]))
    raw_s = outs[-1]
    hm_s, hg_s, c_s, n_s, m_s, s_s = _mix_tok_call(
        ps, state_mlstm_C[0], state_mlstm_n[0].reshape(n_tok, M_WIDTH), state_mlstm_m[0], state_gla_S[0], bb=8)
    y_s = _post_call(xs, p_sample[0].reshape(n_tok, D_PLE), hm_s, hg_s, ps["mo"], ps["gr"], wts, tb=n_tok)

    return (y_p.reshape(n_seq, seq_len, D_MODEL),
            y_s.reshape(n_tok, 1, D_MODEL),
            caug_p[None, :, :, :, :M_DH],
            caug_p[None, :, :, :, M_DH],
            m_p[None, :, :M_HEADS, 0],
            tail_p[None, :, SUBLANES - (CONV_W - 1):, :],
            s_p[None],
            c_s[None],
            n_s.reshape(1, n_tok, M_HEADS, M_DH),
            m_s[None],
            jnp.stack([buf[:, 1], buf[:, 2], raw_s], axis=1)[None],
            s_s[None])
```

```python
import functools

import jax
import jax.numpy as jnp
from jax import lax
from jax.experimental import pallas as pl
from jax.experimental.pallas import tpu as pltpu

D_MODEL = 1024
M_HEADS = 4
M_DH = 128
M_WIDTH = M_HEADS * M_DH
G_HEADS = 4
G_DK = 64
G_DV = 128
G_KW = G_HEADS * G_DK
G_VW = G_HEADS * G_DV
G_RANK = 16
G_TAU = 16.0
CONV_W = 4
QK_CONV = 2 * M_WIDTH
D_FF = 4 * D_MODEL
D_PLE = 256
CHUNK = 64
SUB = 16
EPS = 1e-6
IN_SIZES = (QK_CONV, M_WIDTH, M_WIDTH, M_HEADS, M_HEADS, G_KW, G_KW, G_VW, G_VW, G_RANK)

LANES = 128
SUBLANES = 8
VMEM_LIMIT = 56 * 1024 * 1024

F32 = jnp.float32
BF16 = jnp.bfloat16
NEG = -1e30


def _rms(x, g):
    return x * lax.rsqrt(jnp.mean(x * x, axis=-1, keepdims=True) + EPS) * g


def _log_sigmoid(x):
    return jnp.minimum(x, 0.0) - jnp.log1p(jnp.exp(-jnp.abs(x)))


def _sigmoid(x):
    return 1.0 / (1.0 + jnp.exp(-x))


def _div_pow2(idx, d):
    assert d & (d - 1) == 0
    return lax.shift_right_logical(idx, d.bit_length() - 1)


def _dot(a, b):
    return jnp.dot(a, b, preferred_element_type=F32)


def _dot_nt(a, b):
    return lax.dot_general(a, b, (((1,), (1,)), ((), ())), preferred_element_type=F32)


def _cumsum_dot(tri, x):
    hi = x.astype(BF16)
    r1 = x - hi.astype(F32)
    mid = r1.astype(BF16)
    lo = (r1 - mid.astype(F32)).astype(BF16)
    return _dot(tri, hi) + _dot(tri, mid) + _dot(tri, lo)


def _proj_kernel(*refs, steps_per_seq, per_token_conv):
    if per_token_conv:
        (x_ref, b0_ref, b1_ref, b2_ref, gmix_ref, wbig_ref, wsmall_ref, bsmall_ref, wa2_ref, ba_ref,
         convw_ref, convb_ref,
         mq_ref, mk_ref, mv_ref, mo_ref, gq_ref, gk_ref, gv_ref, gr_ref, gates_ref, la_ref, raw_ref) = refs
    else:
        (x_ref, gmix_ref, wbig_ref, wsmall_ref, bsmall_ref, wa2_ref, ba_ref, convw_ref, convb_ref,
         mq_ref, mk_ref, mv_ref, mo_ref, gq_ref, gk_ref, gv_ref, gr_ref, gates_ref, la_ref, tail_ref,
         cbuf_ref) = refs
    tb = x_ref.shape[0]
    h = _rms(x_ref[...], gmix_ref[...]).astype(BF16)

    qk_raw = _dot(h, wbig_ref[:, 0:QK_CONV])
    cw = convw_ref[...]
    if per_token_conv:
        y = (convb_ref[...] + b0_ref[...] * cw[0:1] + b1_ref[...] * cw[1:2] + b2_ref[...] * cw[2:3]
             + qk_raw * cw[3:4])
        raw_ref[...] = qk_raw
    else:
        i = pl.program_id(0)

        @pl.when(i % steps_per_seq == 0)
        def _():
            cbuf_ref[0:SUBLANES, :] = jnp.zeros((SUBLANES, QK_CONV), F32)

        cbuf_ref[SUBLANES:SUBLANES + tb, :] = qk_raw
        y = convb_ref[...] + qk_raw * cw[3:4]
        for j in range(CONV_W - 1):
            off = SUBLANES - (CONV_W - 1) + j
            y = y + cbuf_ref[off:off + tb, :] * cw[j:j + 1]
        last = cbuf_ref[tb:tb + SUBLANES, :]
        tail_ref[0] = last
        cbuf_ref[0:SUBLANES, :] = last
    qk = y * _sigmoid(y)
    mq_ref[...] = qk[:, :M_WIDTH].astype(mq_ref.dtype)
    mk_ref[...] = qk[:, M_WIDTH:] * (M_DH ** -0.5)

    o = QK_CONV
    mv_ref[...] = _dot(h, wbig_ref[:, o:o + M_WIDTH]).astype(mv_ref.dtype)
    o += M_WIDTH
    mo_ref[...] = _dot(h, wbig_ref[:, o:o + M_WIDTH])
    o += M_WIDTH
    gq_ref[...] = _dot(h, wbig_ref[:, o:o + G_KW]) * (G_DK ** -0.5)
    o += G_KW
    gk_ref[...] = _dot(h, wbig_ref[:, o:o + G_KW])
    o += G_KW
    gv_ref[...] = _dot(h, wbig_ref[:, o:o + G_VW]).astype(gv_ref.dtype)
    o += G_VW
    gr_ref[...] = _dot(h, wbig_ref[:, o:o + G_VW])

    small = _dot(h, wsmall_ref[...])
    g = small + bsmall_ref[...]
    lane = lax.broadcasted_iota(jnp.int32, g.shape, 1)
    gates_ref[...] = jnp.where(lane < M_HEADS, g, _log_sigmoid(g))
    z = _dot(small.astype(BF16), wa2_ref[...]) + ba_ref[...]
    la_ref[...] = _log_sigmoid(z) * (1.0 / G_TAU)


def _proj_call(x2d, conv_rows, wts, *, tb, steps_per_seq, n_seq):
    n = x2d.shape[0]
    per_token_conv = conv_rows is not None
    grid = (n // tb,)
    row = lambda w: pl.BlockSpec((tb, w), lambda i: (i, 0))
    whole = pl.BlockSpec(memory_space=pltpu.VMEM)
    in_specs = [row(D_MODEL)]
    args = [x2d]
    if per_token_conv:
        in_specs += [row(QK_CONV)] * 3
        args += list(conv_rows)
    in_specs += [whole] * 8
    args += [wts["g_mix"], wts["w_big"], wts["w_small"], wts["b_small"], wts["w_a2p"], wts["b_a"],
             wts["conv_w"], wts["conv_b"]]
    out_shape = [
        jax.ShapeDtypeStruct((n, M_WIDTH), BF16),
        jax.ShapeDtypeStruct((n, M_WIDTH), F32),
        jax.ShapeDtypeStruct((n, M_WIDTH), BF16),
        jax.ShapeDtypeStruct((n, M_WIDTH), F32),
        jax.ShapeDtypeStruct((n, G_KW), F32),
        jax.ShapeDtypeStruct((n, G_KW), F32),
        jax.ShapeDtypeStruct((n, G_VW), BF16),
        jax.ShapeDtypeStruct((n, G_VW), F32),
        jax.ShapeDtypeStruct((n, LANES), F32),
        jax.ShapeDtypeStruct((n, G_KW), F32),
    ]
    out_specs = [row(M_WIDTH), row(M_WIDTH), row(M_WIDTH), row(M_WIDTH), row(G_KW), row(G_KW),
                 row(G_VW), row(G_VW), row(LANES), row(G_KW)]
    scratch = []
    if per_token_conv:
        out_shape.append(jax.ShapeDtypeStruct((n, QK_CONV), F32))
        out_specs.append(row(QK_CONV))
    else:
        out_shape.append(jax.ShapeDtypeStruct((n_seq, SUBLANES, QK_CONV), F32))
        out_specs.append(pl.BlockSpec((1, SUBLANES, QK_CONV), lambda i: (i // steps_per_seq, 0, 0)))
        scratch.append(pltpu.VMEM((tb + SUBLANES, QK_CONV), F32))
    return pl.pallas_call(
        functools.partial(_proj_kernel, steps_per_seq=steps_per_seq, per_token_conv=per_token_conv),
        grid=grid, in_specs=in_specs, out_specs=out_specs, out_shape=out_shape, scratch_shapes=scratch,
        compiler_params=pltpu.CompilerParams(dimension_semantics=("arbitrary",), vmem_limit_bytes=VMEM_LIMIT),
        name="proj_tok" if per_token_conv else "proj_seq",
    )(*args)


def _mix_seq_kernel(mq_ref, mk_ref, mv_ref, gates_ref, gq_ref, gk_ref, gv_ref, la_ref, tril_ref, tri_ref,
                    wred_ref,
                    hm_ref, hg_ref, cout_ref, mout_ref, sout_ref,
                    caug_ref, m_ref, sbd_ref):
    ts = mq_ref.shape[0]
    n_chunks = ts // CHUNK
    r = pl.program_id(1)

    @pl.when(r == 0)
    def _():
        caug_ref[...] = jnp.zeros(caug_ref.shape, F32)
        m_ref[...] = jnp.zeros(m_ref.shape, F32)
        sbd_ref[...] = jnp.zeros(sbd_ref.shape, F32)

    gates = gates_ref[...]
    bcum = _cumsum_dot(tril_ref[...], gates)
    gates_t = gates.T
    bcum_t = bcum.T
    causal = (lax.broadcasted_iota(jnp.int32, (ts, ts), 0) >= lax.broadcasted_iota(jnp.int32, (ts, ts), 1))
    ones_col = (lax.broadcasted_iota(jnp.int32, (ts, LANES), 1) == 0).astype(BF16)
    for hd in range(M_HEADS):
        cs_, ce_ = hd * M_DH, (hd + 1) * M_DH
        b_col = bcum[:, M_HEADS + hd:M_HEADS + hd + 1]
        b_row = bcum_t[M_HEADS + hd:M_HEADS + hd + 1, :]
        i_col = gates[:, hd:hd + 1]
        i_row = gates_t[hd:hd + 1, :]
        m_prev = m_ref[hd:hd + 1, 0:1]
        dmat = jnp.where(causal, b_col - b_row + i_row, -jnp.inf)
        inter = b_col + m_prev
        m_tok = jnp.maximum(inter, jnp.max(dmat, axis=1, keepdims=True))
        q = mq_ref[:, cs_:ce_]
        k = mk_ref[:, cs_:ce_]
        vaug = jnp.concatenate([mv_ref[:, cs_:ce_], ones_col], axis=1)
        p = _dot_nt(q, k.astype(BF16)) * jnp.exp(dmat - m_tok)
        scale = jnp.exp(inter - m_tok)
        caug = caug_ref[hd]
        tot = scale * _dot(q, caug.astype(BF16)) + _dot(p.astype(BF16), vaug)
        den = tot[:, M_DH:M_DH + 1]
        hm_ref[:, cs_:ce_] = tot[:, :M_DH] / jnp.maximum(jnp.abs(den), jnp.exp(-m_tok))
        b_last = b_col[ts - 1:ts, :]
        dec = b_last - b_col + i_col
        m_new = jnp.maximum(b_last + m_prev, jnp.max(dec, axis=0, keepdims=True))
        wk = jnp.exp(dec - m_new)
        cscale = jnp.exp(b_last + m_prev - m_new)
        kw_t = (k * wk).T.astype(BF16)
        caug_ref[hd] = cscale * caug + _dot(kw_t, vaug)
        m_ref[hd:hd + 1, :] = jnp.broadcast_to(m_new, (1, LANES))

    gq = gq_ref[...]
    gk = gk_ref[...]
    bc = _cumsum_dot(tri_ref[...], la_ref[...])
    bc_t = bc.T

    nb = ts // SUB
    q3 = gq.reshape(nb, SUB, G_KW)
    k3 = gk.reshape(nb, SUB, G_KW)
    bc3 = bc.reshape(nb, SUB, G_KW)
    tl = lax.broadcasted_iota(jnp.int32, (nb, SUB, G_KW), 1)
    adiag = jnp.zeros((ts, G_KW), F32)
    for j in range(SUB):
        arg = jnp.where(tl >= j, bc3 - bc3[:, j:j + 1, :], NEG)
        e = (q3 * k3[:, j:j + 1, :] * jnp.exp(arg)).reshape(ts, G_KW)
        adiag = adiag + _dot(e.astype(BF16), wred_ref[j])
    sub_of = lambda idx: lax.shift_right_logical(idx & (CHUNK - 1), SUB.bit_length() - 1)
    rowb = sub_of(lax.broadcasted_iota(jnp.int32, (ts, G_KW), 0))
    colb = sub_of(lax.broadcasted_iota(jnp.int32, (ts, G_KW), 1))
    adiag = jnp.where(rowb == colb, adiag, 0.0)

    krow = lax.broadcasted_iota(jnp.int32, (CHUNK, G_KW), 0)
    same_head_kk = (_div_pow2(lax.broadcasted_iota(jnp.int32, (G_KW, G_KW), 0), G_DK)
                    == _div_pow2(lax.broadcasted_iota(jnp.int32, (G_KW, G_KW), 1), G_DK))
    offs = []
    for c in range(n_chunks):
        lo = c * CHUNK
        bc_c = bc[lo:lo + CHUNK]
        k_c = gk[lo:lo + CHUNK]
        offs.append(jnp.zeros((SUB, G_KW), F32))
        for i in range(1, CHUNK // SUB):
            r0 = lo + i * SUB
            r_i = bc[r0 - 1:r0, :]
            qi = (gq[r0:r0 + SUB] * jnp.exp(bc[r0:r0 + SUB] - r_i)).astype(BF16)
            ki = (k_c * jnp.exp(jnp.where(krow < i * SUB, r_i - bc_c, NEG))).astype(BF16)
            kbd = jnp.where(same_head_kk, jnp.concatenate([ki] * G_HEADS, axis=0), jnp.zeros((), BF16))
            offs.append(_dot_nt(qi, kbd))
    intra = (jnp.concatenate(offs, axis=0) + adiag).astype(BF16)

    q_in = (gq * jnp.exp(bc)).astype(BF16)
    same_head_kv = (_div_pow2(lax.broadcasted_iota(jnp.int32, (G_KW, G_VW), 0), G_DK)
                    == _div_pow2(lax.broadcasted_iota(jnp.int32, (G_KW, G_VW), 1), G_DV))
    for c in range(n_chunks):
        lo, hi = c * CHUNK, (c + 1) * CHUNK
        v_c = gv_ref[lo:hi, :]
        vbd = jnp.where(same_head_kv, jnp.concatenate([v_c] * G_HEADS, axis=0), jnp.zeros((), BF16))
        sbd = sbd_ref[...]
        lhs = jnp.concatenate([intra[lo:hi], q_in[lo:hi]], axis=1)
        rhs = jnp.concatenate([vbd, sbd.astype(BF16)], axis=0)
        hg_ref[lo:hi, :] = _dot(lhs, rhs)
        b_end = bc[hi - 1:hi, :]
        k_out_t = (gk[lo:hi] * jnp.exp(b_end - bc[lo:hi])).T.astype(BF16)
        upd = _dot(k_out_t, v_c)
        dcol = jnp.exp(bc_t[:, hi - 1:hi])
        sbd_ref[...] = dcol * sbd + jnp.where(same_head_kv, upd, 0.0)

    @pl.when(r == pl.num_programs(1) - 1)
    def _():
        cout_ref[0] = caug_ref[...]
        mout_ref[0] = m_ref[...]
        for g in range(G_HEADS):
            sout_ref[0, g] = sbd_ref[g * G_DK:(g + 1) * G_DK, g * G_DV:(g + 1) * G_DV]


def _mix_seq_call(p, tril, tri, wred, *, n_seq, seq_len, ts):
    steps = seq_len // ts
    row = lambda w: pl.BlockSpec((ts, w), lambda b, r: (b * steps + r, 0))
    whole = pl.BlockSpec(memory_space=pltpu.VMEM)
    n = n_seq * seq_len
    return pl.pallas_call(
        _mix_seq_kernel,
        grid=(n_seq, steps),
        in_specs=[row(M_WIDTH), row(M_WIDTH), row(M_WIDTH), row(LANES), row(G_KW), row(G_KW), row(G_VW),
                  row(G_KW), whole, whole, whole],
        out_specs=[row(M_WIDTH), row(G_VW),
                   pl.BlockSpec((1, M_HEADS, M_DH, 2 * M_DH), lambda b, r: (b, 0, 0, 0)),
                   pl.BlockSpec((1, SUBLANES, LANES), lambda b, r: (b, 0, 0)),
                   pl.BlockSpec((1, G_HEADS, G_DK, G_DV), lambda b, r: (b, 0, 0, 0))],
        out_shape=[jax.ShapeDtypeStruct((n, M_WIDTH), F32), jax.ShapeDtypeStruct((n, G_VW), F32),
                   jax.ShapeDtypeStruct((n_seq, M_HEADS, M_DH, 2 * M_DH), F32),
                   jax.ShapeDtypeStruct((n_seq, SUBLANES, LANES), F32),
                   jax.ShapeDtypeStruct((n_seq, G_HEADS, G_DK, G_DV), F32)],
        scratch_shapes=[pltpu.VMEM((M_HEADS, M_DH, 2 * M_DH), F32), pltpu.VMEM((SUBLANES, LANES), F32),
                        pltpu.VMEM((G_KW, G_VW), F32)],
        compiler_params=pltpu.CompilerParams(dimension_semantics=("arbitrary", "arbitrary"),
                                             vmem_limit_bytes=VMEM_LIMIT),
        name="mix_seq",
    )(p["mq"], p["mk"], p["mv"], p["gates"], p["gq"], p["gk"], p["gv"], p["la"], tril, tri, wred)


def _mix_tok_kernel(mq_ref, mk_ref, mv_ref, gates_ref, gq_ref, gk_ref, gv_ref, la_ref,
                    c_ref, n_ref, m_ref, s_ref,
                    hm_ref, hg_ref, cn_ref, nn_ref, mn_ref, sn_ref):
    bb = mq_ref.shape[0]
    gates = gates_ref[...]
    m_all = m_ref[...]
    for hd in range(M_HEADS):
        cs_, ce_ = hd * M_DH, (hd + 1) * M_DH
        q = mq_ref[:, cs_:ce_].astype(F32)
        k = mk_ref[:, cs_:ce_]
        v = mv_ref[:, cs_:ce_].astype(F32)
        n_prev = n_ref[:, cs_:ce_]
        ig = gates[:, hd:hd + 1]
        lf = gates[:, M_HEADS + hd:M_HEADS + hd + 1]
        m_prev = m_all[:, hd:hd + 1]
        m_new = jnp.maximum(lf + m_prev, ig)
        scale = jnp.exp(lf + m_prev - m_new)
        wk = jnp.exp(ig - m_new)
        s_qk = jnp.sum(q * k, axis=1, keepdims=True) * wk
        den = scale * jnp.sum(q * n_prev, axis=1, keepdims=True) + s_qk
        inv = 1.0 / jnp.maximum(jnp.abs(den), jnp.exp(-m_new))
        nn_ref[:, cs_:ce_] = scale * n_prev + wk * k
        mn_ref[:, hd:hd + 1] = m_new
        q_t = q.T
        kw_t = (k * wk).T
        for b in range(bb):
            c_prev = c_ref[b, hd]
            q_col = q_t[:, b:b + 1]
            qc = jnp.sum(q_col * c_prev, axis=0, keepdims=True)
            num = scale[b:b + 1] * qc + s_qk[b:b + 1] * v[b:b + 1]
            hm_ref[b:b + 1, cs_:ce_] = num * inv[b:b + 1]
            cn_ref[b, hd] = scale[b:b + 1] * c_prev + kw_t[:, b:b + 1] * v[b:b + 1]
    la = la_ref[...]
    dec = jnp.exp(la)
    gq = gq_ref[...]
    gk = gk_ref[...]
    for g in range(G_HEADS):
        ks_, ke_ = g * G_DK, (g + 1) * G_DK
        vs_, ve_ = g * G_DV, (g + 1) * G_DV
        q = gq[:, ks_:ke_]
        k = gk[:, ks_:ke_]
        v = gv_ref[:, vs_:ve_].astype(F32)
        a = jnp.sum(q * k, axis=1, keepdims=True)
        qd_t = (q * dec[:, ks_:ke_]).T
        k_t = k.T
        d_t = dec[:, ks_:ke_].T
        for b in range(bb):
            s_prev = s_ref[b, g]
            o = jnp.sum(qd_t[:, b:b + 1] * s_prev, axis=0, keepdims=True) + a[b:b + 1] * v[b:b + 1]
            hg_ref[b:b + 1, vs_:ve_] = o
            sn_ref[b, g] = d_t[:, b:b + 1] * s_prev + k_t[:, b:b + 1] * v[b:b + 1]


def _mix_tok_call(p, c0, n0, m0, s0, *, bb):
    n = c0.shape[0]
    row = lambda w: pl.BlockSpec((bb, w), lambda i: (i, 0))
    st4 = lambda a, b_: pl.BlockSpec((bb, M_HEADS, a, b_), lambda i: (i, 0, 0, 0))
    return pl.pallas_call(
        _mix_tok_kernel,
        grid=(n // bb,),
        in_specs=[row(M_WIDTH), row(M_WIDTH), row(M_WIDTH), row(LANES), row(G_KW), row(G_KW), row(G_VW),
                  row(G_KW), st4(M_DH, M_DH), row(M_WIDTH), row(M_HEADS), st4(G_DK, G_DV)],
        out_specs=[row(M_WIDTH), row(G_VW), st4(M_DH, M_DH), row(M_WIDTH), row(M_HEADS), st4(G_DK, G_DV)],
        out_shape=[jax.ShapeDtypeStruct((n, M_WIDTH), F32), jax.ShapeDtypeStruct((n, G_VW), F32),
                   jax.ShapeDtypeStruct(c0.shape, F32), jax.ShapeDtypeStruct((n, M_WIDTH), F32),
                   jax.ShapeDtypeStruct((n, M_HEADS), F32), jax.ShapeDtypeStruct(s0.shape, F32)],
        compiler_params=pltpu.CompilerParams(dimension_semantics=("arbitrary",), vmem_limit_bytes=VMEM_LIMIT),
        name="mix_tok",
    )(p["mq"], p["mk"], p["mv"], p["gates"], p["gq"], p["gk"], p["gv"], p["la"], c0, n0, m0, s0)


def _head_norm(hv, n_heads, width):
    parts = []
    for hd in range(n_heads):
        seg = hv[:, hd * width:(hd + 1) * width]
        parts.append(seg * lax.rsqrt(jnp.mean(seg * seg, axis=-1, keepdims=True) + EPS))
    return jnp.concatenate(parts, axis=1)


def _post_kernel(x_ref, p_ref, hm_ref, hg_ref, mo_ref, gr_ref, gmh_ref, ggh_ref, wout_ref, gmlp_ref,
                 w1_ref, w2_ref, gple_ref, wple_ref, wpg_ref, gfin_ref, y_ref):
    hm = _head_norm(hm_ref[...], M_HEADS, M_DH) * gmh_ref[...] * _sigmoid(mo_ref[...])
    gr = gr_ref[...]
    hg = _head_norm(hg_ref[...], G_HEADS, G_DV) * ggh_ref[...] * (gr * _sigmoid(gr))
    mixed = jnp.concatenate([hm, hg], axis=1).astype(BF16)
    x1 = x_ref[...] + _dot(mixed, wout_ref[...])
    u = _dot(_rms(x1, gmlp_ref[...]).astype(BF16), w1_ref[...])
    act = jnp.square(jnp.maximum(u, 0.0)).astype(BF16)
    x2 = x1 + _dot(act, w2_ref[...])
    ple = _dot(p_ref[...].astype(BF16), wple_ref[...])
    gate = _sigmoid(_dot(_rms(x2, gple_ref[...]).astype(BF16), wpg_ref[...]))
    x3 = x2 + ple * gate
    y_ref[...] = _rms(x3, gfin_ref[...])


def _post_call(x2d, p2d, hm, hg, mo, gr, wts, *, tb):
    n = x2d.shape[0]
    row = lambda w: pl.BlockSpec((tb, w), lambda i: (i, 0))
    whole = pl.BlockSpec(memory_space=pltpu.VMEM)
    return pl.pallas_call(
        _post_kernel,
        grid=(n // tb,),
        in_specs=[row(D_MODEL), row(D_PLE), row(M_WIDTH), row(G_VW), row(M_WIDTH), row(G_VW)] + [whole] * 10,
        out_specs=row(D_MODEL),
        out_shape=jax.ShapeDtypeStruct((n, D_MODEL), F32),
        compiler_params=pltpu.CompilerParams(dimension_semantics=("arbitrary",), vmem_limit_bytes=VMEM_LIMIT),
        name="post",
    )(x2d, p2d, hm, hg, mo, gr, wts["g_mhead"], wts["g_ghead"], wts["w_out"], wts["g_mlp"], wts["w1"],
      wts["w2"], wts["g_ple"], wts["w_ple"], wts["w_pg"], wts["g_final"])


def _prep_weights(w_in, conv_w, conv_b, b_gate, w_a2, b_a, g_mhead, g_ghead, w_out, g_mix, g_mlp, w1, w2,
                  g_ple, w_ple, w_pg, g_final):
    offs = [0]
    for s in IN_SIZES:
        offs.append(offs[-1] + s)
    gates_lo, gates_hi = offs[3], offs[5]
    ga_lo = offs[9]
    w_big = jnp.concatenate([w_in[:, :gates_lo], w_in[:, gates_hi:ga_lo]], axis=1).astype(BF16)
    n_small = 2 * M_HEADS + G_RANK
    w_small = jnp.concatenate([w_in[:, gates_lo:gates_hi], w_in[:, ga_lo:],
                               jnp.zeros((D_MODEL, LANES - n_small), F32)], axis=1).astype(BF16)
    b_small = jnp.concatenate([b_gate, jnp.zeros((LANES - 2 * M_HEADS,), F32)])[None]
    w_a2p = jnp.concatenate([jnp.zeros((2 * M_HEADS, G_KW), F32), w_a2,
                             jnp.zeros((LANES - n_small, G_KW), F32)], axis=0).astype(BF16)
    return dict(
        w_big=w_big, w_small=w_small, b_small=b_small, w_a2p=w_a2p, b_a=b_a[None],
        conv_w=conv_w, conv_b=conv_b[None], g_mix=g_mix[None], g_mhead=g_mhead[None], g_ghead=g_ghead[None],
        w_out=w_out.astype(BF16), g_mlp=g_mlp[None], w1=w1.astype(BF16), w2=w2.astype(BF16),
        g_ple=g_ple[None], w_ple=w_ple.astype(BF16), w_pg=w_pg.astype(BF16), g_final=g_final[None])


def _mix_constants(ts):
    t = jnp.arange(ts)
    tril = (t[None, :] <= t[:, None]).astype(BF16)
    tri = ((t[:, None] // CHUNK == t[None, :] // CHUNK) & (t[None, :] <= t[:, None])).astype(BF16)
    rr = jnp.arange(G_KW)
    wred = ((rr[None, :, None] // G_DK == rr[None, None, :] // G_DK)
            & (rr[None, None, :] % SUB == jnp.arange(SUB)[:, None, None])).astype(BF16)
    return tril, tri, wred


_PROJ_NAMES = ("mq", "mk", "mv", "mo", "gq", "gk", "gv", "gr", "gates", "la")


def kernel(x_prompt, x_sample, p_prompt, p_sample, state_mlstm_C, state_mlstm_n, state_mlstm_m, state_conv,
           state_gla_S, w_in, conv_w, conv_b, b_gate, w_a2, b_a, g_mhead, g_ghead, w_out, g_mix, g_mlp, w1,
           w2, g_ple, w_ple, w_pg, g_final):
    assert w_in.shape[0] == 1, "single-layer trunk"
    n_seq, seq_len, _ = x_prompt.shape
    n_tok = x_sample.shape[0]
    assert x_sample.shape[1] == 1
    wts = _prep_weights(w_in[0], conv_w[0], conv_b[0], b_gate[0], w_a2[0], b_a[0], g_mhead[0], g_ghead[0],
                        w_out[0], g_mix[0], g_mlp[0], w1[0], w2[0], g_ple[0], w_ple[0], w_pg[0], g_final)

    xp = x_prompt.reshape(n_seq * seq_len, D_MODEL)
    tb1 = 512
    outs = _proj_call(xp, None, wts, tb=tb1, steps_per_seq=seq_len // tb1, n_seq=n_seq)
    pp = dict(zip(_PROJ_NAMES, outs[:-1]))
    tail_p = outs[-1]
    ts = 256
    tril, tri, wred = _mix_constants(ts)
    hm_p, hg_p, caug_p, m_p, s_p = _mix_seq_call(pp, tril, tri, wred, n_seq=n_seq, seq_len=seq_len, ts=ts)
    y_p = _post_call(xp, p_prompt[0].reshape(n_seq * seq_len, D_PLE), hm_p, hg_p, pp["mo"], pp["gr"], wts,
                     tb=256)

    xs = x_sample.reshape(n_tok, D_MODEL)
    buf = state_conv[0]
    outs = _proj_call(xs, (buf[:, 0], buf[:, 1], buf[:, 2]), wts, tb=n_tok, steps_per_seq=1, n_seq=n_tok)
    ps = dict(zip(_PROJ_NAMES, outs[:-1]))
    raw_s = outs[-1]
    hm_s, hg_s, c_s, n_s, m_s, s_s = _mix_tok_call(
        ps, state_mlstm_C[0], state_mlstm_n[0].reshape(n_tok, M_WIDTH), state_mlstm_m[0], state_gla_S[0], bb=8)
    y_s = _post_call(xs, p_sample[0].reshape(n_tok, D_PLE), hm_s, hg_s, ps["mo"], ps["gr"], wts, tb=n_tok)

    return (y_p.reshape(n_seq, seq_len, D_MODEL),
            y_s.reshape(n_tok, 1, D_MODEL),
            caug_p[None, :, :, :, :M_DH],
            caug_p[None, :, :, :, M_DH],
            m_p[None, :, :M_HEADS, 0],
            tail_p[None, :, SUBLANES - (CONV_W - 1):, :],
            s_p[None],
            c_s[None],
            n_s.reshape(1, n_tok, M_HEADS, M_DH),
            m_s[None],
            jnp.stack([buf[:, 1], buf[:, 2], raw_s], axis=1)[None],
            s_s[None])
```

```python
import functools

import jax
import jax.numpy as jnp
from jax import lax
from jax.experimental import pallas as pl
from jax.experimental.pallas import tpu as pltpu

D_MODEL = 1024
M_HEADS = 4
M_DH = 128
M_WIDTH = M_HEADS * M_DH
G_HEADS = 4
G_DK = 64
G_DV = 128
G_KW = G_HEADS * G_DK
G_VW = G_HEADS * G_DV
G_RANK = 16
G_TAU = 16.0
CONV_W = 4
QK_CONV = 2 * M_WIDTH
D_FF = 4 * D_MODEL
D_PLE = 256
CHUNK = 64
SUB = 16
EPS = 1e-6
IN_SIZES = (QK_CONV, M_WIDTH, M_WIDTH, M_HEADS, M_HEADS, G_KW, G_KW, G_VW, G_VW, G_RANK)

LANES = 128
SUBLANES = 8
VMEM_LIMIT = 56 * 1024 * 1024

F32 = jnp.float32
BF16 = jnp.bfloat16
NEG = -1e30


def _rms(x, g):
    return x * lax.rsqrt(jnp.mean(x * x, axis=-1, keepdims=True) + EPS) * g


def _log_sigmoid(x):
    return jnp.minimum(x, 0.0) - jnp.log1p(jnp.exp(-jnp.abs(x)))


def _sigmoid(x):
    return 1.0 / (1.0 + jnp.exp(-x))


def _div_pow2(idx, d):
    assert d & (d - 1) == 0
    return lax.shift_right_logical(idx, d.bit_length() - 1)


def _dot(a, b):
    return jnp.dot(a, b, preferred_element_type=F32)


def _dot_nt(a, b):
    return lax.dot_general(a, b, (((1,), (1,)), ((), ())), preferred_element_type=F32)


def _cumsum_dot(tri, x):
    hi = x.astype(BF16)
    r1 = x - hi.astype(F32)
    mid = r1.astype(BF16)
    lo = (r1 - mid.astype(F32)).astype(BF16)
    return _dot(tri, hi) + _dot(tri, mid) + _dot(tri, lo)


def _proj_kernel(*refs, steps_per_seq, per_token_conv):
    if per_token_conv:
        (x_ref, b0_ref, b1_ref, b2_ref, gmix_ref, wbig_ref, wsmall_ref, bsmall_ref, wa2_ref, ba_ref,
         convw_ref, convb_ref,
         mq_ref, mk_ref, mv_ref, mo_ref, gq_ref, gk_ref, gv_ref, gr_ref, gates_ref, la_ref, raw_ref) = refs
    else:
        (x_ref, gmix_ref, wbig_ref, wsmall_ref, bsmall_ref, wa2_ref, ba_ref, convw_ref, convb_ref,
         mq_ref, mk_ref, mv_ref, mo_ref, gq_ref, gk_ref, gv_ref, gr_ref, gates_ref, la_ref, tail_ref,
         cbuf_ref) = refs
    tb = x_ref.shape[0]
    h = _rms(x_ref[...], gmix_ref[...]).astype(BF16)

    qk_raw = _dot(h, wbig_ref[:, 0:QK_CONV])
    cw = convw_ref[...]
    if per_token_conv:
        y = (convb_ref[...] + b0_ref[...] * cw[0:1] + b1_ref[...] * cw[1:2] + b2_ref[...] * cw[2:3]
             + qk_raw * cw[3:4])
        raw_ref[...] = qk_raw
    else:
        i = pl.program_id(0)

        @pl.when(i % steps_per_seq == 0)
        def _():
            cbuf_ref[0:SUBLANES, :] = jnp.zeros((SUBLANES, QK_CONV), F32)

        cbuf_ref[SUBLANES:SUBLANES + tb, :] = qk_raw
        y = convb_ref[...] + qk_raw * cw[3:4]
        for j in range(CONV_W - 1):
            off = SUBLANES - (CONV_W - 1) + j
            y = y + cbuf_ref[off:off + tb, :] * cw[j:j + 1]
        last = cbuf_ref[tb:tb + SUBLANES, :]
        tail_ref[0] = last
        cbuf_ref[0:SUBLANES, :] = last
    qk = y * _sigmoid(y)
    mq_ref[...] = qk[:, :M_WIDTH].astype(mq_ref.dtype)
    mk_ref[...] = qk[:, M_WIDTH:] * (M_DH ** -0.5)

    o = QK_CONV
    mv_ref[...] = _dot(h, wbig_ref[:, o:o + M_WIDTH]).astype(mv_ref.dtype)
    o += M_WIDTH
    mo_ref[...] = _dot(h, wbig_ref[:, o:o + M_WIDTH])
    o += M_WIDTH
    gq_ref[...] = _dot(h, wbig_ref[:, o:o + G_KW]) * (G_DK ** -0.5)
    o += G_KW
    gk_ref[...] = _dot(h, wbig_ref[:, o:o + G_KW])
    o += G_KW
    gv_ref[...] = _dot(h, wbig_ref[:, o:o + G_VW]).astype(gv_ref.dtype)
    o += G_VW
    gr_ref[...] = _dot(h, wbig_ref[:, o:o + G_VW])

    small = _dot(h, wsmall_ref[...])
    g = small + bsmall_ref[...]
    lane = lax.broadcasted_iota(jnp.int32, g.shape, 1)
    gates_ref[...] = jnp.where(lane < M_HEADS, g, _log_sigmoid(g))
    z = _dot(small.astype(BF16), wa2_ref[...]) + ba_ref[...]
    la_ref[...] = _log_sigmoid(z) * (1.0 / G_TAU)


def _proj_call(x2d, conv_rows, wts, *, tb, steps_per_seq, n_seq):
    n = x2d.shape[0]
    per_token_conv = conv_rows is not None
    grid = (n // tb,)
    row = lambda w: pl.BlockSpec((tb, w), lambda i: (i, 0))
    whole = pl.BlockSpec(memory_space=pltpu.VMEM)
    in_specs = [row(D_MODEL)]
    args = [x2d]
    if per_token_conv:
        in_specs += [row(QK_CONV)] * 3
        args += list(conv_rows)
    in_specs += [whole] * 8
    args += [wts["g_mix"], wts["w_big"], wts["w_small"], wts["b_small"], wts["w_a2p"], wts["b_a"],
             wts["conv_w"], wts["conv_b"]]
    out_shape = [
        jax.ShapeDtypeStruct((n, M_WIDTH), BF16),
        jax.ShapeDtypeStruct((n, M_WIDTH), F32),
        jax.ShapeDtypeStruct((n, M_WIDTH), BF16),
        jax.ShapeDtypeStruct((n, M_WIDTH), F32),
        jax.ShapeDtypeStruct((n, G_KW), F32),
        jax.ShapeDtypeStruct((n, G_KW), F32),
        jax.ShapeDtypeStruct((n, G_VW), BF16),
        jax.ShapeDtypeStruct((n, G_VW), F32),
        jax.ShapeDtypeStruct((n, LANES), F32),
        jax.ShapeDtypeStruct((n, G_KW), F32),
    ]
    out_specs = [row(M_WIDTH), row(M_WIDTH), row(M_WIDTH), row(M_WIDTH), row(G_KW), row(G_KW),
                 row(G_VW), row(G_VW), row(LANES), row(G_KW)]
    scratch = []
    if per_token_conv:
        out_shape.append(jax.ShapeDtypeStruct((n, QK_CONV), F32))
        out_specs.append(row(QK_CONV))
    else:
        out_shape.append(jax.ShapeDtypeStruct((n_seq, SUBLANES, QK_CONV), F32))
        out_specs.append(pl.BlockSpec((1, SUBLANES, QK_CONV), lambda i: (i // steps_per_seq, 0, 0)))
        scratch.append(pltpu.VMEM((tb + SUBLANES, QK_CONV), F32))
    return pl.pallas_call(
        functools.partial(_proj_kernel, steps_per_seq=steps_per_seq, per_token_conv=per_token_conv),
        grid=grid, in_specs=in_specs, out_specs=out_specs, out_shape=out_shape, scratch_shapes=scratch,
        compiler_params=pltpu.CompilerParams(dimension_semantics=("arbitrary",), vmem_limit_bytes=VMEM_LIMIT),
        name="proj_tok" if per_token_conv else "proj_seq",
    )(*args)


def _mix_seq_kernel(mq_ref, mk_ref, mv_ref, gates_ref, gq_ref, gk_ref, gv_ref, la_ref, tril_ref, tri_ref,
                    wred_ref,
                    hm_ref, hg_ref, cout_ref, mout_ref, sout_ref,
                    caug_ref, m_ref, sbd_ref):
    ts = mq_ref.shape[0]
    n_chunks = ts // CHUNK
    r = pl.program_id(1)

    @pl.when(r == 0)
    def _():
        caug_ref[...] = jnp.zeros(caug_ref.shape, F32)
        m_ref[...] = jnp.zeros(m_ref.shape, F32)
        sbd_ref[...] = jnp.zeros(sbd_ref.shape, F32)

    gates = gates_ref[...]
    bcum = _cumsum_dot(tril_ref[...], gates)
    gq = gq_ref[...]
    gk = gk_ref[...]
    bc = _cumsum_dot(tri_ref[...], la_ref[...])
    gates_t = gates.T
    bcum_t = bcum.T
    bc_t = bc.T

    nb = ts // SUB
    q3 = gq.reshape(nb, SUB, G_KW)
    k3 = gk.reshape(nb, SUB, G_KW)
    bc3 = bc.reshape(nb, SUB, G_KW)
    tl = lax.broadcasted_iota(jnp.int32, (nb, SUB, G_KW), 1)

    def exact_pass(j, acc):
        arg = jnp.where(tl >= j, bc3 - bc3[:, j:j + 1, :], NEG)
        e = (q3 * k3[:, j:j + 1, :] * jnp.exp(arg)).reshape(ts, G_KW)
        return acc + _dot(e.astype(BF16), wred_ref[j])

    krow = lax.broadcasted_iota(jnp.int32, (CHUNK, G_KW), 0)
    same_head_kk = (_div_pow2(lax.broadcasted_iota(jnp.int32, (G_KW, G_KW), 0), G_DK)
                    == _div_pow2(lax.broadcasted_iota(jnp.int32, (G_KW, G_KW), 1), G_DK))
    same_head_kv = (_div_pow2(lax.broadcasted_iota(jnp.int32, (G_KW, G_VW), 0), G_DK)
                    == _div_pow2(lax.broadcasted_iota(jnp.int32, (G_KW, G_VW), 1), G_DV))

    def cross_block_scores(c):
        lo = c * CHUNK
        bc_c = bc[lo:lo + CHUNK]
        k_c = gk[lo:lo + CHUNK]
        out = [jnp.zeros((SUB, G_KW), F32)]
        for i in range(1, CHUNK // SUB):
            r0 = lo + i * SUB
            r_i = bc[r0 - 1:r0, :]
            qi = (gq[r0:r0 + SUB] * jnp.exp(bc[r0:r0 + SUB] - r_i)).astype(BF16)
            ki = (k_c * jnp.exp(jnp.where(krow < i * SUB, r_i - bc_c, NEG))).astype(BF16)
            kbd = jnp.where(same_head_kk, jnp.concatenate([ki] * G_HEADS, axis=0), jnp.zeros((), BF16))
            out.append(_dot_nt(qi, kbd))
        return out

    def state_update_term(c):
        lo, hi = c * CHUNK, (c + 1) * CHUNK
        b_end = bc[hi - 1:hi, :]
        k_out_t = (gk[lo:hi] * jnp.exp(b_end - bc[lo:hi])).T.astype(BF16)
        return jnp.where(same_head_kv, _dot(k_out_t, gv_ref[lo:hi, :]), 0.0)

    adiag = jnp.zeros((ts, G_KW), F32)
    per_stage = SUB // 4
    assert n_chunks == 4
    offs = []
    s_terms = []

    causal =(lax.broadcasted_iota(jnp.int32, (ts, ts), 0) >= lax.broadcasted_iota(jnp.int32, (ts, ts), 1))
    ones_col = (lax.broadcasted_iota(jnp.int32, (ts, LANES), 1) == 0).astype(BF16)
    heads = range(M_HEADS)
    hsl = [slice(hd * M_DH, (hd + 1) * M_DH) for hd in heads]
    b_col = [bcum[:, M_HEADS + hd:M_HEADS + hd + 1] for hd in heads]
    b_row = [bcum_t[M_HEADS + hd:M_HEADS + hd + 1, :] for hd in heads]
    i_col = [gates[:, hd:hd + 1] for hd in heads]
    i_row = [gates_t[hd:hd + 1, :] for hd in heads]
    m_prev = [m_ref[hd:hd + 1, 0:1] for hd in heads]
    q = [mq_ref[:, hsl[hd]] for hd in heads]
    k = [mk_ref[:, hsl[hd]] for hd in heads]
    vaug = [jnp.concatenate([mv_ref[:, hsl[hd]], ones_col], axis=1) for hd in heads]
    caug = [caug_ref[hd] for hd in heads]
    s_qk = [_dot_nt(q[hd], k[hd].astype(BF16)) for hd in heads]
    qc = [_dot(q[hd], caug[hd].astype(BF16)) for hd in heads]
    for j in range(0, per_stage):
        adiag = exact_pass(j, adiag)
    offs += cross_block_scores(0)
    s_terms.append(state_update_term(0))
    b_last =[b_col[hd][ts - 1:ts, :] for hd in heads]
    dec = [b_last[hd] - b_col[hd] + i_col[hd] for hd in heads]
    m_new = [jnp.maximum(b_last[hd] + m_prev[hd], jnp.max(dec[hd], axis=0, keepdims=True)) for hd in heads]
    kw_t = [(k[hd] * jnp.exp(dec[hd] - m_new[hd])).T.astype(BF16) for hd in heads]
    upd = [_dot(kw_t[hd], vaug[hd]) for hd in heads]
    for j in range(per_stage, 2 * per_stage):
        adiag = exact_pass(j, adiag)
    offs += cross_block_scores(1)
    s_terms.append(state_update_term(1))
    dmat =[jnp.where(causal, b_col[hd] - b_row[hd] + i_row[hd], -jnp.inf) for hd in heads]
    inter = [b_col[hd] + m_prev[hd] for hd in heads]
    m_tok = [jnp.maximum(inter[hd], jnp.max(dmat[hd], axis=1, keepdims=True)) for hd in heads]
    for j in range(2 * per_stage, 3 * per_stage):
        adiag = exact_pass(j, adiag)
    offs += cross_block_scores(2)
    s_terms.append(state_update_term(2))
    p =[(s_qk[hd] * jnp.exp(dmat[hd] - m_tok[hd])).astype(BF16) for hd in heads]
    pv = [_dot(p[hd], vaug[hd]) for hd in heads]
    for j in range(3 * per_stage, SUB):
        adiag = exact_pass(j, adiag)
    offs += cross_block_scores(3)
    s_terms.append(state_update_term(3))
    for hd in heads:
        tot = jnp.exp(inter[hd] - m_tok[hd]) * qc[hd] + pv[hd]
        den = tot[:, M_DH:M_DH + 1]
        hm_ref[:, hsl[hd]] = tot[:, :M_DH] / jnp.maximum(jnp.abs(den), jnp.exp(-m_tok[hd]))
        caug_ref[hd] = jnp.exp(b_last[hd] + m_prev[hd] - m_new[hd]) * caug[hd] + upd[hd]
        m_ref[hd:hd + 1, :] = jnp.broadcast_to(m_new[hd], (1, LANES))

    sub_of = lambda idx: lax.shift_right_logical(idx & (CHUNK - 1), SUB.bit_length() - 1)
    rowb = sub_of(lax.broadcasted_iota(jnp.int32, (ts, G_KW), 0))
    colb = sub_of(lax.broadcasted_iota(jnp.int32, (ts, G_KW), 1))
    adiag = jnp.where(rowb == colb, adiag, 0.0)

    intra = (jnp.concatenate(offs, axis=0) + adiag).astype(BF16)
    q_in = (gq * jnp.exp(bc)).astype(BF16)
    o_intra = []
    for c in range(n_chunks):
        lo, hi = c * CHUNK, (c + 1) * CHUNK
        vbd = jnp.where(same_head_kv, jnp.concatenate([gv_ref[lo:hi, :]] * G_HEADS, axis=0),
                        jnp.zeros((), BF16))
        o_intra.append(_dot(intra[lo:hi], vbd))
    sbd = sbd_ref[...]
    for c in range(n_chunks):
        lo, hi = c * CHUNK, (c + 1) * CHUNK
        hg_ref[lo:hi, :] = o_intra[c] + _dot(q_in[lo:hi], sbd.astype(BF16))
        dcol = jnp.exp(bc_t[:, hi - 1:hi])
        sbd = dcol * sbd + s_terms[c]
    sbd_ref[...] = sbd

    @pl.when(r == pl.num_programs(1) - 1)
    def _():
        cout_ref[0] = caug_ref[...]
        mout_ref[0] = m_ref[...]
        for g in range(G_HEADS):
            sout_ref[0, g] = sbd_ref[g * G_DK:(g + 1) * G_DK, g * G_DV:(g + 1) * G_DV]


def _mix_seq_call(p, tril, tri, wred, *, n_seq, seq_len, ts):
    steps = seq_len // ts
    row = lambda w: pl.BlockSpec((ts, w), lambda b, r: (b * steps + r, 0))
    whole = pl.BlockSpec(memory_space=pltpu.VMEM)
    n = n_seq * seq_len
    return pl.pallas_call(
        _mix_seq_kernel,
        grid=(n_seq, steps),
        in_specs=[row(M_WIDTH), row(M_WIDTH), row(M_WIDTH), row(LANES), row(G_KW), row(G_KW), row(G_VW),
                  row(G_KW), whole, whole, whole],
        out_specs=[row(M_WIDTH), row(G_VW),
                   pl.BlockSpec((1, M_HEADS, M_DH, 2 * M_DH), lambda b, r: (b, 0, 0, 0)),
                   pl.BlockSpec((1, SUBLANES, LANES), lambda b, r: (b, 0, 0)),
                   pl.BlockSpec((1, G_HEADS, G_DK, G_DV), lambda b, r: (b, 0, 0, 0))],
        out_shape=[jax.ShapeDtypeStruct((n, M_WIDTH), F32), jax.ShapeDtypeStruct((n, G_VW), F32),
                   jax.ShapeDtypeStruct((n_seq, M_HEADS, M_DH, 2 * M_DH), F32),
                   jax.ShapeDtypeStruct((n_seq, SUBLANES, LANES), F32),
                   jax.ShapeDtypeStruct((n_seq, G_HEADS, G_DK, G_DV), F32)],
        scratch_shapes=[pltpu.VMEM((M_HEADS, M_DH, 2 * M_DH), F32), pltpu.VMEM((SUBLANES, LANES), F32),
                        pltpu.VMEM((G_KW, G_VW), F32)],
        compiler_params=pltpu.CompilerParams(dimension_semantics=("arbitrary", "arbitrary"),
                                             vmem_limit_bytes=VMEM_LIMIT),
        name="mix_seq",
    )(p["mq"], p["mk"], p["mv"], p["gates"], p["gq"], p["gk"], p["gv"], p["la"], tril, tri, wred)


def _mix_tok_kernel(mq_ref, mk_ref, mv_ref, gates_ref, gq_ref, gk_ref, gv_ref, la_ref,
                    c_ref, n_ref, m_ref, s_ref,
                    hm_ref, hg_ref, cn_ref, nn_ref, mn_ref, sn_ref):
    bb = mq_ref.shape[0]
    gates = gates_ref[...]
    m_all = m_ref[...]
    for hd in range(M_HEADS):
        cs_, ce_ = hd * M_DH, (hd + 1) * M_DH
        q = mq_ref[:, cs_:ce_].astype(F32)
        k = mk_ref[:, cs_:ce_]
        v = mv_ref[:, cs_:ce_].astype(F32)
        n_prev = n_ref[:, cs_:ce_]
        ig = gates[:, hd:hd + 1]
        lf = gates[:, M_HEADS + hd:M_HEADS + hd + 1]
        m_prev = m_all[:, hd:hd + 1]
        m_new = jnp.maximum(lf + m_prev, ig)
        scale = jnp.exp(lf + m_prev - m_new)
        wk = jnp.exp(ig - m_new)
        s_qk = jnp.sum(q * k, axis=1, keepdims=True) * wk
        den = scale * jnp.sum(q * n_prev, axis=1, keepdims=True) + s_qk
        inv = 1.0 / jnp.maximum(jnp.abs(den), jnp.exp(-m_new))
        nn_ref[:, cs_:ce_] = scale * n_prev + wk * k
        mn_ref[:, hd:hd + 1] = m_new
        q_t = q.T
        kw_t = (k * wk).T
        for b in range(bb):
            c_prev = c_ref[b, hd]
            q_col = q_t[:, b:b + 1]
            qc = jnp.sum(q_col * c_prev, axis=0, keepdims=True)
            num = scale[b:b + 1] * qc + s_qk[b:b + 1] * v[b:b + 1]
            hm_ref[b:b + 1, cs_:ce_] = num * inv[b:b + 1]
            cn_ref[b, hd] = scale[b:b + 1] * c_prev + kw_t[:, b:b + 1] * v[b:b + 1]
    la = la_ref[...]
    dec = jnp.exp(la)
    gq = gq_ref[...]
    gk = gk_ref[...]
    for g in range(G_HEADS):
        ks_, ke_ = g * G_DK, (g + 1) * G_DK
        vs_, ve_ = g * G_DV, (g + 1) * G_DV
        q = gq[:, ks_:ke_]
        k = gk[:, ks_:ke_]
        v = gv_ref[:, vs_:ve_].astype(F32)
        a = jnp.sum(q * k, axis=1, keepdims=True)
        qd_t = (q * dec[:, ks_:ke_]).T
        k_t = k.T
        d_t = dec[:, ks_:ke_].T
        for b in range(bb):
            s_prev = s_ref[b, g]
            o = jnp.sum(qd_t[:, b:b + 1] * s_prev, axis=0, keepdims=True) + a[b:b + 1] * v[b:b + 1]
            hg_ref[b:b + 1, vs_:ve_] = o
            sn_ref[b, g] = d_t[:, b:b + 1] * s_prev + k_t[:, b:b + 1] * v[b:b + 1]


def _mix_tok_call(p, c0, n0, m0, s0, *, bb):
    n = c0.shape[0]
    row = lambda w: pl.BlockSpec((bb, w), lambda i: (i, 0))
    st4 = lambda a, b_: pl.BlockSpec((bb, M_HEADS, a, b_), lambda i: (i, 0, 0, 0))
    return pl.pallas_call(
        _mix_tok_kernel,
        grid=(n // bb,),
        in_specs=[row(M_WIDTH), row(M_WIDTH), row(M_WIDTH), row(LANES), row(G_KW), row(G_KW), row(G_VW),
                  row(G_KW), st4(M_DH, M_DH), row(M_WIDTH), row(M_HEADS), st4(G_DK, G_DV)],
        out_specs=[row(M_WIDTH), row(G_VW), st4(M_DH, M_DH), row(M_WIDTH), row(M_HEADS), st4(G_DK, G_DV)],
        out_shape=[jax.ShapeDtypeStruct((n, M_WIDTH), F32), jax.ShapeDtypeStruct((n, G_VW), F32),
                   jax.ShapeDtypeStruct(c0.shape, F32), jax.ShapeDtypeStruct((n, M_WIDTH), F32),
                   jax.ShapeDtypeStruct((n, M_HEADS), F32), jax.ShapeDtypeStruct(s0.shape, F32)],
        compiler_params=pltpu.CompilerParams(dimension_semantics=("arbitrary",), vmem_limit_bytes=VMEM_LIMIT),
        name="mix_tok",
    )(p["mq"], p["mk"], p["mv"], p["gates"], p["gq"], p["gk"], p["gv"], p["la"], c0, n0, m0, s0)


def _head_norm(hv, n_heads, width):
    parts = []
    for hd in range(n_heads):
        seg = hv[:, hd * width:(hd + 1) * width]
        parts.append(seg * lax.rsqrt(jnp.mean(seg * seg, axis=-1, keepdims=True) + EPS))
    return jnp.concatenate(parts, axis=1)


def _post_kernel(x_ref, p_ref, hm_ref, hg_ref, mo_ref, gr_ref, gmh_ref, ggh_ref, wout_ref, gmlp_ref,
                 w1_ref, w2_ref, gple_ref, wple_ref, wpg_ref, gfin_ref, y_ref):
    hm = _head_norm(hm_ref[...], M_HEADS, M_DH) * gmh_ref[...] * _sigmoid(mo_ref[...])
    gr = gr_ref[...]
    hg = _head_norm(hg_ref[...], G_HEADS, G_DV) * ggh_ref[...] * (gr * _sigmoid(gr))
    mixed = jnp.concatenate([hm, hg], axis=1).astype(BF16)
    x1 = x_ref[...] + _dot(mixed, wout_ref[...])
    u = _dot(_rms(x1, gmlp_ref[...]).astype(BF16), w1_ref[...])
    act = jnp.square(jnp.maximum(u, 0.0)).astype(BF16)
    x2 = x1 + _dot(act, w2_ref[...])
    ple = _dot(p_ref[...].astype(BF16), wple_ref[...])
    gate = _sigmoid(_dot(_rms(x2, gple_ref[...]).astype(BF16), wpg_ref[...]))
    x3 = x2 + ple * gate
    y_ref[...] = _rms(x3, gfin_ref[...])


def _post_call(x2d, p2d, hm, hg, mo, gr, wts, *, tb):
    n = x2d.shape[0]
    row = lambda w: pl.BlockSpec((tb, w), lambda i: (i, 0))
    whole = pl.BlockSpec(memory_space=pltpu.VMEM)
    return pl.pallas_call(
        _post_kernel,
        grid=(n // tb,),
        in_specs=[row(D_MODEL), row(D_PLE), row(M_WIDTH), row(G_VW), row(M_WIDTH), row(G_VW)] + [whole] * 10,
        out_specs=row(D_MODEL),
        out_shape=jax.ShapeDtypeStruct((n, D_MODEL), F32),
        compiler_params=pltpu.CompilerParams(dimension_semantics=("arbitrary",), vmem_limit_bytes=VMEM_LIMIT),
        name="post",
    )(x2d, p2d, hm, hg, mo, gr, wts["g_mhead"], wts["g_ghead"], wts["w_out"], wts["g_mlp"], wts["w1"],
      wts["w2"], wts["g_ple"], wts["w_ple"], wts["w_pg"], wts["g_final"])


def _prep_weights(w_in, conv_w, conv_b, b_gate, w_a2, b_a, g_mhead, g_ghead, w_out, g_mix, g_mlp, w1, w2,
                  g_ple, w_ple, w_pg, g_final):
    offs = [0]
    for s in IN_SIZES:
        offs.append(offs[-1] + s)
    gates_lo, gates_hi = offs[3], offs[5]
    ga_lo = offs[9]
    w_big = jnp.concatenate([w_in[:, :gates_lo], w_in[:, gates_hi:ga_lo]], axis=1).astype(BF16)
    n_small = 2 * M_HEADS + G_RANK
    w_small = jnp.concatenate([w_in[:, gates_lo:gates_hi], w_in[:, ga_lo:],
                               jnp.zeros((D_MODEL, LANES - n_small), F32)], axis=1).astype(BF16)
    b_small = jnp.concatenate([b_gate, jnp.zeros((LANES - 2 * M_HEADS,), F32)])[None]
    w_a2p = jnp.concatenate([jnp.zeros((2 * M_HEADS, G_KW), F32), w_a2,
                             jnp.zeros((LANES - n_small, G_KW), F32)], axis=0).astype(BF16)
    return dict(
        w_big=w_big, w_small=w_small, b_small=b_small, w_a2p=w_a2p, b_a=b_a[None],
        conv_w=conv_w, conv_b=conv_b[None], g_mix=g_mix[None], g_mhead=g_mhead[None], g_ghead=g_ghead[None],
        w_out=w_out.astype(BF16), g_mlp=g_mlp[None], w1=w1.astype(BF16), w2=w2.astype(BF16),
        g_ple=g_ple[None], w_ple=w_ple.astype(BF16), w_pg=w_pg.astype(BF16), g_final=g_final[None])


def _mix_constants(ts):
    t = jnp.arange(ts)
    tril = (t[None, :] <= t[:, None]).astype(BF16)
    tri = ((t[:, None] // CHUNK == t[None, :] // CHUNK) & (t[None, :] <= t[:, None])).astype(BF16)
    rr = jnp.arange(G_KW)
    wred = ((rr[None, :, None] // G_DK == rr[None, None, :] // G_DK)
            & (rr[None, None, :] % SUB == jnp.arange(SUB)[:, None, None])).astype(BF16)
    return tril, tri, wred


_PROJ_NAMES = ("mq", "mk", "mv", "mo", "gq", "gk", "gv", "gr", "gates", "la")


def kernel(x_prompt, x_sample, p_prompt, p_sample, state_mlstm_C, state_mlstm_n, state_mlstm_m, state_conv,
           state_gla_S, w_in, conv_w, conv_b, b_gate, w_a2, b_a, g_mhead, g_ghead, w_out, g_mix, g_mlp, w1,
           w2, g_ple, w_ple, w_pg, g_final):
    assert w_in.shape[0] == 1, "single-layer trunk"
    n_seq, seq_len, _ = x_prompt.shape
    n_tok = x_sample.shape[0]
    assert x_sample.shape[1] == 1
    wts = _prep_weights(w_in[0], conv_w[0], conv_b[0], b_gate[0], w_a2[0], b_a[0], g_mhead[0], g_ghead[0],
                        w_out[0], g_mix[0], g_mlp[0], w1[0], w2[0], g_ple[0], w_ple[0], w_pg[0], g_final)

    xp = x_prompt.reshape(n_seq * seq_len, D_MODEL)
    tb1 = 512
    outs = _proj_call(xp, None, wts, tb=tb1, steps_per_seq=seq_len // tb1, n_seq=n_seq)
    pp = dict(zip(_PROJ_NAMES, outs[:-1]))
    tail_p = outs[-1]
    ts = 256
    tril, tri, wred = _mix_constants(ts)
    hm_p, hg_p, caug_p, m_p, s_p = _mix_seq_call(pp, tril, tri, wred, n_seq=n_seq, seq_len=seq_len, ts=ts)
    y_p = _post_call(xp, p_prompt[0].reshape(n_seq * seq_len, D_PLE), hm_p, hg_p, pp["mo"], pp["gr"], wts,
                     tb=256)

    xs = x_sample.reshape(n_tok, D_MODEL)
    buf = state_conv[0]
    outs = _proj_call(xs, (buf[:, 0], buf[:, 1], buf[:, 2]), wts, tb=n_tok, steps_per_seq=1, n_seq=n_tok)
    ps = dict(zip(_PROJ_NAMES, outs[:-1]))
    raw_s = outs[-1]
    hm_s, hg_s, c_s, n_s, m_s, s_s = _mix_tok_call(
        ps, state_mlstm_C[0], state_mlstm_n[0].reshape(n_tok, M_WIDTH), state_mlstm_m[0], state_gla_S[0], bb=8)
    y_s = _post_call(xs, p_sample[0].reshape(n_tok, D_PLE), hm_s, hg_s, ps["mo"], ps["gr"], wts, tb=n_tok)

    return (y_p.reshape(n_seq, seq_len, D_MODEL),
            y_s.reshape(n_tok, 1, D_MODEL),
            caug_p[None, :, :, :, :M_DH],
            caug_p[None, :, :, :, M_DH],
            m_p[None, :, :M_HEADS, 0],
            tail_p[None, :, SUBLANES - (CONV_W - 1):, :],
            s_p[None],
            c_s[None],
            n_s.reshape(1, n_tok, M_HEADS, M_DH),
            m_s[None],
            jnp.stack([buf[:, 1], buf[:, 2], raw_s], axis=1)[None],
            s_s[None])
```

```python
import functools

import jax
import jax.numpy as jnp
from jax import lax
from jax.experimental import pallas as pl
from jax.experimental.pallas import tpu as pltpu

D_MODEL = 1024
M_HEADS = 4
M_DH = 128
M_WIDTH = M_HEADS * M_DH
G_HEADS = 4
G_DK = 64
G_DV = 128
G_KW = G_HEADS * G_DK
G_VW = G_HEADS * G_DV
G_RANK = 16
G_TAU = 16.0
CONV_W = 4
QK_CONV = 2 * M_WIDTH
D_FF = 4 * D_MODEL
D_PLE = 256
CHUNK = 64
SUB = 16
EPS = 1e-6
IN_SIZES = (QK_CONV, M_WIDTH, M_WIDTH, M_HEADS, M_HEADS, G_KW, G_KW, G_VW, G_VW, G_RANK)

LANES = 128
SUBLANES = 8
VMEM_LIMIT = 56 * 1024 * 1024

F32 = jnp.float32
BF16 = jnp.bfloat16
NEG = -1e30


def _rms(x, g):
    return x * lax.rsqrt(jnp.mean(x * x, axis=-1, keepdims=True) + EPS) * g


def _log_sigmoid(x):
    return jnp.minimum(x, 0.0) - jnp.log(1.0 + jnp.exp(-jnp.abs(x)))


def _sigmoid(x):
    return 0.5 * jnp.tanh(0.5 * x) + 0.5


def _div_pow2(idx, d):
    assert d & (d - 1) == 0
    return lax.shift_right_logical(idx, d.bit_length() - 1)


def _dot(a, b):
    return jnp.dot(a, b, preferred_element_type=F32)


def _dot_nt(a, b):
    return lax.dot_general(a, b, (((1,), (1,)), ((), ())), preferred_element_type=F32)


def _cumsum_dot(tri, x):
    hi = x.astype(BF16)
    r1 = x - hi.astype(F32)
    mid = r1.astype(BF16)
    lo = (r1 - mid.astype(F32)).astype(BF16)
    return _dot(tri, hi) + _dot(tri, mid) + _dot(tri, lo)


def _proj_kernel(*refs, steps_per_seq, per_token_conv):
    if per_token_conv:
        (x_ref, b0_ref, b1_ref, b2_ref, gmix_ref, wbig_ref, wsmall_ref, bsmall_ref, wa2_ref, ba_ref,
         convw_ref, convb_ref,
         mq_ref, mk_ref, mv_ref, mo_ref, gq_ref, gk_ref, gv_ref, gr_ref, gates_ref, la_ref, raw_ref) = refs
    else:
        (x_ref, gmix_ref, wbig_ref, wsmall_ref, bsmall_ref, wa2_ref, ba_ref, convw_ref, convb_ref,
         mq_ref, mk_ref, mv_ref, mo_ref, gq_ref, gk_ref, gv_ref, gr_ref, gates_ref, la_ref, tail_ref,
         cbuf_ref) = refs
    tb = x_ref.shape[0]
    if not per_token_conv:
        @pl.when(pl.program_id(0) % steps_per_seq == 0)
        def _():
            cbuf_ref[0:SUBLANES, :] = jnp.zeros((SUBLANES, QK_CONV), F32)

    h = _rms(x_ref[...], gmix_ref[...]).astype(BF16)
    cw = convw_ref[...]
    width = M_WIDTH

    def conv_piece(raw, lo):
        cols = slice(lo, lo + width)
        if per_token_conv:
            raw_ref[:, cols] = raw
            y = (convb_ref[:, cols] + b0_ref[:, cols] * cw[0:1, cols] + b1_ref[:, cols] * cw[1:2, cols]
                 + b2_ref[:, cols] * cw[2:3, cols] + raw * cw[3:4, cols])
        else:
            cbuf_ref[SUBLANES:SUBLANES + tb, cols] = raw
            y = convb_ref[:, cols] + raw * cw[3:4, cols]
            for j in range(CONV_W - 1):
                off = SUBLANES - (CONV_W - 1) + j
                y = y + cbuf_ref[off:off + tb, cols] * cw[j:j + 1, cols]
            last = cbuf_ref[tb:tb + SUBLANES, cols]
            tail_ref[0, :, cols] = last
            cbuf_ref[0:SUBLANES, cols] = last
        return y * _sigmoid(y)

    def put_q(raw):
        mq_ref[...] = conv_piece(raw, 0).astype(mq_ref.dtype)

    def put_k(raw):
        mk_ref[...] = conv_piece(raw, M_WIDTH) * (M_DH ** -0.5)

    def put_mv(raw):
        mv_ref[...] = raw.astype(mv_ref.dtype)

    def put_mo(raw):
        mo_ref[...] = raw

    def put_gqk(raw):
        gq_ref[...] = raw[:, :G_KW] * (G_DK ** -0.5)
        gk_ref[...] = raw[:, G_KW:]

    def put_gv(raw):
        gv_ref[...] = raw.astype(gv_ref.dtype)

    def put_gr(raw):
        gr_ref[...] = raw

    def put_small(small):
        g = small + bsmall_ref[...]
        lane = lax.broadcasted_iota(jnp.int32, g.shape, 1)
        gates_ref[...] = jnp.where(lane < M_HEADS, g, _log_sigmoid(g))
        z = _dot(small.astype(BF16), wa2_ref[...]) + ba_ref[...]
        la_ref[...] = _log_sigmoid(z) * (1.0 / G_TAU)

    epilogues = [put_small, put_q, put_k, put_mv, put_mo, put_gqk, put_gv, put_gr]
    assert wbig_ref.shape[1] == width * (len(epilogues) - 1)
    pending = (_dot(h, wsmall_ref[...]), epilogues[0])
    for n, epi in enumerate(epilogues[1:]):
        cur = _dot(h, wbig_ref[:, n * width:(n + 1) * width])
        pending[1](pending[0])
        pending = (cur, epi)
    pending[1](pending[0])


def _proj_call(x2d, conv_rows, wts, *, tb, steps_per_seq, n_seq):
    n = x2d.shape[0]
    per_token_conv = conv_rows is not None
    grid = (n // tb,)
    row = lambda w: pl.BlockSpec((tb, w), lambda i: (i, 0))
    whole = pl.BlockSpec(memory_space=pltpu.VMEM)
    in_specs = [row(D_MODEL)]
    args = [x2d]
    if per_token_conv:
        in_specs += [row(QK_CONV)] * 3
        args += list(conv_rows)
    in_specs += [whole] * 8
    args += [wts["g_mix"], wts["w_big"], wts["w_small"], wts["b_small"], wts["w_a2p"], wts["b_a"],
             wts["conv_w"], wts["conv_b"]]
    out_shape = [
        jax.ShapeDtypeStruct((n, M_WIDTH), BF16),
        jax.ShapeDtypeStruct((n, M_WIDTH), F32),
        jax.ShapeDtypeStruct((n, M_WIDTH), BF16),
        jax.ShapeDtypeStruct((n, M_WIDTH), F32),
        jax.ShapeDtypeStruct((n, G_KW), F32),
        jax.ShapeDtypeStruct((n, G_KW), F32),
        jax.ShapeDtypeStruct((n, G_VW), BF16),
        jax.ShapeDtypeStruct((n, G_VW), F32),
        jax.ShapeDtypeStruct((n, LANES), F32),
        jax.ShapeDtypeStruct((n, G_KW), F32),
    ]
    out_specs = [row(M_WIDTH), row(M_WIDTH), row(M_WIDTH), row(M_WIDTH), row(G_KW), row(G_KW),
                 row(G_VW), row(G_VW), row(LANES), row(G_KW)]
    scratch = []
    if per_token_conv:
        out_shape.append(jax.ShapeDtypeStruct((n, QK_CONV), F32))
        out_specs.append(row(QK_CONV))
    else:
        out_shape.append(jax.ShapeDtypeStruct((n_seq, SUBLANES, QK_CONV), F32))
        out_specs.append(pl.BlockSpec((1, SUBLANES, QK_CONV), lambda i: (i // steps_per_seq, 0, 0)))
        scratch.append(pltpu.VMEM((tb + SUBLANES, QK_CONV), F32))
    return pl.pallas_call(
        functools.partial(_proj_kernel, steps_per_seq=steps_per_seq, per_token_conv=per_token_conv),
        grid=grid, in_specs=in_specs, out_specs=out_specs, out_shape=out_shape, scratch_shapes=scratch,
        compiler_params=pltpu.CompilerParams(dimension_semantics=("arbitrary",), vmem_limit_bytes=VMEM_LIMIT),
        name="proj_tok" if per_token_conv else "proj_seq",
    )(*args)


def _mix_seq_kernel(mq_ref, mk_ref, mv_ref, gates_ref, gq_ref, gk_ref, gv_ref, la_ref, tril_ref, tri_ref,
                    wred_ref,
                    hm_ref, hg_ref, cout_ref, mout_ref, sout_ref,
                    caug_ref, m_ref, sbd_ref):
    ts = mq_ref.shape[0]
    n_chunks = ts // CHUNK
    r = pl.program_id(1)

    @pl.when(r == 0)
    def _():
        caug_ref[...] = jnp.zeros(caug_ref.shape, F32)
        m_ref[...] = jnp.zeros(m_ref.shape, F32)
        sbd_ref[...] = jnp.zeros(sbd_ref.shape, F32)

    gates = gates_ref[...]
    bcum = _cumsum_dot(tril_ref[...], gates)
    gq = gq_ref[...]
    gk = gk_ref[...]
    bc = _cumsum_dot(tri_ref[...], la_ref[...])
    gates_t = gates.T
    bcum_t = bcum.T
    bc_t = bc.T

    nb = ts // SUB
    q3 = gq.reshape(nb, SUB, G_KW)
    k3 = gk.reshape(nb, SUB, G_KW)
    bc3 = bc.reshape(nb, SUB, G_KW)
    tl = lax.broadcasted_iota(jnp.int32, (nb, SUB, G_KW), 1)

    def exact_pass(j, acc):
        arg = jnp.where(tl >= j, bc3 - bc3[:, j:j + 1, :], NEG)
        e = (q3 * k3[:, j:j + 1, :] * jnp.exp(arg)).reshape(ts, G_KW)
        return acc + _dot(e.astype(BF16), wred_ref[j])

    krow = lax.broadcasted_iota(jnp.int32, (CHUNK, G_KW), 0)
    same_head_kk = (_div_pow2(lax.broadcasted_iota(jnp.int32, (G_KW, G_KW), 0), G_DK)
                    == _div_pow2(lax.broadcasted_iota(jnp.int32, (G_KW, G_KW), 1), G_DK))
    same_head_kv = (_div_pow2(lax.broadcasted_iota(jnp.int32, (G_KW, G_VW), 0), G_DK)
                    == _div_pow2(lax.broadcasted_iota(jnp.int32, (G_KW, G_VW), 1), G_DV))

    def cross_block_scores(c):
        lo = c * CHUNK
        bc_c = bc[lo:lo + CHUNK]
        k_c = gk[lo:lo + CHUNK]
        out = [jnp.zeros((SUB, G_KW), F32)]
        for i in range(1, CHUNK // SUB):
            r0 = lo + i * SUB
            r_i = bc[r0 - 1:r0, :]
            qi = (gq[r0:r0 + SUB] * jnp.exp(bc[r0:r0 + SUB] - r_i)).astype(BF16)
            ki = (k_c * jnp.exp(jnp.where(krow < i * SUB, r_i - bc_c, NEG))).astype(BF16)
            kbd = jnp.where(same_head_kk, jnp.concatenate([ki] * G_HEADS, axis=0), jnp.zeros((), BF16))
            out.append(_dot_nt(qi, kbd))
        return out

    def state_update_term(c):
        lo, hi = c * CHUNK, (c + 1) * CHUNK
        b_end = bc[hi - 1:hi, :]
        k_out_t = (gk[lo:hi] * jnp.exp(b_end - bc[lo:hi])).T.astype(BF16)
        return jnp.where(same_head_kv, _dot(k_out_t, gv_ref[lo:hi, :]), 0.0)

    adiag = jnp.zeros((ts, G_KW), F32)
    per_stage = SUB // 4
    assert n_chunks == 4
    offs = []
    s_terms = []

    causal =(lax.broadcasted_iota(jnp.int32, (ts, ts), 0) >= lax.broadcasted_iota(jnp.int32, (ts, ts), 1))
    ones_col = (lax.broadcasted_iota(jnp.int32, (ts, LANES), 1) == 0).astype(BF16)
    heads = range(M_HEADS)
    hsl = [slice(hd * M_DH, (hd + 1) * M_DH) for hd in heads]
    b_col = [bcum[:, M_HEADS + hd:M_HEADS + hd + 1] for hd in heads]
    b_row = [bcum_t[M_HEADS + hd:M_HEADS + hd + 1, :] for hd in heads]
    i_col = [gates[:, hd:hd + 1] for hd in heads]
    i_row = [gates_t[hd:hd + 1, :] for hd in heads]
    m_prev = [m_ref[hd:hd + 1, 0:1] for hd in heads]
    q = [mq_ref[:, hsl[hd]] for hd in heads]
    k = [mk_ref[:, hsl[hd]] for hd in heads]
    vaug = [jnp.concatenate([mv_ref[:, hsl[hd]], ones_col], axis=1) for hd in heads]
    caug = [caug_ref[hd] for hd in heads]
    s_qk = [_dot_nt(q[hd], k[hd].astype(BF16)) for hd in heads]
    qc = [_dot(q[hd], caug[hd].astype(BF16)) for hd in heads]
    for j in range(0, per_stage):
        adiag = exact_pass(j, adiag)
    offs += cross_block_scores(0)
    s_terms.append(state_update_term(0))
    b_last =[b_col[hd][ts - 1:ts, :] for hd in heads]
    dec = [b_last[hd] - b_col[hd] + i_col[hd] for hd in heads]
    m_new = [jnp.maximum(b_last[hd] + m_prev[hd], jnp.max(dec[hd], axis=0, keepdims=True)) for hd in heads]
    kw_t = [(k[hd] * jnp.exp(dec[hd] - m_new[hd])).T.astype(BF16) for hd in heads]
    upd = [_dot(kw_t[hd], vaug[hd]) for hd in heads]
    for j in range(per_stage, 2 * per_stage):
        adiag = exact_pass(j, adiag)
    offs += cross_block_scores(1)
    s_terms.append(state_update_term(1))
    dmat =[jnp.where(causal, b_col[hd] - b_row[hd] + i_row[hd], -jnp.inf) for hd in heads]
    inter = [b_col[hd] + m_prev[hd] for hd in heads]
    m_tok = [jnp.maximum(inter[hd], jnp.max(dmat[hd], axis=1, keepdims=True)) for hd in heads]
    for j in range(2 * per_stage, 3 * per_stage):
        adiag = exact_pass(j, adiag)
    offs += cross_block_scores(2)
    s_terms.append(state_update_term(2))
    p =[(s_qk[hd] * jnp.exp(dmat[hd] - m_tok[hd])).astype(BF16) for hd in heads]
    pv = [_dot(p[hd], vaug[hd]) for hd in heads]
    for j in range(3 * per_stage, SUB):
        adiag = exact_pass(j, adiag)
    offs += cross_block_scores(3)
    s_terms.append(state_update_term(3))
    for hd in heads:
        tot = jnp.exp(inter[hd] - m_tok[hd]) * qc[hd] + pv[hd]
        den = tot[:, M_DH:M_DH + 1]
        hm_ref[:, hsl[hd]] = tot[:, :M_DH] / jnp.maximum(jnp.abs(den), jnp.exp(-m_tok[hd]))
        caug_ref[hd] = jnp.exp(b_last[hd] + m_prev[hd] - m_new[hd]) * caug[hd] + upd[hd]
        m_ref[hd:hd + 1, :] = jnp.broadcast_to(m_new[hd], (1, LANES))

    sub_of = lambda idx: lax.shift_right_logical(idx & (CHUNK - 1), SUB.bit_length() - 1)
    rowb = sub_of(lax.broadcasted_iota(jnp.int32, (ts, G_KW), 0))
    colb = sub_of(lax.broadcasted_iota(jnp.int32, (ts, G_KW), 1))
    adiag = jnp.where(rowb == colb, adiag, 0.0)

    intra = (jnp.concatenate(offs, axis=0) + adiag).astype(BF16)
    q_in = (gq * jnp.exp(bc)).astype(BF16)
    o_intra = []
    for c in range(n_chunks):
        lo, hi = c * CHUNK, (c + 1) * CHUNK
        vbd = jnp.where(same_head_kv, jnp.concatenate([gv_ref[lo:hi, :]] * G_HEADS, axis=0),
                        jnp.zeros((), BF16))
        o_intra.append(_dot(intra[lo:hi], vbd))
    sbd = sbd_ref[...]
    for c in range(n_chunks):
        lo, hi = c * CHUNK, (c + 1) * CHUNK
        hg_ref[lo:hi, :] = o_intra[c] + _dot(q_in[lo:hi], sbd.astype(BF16))
        dcol = jnp.exp(bc_t[:, hi - 1:hi])
        sbd = dcol * sbd + s_terms[c]
    sbd_ref[...] = sbd

    @pl.when(r == pl.num_programs(1) - 1)
    def _():
        cout_ref[0] = caug_ref[...]
        mout_ref[0] = m_ref[...]
        for g in range(G_HEADS):
            sout_ref[0, g] = sbd_ref[g * G_DK:(g + 1) * G_DK, g * G_DV:(g + 1) * G_DV]


def _mix_seq_call(p, tril, tri, wred, *, n_seq, seq_len, ts):
    steps = seq_len // ts
    row = lambda w: pl.BlockSpec((ts, w), lambda b, r: (b * steps + r, 0))
    whole = pl.BlockSpec(memory_space=pltpu.VMEM)
    n = n_seq * seq_len
    return pl.pallas_call(
        _mix_seq_kernel,
        grid=(n_seq, steps),
        in_specs=[row(M_WIDTH), row(M_WIDTH), row(M_WIDTH), row(LANES), row(G_KW), row(G_KW), row(G_VW),
                  row(G_KW), whole, whole, whole],
        out_specs=[row(M_WIDTH), row(G_VW),
                   pl.BlockSpec((1, M_HEADS, M_DH, 2 * M_DH), lambda b, r: (b, 0, 0, 0)),
                   pl.BlockSpec((1, SUBLANES, LANES), lambda b, r: (b, 0, 0)),
                   pl.BlockSpec((1, G_HEADS, G_DK, G_DV), lambda b, r: (b, 0, 0, 0))],
        out_shape=[jax.ShapeDtypeStruct((n, M_WIDTH), F32), jax.ShapeDtypeStruct((n, G_VW), F32),
                   jax.ShapeDtypeStruct((n_seq, M_HEADS, M_DH, 2 * M_DH), F32),
                   jax.ShapeDtypeStruct((n_seq, SUBLANES, LANES), F32),
                   jax.ShapeDtypeStruct((n_seq, G_HEADS, G_DK, G_DV), F32)],
        scratch_shapes=[pltpu.VMEM((M_HEADS, M_DH, 2 * M_DH), F32), pltpu.VMEM((SUBLANES, LANES), F32),
                        pltpu.VMEM((G_KW, G_VW), F32)],
        compiler_params=pltpu.CompilerParams(dimension_semantics=("arbitrary", "arbitrary"),
                                             vmem_limit_bytes=VMEM_LIMIT),
        name="mix_seq",
    )(p["mq"], p["mk"], p["mv"], p["gates"], p["gq"], p["gk"], p["gv"], p["la"], tril, tri, wred)


def _mix_tok_kernel(mq_ref, mk_ref, mv_ref, gates_ref, gq_ref, gk_ref, gv_ref, la_ref,
                    c_ref, n_ref, m_ref, s_ref,
                    hm_ref, hg_ref, cn_ref, nn_ref, mn_ref, sn_ref):
    bb = mq_ref.shape[0]
    gates = gates_ref[...]
    m_all = m_ref[...]
    for hd in range(M_HEADS):
        cs_, ce_ = hd * M_DH, (hd + 1) * M_DH
        q = mq_ref[:, cs_:ce_].astype(F32)
        k = mk_ref[:, cs_:ce_]
        v = mv_ref[:, cs_:ce_].astype(F32)
        n_prev = n_ref[:, cs_:ce_]
        ig = gates[:, hd:hd + 1]
        lf = gates[:, M_HEADS + hd:M_HEADS + hd + 1]
        m_prev = m_all[:, hd:hd + 1]
        m_new = jnp.maximum(lf + m_prev, ig)
        scale = jnp.exp(lf + m_prev - m_new)
        wk = jnp.exp(ig - m_new)
        s_qk = jnp.sum(q * k, axis=1, keepdims=True) * wk
        den = scale * jnp.sum(q * n_prev, axis=1, keepdims=True) + s_qk
        inv = 1.0 / jnp.maximum(jnp.abs(den), jnp.exp(-m_new))
        nn_ref[:, cs_:ce_] = scale * n_prev + wk * k
        mn_ref[:, hd:hd + 1] = m_new
        q_t = q.T
        kw_t = (k * wk).T
        for b in range(bb):
            c_prev = c_ref[b, hd]
            q_col = q_t[:, b:b + 1]
            qc = jnp.sum(q_col * c_prev, axis=0, keepdims=True)
            num = scale[b:b + 1] * qc + s_qk[b:b + 1] * v[b:b + 1]
            hm_ref[b:b + 1, cs_:ce_] = num * inv[b:b + 1]
            cn_ref[b, hd] = scale[b:b + 1] * c_prev + kw_t[:, b:b + 1] * v[b:b + 1]
    la = la_ref[...]
    dec = jnp.exp(la)
    gq = gq_ref[...]
    gk = gk_ref[...]
    for g in range(G_HEADS):
        ks_, ke_ = g * G_DK, (g + 1) * G_DK
        vs_, ve_ = g * G_DV, (g + 1) * G_DV
        q = gq[:, ks_:ke_]
        k = gk[:, ks_:ke_]
        v = gv_ref[:, vs_:ve_].astype(F32)
        a = jnp.sum(q * k, axis=1, keepdims=True)
        qd_t = (q * dec[:, ks_:ke_]).T
        k_t = k.T
        d_t = dec[:, ks_:ke_].T
        for b in range(bb):
            s_prev = s_ref[b, g]
            o = jnp.sum(qd_t[:, b:b + 1] * s_prev, axis=0, keepdims=True) + a[b:b + 1] * v[b:b + 1]
            hg_ref[b:b + 1, vs_:ve_] = o
            sn_ref[b, g] = d_t[:, b:b + 1] * s_prev + k_t[:, b:b + 1] * v[b:b + 1]


def _mix_tok_call(p, c0, n0, m0, s0, *, bb):
    n = c0.shape[0]
    row = lambda w: pl.BlockSpec((bb, w), lambda i: (i, 0))
    st4 = lambda a, b_: pl.BlockSpec((bb, M_HEADS, a, b_), lambda i: (i, 0, 0, 0))
    return pl.pallas_call(
        _mix_tok_kernel,
        grid=(n // bb,),
        in_specs=[row(M_WIDTH), row(M_WIDTH), row(M_WIDTH), row(LANES), row(G_KW), row(G_KW), row(G_VW),
                  row(G_KW), st4(M_DH, M_DH), row(M_WIDTH), row(M_HEADS), st4(G_DK, G_DV)],
        out_specs=[row(M_WIDTH), row(G_VW), st4(M_DH, M_DH), row(M_WIDTH), row(M_HEADS), st4(G_DK, G_DV)],
        out_shape=[jax.ShapeDtypeStruct((n, M_WIDTH), F32), jax.ShapeDtypeStruct((n, G_VW), F32),
                   jax.ShapeDtypeStruct(c0.shape, F32), jax.ShapeDtypeStruct((n, M_WIDTH), F32),
                   jax.ShapeDtypeStruct((n, M_HEADS), F32), jax.ShapeDtypeStruct(s0.shape, F32)],
        compiler_params=pltpu.CompilerParams(dimension_semantics=("arbitrary",), vmem_limit_bytes=VMEM_LIMIT),
        name="mix_tok",
    )(p["mq"], p["mk"], p["mv"], p["gates"], p["gq"], p["gk"], p["gv"], p["la"], c0, n0, m0, s0)


def _head_norm(hv, n_heads, width):
    parts = []
    for hd in range(n_heads):
        seg = hv[:, hd * width:(hd + 1) * width]
        parts.append(seg * lax.rsqrt(jnp.mean(seg * seg, axis=-1, keepdims=True) + EPS))
    return jnp.concatenate(parts, axis=1)


def _post_kernel(x_ref, p_ref, hm_ref, hg_ref, mo_ref, gr_ref, gmh_ref, ggh_ref, wout_ref, gmlp_ref,
                 w1_ref, w2_ref, gple_ref, wple_ref, wpg_ref, gfin_ref, y_ref):
    hm = _head_norm(hm_ref[...], M_HEADS, M_DH) * gmh_ref[...] * _sigmoid(mo_ref[...])
    gr = gr_ref[...]
    hg = _head_norm(hg_ref[...], G_HEADS, G_DV) * ggh_ref[...] * (gr * _sigmoid(gr))
    mixed = jnp.concatenate([hm, hg], axis=1).astype(BF16)
    x1 = x_ref[...] + _dot(mixed, wout_ref[...])
    u = _dot(_rms(x1, gmlp_ref[...]).astype(BF16), w1_ref[...])
    act = jnp.square(jnp.maximum(u, 0.0)).astype(BF16)
    x2 = x1 + _dot(act, w2_ref[...])
    ple = _dot(p_ref[...].astype(BF16), wple_ref[...])
    gate = _sigmoid(_dot(_rms(x2, gple_ref[...]).astype(BF16), wpg_ref[...]))
    x3 = x2 + ple * gate
    y_ref[...] = _rms(x3, gfin_ref[...])


def _post_call(x2d, p2d, hm, hg, mo, gr, wts, *, tb):
    n = x2d.shape[0]
    row = lambda w: pl.BlockSpec((tb, w), lambda i: (i, 0))
    whole = pl.BlockSpec(memory_space=pltpu.VMEM)
    return pl.pallas_call(
        _post_kernel,
        grid=(n // tb,),
        in_specs=[row(D_MODEL), row(D_PLE), row(M_WIDTH), row(G_VW), row(M_WIDTH), row(G_VW)] + [whole] * 10,
        out_specs=row(D_MODEL),
        out_shape=jax.ShapeDtypeStruct((n, D_MODEL), F32),
        compiler_params=pltpu.CompilerParams(dimension_semantics=("arbitrary",), vmem_limit_bytes=VMEM_LIMIT),
        name="post",
    )(x2d, p2d, hm, hg, mo, gr, wts["g_mhead"], wts["g_ghead"], wts["w_out"], wts["g_mlp"], wts["w1"],
      wts["w2"], wts["g_ple"], wts["w_ple"], wts["w_pg"], wts["g_final"])


def _prep_weights(w_in, conv_w, conv_b, b_gate, w_a2, b_a, g_mhead, g_ghead, w_out, g_mix, g_mlp, w1, w2,
                  g_ple, w_ple, w_pg, g_final):
    offs = [0]
    for s in IN_SIZES:
        offs.append(offs[-1] + s)
    gates_lo, gates_hi = offs[3], offs[5]
    ga_lo = offs[9]
    w_big = jnp.concatenate([w_in[:, :gates_lo], w_in[:, gates_hi:ga_lo]], axis=1).astype(BF16)
    n_small = 2 * M_HEADS + G_RANK
    w_small = jnp.concatenate([w_in[:, gates_lo:gates_hi], w_in[:, ga_lo:],
                               jnp.zeros((D_MODEL, LANES - n_small), F32)], axis=1).astype(BF16)
    b_small = jnp.concatenate([b_gate, jnp.zeros((LANES - 2 * M_HEADS,), F32)])[None]
    w_a2p = jnp.concatenate([jnp.zeros((2 * M_HEADS, G_KW), F32), w_a2,
                             jnp.zeros((LANES - n_small, G_KW), F32)], axis=0).astype(BF16)
    return dict(
        w_big=w_big, w_small=w_small, b_small=b_small, w_a2p=w_a2p, b_a=b_a[None],
        conv_w=conv_w, conv_b=conv_b[None], g_mix=g_mix[None], g_mhead=g_mhead[None], g_ghead=g_ghead[None],
        w_out=w_out.astype(BF16), g_mlp=g_mlp[None], w1=w1.astype(BF16), w2=w2.astype(BF16),
        g_ple=g_ple[None], w_ple=w_ple.astype(BF16), w_pg=w_pg.astype(BF16), g_final=g_final[None])


def _mix_constants(ts):
    t = jnp.arange(ts)
    tril = (t[None, :] <= t[:, None]).astype(BF16)
    tri = ((t[:, None] // CHUNK == t[None, :] // CHUNK) & (t[None, :] <= t[:, None])).astype(BF16)
    rr = jnp.arange(G_KW)
    wred = ((rr[None, :, None] // G_DK == rr[None, None, :] // G_DK)
            & (rr[None, None, :] % SUB == jnp.arange(SUB)[:, None, None])).astype(BF16)
    return tril, tri, wred


_PROJ_NAMES = ("mq", "mk", "mv", "mo", "gq", "gk", "gv", "gr", "gates", "la")


def kernel(x_prompt, x_sample, p_prompt, p_sample, state_mlstm_C, state_mlstm_n, state_mlstm_m, state_conv,
           state_gla_S, w_in, conv_w, conv_b, b_gate, w_a2, b_a, g_mhead, g_ghead, w_out, g_mix, g_mlp, w1,
           w2, g_ple, w_ple, w_pg, g_final):
    assert w_in.shape[0] == 1, "single-layer trunk"
    n_seq, seq_len, _ = x_prompt.shape
    n_tok = x_sample.shape[0]
    assert x_sample.shape[1] == 1
    wts = _prep_weights(w_in[0], conv_w[0], conv_b[0], b_gate[0], w_a2[0], b_a[0], g_mhead[0], g_ghead[0],
                        w_out[0], g_mix[0], g_mlp[0], w1[0], w2[0], g_ple[0], w_ple[0], w_pg[0], g_final)

    xp = x_prompt.reshape(n_seq * seq_len, D_MODEL)
    tb1 = 512
    outs = _proj_call(xp, None, wts, tb=tb1, steps_per_seq=seq_len // tb1, n_seq=n_seq)
    pp = dict(zip(_PROJ_NAMES, outs[:-1]))
    tail_p = outs[-1]
    ts = 256
    tril, tri, wred = _mix_constants(ts)
    hm_p, hg_p, caug_p, m_p, s_p = _mix_seq_call(pp, tril, tri, wred, n_seq=n_seq, seq_len=seq_len, ts=ts)
    y_p = _post_call(xp, p_prompt[0].reshape(n_seq * seq_len, D_PLE), hm_p, hg_p, pp["mo"], pp["gr"], wts,
                     tb=256)

    xs = x_sample.reshape(n_tok, D_MODEL)
    buf = state_conv[0]
    outs = _proj_call(xs, (buf[:, 0], buf[:, 1], buf[:, 2]), wts, tb=n_tok, steps_per_seq=1, n_seq=n_tok)
    ps = dict(zip(_PROJ_NAMES, outs[:-1]))
    raw_s = outs[-1]
    hm_s, hg_s, c_s, n_s, m_s, s_s = _mix_tok_call(
        ps, state_mlstm_C[0], state_mlstm_n[0].reshape(n_tok, M_WIDTH), state_mlstm_m[0], state_gla_S[0], bb=8)
    y_s = _post_call(xs, p_sample[0].reshape(n_tok, D_PLE), hm_s, hg_s, ps["mo"], ps["gr"], wts, tb=n_tok)

    return (y_p.reshape(n_seq, seq_len, D_MODEL),
            y_s.reshape(n_tok, 1, D_MODEL),
            caug_p[None, :, :, :, :M_DH],
            caug_p[None, :, :, :, M_DH],
            m_p[None, :, :M_HEADS, 0],
            tail_p[None, :, SUBLANES - (CONV_W - 1):, :],
            s_p[None],
            c_s[None],
            n_s.reshape(1, n_tok, M_HEADS, M_DH),
            m_s[None],
            jnp.stack([buf[:, 1], buf[:, 2], raw_s], axis=1)[None],
            s_s[None])
```

```python
import functools

import jax
import jax.numpy as jnp
from jax import lax
from jax.experimental import pallas as pl
from jax.experimental.pallas import tpu as pltpu

D_MODEL = 1024
M_HEADS = 4
M_DH = 128
M_WIDTH = M_HEADS * M_DH
G_HEADS = 4
G_DK = 64
G_DV = 128
G_KW = G_HEADS * G_DK
G_VW = G_HEADS * G_DV
G_RANK = 16
G_TAU = 16.0
CONV_W = 4
QK_CONV = 2 * M_WIDTH
D_FF = 4 * D_MODEL
D_PLE = 256
CHUNK = 64
SUB = 16
EPS = 1e-6
IN_SIZES = (QK_CONV, M_WIDTH, M_WIDTH, M_HEADS, M_HEADS, G_KW, G_KW, G_VW, G_VW, G_RANK)

LANES = 128
SUBLANES = 8
VMEM_LIMIT = 60 * 1024 * 1024
SEQ_BLOCK = 256
TOK_BATCH = 8

F32 = jnp.float32
BF16 = jnp.bfloat16
NEG = -1e30


def _rms(x, g):
    return x * lax.rsqrt(jnp.mean(x * x, axis=-1, keepdims=True) + EPS) * g


def _log_sigmoid(x):
    return jnp.minimum(x, 0.0) - jnp.log(1.0 + jnp.exp(-jnp.abs(x)))


def _sigmoid(x):
    return 0.5 * jnp.tanh(0.5 * x) + 0.5


def _div_pow2(idx, d):
    assert d & (d - 1) == 0
    return lax.shift_right_logical(idx, d.bit_length() - 1)


def _dot(a, b):
    return jnp.dot(a, b, preferred_element_type=F32)


def _dot_nt(a, b):
    return lax.dot_general(a, b, (((1,), (1,)), ((), ())), preferred_element_type=F32)


def _cumsum_dot(tri, x):
    hi = x.astype(BF16)
    r1 = x - hi.astype(F32)
    mid = r1.astype(BF16)
    lo = (r1 - mid.astype(F32)).astype(BF16)
    return _dot(tri, hi) + _dot(tri, mid) + _dot(tri, lo)


def _interleave(*streams):
    live = [[gen, per_turn] for gen, per_turn in streams]
    while live:
        for item in list(live):
            gen, per_turn = item
            for _ in range(per_turn):
                try:
                    next(gen)
                except StopIteration:
                    live.remove(item)
                    break


def _run(gen):
    for _ in gen:
        pass


def _proj_stages(h, w, sinks, conv_piece):
    width = M_WIDTH

    def put_q(raw):
        sinks["mq"][...] = conv_piece(raw, 0).astype(sinks["mq"].dtype)

    def put_k(raw):
        sinks["mk"][...] = conv_piece(raw, M_WIDTH) * (M_DH ** -0.5)

    def put_mv(raw):
        sinks["mv"][...] = raw.astype(sinks["mv"].dtype)

    def put_mo(raw):
        sinks["mo"][...] = raw

    def put_gqk(raw):
        sinks["gq"][...] = raw[:, :G_KW] * (G_DK ** -0.5)
        sinks["gk"][...] = raw[:, G_KW:]

    def put_gv(raw):
        sinks["gv"][...] = raw.astype(sinks["gv"].dtype)

    def put_gr(raw):
        sinks["gr"][...] = raw

    def put_small(small):
        g = small + w["b_small"][...]
        lane = lax.broadcasted_iota(jnp.int32, g.shape, 1)
        sinks["gates"][...] = jnp.where(lane < M_HEADS, g, _log_sigmoid(g))
        z = _dot(small.astype(BF16), w["w_a2p"][...]) + w["b_a"][...]
        sinks["la"][...] = _log_sigmoid(z) * (1.0 / G_TAU)

    epilogues = [put_small, put_q, put_k, put_mv, put_mo, put_gqk, put_gv, put_gr]
    assert w["w_big"].shape[1] == width * (len(epilogues) - 1)
    pending = (_dot(h, w["w_small"][...]), epilogues[0])
    yield
    for n, epi in enumerate(epilogues[1:]):
        cur = _dot(h, w["w_big"][:, n * width:(n + 1) * width])
        pending[1](pending[0])
        pending = (cur, epi)
        yield
    pending[1](pending[0])
    yield


def _seq_conv(cbuf_ref, tail_ref, convw_ref, convb_ref, tb):
    cw = convw_ref[...]

    def conv_piece(raw, lo):
        cols = slice(lo, lo + raw.shape[1])
        cbuf_ref[SUBLANES:SUBLANES + tb, cols] = raw
        y = convb_ref[:, cols] + raw * cw[3:4, cols]
        for j in range(CONV_W - 1):
            off = SUBLANES - (CONV_W - 1) + j
            y = y + cbuf_ref[off:off + tb, cols] * cw[j:j + 1, cols]
        last = cbuf_ref[tb:tb + SUBLANES, cols]
        tail_ref[0, :, cols] = last
        cbuf_ref[0:SUBLANES, cols] = last
        return y * _sigmoid(y)

    return conv_piece


def _tok_conv(b0_ref, b1_ref, b2_ref, raw_ref, convw_ref, convb_ref):
    cw = convw_ref[...]

    def conv_piece(raw, lo):
        cols = slice(lo, lo + raw.shape[1])
        raw_ref[:, cols] = raw
        y = (convb_ref[:, cols] + b0_ref[:, cols] * cw[0:1, cols] + b1_ref[:, cols] * cw[1:2, cols]
             + b2_ref[:, cols] * cw[2:3, cols] + raw * cw[3:4, cols])
        return y * _sigmoid(y)

    return conv_piece


_PROJ_NAMES = ("mq", "mk", "mv", "mo", "gq", "gk", "gv", "gr", "gates", "la")
_PROJ_WIDTH = dict(mq=M_WIDTH, mk=M_WIDTH, mv=M_WIDTH, mo=M_WIDTH, gq=G_KW, gk=G_KW, gv=G_VW, gr=G_VW,
                   gates=LANES, la=G_KW)
_PROJ_DTYPE = dict(mq=BF16, mk=F32, mv=BF16, mo=F32, gq=F32, gk=F32, gv=BF16, gr=F32, gates=F32, la=F32)
_PROJ_WEIGHTS = ("w_big", "w_small", "b_small", "w_a2p", "b_a")


def _proj_tok_kernel(x_ref, b0_ref, b1_ref, b2_ref, gmix_ref, wbig_ref, wsmall_ref, bsmall_ref, wa2_ref,
                     ba_ref, convw_ref, convb_ref, *out_refs):
    sinks = dict(zip(_PROJ_NAMES, out_refs[:-1]))
    raw_ref = out_refs[-1]
    w = dict(zip(_PROJ_WEIGHTS, (wbig_ref, wsmall_ref, bsmall_ref, wa2_ref, ba_ref)))
    h = _rms(x_ref[...], gmix_ref[...]).astype(BF16)
    _run(_proj_stages(h, w, sinks, _tok_conv(b0_ref, b1_ref, b2_ref, raw_ref, convw_ref, convb_ref)))


def _proj_tok_call(x2d, conv_rows, wts):
    n = x2d.shape[0]
    row = lambda width: pl.BlockSpec((n, width), lambda i: (0, 0))
    whole = pl.BlockSpec(memory_space=pltpu.VMEM)
    names = list(_PROJ_NAMES)
    return pl.pallas_call(
        _proj_tok_kernel,
        grid=(1,),
        in_specs=[row(D_MODEL)] + [row(QK_CONV)] * 3 + [whole] * 8,
        out_specs=[row(_PROJ_WIDTH[k]) for k in names] + [row(QK_CONV)],
        out_shape=[jax.ShapeDtypeStruct((n, _PROJ_WIDTH[k]), _PROJ_DTYPE[k]) for k in names]
        + [jax.ShapeDtypeStruct((n, QK_CONV), F32)],
        compiler_params=pltpu.CompilerParams(dimension_semantics=("arbitrary",), vmem_limit_bytes=VMEM_LIMIT),
        name="proj_tok",
    )(x2d, *conv_rows, wts["g_mix"], wts["w_big"], wts["w_small"], wts["b_small"], wts["w_a2p"], wts["b_a"],
      wts["conv_w"], wts["conv_b"])


def _mix_stages(src, tril_ref, tri_ref, wred_ref, caug_ref, m_ref, sbd_ref, hm_ref, hg_ref):
    ts = src["mq"].shape[0]
    n_chunks = ts // CHUNK
    gates = src["gates"][...]
    bcum = _cumsum_dot(tril_ref[...], gates)
    gq = src["gq"][...]
    gk = src["gk"][...]
    gv_ref = src["gv"]
    bc = _cumsum_dot(tri_ref[...], src["la"][...])
    gates_t = gates.T
    bcum_t = bcum.T
    bc_t = bc.T
    yield

    nb = ts // SUB
    q3 = gq.reshape(nb, SUB, G_KW)
    k3 = gk.reshape(nb, SUB, G_KW)
    bc3 = bc.reshape(nb, SUB, G_KW)
    tl = lax.broadcasted_iota(jnp.int32, (nb, SUB, G_KW), 1)

    def exact_pass(j, acc):
        arg = jnp.where(tl >= j, bc3 - bc3[:, j:j + 1, :], NEG)
        e = (q3 * k3[:, j:j + 1, :] * jnp.exp(arg)).reshape(ts, G_KW)
        return acc + _dot(e.astype(BF16), wred_ref[j])

    krow = lax.broadcasted_iota(jnp.int32, (CHUNK, G_KW), 0)
    same_head_kk = (_div_pow2(lax.broadcasted_iota(jnp.int32, (G_KW, G_KW), 0), G_DK)
                    == _div_pow2(lax.broadcasted_iota(jnp.int32, (G_KW, G_KW), 1), G_DK))
    same_head_kv = (_div_pow2(lax.broadcasted_iota(jnp.int32, (G_KW, G_VW), 0), G_DK)
                    == _div_pow2(lax.broadcasted_iota(jnp.int32, (G_KW, G_VW), 1), G_DV))

    def cross_block_scores(c):
        lo = c * CHUNK
        bc_c = bc[lo:lo + CHUNK]
        k_c = gk[lo:lo + CHUNK]
        out = [jnp.zeros((SUB, G_KW), F32)]
        for i in range(1, CHUNK // SUB):
            r0 = lo + i * SUB
            r_i = bc[r0 - 1:r0, :]
            qi = (gq[r0:r0 + SUB] * jnp.exp(bc[r0:r0 + SUB] - r_i)).astype(BF16)
            ki = (k_c * jnp.exp(jnp.where(krow < i * SUB, r_i - bc_c, NEG))).astype(BF16)
            kbd = jnp.where(same_head_kk, jnp.concatenate([ki] * G_HEADS, axis=0), jnp.zeros((), BF16))
            out.append(_dot_nt(qi, kbd))
        return out

    def state_update_term(c):
        lo, hi = c * CHUNK, (c + 1) * CHUNK
        b_end = bc[hi - 1:hi, :]
        k_out_t = (gk[lo:hi] * jnp.exp(b_end - bc[lo:hi])).T.astype(BF16)
        return jnp.where(same_head_kv, _dot(k_out_t, gv_ref[lo:hi, :]), 0.0)

    adiag = jnp.zeros((ts, G_KW), F32)
    per_stage = SUB // 4
    assert n_chunks == 4
    offs = []
    s_terms = []

    causal = (lax.broadcasted_iota(jnp.int32, (ts, ts), 0) >= lax.broadcasted_iota(jnp.int32, (ts, ts), 1))
    ones_col = (lax.broadcasted_iota(jnp.int32, (ts, LANES), 1) == 0).astype(BF16)
    heads = range(M_HEADS)
    hsl = [slice(hd * M_DH, (hd + 1) * M_DH) for hd in heads]
    b_col = [bcum[:, M_HEADS + hd:M_HEADS + hd + 1] for hd in heads]
    b_row = [bcum_t[M_HEADS + hd:M_HEADS + hd + 1, :] for hd in heads]
    i_col = [gates[:, hd:hd + 1] for hd in heads]
    i_row = [gates_t[hd:hd + 1, :] for hd in heads]
    m_prev = [m_ref[hd:hd + 1, 0:1] for hd in heads]
    q = [src["mq"][:, hsl[hd]] for hd in heads]
    k = [src["mk"][:, hsl[hd]] for hd in heads]
    vaug = [jnp.concatenate([src["mv"][:, hsl[hd]], ones_col], axis=1) for hd in heads]
    caug = [caug_ref[hd] for hd in heads]
    s_qk = [_dot_nt(q[hd], k[hd].astype(BF16)) for hd in heads]
    qc = [_dot(q[hd], caug[hd].astype(BF16)) for hd in heads]
    for j in range(0, per_stage):
        adiag = exact_pass(j, adiag)
    offs += cross_block_scores(0)
    s_terms.append(state_update_term(0))
    yield
    b_last = [b_col[hd][ts - 1:ts, :] for hd in heads]
    dec = [b_last[hd] - b_col[hd] + i_col[hd] for hd in heads]
    m_new = [jnp.maximum(b_last[hd] + m_prev[hd], jnp.max(dec[hd], axis=0, keepdims=True)) for hd in heads]
    kw_t = [(k[hd] * jnp.exp(dec[hd] - m_new[hd])).T.astype(BF16) for hd in heads]
    upd = [_dot(kw_t[hd], vaug[hd]) for hd in heads]
    for j in range(per_stage, 2 * per_stage):
        adiag = exact_pass(j, adiag)
    offs += cross_block_scores(1)
    s_terms.append(state_update_term(1))
    yield
    dmat = [jnp.where(causal, b_col[hd] - b_row[hd] + i_row[hd], -jnp.inf) for hd in heads]
    inter = [b_col[hd] + m_prev[hd] for hd in heads]
    m_tok = [jnp.maximum(inter[hd], jnp.max(dmat[hd], axis=1, keepdims=True)) for hd in heads]
    for j in range(2 * per_stage, 3 * per_stage):
        adiag = exact_pass(j, adiag)
    offs += cross_block_scores(2)
    s_terms.append(state_update_term(2))
    yield
    p = [(s_qk[hd] * jnp.exp(dmat[hd] - m_tok[hd])).astype(BF16) for hd in heads]
    pv = [_dot(p[hd], vaug[hd]) for hd in heads]
    for j in range(3 * per_stage, SUB):
        adiag = exact_pass(j, adiag)
    offs += cross_block_scores(3)
    s_terms.append(state_update_term(3))
    yield
    for hd in heads:
        tot = jnp.exp(inter[hd] - m_tok[hd]) * qc[hd] + pv[hd]
        den = tot[:, M_DH:M_DH + 1]
        hm_ref[:, hsl[hd]] = tot[:, :M_DH] / jnp.maximum(jnp.abs(den), jnp.exp(-m_tok[hd]))
        caug_ref[hd] = jnp.exp(b_last[hd] + m_prev[hd] - m_new[hd]) * caug[hd] + upd[hd]
        m_ref[hd:hd + 1, :] = jnp.broadcast_to(m_new[hd], (1, LANES))
    yield

    sub_of = lambda idx: lax.shift_right_logical(idx & (CHUNK - 1), SUB.bit_length() - 1)
    rowb = sub_of(lax.broadcasted_iota(jnp.int32, (ts, G_KW), 0))
    colb = sub_of(lax.broadcasted_iota(jnp.int32, (ts, G_KW), 1))
    adiag = jnp.where(rowb == colb, adiag, 0.0)
    intra = (jnp.concatenate(offs, axis=0) + adiag).astype(BF16)
    q_in = (gq * jnp.exp(bc)).astype(BF16)
    o_intra = []
    for c in range(n_chunks):
        lo, hi = c * CHUNK, (c + 1) * CHUNK
        vbd = jnp.where(same_head_kv, jnp.concatenate([gv_ref[lo:hi, :]] * G_HEADS, axis=0),
                        jnp.zeros((), BF16))
        o_intra.append(_dot(intra[lo:hi], vbd))
    sbd = sbd_ref[...]
    for c in range(n_chunks):
        lo, hi = c * CHUNK, (c + 1) * CHUNK
        hg_ref[lo:hi, :] = o_intra[c] + _dot(q_in[lo:hi], sbd.astype(BF16))
        dcol = jnp.exp(bc_t[:, hi - 1:hi])
        sbd = dcol * sbd + s_terms[c]
    sbd_ref[...] = sbd
    yield


def _head_norm(hv, n_heads, width):
    parts = []
    for hd in range(n_heads):
        seg = hv[:, hd * width:(hd + 1) * width]
        parts.append(seg * lax.rsqrt(jnp.mean(seg * seg, axis=-1, keepdims=True) + EPS))
    return jnp.concatenate(parts, axis=1)


def _gate_heads(hm, hg, mo, gr, gmh, ggh):
    hm = _head_norm(hm, M_HEADS, M_DH) * gmh * _sigmoid(mo)
    hg = _head_norm(hg, G_HEADS, G_DV) * ggh * (gr * _sigmoid(gr))
    return jnp.concatenate([hm, hg], axis=1).astype(BF16)


_POST_WEIGHTS = ("w_out", "g_mlp", "w1", "w2", "g_ple", "w_ple", "w_pg", "g_final")


def _post_stages(x, p, mixed, w, y_ref):
    x1 = x + _dot(mixed, w["w_out"][...])
    n1 = _rms(x1, w["g_mlp"][...]).astype(BF16)
    yield
    n_ff = 4
    ff = D_FF // n_ff
    acts = []
    for j in range(n_ff):
        u = _dot(n1, w["w1"][:, j * ff:(j + 1) * ff])
        acts.append(jnp.square(jnp.maximum(u, 0.0)).astype(BF16))
        yield
    act = jnp.concatenate(acts, axis=1)
    half = D_MODEL // 2
    mlp = []
    for j in range(2):
        mlp.append(_dot(act, w["w2"][:, j * half:(j + 1) * half]))
        yield
    x2 = x1 + jnp.concatenate(mlp, axis=1)
    ple = _dot(p.astype(BF16), w["w_ple"][...])
    n2 = _rms(x2, w["g_ple"][...]).astype(BF16)
    yield
    gate = _sigmoid(_dot(n2, w["w_pg"][...]))
    x3 = x2 + ple * gate
    y_ref[...] = _rms(x3, w["g_final"][...])
    yield


def _post_tok_kernel(x_ref, p_ref, hm_ref, hg_ref, mo_ref, gr_ref, gmh_ref, ggh_ref, wout_ref, gmlp_ref,
                     w1_ref, w2_ref, gple_ref, wple_ref, wpg_ref, gfin_ref, y_ref):
    mixed = _gate_heads(hm_ref[...], hg_ref[...], mo_ref[...], gr_ref[...], gmh_ref[...], ggh_ref[...])
    w = dict(zip(_POST_WEIGHTS, (wout_ref, gmlp_ref, w1_ref, w2_ref, gple_ref, wple_ref, wpg_ref, gfin_ref)))
    _run(_post_stages(x_ref[...], p_ref[...], mixed, w, y_ref))


def _post_tok_call(x2d, p2d, hm, hg, mo, gr, wts):
    n = x2d.shape[0]
    row = lambda width: pl.BlockSpec((n, width), lambda i: (0, 0))
    whole = pl.BlockSpec(memory_space=pltpu.VMEM)
    return pl.pallas_call(
        _post_tok_kernel,
        grid=(1,),
        in_specs=[row(D_MODEL), row(D_PLE), row(M_WIDTH), row(G_VW), row(M_WIDTH), row(G_VW)] + [whole] * 10,
        out_specs=row(D_MODEL),
        out_shape=jax.ShapeDtypeStruct((n, D_MODEL), F32),
        compiler_params=pltpu.CompilerParams(dimension_semantics=("arbitrary",), vmem_limit_bytes=VMEM_LIMIT),
        name="post_tok",
    )(x2d, p2d, hm, hg, mo, gr, wts["g_mhead"], wts["g_ghead"], *[wts[k] for k in _POST_WEIGHTS])


def _seq_kernel(xa_ref, xb_ref, pb_ref, gmix_ref, wbig_ref, wsmall_ref, bsmall_ref, wa2_ref, ba_ref,
                convw_ref, convb_ref, tril_ref, tri_ref, wred_ref, gmh_ref, ggh_ref,
                wout_ref, gmlp_ref, w1_ref, w2_ref, gple_ref, wple_ref, wpg_ref, gfin_ref,
                y_ref, tail_ref, cout_ref, mout_ref, sout_ref,
                cbuf_ref, caug_ref, m_ref, sbd_ref, mixed_ref,
                mq_s, mk_s, mv_s, mo_s, gq_s, gk_s, gv_s, gr_s, gates_s, la_s, hm_s, hg_s,
                *, steps_per_seq, n_blocks):
    tb = xa_ref.shape[0]
    s = pl.program_id(0)
    r = lax.rem(jnp.minimum(s, n_blocks - 1), steps_per_seq)

    @pl.when(s == 0)
    def _():
        mixed_ref[...] = jnp.zeros(mixed_ref.shape, BF16)

    @pl.when(r == 0)
    def _():
        cbuf_ref[0:SUBLANES, :] = jnp.zeros((SUBLANES, QK_CONV), F32)
        caug_ref[...] = jnp.zeros(caug_ref.shape, F32)
        m_ref[...] = jnp.zeros(m_ref.shape, F32)
        sbd_ref[...] = jnp.zeros(sbd_ref.shape, F32)

    slot = lax.rem(s, 2)
    src = dict(zip(_PROJ_NAMES, (mq_s, mk_s, mv_s, mo_s, gq_s, gk_s, gv_s, gr_s, gates_s, la_s)))
    w_in = dict(zip(_PROJ_WEIGHTS, (wbig_ref, wsmall_ref, bsmall_ref, wa2_ref, ba_ref)))
    w_post = dict(zip(_POST_WEIGHTS, (wout_ref, gmlp_ref, w1_ref, w2_ref, gple_ref, wple_ref, wpg_ref, gfin_ref)))

    def front():
        h = _rms(xa_ref[...], gmix_ref[...]).astype(BF16)
        yield from _proj_stages(h, w_in, src, _seq_conv(cbuf_ref, tail_ref, convw_ref, convb_ref, tb))
        yield from _mix_stages(src, tril_ref, tri_ref, wred_ref, caug_ref, m_ref, sbd_ref, hm_s, hg_s)
        mixed_ref[slot] = _gate_heads(hm_s[...], hg_s[...], mo_s[...], gr_s[...], gmh_ref[...], ggh_ref[...])
        yield

    def back():
        yield from _post_stages(xb_ref[...], pb_ref[...], mixed_ref[1 - slot], w_post, y_ref)

    _interleave((front(), 2), (back(), 1))

    @pl.when(jnp.logical_and(s < n_blocks, r == steps_per_seq - 1))
    def _():
        cout_ref[0] = caug_ref[...]
        mout_ref[0] = m_ref[...]
        for g in range(G_HEADS):
            sout_ref[0, g] = sbd_ref[g * G_DK:(g + 1) * G_DK, g * G_DV:(g + 1) * G_DV]


def _seq_call(x2d, p2d, wts, consts, *, n_seq, seq_len):
    tb = SEQ_BLOCK
    steps_per_seq = seq_len // tb
    n_blocks = n_seq * steps_per_seq
    tril, tri, wred = consts
    front_blk = lambda s: jnp.minimum(s, n_blocks - 1)
    back_blk = lambda s: jnp.maximum(s - 1, 0)
    whole = pl.BlockSpec(memory_space=pltpu.VMEM)
    per_seq = lambda shape: pl.BlockSpec((1,) + shape, lambda s: (front_blk(s) // steps_per_seq,) + (0,) * len(shape))
    scratch = [
        pltpu.VMEM((tb + SUBLANES, QK_CONV), F32),
        pltpu.VMEM((M_HEADS, M_DH, 2 * M_DH), F32),
        pltpu.VMEM((SUBLANES, LANES), F32),
        pltpu.VMEM((G_KW, G_VW), F32),
        pltpu.VMEM((2, tb, D_MODEL), BF16),
    ] + [pltpu.VMEM((tb, _PROJ_WIDTH[k]), _PROJ_DTYPE[k]) for k in _PROJ_NAMES] + [
        pltpu.VMEM((tb, M_WIDTH), F32), pltpu.VMEM((tb, G_VW), F32)]
    return pl.pallas_call(
        functools.partial(_seq_kernel, steps_per_seq=steps_per_seq, n_blocks=n_blocks),
        grid=(n_blocks + 1,),
        in_specs=[pl.BlockSpec((tb, D_MODEL), lambda s: (front_blk(s), 0)),
                  pl.BlockSpec((tb, D_MODEL), lambda s: (back_blk(s), 0)),
                  pl.BlockSpec((tb, D_PLE), lambda s: (back_blk(s), 0))] + [whole] * 21,
        out_specs=[pl.BlockSpec((tb, D_MODEL), lambda s: (back_blk(s), 0)),
                   per_seq((SUBLANES, QK_CONV)), per_seq((M_HEADS, M_DH, 2 * M_DH)),
                   per_seq((SUBLANES, LANES)), per_seq((G_HEADS, G_DK, G_DV))],
        out_shape=[jax.ShapeDtypeStruct((n_blocks * tb, D_MODEL), F32),
                   jax.ShapeDtypeStruct((n_seq, SUBLANES, QK_CONV), F32),
                   jax.ShapeDtypeStruct((n_seq, M_HEADS, M_DH, 2 * M_DH), F32),
                   jax.ShapeDtypeStruct((n_seq, SUBLANES, LANES), F32),
                   jax.ShapeDtypeStruct((n_seq, G_HEADS, G_DK, G_DV), F32)],
        scratch_shapes=scratch,
        compiler_params=pltpu.CompilerParams(dimension_semantics=("arbitrary",), vmem_limit_bytes=VMEM_LIMIT),
        name="seq_fused",
    )(x2d, x2d, p2d, wts["g_mix"], wts["w_big"], wts["w_small"], wts["b_small"], wts["w_a2p"], wts["b_a"],
      wts["conv_w"], wts["conv_b"], tril, tri, wred, wts["g_mhead"], wts["g_ghead"],
      *[wts[k] for k in _POST_WEIGHTS])


def _mix_tok_kernel(mq_ref, mk_ref, mv_ref, gates_ref, gq_ref, gk_ref, gv_ref, la_ref,
                    c_ref, n_ref, m_ref, s_ref,
                    hm_ref, hg_ref, cn_ref, nn_ref, mn_ref, sn_ref):
    bb = mq_ref.shape[0]
    gates = gates_ref[...]
    m_all = m_ref[...]
    for hd in range(M_HEADS):
        cs_, ce_ = hd * M_DH, (hd + 1) * M_DH
        q = mq_ref[:, cs_:ce_].astype(F32)
        k = mk_ref[:, cs_:ce_]
        v = mv_ref[:, cs_:ce_].astype(F32)
        n_prev = n_ref[:, cs_:ce_]
        ig = gates[:, hd:hd + 1]
        lf = gates[:, M_HEADS + hd:M_HEADS + hd + 1]
        m_prev = m_all[:, hd:hd + 1]
        m_new = jnp.maximum(lf + m_prev, ig)
        scale = jnp.exp(lf + m_prev - m_new)
        wk = jnp.exp(ig - m_new)
        s_qk = jnp.sum(q * k, axis=1, keepdims=True) * wk
        den = scale * jnp.sum(q * n_prev, axis=1, keepdims=True) + s_qk
        inv = 1.0 / jnp.maximum(jnp.abs(den), jnp.exp(-m_new))
        nn_ref[:, cs_:ce_] = scale * n_prev + wk * k
        mn_ref[:, hd:hd + 1] = m_new
        q_t = q.T
        kw_t = (k * wk).T
        for b in range(bb):
            c_prev = c_ref[b, hd]
            q_col = q_t[:, b:b + 1]
            qc = jnp.sum(q_col * c_prev, axis=0, keepdims=True)
            num = scale[b:b + 1] * qc + s_qk[b:b + 1] * v[b:b + 1]
            hm_ref[b:b + 1, cs_:ce_] = num * inv[b:b + 1]
            cn_ref[b, hd] = scale[b:b + 1] * c_prev + kw_t[:, b:b + 1] * v[b:b + 1]
    la = la_ref[...]
    dec = jnp.exp(la)
    gq = gq_ref[...]
    gk = gk_ref[...]
    for g in range(G_HEADS):
        ks_, ke_ = g * G_DK, (g + 1) * G_DK
        vs_, ve_ = g * G_DV, (g + 1) * G_DV
        q = gq[:, ks_:ke_]
        k = gk[:, ks_:ke_]
        v = gv_ref[:, vs_:ve_].astype(F32)
        a = jnp.sum(q * k, axis=1, keepdims=True)
        qd_t = (q * dec[:, ks_:ke_]).T
        k_t = k.T
        d_t = dec[:, ks_:ke_].T
        for b in range(bb):
            s_prev = s_ref[b, g]
            o = jnp.sum(qd_t[:, b:b + 1] * s_prev, axis=0, keepdims=True) + a[b:b + 1] * v[b:b + 1]
            hg_ref[b:b + 1, vs_:ve_] = o
            sn_ref[b, g] = d_t[:, b:b + 1] * s_prev + k_t[:, b:b + 1] * v[b:b + 1]


def _mix_tok_call(p, c0, n0, m0, s0):
    bb = TOK_BATCH
    n = c0.shape[0]
    row = lambda w: pl.BlockSpec((bb, w), lambda i: (i, 0))
    st4 = lambda a, b_: pl.BlockSpec((bb, M_HEADS, a, b_), lambda i: (i, 0, 0, 0))
    return pl.pallas_call(
        _mix_tok_kernel,
        grid=(n // bb,),
        in_specs=[row(M_WIDTH), row(M_WIDTH), row(M_WIDTH), row(LANES), row(G_KW), row(G_KW), row(G_VW),
                  row(G_KW), st4(M_DH, M_DH), row(M_WIDTH), row(M_HEADS), st4(G_DK, G_DV)],
        out_specs=[row(M_WIDTH), row(G_VW), st4(M_DH, M_DH), row(M_WIDTH), row(M_HEADS), st4(G_DK, G_DV)],
        out_shape=[jax.ShapeDtypeStruct((n, M_WIDTH), F32), jax.ShapeDtypeStruct((n, G_VW), F32),
                   jax.ShapeDtypeStruct(c0.shape, F32), jax.ShapeDtypeStruct((n, M_WIDTH), F32),
                   jax.ShapeDtypeStruct((n, M_HEADS), F32), jax.ShapeDtypeStruct(s0.shape, F32)],
        compiler_params=pltpu.CompilerParams(dimension_semantics=("arbitrary",), vmem_limit_bytes=VMEM_LIMIT),
        name="mix_tok",
    )(p["mq"], p["mk"], p["mv"], p["gates"], p["gq"], p["gk"], p["gv"], p["la"], c0, n0, m0, s0)


def _prep_weights(w_in, conv_w, conv_b, b_gate, w_a2, b_a, g_mhead, g_ghead, w_out, g_mix, g_mlp, w1, w2,
                  g_ple, w_ple, w_pg, g_final):
    offs = [0]
    for s in IN_SIZES:
        offs.append(offs[-1] + s)
    gates_lo, gates_hi = offs[3], offs[5]
    ga_lo = offs[9]
    w_big = jnp.concatenate([w_in[:, :gates_lo], w_in[:, gates_hi:ga_lo]], axis=1).astype(BF16)
    n_small = 2 * M_HEADS + G_RANK
    w_small = jnp.concatenate([w_in[:, gates_lo:gates_hi], w_in[:, ga_lo:],
                               jnp.zeros((D_MODEL, LANES - n_small), F32)], axis=1).astype(BF16)
    b_small = jnp.concatenate([b_gate, jnp.zeros((LANES - 2 * M_HEADS,), F32)])[None]
    w_a2p = jnp.concatenate([jnp.zeros((2 * M_HEADS, G_KW), F32), w_a2,
                             jnp.zeros((LANES - n_small, G_KW), F32)], axis=0).astype(BF16)
    return dict(
        w_big=w_big, w_small=w_small, b_small=b_small, w_a2p=w_a2p, b_a=b_a[None],
        conv_w=conv_w, conv_b=conv_b[None], g_mix=g_mix[None], g_mhead=g_mhead[None], g_ghead=g_ghead[None],
        w_out=w_out.astype(BF16), g_mlp=g_mlp[None], w1=w1.astype(BF16), w2=w2.astype(BF16),
        g_ple=g_ple[None], w_ple=w_ple.astype(BF16), w_pg=w_pg.astype(BF16), g_final=g_final[None])


def _mix_constants(ts):
    t = jnp.arange(ts)
    tril = (t[None, :] <= t[:, None]).astype(BF16)
    tri = ((t[:, None] // CHUNK == t[None, :] // CHUNK) & (t[None, :] <= t[:, None])).astype(BF16)
    rr = jnp.arange(G_KW)
    wred = ((rr[None, :, None] // G_DK == rr[None, None, :] // G_DK)
            & (rr[None, None, :] % SUB == jnp.arange(SUB)[:, None, None])).astype(BF16)
    return tril, tri, wred


def kernel(x_prompt, x_sample, p_prompt, p_sample, state_mlstm_C, state_mlstm_n, state_mlstm_m, state_conv,
           state_gla_S, w_in, conv_w, conv_b, b_gate, w_a2, b_a, g_mhead, g_ghead, w_out, g_mix, g_mlp, w1,
           w2, g_ple, w_ple, w_pg, g_final):
    assert w_in.shape[0] == 1, "single-layer trunk"
    n_seq, seq_len, _ = x_prompt.shape
    n_tok = x_sample.shape[0]
    assert x_sample.shape[1] == 1 and seq_len % SEQ_BLOCK == 0 and n_tok % TOK_BATCH == 0
    wts = _prep_weights(w_in[0], conv_w[0], conv_b[0], b_gate[0], w_a2[0], b_a[0], g_mhead[0], g_ghead[0],
                        w_out[0], g_mix[0], g_mlp[0], w1[0], w2[0], g_ple[0], w_ple[0], w_pg[0], g_final)

    y_p, tail_p, caug_p, m_p, s_p = _seq_call(
        x_prompt.reshape(n_seq * seq_len, D_MODEL), p_prompt[0].reshape(n_seq * seq_len, D_PLE), wts,
        _mix_constants(SEQ_BLOCK), n_seq=n_seq, seq_len=seq_len)

    xs = x_sample.reshape(n_tok, D_MODEL)
    buf = state_conv[0]
    outs = _proj_tok_call(xs, (buf[:, 0], buf[:, 1], buf[:, 2]), wts)
    ps = dict(zip(_PROJ_NAMES, outs[:-1]))
    raw_s = outs[-1]
    hm_s, hg_s, c_s, n_s, m_s, s_s = _mix_tok_call(
        ps, state_mlstm_C[0], state_mlstm_n[0].reshape(n_tok, M_WIDTH), state_mlstm_m[0], state_gla_S[0])
    y_s = _post_tok_call(xs, p_sample[0].reshape(n_tok, D_PLE), hm_s, hg_s, ps["mo"], ps["gr"], wts)

    return (y_p.reshape(n_seq, seq_len, D_MODEL),
            y_s.reshape(n_tok, 1, D_MODEL),
            caug_p[None, :, :, :, :M_DH],
            caug_p[None, :, :, :, M_DH],
            m_p[None, :, :M_HEADS, 0],
            tail_p[None, :, SUBLANES - (CONV_W - 1):, :],
            s_p[None],
            c_s[None],
            n_s.reshape(1, n_tok, M_HEADS, M_DH),
            m_s[None],
            jnp.stack([buf[:, 1], buf[:, 2], raw_s], axis=1)[None],
            s_s[None])
```

```python
import functools

import jax
import jax.numpy as jnp
from jax import lax
from jax.experimental import pallas as pl
from jax.experimental.pallas import tpu as pltpu

D_MODEL = 1024
M_HEADS = 4
M_DH = 128
M_WIDTH = M_HEADS * M_DH
G_HEADS = 4
G_DK = 64
G_DV = 128
G_KW = G_HEADS * G_DK
G_VW = G_HEADS * G_DV
G_RANK = 16
G_TAU = 16.0
CONV_W = 4
QK_CONV = 2 * M_WIDTH
D_FF = 4 * D_MODEL
D_PLE = 256
CHUNK = 64
SUB = 16
EPS = 1e-6
IN_SIZES = (QK_CONV, M_WIDTH, M_WIDTH, M_HEADS, M_HEADS, G_KW, G_KW, G_VW, G_VW, G_RANK)

LANES = 128
SUBLANES = 8
VMEM_LIMIT = 60 * 1024 * 1024
SEQ_BLOCK = 256
TOK_BATCH = 16

F32 = jnp.float32
BF16 = jnp.bfloat16
NEG = -1e30


def _rms(x, g):
    return x * lax.rsqrt(jnp.mean(x * x, axis=-1, keepdims=True) + EPS) * g


def _log_sigmoid(x):
    return jnp.minimum(x, 0.0) - jnp.log(1.0 + jnp.exp(-jnp.abs(x)))


def _sigmoid(x):
    return 0.5 * jnp.tanh(0.5 * x) + 0.5


def _div_pow2(idx, d):
    assert d & (d - 1) == 0
    return lax.shift_right_logical(idx, d.bit_length() - 1)


def _dot(a, b):
    return jnp.dot(a, b, preferred_element_type=F32)


def _dot_nt(a, b):
    return lax.dot_general(a, b, (((1,), (1,)), ((), ())), preferred_element_type=F32)


def _split3(x):
    hi = x.astype(BF16)
    r1 = x - hi.astype(F32)
    mid = r1.astype(BF16)
    lo = (r1 - mid.astype(F32)).astype(BF16)
    return hi, mid, lo


def _cumsum_dot(tri, x):
    hi, mid, lo = _split3(x)
    return _dot(tri, hi) + _dot(tri, mid) + _dot(tri, lo)


def _spread_dot(x, onehot):
    hi, mid, lo = _split3(x)
    return _dot(hi, onehot) + _dot(mid, onehot) + _dot(lo, onehot)


def _interleave(pattern, **streams):
    by_letter = {name[0]: gen for name, gen in streams.items()}
    for letter in pattern:
        next(by_letter[letter], None)
    for gen in by_letter.values():
        _run(gen)


def _run(gen):
    for _ in gen:
        pass


def _proj_stages(h, w, sinks, conv_piece):
    width = M_WIDTH

    def put_q(raw):
        sinks["mq"][...] = conv_piece(raw, 0).astype(sinks["mq"].dtype)

    def put_k(raw):
        sinks["mk"][...] = conv_piece(raw, M_WIDTH) * (M_DH ** -0.5)

    def put_mv(raw):
        sinks["mv"][...] = raw.astype(sinks["mv"].dtype)

    def put_mo(raw):
        sinks["mo"][...] = raw

    def put_gqk(raw):
        sinks["gq"][...] = raw[:, :G_KW] * (G_DK ** -0.5)
        sinks["gk"][...] = raw[:, G_KW:]

    def put_gv(raw):
        sinks["gv"][...] = raw.astype(sinks["gv"].dtype)

    def put_gr(raw):
        sinks["gr"][...] = raw

    def put_small(small):
        g = small + w["b_small"][...]
        lane = lax.broadcasted_iota(jnp.int32, g.shape, 1)
        sinks["gates"][...] = jnp.where(lane < M_HEADS, g, _log_sigmoid(g))
        z = _dot(small.astype(BF16), w["w_a2p"][...]) + w["b_a"][...]
        sinks["la"][...] = _log_sigmoid(z) * (1.0 / G_TAU)

    epilogues = [put_small, put_q, put_k, put_mv, put_mo, put_gqk, put_gv, put_gr]
    assert w["w_big"].shape[1] == width * (len(epilogues) - 1)
    pending = (_dot(h, w["w_small"][...]), epilogues[0])
    yield
    for n, epi in enumerate(epilogues[1:]):
        cur = _dot(h, w["w_big"][:, n * width:(n + 1) * width])
        pending[1](pending[0])
        pending = (cur, epi)
        yield
    pending[1](pending[0])
    yield


def _seq_conv(cbuf_ref, tail_ref, convw_ref, convb_ref, tb):
    cw = convw_ref[...]

    def conv_piece(raw, lo):
        cols = slice(lo, lo + raw.shape[1])
        cbuf_ref[SUBLANES:SUBLANES + tb, cols] = raw
        y = convb_ref[:, cols] + raw * cw[3:4, cols]
        for j in range(CONV_W - 1):
            off = SUBLANES - (CONV_W - 1) + j
            y = y + cbuf_ref[off:off + tb, cols] * cw[j:j + 1, cols]
        last = cbuf_ref[tb:tb + SUBLANES, cols]
        tail_ref[0, :, cols] = last
        cbuf_ref[0:SUBLANES, cols] = last
        return y * _sigmoid(y)

    return conv_piece


def _tok_conv(b0_ref, b1_ref, b2_ref, raw_ref, convw_ref, convb_ref):
    cw = convw_ref[...]

    def conv_piece(raw, lo):
        cols = slice(lo, lo + raw.shape[1])
        raw_ref[:, cols] = raw
        y = (convb_ref[:, cols] + b0_ref[:, cols] * cw[0:1, cols] + b1_ref[:, cols] * cw[1:2, cols]
             + b2_ref[:, cols] * cw[2:3, cols] + raw * cw[3:4, cols])
        return y * _sigmoid(y)

    return conv_piece


_PROJ_NAMES = ("mq", "mk", "mv", "mo", "gq", "gk", "gv", "gr", "gates", "la")
_PROJ_WIDTH = dict(mq=M_WIDTH, mk=M_WIDTH, mv=M_WIDTH, mo=M_WIDTH, gq=G_KW, gk=G_KW, gv=G_VW, gr=G_VW,
                   gates=LANES, la=G_KW)
_PROJ_DTYPE = dict(mq=BF16, mk=F32, mv=BF16, mo=F32, gq=F32, gk=F32, gv=BF16, gr=F32, gates=F32, la=F32)
_PROJ_WEIGHTS = ("w_big", "w_small", "b_small", "w_a2p", "b_a")


def _proj_tok_kernel(x_ref, b0_ref, b1_ref, b2_ref, gmix_ref, wbig_ref, wsmall_ref, bsmall_ref, wa2_ref,
                     ba_ref, convw_ref, convb_ref, *out_refs):
    sinks = dict(zip(_PROJ_NAMES, out_refs[:-1]))
    raw_ref = out_refs[-1]
    w = dict(zip(_PROJ_WEIGHTS, (wbig_ref, wsmall_ref, bsmall_ref, wa2_ref, ba_ref)))
    h = _rms(x_ref[...], gmix_ref[...]).astype(BF16)
    _run(_proj_stages(h, w, sinks, _tok_conv(b0_ref, b1_ref, b2_ref, raw_ref, convw_ref, convb_ref)))


def _proj_tok_call(x2d, conv_rows, wts):
    n = x2d.shape[0]
    row = lambda width: pl.BlockSpec((n, width), lambda i: (0, 0))
    whole = pl.BlockSpec(memory_space=pltpu.VMEM)
    names = list(_PROJ_NAMES)
    return pl.pallas_call(
        _proj_tok_kernel,
        grid=(1,),
        in_specs=[row(D_MODEL)] + [row(QK_CONV)] * 3 + [whole] * 8,
        out_specs=[row(_PROJ_WIDTH[k]) for k in names] + [row(QK_CONV)],
        out_shape=[jax.ShapeDtypeStruct((n, _PROJ_WIDTH[k]), _PROJ_DTYPE[k]) for k in names]
        + [jax.ShapeDtypeStruct((n, QK_CONV), F32)],
        compiler_params=pltpu.CompilerParams(dimension_semantics=("arbitrary",), vmem_limit_bytes=VMEM_LIMIT),
        name="proj_tok",
    )(x2d, *conv_rows, wts["g_mix"], wts["w_big"], wts["w_small"], wts["b_small"], wts["w_a2p"], wts["b_a"],
      wts["conv_w"], wts["conv_b"])


def _mix_stages(src, tril_ref, tri_ref, wred_ref, caug_ref, m_ref, sbd_ref, hm_ref, hg_ref):
    ts = src["mq"].shape[0]
    n_chunks = ts // CHUNK
    gates = src["gates"][...]
    bcum = _cumsum_dot(tril_ref[...], gates)
    gq = src["gq"][...]
    gk = src["gk"][...]
    gv_ref = src["gv"]
    bc = _cumsum_dot(tri_ref[...], src["la"][...])
    gates_t = gates.T
    bcum_t = bcum.T
    bc_t = bc.T
    yield

    nb = ts // SUB
    q3 = gq.reshape(nb, SUB, G_KW)
    k3 = gk.reshape(nb, SUB, G_KW)
    bc3 = bc.reshape(nb, SUB, G_KW)
    tl = lax.broadcasted_iota(jnp.int32, (nb, SUB, G_KW), 1)

    def exact_pass(j, acc):
        arg = jnp.where(tl >= j, bc3 - bc3[:, j:j + 1, :], NEG)
        e = (q3 * k3[:, j:j + 1, :] * jnp.exp(arg)).reshape(ts, G_KW)
        return acc + _dot(e.astype(BF16), wred_ref[j])

    krow = lax.broadcasted_iota(jnp.int32, (CHUNK, G_KW), 0)
    same_head_kk = (_div_pow2(lax.broadcasted_iota(jnp.int32, (G_KW, G_KW), 0), G_DK)
                    == _div_pow2(lax.broadcasted_iota(jnp.int32, (G_KW, G_KW), 1), G_DK))
    same_head_kv = (_div_pow2(lax.broadcasted_iota(jnp.int32, (G_KW, G_VW), 0), G_DK)
                    == _div_pow2(lax.broadcasted_iota(jnp.int32, (G_KW, G_VW), 1), G_DV))

    def cross_block_scores(c):
        lo = c * CHUNK
        bc_c = bc[lo:lo + CHUNK]
        k_c = gk[lo:lo + CHUNK]
        out = [jnp.zeros((SUB, G_KW), F32)]
        for i in range(1, CHUNK // SUB):
            r0 = lo + i * SUB
            r_i = bc[r0 - 1:r0, :]
            qi = (gq[r0:r0 + SUB] * jnp.exp(bc[r0:r0 + SUB] - r_i)).astype(BF16)
            ki = (k_c * jnp.exp(jnp.where(krow < i * SUB, r_i - bc_c, NEG))).astype(BF16)
            kbd = jnp.where(same_head_kk, jnp.concatenate([ki] * G_HEADS, axis=0), jnp.zeros((), BF16))
            out.append(_dot_nt(qi, kbd))
        return out

    def state_update_term(c):
        lo, hi = c * CHUNK, (c + 1) * CHUNK
        b_end = bc[hi - 1:hi, :]
        k_out_t = (gk[lo:hi] * jnp.exp(b_end - bc[lo:hi])).T.astype(BF16)
        return jnp.where(same_head_kv, _dot(k_out_t, gv_ref[lo:hi, :]), 0.0)

    adiag = jnp.zeros((ts, G_KW), F32)
    per_stage = SUB // 4
    assert n_chunks == 4
    offs = []
    s_terms = []

    causal = (lax.broadcasted_iota(jnp.int32, (ts, ts), 0) >= lax.broadcasted_iota(jnp.int32, (ts, ts), 1))
    ones_col = (lax.broadcasted_iota(jnp.int32, (ts, LANES), 1) == 0).astype(BF16)
    heads = range(M_HEADS)
    hsl = [slice(hd * M_DH, (hd + 1) * M_DH) for hd in heads]
    b_col = [bcum[:, M_HEADS + hd:M_HEADS + hd + 1] for hd in heads]
    b_row = [bcum_t[M_HEADS + hd:M_HEADS + hd + 1, :] for hd in heads]
    i_col = [gates[:, hd:hd + 1] for hd in heads]
    i_row = [gates_t[hd:hd + 1, :] for hd in heads]
    m_prev = [m_ref[hd:hd + 1, 0:1] for hd in heads]
    q = [src["mq"][:, hsl[hd]] for hd in heads]
    k = [src["mk"][:, hsl[hd]] for hd in heads]
    vaug = [jnp.concatenate([src["mv"][:, hsl[hd]], ones_col], axis=1) for hd in heads]
    caug = [caug_ref[hd] for hd in heads]
    s_qk = [_dot_nt(q[hd], k[hd].astype(BF16)) for hd in heads]
    qc = [_dot(q[hd], caug[hd].astype(BF16)) for hd in heads]
    for j in range(0, per_stage):
        adiag = exact_pass(j, adiag)
    offs += cross_block_scores(0)
    s_terms.append(state_update_term(0))
    yield
    b_last = [b_col[hd][ts - 1:ts, :] for hd in heads]
    dec = [b_last[hd] - b_col[hd] + i_col[hd] for hd in heads]
    m_new = [jnp.maximum(b_last[hd] + m_prev[hd], jnp.max(dec[hd], axis=0, keepdims=True)) for hd in heads]
    kw_t = [(k[hd] * jnp.exp(dec[hd] - m_new[hd])).T.astype(BF16) for hd in heads]
    upd = [_dot(kw_t[hd], vaug[hd]) for hd in heads]
    for j in range(per_stage, 2 * per_stage):
        adiag = exact_pass(j, adiag)
    offs += cross_block_scores(1)
    s_terms.append(state_update_term(1))
    yield
    dmat = [jnp.where(causal, b_col[hd] - b_row[hd] + i_row[hd], -jnp.inf) for hd in heads]
    inter = [b_col[hd] + m_prev[hd] for hd in heads]
    m_tok = [jnp.maximum(inter[hd], jnp.max(dmat[hd], axis=1, keepdims=True)) for hd in heads]
    for j in range(2 * per_stage, 3 * per_stage):
        adiag = exact_pass(j, adiag)
    offs += cross_block_scores(2)
    s_terms.append(state_update_term(2))
    yield
    p = [(s_qk[hd] * jnp.exp(dmat[hd] - m_tok[hd])).astype(BF16) for hd in heads]
    pv = [_dot(p[hd], vaug[hd]) for hd in heads]
    for j in range(3 * per_stage, SUB):
        adiag = exact_pass(j, adiag)
    offs += cross_block_scores(3)
    s_terms.append(state_update_term(3))
    yield
    for hd in heads:
        tot = jnp.exp(inter[hd] - m_tok[hd]) * qc[hd] + pv[hd]
        den = tot[:, M_DH:M_DH + 1]
        hm_ref[:, hsl[hd]] = tot[:, :M_DH] / jnp.maximum(jnp.abs(den), jnp.exp(-m_tok[hd]))
        caug_ref[hd] = jnp.exp(b_last[hd] + m_prev[hd] - m_new[hd]) * caug[hd] + upd[hd]
        m_ref[hd:hd + 1, :] = jnp.broadcast_to(m_new[hd], (1, LANES))
    yield

    sub_of = lambda idx: lax.shift_right_logical(idx & (CHUNK - 1), SUB.bit_length() - 1)
    rowb = sub_of(lax.broadcasted_iota(jnp.int32, (ts, G_KW), 0))
    colb = sub_of(lax.broadcasted_iota(jnp.int32, (ts, G_KW), 1))
    adiag = jnp.where(rowb == colb, adiag, 0.0)
    intra = (jnp.concatenate(offs, axis=0) + adiag).astype(BF16)
    q_in = (gq * jnp.exp(bc)).astype(BF16)
    o_intra = []
    for c in range(n_chunks):
        lo, hi = c * CHUNK, (c + 1) * CHUNK
        vbd = jnp.where(same_head_kv, jnp.concatenate([gv_ref[lo:hi, :]] * G_HEADS, axis=0),
                        jnp.zeros((), BF16))
        o_intra.append(_dot(intra[lo:hi], vbd))
    sbd = sbd_ref[...]
    for c in range(n_chunks):
        lo, hi = c * CHUNK, (c + 1) * CHUNK
        hg_ref[lo:hi, :] = o_intra[c] + _dot(q_in[lo:hi], sbd.astype(BF16))
        dcol = jnp.exp(bc_t[:, hi - 1:hi])
        sbd = dcol * sbd + s_terms[c]
    sbd_ref[...] = sbd
    yield


def _head_norm(hv, n_heads, width):
    parts = []
    for hd in range(n_heads):
        seg = hv[:, hd * width:(hd + 1) * width]
        parts.append(seg * lax.rsqrt(jnp.mean(seg * seg, axis=-1, keepdims=True) + EPS))
    return jnp.concatenate(parts, axis=1)


def _gate_heads(hm, hg, mo, gr, gmh, ggh):
    hm = _head_norm(hm, M_HEADS, M_DH) * gmh * _sigmoid(mo)
    hg = _head_norm(hg, G_HEADS, G_DV) * ggh * (gr * _sigmoid(gr))
    return jnp.concatenate([hm, hg], axis=1).astype(BF16)


_POST_WEIGHTS = ("w_out", "g_mlp", "w1", "w2", "g_ple", "w_ple", "w_pg", "g_final")


def _post_stages(x, p, mixed, w, y_ref):
    x1 = x + _dot(mixed, w["w_out"][...])
    n1 = _rms(x1, w["g_mlp"][...]).astype(BF16)
    yield
    n_ff = 4
    ff = D_FF // n_ff
    acts = []
    for j in range(n_ff):
        u = _dot(n1, w["w1"][:, j * ff:(j + 1) * ff])
        acts.append(jnp.square(jnp.maximum(u, 0.0)).astype(BF16))
        yield
    act = jnp.concatenate(acts, axis=1)
    half = D_MODEL // 2
    mlp = []
    for j in range(2):
        mlp.append(_dot(act, w["w2"][:, j * half:(j + 1) * half]))
        yield
    x2 = x1 + jnp.concatenate(mlp, axis=1)
    ple = _dot(p.astype(BF16), w["w_ple"][...])
    n2 = _rms(x2, w["g_ple"][...]).astype(BF16)
    yield
    gate = _sigmoid(_dot(n2, w["w_pg"][...]))
    x3 = x2 + ple * gate
    y_ref[...] = _rms(x3, w["g_final"][...])
    yield


def _post_tok_kernel(x_ref, p_ref, hm_ref, hg_ref, mo_ref, gr_ref, gmh_ref, ggh_ref, wout_ref, gmlp_ref,
                     w1_ref, w2_ref, gple_ref, wple_ref, wpg_ref, gfin_ref, y_ref):
    mixed = _gate_heads(hm_ref[...], hg_ref[...], mo_ref[...], gr_ref[...], gmh_ref[...], ggh_ref[...])
    w = dict(zip(_POST_WEIGHTS, (wout_ref, gmlp_ref, w1_ref, w2_ref, gple_ref, wple_ref, wpg_ref, gfin_ref)))
    _run(_post_stages(x_ref[...], p_ref[...], mixed, w, y_ref))


def _post_tok_call(x2d, p2d, hm, hg, mo, gr, wts):
    n = x2d.shape[0]
    row = lambda width: pl.BlockSpec((n, width), lambda i: (0, 0))
    whole = pl.BlockSpec(memory_space=pltpu.VMEM)
    return pl.pallas_call(
        _post_tok_kernel,
        grid=(1,),
        in_specs=[row(D_MODEL), row(D_PLE), row(M_WIDTH), row(G_VW), row(M_WIDTH), row(G_VW)] + [whole] * 10,
        out_specs=row(D_MODEL),
        out_shape=jax.ShapeDtypeStruct((n, D_MODEL), F32),
        compiler_params=pltpu.CompilerParams(dimension_semantics=("arbitrary",), vmem_limit_bytes=VMEM_LIMIT),
        name="post_tok",
    )(x2d, p2d, hm, hg, mo, gr, wts["g_mhead"], wts["g_ghead"], *[wts[k] for k in _POST_WEIGHTS])


def _seq_kernel(xa_ref, xb_ref, pb_ref, gmix_ref, wbig_ref, wsmall_ref, bsmall_ref, wa2_ref, ba_ref,
                convw_ref, convb_ref, tril_ref, tri_ref, wred_ref, gmh_ref, ggh_ref,
                wout_ref, gmlp_ref, w1_ref, w2_ref, gple_ref, wple_ref, wpg_ref, gfin_ref,
                y_ref, tail_ref, cout_ref, mout_ref, sout_ref,
                cbuf_ref, caug_ref, m_ref, sbd_ref, mixed_ref,
                mq_s, mk_s, mv_s, mo_s, gq_s, gk_s, gv_s, gr_s, gates_s, la_s, hm_s, hg_s,
                *, steps_per_seq, n_blocks):
    tb = xa_ref.shape[0]
    s = pl.program_id(0)
    r = lax.rem(jnp.minimum(s, n_blocks - 1), steps_per_seq)

    @pl.when(s == 0)
    def _():
        mixed_ref[...] = jnp.zeros(mixed_ref.shape, BF16)

    @pl.when(r == 0)
    def _():
        cbuf_ref[0:SUBLANES, :] = jnp.zeros((SUBLANES, QK_CONV), F32)
        caug_ref[...] = jnp.zeros(caug_ref.shape, F32)
        m_ref[...] = jnp.zeros(m_ref.shape, F32)
        sbd_ref[...] = jnp.zeros(sbd_ref.shape, F32)

    slot = lax.rem(s, 2)
    src = dict(zip(_PROJ_NAMES, (mq_s, mk_s, mv_s, mo_s, gq_s, gk_s, gv_s, gr_s, gates_s, la_s)))
    w_in = dict(zip(_PROJ_WEIGHTS, (wbig_ref, wsmall_ref, bsmall_ref, wa2_ref, ba_ref)))
    w_post = dict(zip(_POST_WEIGHTS, (wout_ref, gmlp_ref, w1_ref, w2_ref, gple_ref, wple_ref, wpg_ref, gfin_ref)))

    def front():
        h = _rms(xa_ref[...], gmix_ref[...]).astype(BF16)
        yield from _proj_stages(h, w_in, src, _seq_conv(cbuf_ref, tail_ref, convw_ref, convb_ref, tb))
        yield from _mix_stages(src, tril_ref, tri_ref, wred_ref, caug_ref, m_ref, sbd_ref, hm_s, hg_s)
        mixed_ref[slot] = _gate_heads(hm_s[...], hg_s[...], mo_s[...], gr_s[...], gmh_ref[...], ggh_ref[...])
        yield

    def back():
        yield from _post_stages(xb_ref[...], pb_ref[...], mixed_ref[1 - slot], w_post, y_ref)

    _interleave("fffb" "fffb" "fffb" "ffb" "fb" "fb" "fb" "fb" "ffb", front=front(), back=back())

    @pl.when(jnp.logical_and(s < n_blocks, r == steps_per_seq - 1))
    def _():
        cout_ref[0] = caug_ref[...]
        mout_ref[0] = m_ref[...]
        for g in range(G_HEADS):
            sout_ref[0, g] = sbd_ref[g * G_DK:(g + 1) * G_DK, g * G_DV:(g + 1) * G_DV]


def _seq_call(x2d, p2d, wts, consts, *, n_seq, seq_len):
    tb = SEQ_BLOCK
    steps_per_seq = seq_len // tb
    n_blocks = n_seq * steps_per_seq
    tril, tri, wred = consts
    front_blk = lambda s: jnp.minimum(s, n_blocks - 1)
    back_blk = lambda s: jnp.maximum(s - 1, 0)
    whole = pl.BlockSpec(memory_space=pltpu.VMEM)
    per_seq = lambda shape: pl.BlockSpec((1,) + shape, lambda s: (front_blk(s) // steps_per_seq,) + (0,) * len(shape))
    scratch = [
        pltpu.VMEM((tb + SUBLANES, QK_CONV), F32),
        pltpu.VMEM((M_HEADS, M_DH, 2 * M_DH), F32),
        pltpu.VMEM((SUBLANES, LANES), F32),
        pltpu.VMEM((G_KW, G_VW), F32),
        pltpu.VMEM((2, tb, D_MODEL), BF16),
    ] + [pltpu.VMEM((tb, _PROJ_WIDTH[k]), _PROJ_DTYPE[k]) for k in _PROJ_NAMES] + [
        pltpu.VMEM((tb, M_WIDTH), F32), pltpu.VMEM((tb, G_VW), F32)]
    return pl.pallas_call(
        functools.partial(_seq_kernel, steps_per_seq=steps_per_seq, n_blocks=n_blocks),
        grid=(n_blocks + 1,),
        in_specs=[pl.BlockSpec((tb, D_MODEL), lambda s: (front_blk(s), 0)),
                  pl.BlockSpec((tb, D_MODEL), lambda s: (back_blk(s), 0)),
                  pl.BlockSpec((tb, D_PLE), lambda s: (back_blk(s), 0))] + [whole] * 21,
        out_specs=[pl.BlockSpec((tb, D_MODEL), lambda s: (back_blk(s), 0)),
                   per_seq((SUBLANES, QK_CONV)), per_seq((M_HEADS, M_DH, 2 * M_DH)),
                   per_seq((SUBLANES, LANES)), per_seq((G_HEADS, G_DK, G_DV))],
        out_shape=[jax.ShapeDtypeStruct((n_blocks * tb, D_MODEL), F32),
                   jax.ShapeDtypeStruct((n_seq, SUBLANES, QK_CONV), F32),
                   jax.ShapeDtypeStruct((n_seq, M_HEADS, M_DH, 2 * M_DH), F32),
                   jax.ShapeDtypeStruct((n_seq, SUBLANES, LANES), F32),
                   jax.ShapeDtypeStruct((n_seq, G_HEADS, G_DK, G_DV), F32)],
        scratch_shapes=scratch,
        compiler_params=pltpu.CompilerParams(dimension_semantics=("arbitrary",), vmem_limit_bytes=VMEM_LIMIT),
        name="seq_fused",
    )(x2d, x2d, p2d, wts["g_mix"], wts["w_big"], wts["w_small"], wts["b_small"], wts["w_a2p"], wts["b_a"],
      wts["conv_w"], wts["conv_b"], tril, tri, wred, wts["g_mhead"], wts["g_ghead"],
      *[wts[k] for k in _POST_WEIGHTS])


def _mix_tok_kernel(mq_ref, mk_ref, mv_ref, gates_ref, gq_ref, gk_ref, gv_ref, la_ref,
                    c_ref, n_ref, m_ref, s_ref,
                    hm_ref, hg_ref, cn_ref, nn_ref, mn_ref, sn_ref):
    bb = mq_ref.shape[0]
    wide = bb * LANES
    diag = (lax.broadcasted_iota(jnp.int32, (bb, wide), 0)
            == _div_pow2(lax.broadcasted_iota(jnp.int32, (bb, wide), 1), LANES))
    block_ones = diag.astype(BF16)
    ones_rows = jnp.ones((M_DH, bb), BF16)

    def block_diag(rows):
        return jnp.where(diag, jnp.concatenate([rows] * bb, axis=1), jnp.zeros((), rows.dtype))

    def block_diag_f32(col):
        return jnp.where(diag, col, 0.0)

    def own_block(x):
        x = jnp.where(diag, x, 0.0)
        out = x[:, 0:LANES]
        for b in range(1, bb):
            out = out + x[:, b * LANES:(b + 1) * LANES]
        return out

    gates = gates_ref[...]
    m_all = m_ref[...]
    for hd in range(M_HEADS):
        cs_, ce_ = hd * M_DH, (hd + 1) * M_DH
        q_bf = mq_ref[:, cs_:ce_]
        q = q_bf.astype(F32)
        k = mk_ref[:, cs_:ce_]
        v_bf = mv_ref[:, cs_:ce_]
        v = v_bf.astype(F32)
        n_prev = n_ref[:, cs_:ce_]
        ig = gates[:, hd:hd + 1]
        lf = gates[:, M_HEADS + hd:M_HEADS + hd + 1]
        m_prev = m_all[:, hd:hd + 1]
        m_new = jnp.maximum(lf + m_prev, ig)
        scale = jnp.exp(lf + m_prev - m_new)
        wk = jnp.exp(ig - m_new)
        s_qk = jnp.sum(q * k, axis=1, keepdims=True) * wk
        den = scale * jnp.sum(q * n_prev, axis=1, keepdims=True) + s_qk
        inv = 1.0 / jnp.maximum(jnp.abs(den), jnp.exp(-m_new))
        nn_ref[:, cs_:ce_] = scale * n_prev + wk * k
        mn_ref[:, hd:hd + 1] = m_new
        c_prev = [c_ref[b, hd] for b in range(bb)]
        qc = own_block(_dot(q_bf, jnp.concatenate([c.astype(BF16) for c in c_prev], axis=1)))
        hm_ref[:, cs_:ce_] = (scale * qc + s_qk * v) * inv
        outer = _dot((k * wk).T.astype(BF16), block_diag(v_bf))
        scale_w = _cumsum_dot(ones_rows, block_diag_f32(scale))
        for b in range(bb):
            blk = slice(b * M_DH, (b + 1) * M_DH)
            cn_ref[b, hd] = scale_w[:, blk] * c_prev[b] + outer[:, blk]
    la = la_ref[...]
    dec = jnp.exp(la)
    gq = gq_ref[...]
    gk = gk_ref[...]
    for g in range(G_HEADS):
        ks_, ke_ = g * G_DK, (g + 1) * G_DK
        vs_, ve_ = g * G_DV, (g + 1) * G_DV
        q = gq[:, ks_:ke_]
        k = gk[:, ks_:ke_]
        v_bf = gv_ref[:, vs_:ve_]
        a = jnp.sum(q * k, axis=1, keepdims=True)
        s_prev = [s_ref[b, g] for b in range(bb)]
        o = own_block(_dot((q * dec[:, ks_:ke_]).astype(BF16),
                           jnp.concatenate([s.astype(BF16) for s in s_prev], axis=1)))
        hg_ref[:, vs_:ve_] = o + a * v_bf.astype(F32)
        outer = _dot(k.T.astype(BF16), block_diag(v_bf))
        dec_w = _spread_dot(dec[:, ks_:ke_].T, block_ones)
        for b in range(bb):
            blk = slice(b * G_DV, (b + 1) * G_DV)
            sn_ref[b, g] = dec_w[:, blk] * s_prev[b] + outer[:, blk]


def _mix_tok_call(p, c0, n0, m0, s0):
    bb = TOK_BATCH
    n = c0.shape[0]
    row = lambda w: pl.BlockSpec((bb, w), lambda i: (i, 0))
    st4 = lambda a, b_: pl.BlockSpec((bb, M_HEADS, a, b_), lambda i: (i, 0, 0, 0))
    return pl.pallas_call(
        _mix_tok_kernel,
        grid=(n // bb,),
        in_specs=[row(M_WIDTH), row(M_WIDTH), row(M_WIDTH), row(LANES), row(G_KW), row(G_KW), row(G_VW),
                  row(G_KW), st4(M_DH, M_DH), row(M_WIDTH), row(M_HEADS), st4(G_DK, G_DV)],
        out_specs=[row(M_WIDTH), row(G_VW), st4(M_DH, M_DH), row(M_WIDTH), row(M_HEADS), st4(G_DK, G_DV)],
        out_shape=[jax.ShapeDtypeStruct((n, M_WIDTH), F32), jax.ShapeDtypeStruct((n, G_VW), F32),
                   jax.ShapeDtypeStruct(c0.shape, F32), jax.ShapeDtypeStruct((n, M_WIDTH), F32),
                   jax.ShapeDtypeStruct((n, M_HEADS), F32), jax.ShapeDtypeStruct(s0.shape, F32)],
        compiler_params=pltpu.CompilerParams(dimension_semantics=("arbitrary",), vmem_limit_bytes=VMEM_LIMIT),
        name="mix_tok",
    )(p["mq"], p["mk"], p["mv"], p["gates"], p["gq"], p["gk"], p["gv"], p["la"], c0, n0, m0, s0)


def _prep_weights(w_in, conv_w, conv_b, b_gate, w_a2, b_a, g_mhead, g_ghead, w_out, g_mix, g_mlp, w1, w2,
                  g_ple, w_ple, w_pg, g_final):
    offs = [0]
    for s in IN_SIZES:
        offs.append(offs[-1] + s)
    gates_lo, gates_hi = offs[3], offs[5]
    ga_lo = offs[9]
    w_big = jnp.concatenate([w_in[:, :gates_lo], w_in[:, gates_hi:ga_lo]], axis=1).astype(BF16)
    n_small = 2 * M_HEADS + G_RANK
    w_small = jnp.concatenate([w_in[:, gates_lo:gates_hi], w_in[:, ga_lo:],
                               jnp.zeros((D_MODEL, LANES - n_small), F32)], axis=1).astype(BF16)
    b_small = jnp.concatenate([b_gate, jnp.zeros((LANES - 2 * M_HEADS,), F32)])[None]
    w_a2p = jnp.concatenate([jnp.zeros((2 * M_HEADS, G_KW), F32), w_a2,
                             jnp.zeros((LANES - n_small, G_KW), F32)], axis=0).astype(BF16)
    return dict(
        w_big=w_big, w_small=w_small, b_small=b_small, w_a2p=w_a2p, b_a=b_a[None],
        conv_w=conv_w, conv_b=conv_b[None], g_mix=g_mix[None], g_mhead=g_mhead[None], g_ghead=g_ghead[None],
        w_out=w_out.astype(BF16), g_mlp=g_mlp[None], w1=w1.astype(BF16), w2=w2.astype(BF16),
        g_ple=g_ple[None], w_ple=w_ple.astype(BF16), w_pg=w_pg.astype(BF16), g_final=g_final[None])


def _mix_constants(ts):
    t = jnp.arange(ts)
    tril = (t[None, :] <= t[:, None]).astype(BF16)
    tri = ((t[:, None] // CHUNK == t[None, :] // CHUNK) & (t[None, :] <= t[:, None])).astype(BF16)
    rr = jnp.arange(G_KW)
    wred = ((rr[None, :, None] // G_DK == rr[None, None, :] // G_DK)
            & (rr[None, None, :] % SUB == jnp.arange(SUB)[:, None, None])).astype(BF16)
    return tril, tri, wred


def kernel(x_prompt, x_sample, p_prompt, p_sample, state_mlstm_C, state_mlstm_n, state_mlstm_m, state_conv,
           state_gla_S, w_in, conv_w, conv_b, b_gate, w_a2, b_a, g_mhead, g_ghead, w_out, g_mix, g_mlp, w1,
           w2, g_ple, w_ple, w_pg, g_final):
    assert w_in.shape[0] == 1, "single-layer trunk"
    n_seq, seq_len, _ = x_prompt.shape
    n_tok = x_sample.shape[0]
    assert x_sample.shape[1] == 1 and seq_len % SEQ_BLOCK == 0 and n_tok % TOK_BATCH == 0
    wts = _prep_weights(w_in[0], conv_w[0], conv_b[0], b_gate[0], w_a2[0], b_a[0], g_mhead[0], g_ghead[0],
                        w_out[0], g_mix[0], g_mlp[0], w1[0], w2[0], g_ple[0], w_ple[0], w_pg[0], g_final)

    y_p, tail_p, caug_p, m_p, s_p = _seq_call(
        x_prompt.reshape(n_seq * seq_len, D_MODEL), p_prompt[0].reshape(n_seq * seq_len, D_PLE), wts,
        _mix_constants(SEQ_BLOCK), n_seq=n_seq, seq_len=seq_len)

    xs = x_sample.reshape(n_tok, D_MODEL)
    buf = state_conv[0]
    outs = _proj_tok_call(xs, (buf[:, 0], buf[:, 1], buf[:, 2]), wts)
    ps = dict(zip(_PROJ_NAMES, outs[:-1]))
    raw_s = outs[-1]
    hm_s, hg_s, c_s, n_s, m_s, s_s = _mix_tok_call(
        ps, state_mlstm_C[0], state_mlstm_n[0].reshape(n_tok, M_WIDTH), state_mlstm_m[0], state_gla_S[0])
    y_s = _post_tok_call(xs, p_sample[0].reshape(n_tok, D_PLE), hm_s, hg_s, ps["mo"], ps["gr"], wts)

    return (y_p.reshape(n_seq, seq_len, D_MODEL),
            y_s.reshape(n_tok, 1, D_MODEL),
            caug_p[None, :, :, :, :M_DH],
            caug_p[None, :, :, :, M_DH],
            m_p[None, :, :M_HEADS, 0],
            tail_p[None, :, SUBLANES - (CONV_W - 1):, :],
            s_p[None],
            c_s[None],
            n_s.reshape(1, n_tok, M_HEADS, M_DH),
            m_s[None],
            jnp.stack([buf[:, 1], buf[:, 2], raw_s], axis=1)[None],
            s_s[None])
```

```python
import functools

import jax
import jax.numpy as jnp
from jax import lax
from jax.experimental import pallas as pl
from jax.experimental.pallas import tpu as pltpu

D_MODEL = 1024
M_HEADS = 4
M_DH = 128
M_WIDTH = M_HEADS * M_DH
G_HEADS = 4
G_DK = 64
G_DV = 128
G_KW = G_HEADS * G_DK
G_VW = G_HEADS * G_DV
G_RANK = 16
G_TAU = 16.0
CONV_W = 4
QK_CONV = 2 * M_WIDTH
D_FF = 4 * D_MODEL
D_PLE = 256
CHUNK = 64
SUB = 16
EPS = 1e-6
IN_SIZES = (QK_CONV, M_WIDTH, M_WIDTH, M_HEADS, M_HEADS, G_KW, G_KW, G_VW, G_VW, G_RANK)

LANES = 128
SUBLANES = 8
VMEM_LIMIT = 60 * 1024 * 1024
SEQ_BLOCK = 256
TOK_BATCH = 16

F32 = jnp.float32
BF16 = jnp.bfloat16
NEG = -1e30


def _rms(x, g):
    return x * lax.rsqrt(jnp.mean(x * x, axis=-1, keepdims=True) + EPS) * g


def _log_sigmoid(x):
    return jnp.minimum(x, 0.0) - jnp.log(1.0 + jnp.exp(-jnp.abs(x)))


def _sigmoid(x):
    return 0.5 * jnp.tanh(0.5 * x) + 0.5


def _div_pow2(idx, d):
    assert d & (d - 1) == 0
    return lax.shift_right_logical(idx, d.bit_length() - 1)


def _dot(a, b):
    return jnp.dot(a, b, preferred_element_type=F32)


def _dot_nt(a, b):
    return lax.dot_general(a, b, (((1,), (1,)), ((), ())), preferred_element_type=F32)


def _split3(x):
    hi = x.astype(BF16)
    r1 = x - hi.astype(F32)
    mid = r1.astype(BF16)
    lo = (r1 - mid.astype(F32)).astype(BF16)
    return hi, mid, lo


def _cumsum_dot(tri, x):
    hi, mid, lo = _split3(x)
    return _dot(tri, hi) + _dot(tri, mid) + _dot(tri, lo)


def _spread_dot(x, onehot):
    hi, mid, lo = _split3(x)
    return _dot(hi, onehot) + _dot(mid, onehot) + _dot(lo, onehot)


def _interleave(pattern, **streams):
    by_letter = {name[0]: gen for name, gen in streams.items()}
    for letter in pattern:
        next(by_letter[letter], None)
    for gen in by_letter.values():
        _run(gen)


def _run(gen):
    for _ in gen:
        pass


def _proj_stages(h, w, sinks, conv_piece):
    width = M_WIDTH

    def put_q(raw):
        sinks["mq"][...] = conv_piece(raw, 0).astype(sinks["mq"].dtype)

    def put_k(raw):
        sinks["mk"][...] = conv_piece(raw, M_WIDTH) * (M_DH ** -0.5)

    def put_mv(raw):
        sinks["mv"][...] = raw.astype(sinks["mv"].dtype)

    def put_mo(raw):
        sinks["mo"][...] = raw

    def put_gqk(raw):
        sinks["gq"][...] = raw[:, :G_KW] * (G_DK ** -0.5)
        sinks["gk"][...] = raw[:, G_KW:]

    def put_gv(raw):
        sinks["gv"][...] = raw.astype(sinks["gv"].dtype)

    def put_gr(raw):
        sinks["gr"][...] = raw

    def put_small(small):
        g = small + w["b_small"][...]
        lane = lax.broadcasted_iota(jnp.int32, g.shape, 1)
        sinks["gates"][...] = jnp.where(lane < M_HEADS, g, _log_sigmoid(g))
        z = _dot(small.astype(BF16), w["w_a2p"][...]) + w["b_a"][...]
        sinks["la"][...] = _log_sigmoid(z) * (1.0 / G_TAU)

    epilogues = [put_small, put_q, put_k, put_mv, put_mo, put_gqk, put_gv, put_gr]
    assert w["w_big"].shape[1] == width * (len(epilogues) - 1)
    pending = (_dot(h, w["w_small"][...]), epilogues[0])
    yield
    for n, epi in enumerate(epilogues[1:]):
        cur = _dot(h, w["w_big"][:, n * width:(n + 1) * width])
        pending[1](pending[0])
        pending = (cur, epi)
        yield
    pending[1](pending[0])
    yield


def _seq_conv(cbuf_ref, tail_ref, convw_ref, convb_ref, tb):
    cw = convw_ref[...]

    def conv_piece(raw, lo):
        cols = slice(lo, lo + raw.shape[1])
        cbuf_ref[SUBLANES:SUBLANES + tb, cols] = raw
        y = convb_ref[:, cols] + raw * cw[3:4, cols]
        for j in range(CONV_W - 1):
            off = SUBLANES - (CONV_W - 1) + j
            y = y + cbuf_ref[off:off + tb, cols] * cw[j:j + 1, cols]
        last = cbuf_ref[tb:tb + SUBLANES, cols]
        tail_ref[0, :, cols] = last
        cbuf_ref[0:SUBLANES, cols] = last
        return y * _sigmoid(y)

    return conv_piece


def _tok_conv(b0_ref, b1_ref, b2_ref, raw_ref, convw_ref, convb_ref):
    cw = convw_ref[...]

    def conv_piece(raw, lo):
        cols = slice(lo, lo + raw.shape[1])
        raw_ref[:, cols] = raw
        y = (convb_ref[:, cols] + b0_ref[:, cols] * cw[0:1, cols] + b1_ref[:, cols] * cw[1:2, cols]
             + b2_ref[:, cols] * cw[2:3, cols] + raw * cw[3:4, cols])
        return y * _sigmoid(y)

    return conv_piece


_PROJ_NAMES = ("mq", "mk", "mv", "mo", "gq", "gk", "gv", "gr", "gates", "la")
_PROJ_WIDTH = dict(mq=M_WIDTH, mk=M_WIDTH, mv=M_WIDTH, mo=M_WIDTH, gq=G_KW, gk=G_KW, gv=G_VW, gr=G_VW,
                   gates=LANES, la=G_KW)
_PROJ_DTYPE = dict(mq=BF16, mk=F32, mv=BF16, mo=F32, gq=F32, gk=F32, gv=BF16, gr=F32, gates=F32, la=F32)
_PROJ_WEIGHTS = ("w_big", "w_small", "b_small", "w_a2p", "b_a")


def _proj_tok_kernel(x_ref, b0_ref, b1_ref, b2_ref, gmix_ref, wbig_ref, wsmall_ref, bsmall_ref, wa2_ref,
                     ba_ref, convw_ref, convb_ref, *out_refs):
    sinks = dict(zip(_PROJ_NAMES, out_refs[:-1]))
    raw_ref = out_refs[-1]
    w = dict(zip(_PROJ_WEIGHTS, (wbig_ref, wsmall_ref, bsmall_ref, wa2_ref, ba_ref)))
    h = _rms(x_ref[...], gmix_ref[...]).astype(BF16)
    _run(_proj_stages(h, w, sinks, _tok_conv(b0_ref, b1_ref, b2_ref, raw_ref, convw_ref, convb_ref)))


def _proj_tok_call(x2d, conv_rows, wts):
    n = x2d.shape[0]
    row = lambda width: pl.BlockSpec((n, width), lambda i: (0, 0))
    whole = pl.BlockSpec(memory_space=pltpu.VMEM)
    names = list(_PROJ_NAMES)
    return pl.pallas_call(
        _proj_tok_kernel,
        grid=(1,),
        in_specs=[row(D_MODEL)] + [row(QK_CONV)] * 3 + [whole] * 8,
        out_specs=[row(_PROJ_WIDTH[k]) for k in names] + [row(QK_CONV)],
        out_shape=[jax.ShapeDtypeStruct((n, _PROJ_WIDTH[k]), _PROJ_DTYPE[k]) for k in names]
        + [jax.ShapeDtypeStruct((n, QK_CONV), F32)],
        compiler_params=pltpu.CompilerParams(dimension_semantics=("arbitrary",), vmem_limit_bytes=VMEM_LIMIT),
        name="proj_tok",
    )(x2d, *conv_rows, wts["g_mix"], wts["w_big"], wts["w_small"], wts["b_small"], wts["w_a2p"], wts["b_a"],
      wts["conv_w"], wts["conv_b"])


def _mix_stages(src, tril_ref, tri_ref, wred_ref, caug_ref, m_ref, sbd_ref, hm_ref, hg_ref):
    ts = src["mq"].shape[0]
    n_chunks = ts // CHUNK
    gates = src["gates"][...]
    bcum = _cumsum_dot(tril_ref[...], gates)
    gq = src["gq"][...]
    gk = src["gk"][...]
    gv_ref = src["gv"]
    bc = _cumsum_dot(tri_ref[...], src["la"][...])
    gates_t = gates.T
    bcum_t = bcum.T
    bc_t = bc.T
    yield

    nb = ts // SUB
    half = SUB // 2
    assert half == SUBLANES
    q3 = gq.reshape(nb, SUB, G_KW)
    k3 = gk.reshape(nb, SUB, G_KW)
    bc3 = bc.reshape(nb, SUB, G_KW)
    tl = lax.broadcasted_iota(jnp.int32, (nb, SUB, G_KW), 1)
    q3u = gq.reshape(nb, 2, half, G_KW)[:, 1]
    bc3u = bc.reshape(nb, 2, half, G_KW)[:, 1]
    tlu = lax.broadcasted_iota(jnp.int32, (nb, half, G_KW), 1) + half

    def exact_pass(j, acc):
        acc_all, acc_upper = acc
        if j < half:
            arg = jnp.where(tl >= j, bc3 - bc3[:, j:j + 1, :], NEG)
            e = (q3 * k3[:, j:j + 1, :] * jnp.exp(arg)).reshape(ts, G_KW)
            return acc_all + _dot(e.astype(BF16), wred_ref[j]), acc_upper
        arg = jnp.where(tlu >= j, bc3u - bc3[:, j:j + 1, :], NEG)
        e = (q3u * k3[:, j:j + 1, :] * jnp.exp(arg)).reshape(ts // 2, G_KW)
        return acc_all, acc_upper + _dot(e.astype(BF16), wred_ref[j])

    krow = lax.broadcasted_iota(jnp.int32, (CHUNK, G_KW), 0)
    same_head_kk = (_div_pow2(lax.broadcasted_iota(jnp.int32, (G_KW, G_KW), 0), G_DK)
                    == _div_pow2(lax.broadcasted_iota(jnp.int32, (G_KW, G_KW), 1), G_DK))
    same_head_kv = (_div_pow2(lax.broadcasted_iota(jnp.int32, (G_KW, G_VW), 0), G_DK)
                    == _div_pow2(lax.broadcasted_iota(jnp.int32, (G_KW, G_VW), 1), G_DV))

    def cross_block_scores(c):
        lo = c * CHUNK
        bc_c = bc[lo:lo + CHUNK]
        k_c = gk[lo:lo + CHUNK]
        out = [jnp.zeros((SUB, G_KW), F32)]
        for i in range(1, CHUNK // SUB):
            r0 = lo + i * SUB
            r_i = bc[r0 - 1:r0, :]
            qi = (gq[r0:r0 + SUB] * jnp.exp(bc[r0:r0 + SUB] - r_i)).astype(BF16)
            ki = (k_c * jnp.exp(jnp.where(krow < i * SUB, r_i - bc_c, NEG))).astype(BF16)
            kbd = jnp.where(same_head_kk, jnp.concatenate([ki] * G_HEADS, axis=0), jnp.zeros((), BF16))
            out.append(_dot_nt(qi, kbd))
        return out

    def state_update_term(c):
        lo, hi = c * CHUNK, (c + 1) * CHUNK
        b_end = bc[hi - 1:hi, :]
        k_out_t = (gk[lo:hi] * jnp.exp(b_end - bc[lo:hi])).T.astype(BF16)
        rows = []
        for g in range(G_HEADS):
            blk = _dot(k_out_t[g * G_DK:(g + 1) * G_DK], gv_ref[lo:hi, g * G_DV:(g + 1) * G_DV])
            zero = jnp.zeros((G_DK, G_DV), F32)
            rows.append(jnp.concatenate([zero] * g + [blk] + [zero] * (G_HEADS - 1 - g), axis=1))
        return jnp.concatenate(rows, axis=0)

    adiag = (jnp.zeros((ts, G_KW), F32), jnp.zeros((ts // 2, G_KW), F32))
    per_stage = SUB // 4
    assert n_chunks == 4
    offs = []
    s_terms = []

    causal = (lax.broadcasted_iota(jnp.int32, (ts, ts), 0) >= lax.broadcasted_iota(jnp.int32, (ts, ts), 1))
    ones_col = (lax.broadcasted_iota(jnp.int32, (ts, LANES), 1) == 0).astype(BF16)
    heads = range(M_HEADS)
    hsl = [slice(hd * M_DH, (hd + 1) * M_DH) for hd in heads]
    b_col = [bcum[:, M_HEADS + hd:M_HEADS + hd + 1] for hd in heads]
    b_row = [bcum_t[M_HEADS + hd:M_HEADS + hd + 1, :] for hd in heads]
    i_col = [gates[:, hd:hd + 1] for hd in heads]
    i_row = [gates_t[hd:hd + 1, :] for hd in heads]
    m_prev = [m_ref[hd:hd + 1, 0:1] for hd in heads]
    q = [src["mq"][:, hsl[hd]] for hd in heads]
    k = [src["mk"][:, hsl[hd]] for hd in heads]
    vaug = [jnp.concatenate([src["mv"][:, hsl[hd]], ones_col], axis=1) for hd in heads]
    caug = [caug_ref[hd] for hd in heads]
    s_qk = [_dot_nt(q[hd], k[hd].astype(BF16)) for hd in heads]
    qc = [_dot(q[hd], caug[hd].astype(BF16)) for hd in heads]
    for j in range(0, per_stage):
        adiag = exact_pass(j, adiag)
    offs += cross_block_scores(0)
    s_terms.append(state_update_term(0))
    yield
    b_last = [b_col[hd][ts - 1:ts, :] for hd in heads]
    dec = [b_last[hd] - b_col[hd] + i_col[hd] for hd in heads]
    m_new = [jnp.maximum(b_last[hd] + m_prev[hd], jnp.max(dec[hd], axis=0, keepdims=True)) for hd in heads]
    kw_t = [(k[hd] * jnp.exp(dec[hd] - m_new[hd])).T.astype(BF16) for hd in heads]
    upd = [_dot(kw_t[hd], vaug[hd]) for hd in heads]
    for j in range(per_stage, 2 * per_stage):
        adiag = exact_pass(j, adiag)
    offs += cross_block_scores(1)
    s_terms.append(state_update_term(1))
    yield
    dmat = [jnp.where(causal, b_col[hd] - b_row[hd] + i_row[hd], -jnp.inf) for hd in heads]
    inter = [b_col[hd] + m_prev[hd] for hd in heads]
    m_tok = [jnp.maximum(inter[hd], jnp.max(dmat[hd], axis=1, keepdims=True)) for hd in heads]
    for j in range(2 * per_stage, 3 * per_stage):
        adiag = exact_pass(j, adiag)
    offs += cross_block_scores(2)
    s_terms.append(state_update_term(2))
    yield
    p = [(s_qk[hd] * jnp.exp(dmat[hd] - m_tok[hd])).astype(BF16) for hd in heads]
    pv = [_dot(p[hd], vaug[hd]) for hd in heads]
    for j in range(3 * per_stage, SUB):
        adiag = exact_pass(j, adiag)
    offs += cross_block_scores(3)
    s_terms.append(state_update_term(3))
    yield
    for hd in heads:
        tot = jnp.exp(inter[hd] - m_tok[hd]) * qc[hd] + pv[hd]
        den = tot[:, M_DH:M_DH + 1]
        hm_ref[:, hsl[hd]] = tot[:, :M_DH] / jnp.maximum(jnp.abs(den), jnp.exp(-m_tok[hd]))
        caug_ref[hd] = jnp.exp(b_last[hd] + m_prev[hd] - m_new[hd]) * caug[hd] + upd[hd]
        m_ref[hd:hd + 1, :] = jnp.broadcast_to(m_new[hd], (1, LANES))
    yield

    sub_of = lambda idx: lax.shift_right_logical(idx & (CHUNK - 1), SUB.bit_length() - 1)
    rowb = sub_of(lax.broadcasted_iota(jnp.int32, (ts, G_KW), 0))
    colb = sub_of(lax.broadcasted_iota(jnp.int32, (ts, G_KW), 1))
    acc_all, acc_upper = adiag
    upper = acc_upper.reshape(nb, half, G_KW)
    acc_all = acc_all + jnp.stack([jnp.zeros_like(upper), upper], axis=1).reshape(ts, G_KW)
    adiag = jnp.where(rowb == colb, acc_all, 0.0)
    intra = (jnp.concatenate(offs, axis=0) + adiag).astype(BF16)
    q_in = (gq * jnp.exp(bc)).astype(BF16)
    o_intra = []
    for c in range(n_chunks):
        lo, hi = c * CHUNK, (c + 1) * CHUNK
        vbd = jnp.where(same_head_kv, jnp.concatenate([gv_ref[lo:hi, :]] * G_HEADS, axis=0),
                        jnp.zeros((), BF16))
        o_intra.append(_dot(intra[lo:hi], vbd))
    sbd = sbd_ref[...]
    for c in range(n_chunks):
        lo, hi = c * CHUNK, (c + 1) * CHUNK
        hg_ref[lo:hi, :] = o_intra[c] + _dot(q_in[lo:hi], sbd.astype(BF16))
        dcol = jnp.exp(bc_t[:, hi - 1:hi])
        sbd = dcol * sbd + s_terms[c]
    sbd_ref[...] = sbd
    yield


def _head_norm(hv, n_heads, width):
    parts = []
    for hd in range(n_heads):
        seg = hv[:, hd * width:(hd + 1) * width]
        parts.append(seg * lax.rsqrt(jnp.mean(seg * seg, axis=-1, keepdims=True) + EPS))
    return jnp.concatenate(parts, axis=1)


def _gate_heads(hm, hg, mo, gr, gmh, ggh):
    hm = _head_norm(hm, M_HEADS, M_DH) * gmh * _sigmoid(mo)
    hg = _head_norm(hg, G_HEADS, G_DV) * ggh * (gr * _sigmoid(gr))
    return jnp.concatenate([hm, hg], axis=1).astype(BF16)


_POST_WEIGHTS = ("w_out", "g_mlp", "w1", "w2", "g_ple", "w_ple", "w_pg", "g_final")


def _post_stages(x, p, mixed, w, y_ref):
    x1 = x + _dot(mixed, w["w_out"][...])
    n1 = _rms(x1, w["g_mlp"][...]).astype(BF16)
    yield
    n_ff = 4
    ff = D_FF // n_ff
    acts = []
    for j in range(n_ff):
        u = _dot(n1, w["w1"][:, j * ff:(j + 1) * ff])
        acts.append(jnp.square(jnp.maximum(u, 0.0)).astype(BF16))
        yield
    act = jnp.concatenate(acts, axis=1)
    half = D_MODEL // 2
    mlp = []
    for j in range(2):
        mlp.append(_dot(act, w["w2"][:, j * half:(j + 1) * half]))
        yield
    x2 = x1 + jnp.concatenate(mlp, axis=1)
    ple = _dot(p.astype(BF16), w["w_ple"][...])
    n2 = _rms(x2, w["g_ple"][...]).astype(BF16)
    yield
    gate = _sigmoid(_dot(n2, w["w_pg"][...]))
    x3 = x2 + ple * gate
    y_ref[...] = _rms(x3, w["g_final"][...])
    yield


def _post_tok_kernel(x_ref, p_ref, hm_ref, hg_ref, mo_ref, gr_ref, gmh_ref, ggh_ref, wout_ref, gmlp_ref,
                     w1_ref, w2_ref, gple_ref, wple_ref, wpg_ref, gfin_ref, y_ref):
    mixed = _gate_heads(hm_ref[...], hg_ref[...], mo_ref[...], gr_ref[...], gmh_ref[...], ggh_ref[...])
    w = dict(zip(_POST_WEIGHTS, (wout_ref, gmlp_ref, w1_ref, w2_ref, gple_ref, wple_ref, wpg_ref, gfin_ref)))
    _run(_post_stages(x_ref[...], p_ref[...], mixed, w, y_ref))


def _post_tok_call(x2d, p2d, hm, hg, mo, gr, wts):
    n = x2d.shape[0]
    row = lambda width: pl.BlockSpec((n, width), lambda i: (0, 0))
    whole = pl.BlockSpec(memory_space=pltpu.VMEM)
    return pl.pallas_call(
        _post_tok_kernel,
        grid=(1,),
        in_specs=[row(D_MODEL), row(D_PLE), row(M_WIDTH), row(G_VW), row(M_WIDTH), row(G_VW)] + [whole] * 10,
        out_specs=row(D_MODEL),
        out_shape=jax.ShapeDtypeStruct((n, D_MODEL), F32),
        compiler_params=pltpu.CompilerParams(dimension_semantics=("arbitrary",), vmem_limit_bytes=VMEM_LIMIT),
        name="post_tok",
    )(x2d, p2d, hm, hg, mo, gr, wts["g_mhead"], wts["g_ghead"], *[wts[k] for k in _POST_WEIGHTS])


def _seq_kernel(xa_ref, xb_ref, pb_ref, gmix_ref, wbig_ref, wsmall_ref, bsmall_ref, wa2_ref, ba_ref,
                convw_ref, convb_ref, tril_ref, tri_ref, wred_ref, gmh_ref, ggh_ref,
                wout_ref, gmlp_ref, w1_ref, w2_ref, gple_ref, wple_ref, wpg_ref, gfin_ref,
                y_ref, tail_ref, cout_ref, mout_ref, sout_ref,
                cbuf_ref, caug_ref, m_ref, sbd_ref, mixed_ref,
                mq_s, mk_s, mv_s, mo_s, gq_s, gk_s, gv_s, gr_s, gates_s, la_s, hm_s, hg_s,
                *, steps_per_seq, n_blocks):
    tb = xa_ref.shape[0]
    s = pl.program_id(0)
    r = lax.rem(jnp.minimum(s, n_blocks - 1), steps_per_seq)

    @pl.when(s == 0)
    def _():
        mixed_ref[...] = jnp.zeros(mixed_ref.shape, BF16)

    @pl.when(r == 0)
    def _():
        cbuf_ref[0:SUBLANES, :] = jnp.zeros((SUBLANES, QK_CONV), F32)
        caug_ref[...] = jnp.zeros(caug_ref.shape, F32)
        m_ref[...] = jnp.zeros(m_ref.shape, F32)
        sbd_ref[...] = jnp.zeros(sbd_ref.shape, F32)

    slot = lax.rem(s, 2)
    src = dict(zip(_PROJ_NAMES, (mq_s, mk_s, mv_s, mo_s, gq_s, gk_s, gv_s, gr_s, gates_s, la_s)))
    w_in = dict(zip(_PROJ_WEIGHTS, (wbig_ref, wsmall_ref, bsmall_ref, wa2_ref, ba_ref)))
    w_post = dict(zip(_POST_WEIGHTS, (wout_ref, gmlp_ref, w1_ref, w2_ref, gple_ref, wple_ref, wpg_ref, gfin_ref)))

    def front():
        h = _rms(xa_ref[...], gmix_ref[...]).astype(BF16)
        yield from _proj_stages(h, w_in, src, _seq_conv(cbuf_ref, tail_ref, convw_ref, convb_ref, tb))
        yield from _mix_stages(src, tril_ref, tri_ref, wred_ref, caug_ref, m_ref, sbd_ref, hm_s, hg_s)
        mixed_ref[slot] = _gate_heads(hm_s[...], hg_s[...], mo_s[...], gr_s[...], gmh_ref[...], ggh_ref[...])
        yield

    def back():
        yield from _post_stages(xb_ref[...], pb_ref[...], mixed_ref[1 - slot], w_post, y_ref)

    _interleave("bfff" "bfff" "bfff" "bff" "bf" "bf" "bf" "bf" "bff", front=front(), back=back())

    @pl.when(jnp.logical_and(s < n_blocks, r == steps_per_seq - 1))
    def _():
        cout_ref[0] = caug_ref[...]
        mout_ref[0] = m_ref[...]
        for g in range(G_HEADS):
            sout_ref[0, g] = sbd_ref[g * G_DK:(g + 1) * G_DK, g * G_DV:(g + 1) * G_DV]


def _seq_call(x2d, p2d, wts, consts, *, n_seq, seq_len):
    tb = SEQ_BLOCK
    steps_per_seq = seq_len // tb
    n_blocks = n_seq * steps_per_seq
    tril, tri, wred = consts
    front_blk = lambda s: jnp.minimum(s, n_blocks - 1)
    back_blk = lambda s: jnp.maximum(s - 1, 0)
    whole = pl.BlockSpec(memory_space=pltpu.VMEM)
    per_seq = lambda shape: pl.BlockSpec((1,) + shape, lambda s: (front_blk(s) // steps_per_seq,) + (0,) * len(shape))
    scratch = [
        pltpu.VMEM((tb + SUBLANES, QK_CONV), F32),
        pltpu.VMEM((M_HEADS, M_DH, 2 * M_DH), F32),
        pltpu.VMEM((SUBLANES, LANES), F32),
        pltpu.VMEM((G_KW, G_VW), F32),
        pltpu.VMEM((2, tb, D_MODEL), BF16),
    ] + [pltpu.VMEM((tb, _PROJ_WIDTH[k]), _PROJ_DTYPE[k]) for k in _PROJ_NAMES] + [
        pltpu.VMEM((tb, M_WIDTH), F32), pltpu.VMEM((tb, G_VW), F32)]
    return pl.pallas_call(
        functools.partial(_seq_kernel, steps_per_seq=steps_per_seq, n_blocks=n_blocks),
        grid=(n_blocks + 1,),
        in_specs=[pl.BlockSpec((tb, D_MODEL), lambda s: (front_blk(s), 0)),
                  pl.BlockSpec((tb, D_MODEL), lambda s: (back_blk(s), 0)),
                  pl.BlockSpec((tb, D_PLE), lambda s: (back_blk(s), 0))] + [whole] * 21,
        out_specs=[pl.BlockSpec((tb, D_MODEL), lambda s: (back_blk(s), 0)),
                   per_seq((SUBLANES, QK_CONV)), per_seq((M_HEADS, M_DH, 2 * M_DH)),
                   per_seq((SUBLANES, LANES)), per_seq((G_HEADS, G_DK, G_DV))],
        out_shape=[jax.ShapeDtypeStruct((n_blocks * tb, D_MODEL), F32),
                   jax.ShapeDtypeStruct((n_seq, SUBLANES, QK_CONV), F32),
                   jax.ShapeDtypeStruct((n_seq, M_HEADS, M_DH, 2 * M_DH), F32),
                   jax.ShapeDtypeStruct((n_seq, SUBLANES, LANES), F32),
                   jax.ShapeDtypeStruct((n_seq, G_HEADS, G_DK, G_DV), F32)],
        scratch_shapes=scratch,
        compiler_params=pltpu.CompilerParams(dimension_semantics=("arbitrary",), vmem_limit_bytes=VMEM_LIMIT),
        name="seq_fused",
    )(x2d, x2d, p2d, wts["g_mix"], wts["w_big"], wts["w_small"], wts["b_small"], wts["w_a2p"], wts["b_a"],
      wts["conv_w"], wts["conv_b"], tril, tri, wred, wts["g_mhead"], wts["g_ghead"],
      *[wts[k] for k in _POST_WEIGHTS])


def _mix_tok_kernel(mq_ref, mk_ref, mv_ref, gates_ref, gq_ref, gk_ref, gv_ref, la_ref,
                    c_ref, n_ref, m_ref, s_ref,
                    hm_ref, hg_ref, cn_ref, nn_ref, mn_ref, sn_ref):
    bb = mq_ref.shape[0]
    wide = bb * LANES
    diag = (lax.broadcasted_iota(jnp.int32, (bb, wide), 0)
            == _div_pow2(lax.broadcasted_iota(jnp.int32, (bb, wide), 1), LANES))
    block_ones = diag.astype(BF16)
    ones_rows = jnp.ones((M_DH, bb), BF16)

    def block_diag(rows):
        return jnp.where(diag, jnp.concatenate([rows] * bb, axis=1), jnp.zeros((), rows.dtype))

    def block_diag_f32(col):
        return jnp.where(diag, col, 0.0)

    def own_block(x):
        x = jnp.where(diag, x, 0.0)
        out = x[:, 0:LANES]
        for b in range(1, bb):
            out = out + x[:, b * LANES:(b + 1) * LANES]
        return out

    gates = gates_ref[...]
    m_all = m_ref[...]
    for hd in range(M_HEADS):
        cs_, ce_ = hd * M_DH, (hd + 1) * M_DH
        q_bf = mq_ref[:, cs_:ce_]
        q = q_bf.astype(F32)
        k = mk_ref[:, cs_:ce_]
        v_bf = mv_ref[:, cs_:ce_]
        v = v_bf.astype(F32)
        n_prev = n_ref[:, cs_:ce_]
        ig = gates[:, hd:hd + 1]
        lf = gates[:, M_HEADS + hd:M_HEADS + hd + 1]
        m_prev = m_all[:, hd:hd + 1]
        m_new = jnp.maximum(lf + m_prev, ig)
        scale = jnp.exp(lf + m_prev - m_new)
        wk = jnp.exp(ig - m_new)
        s_qk = jnp.sum(q * k, axis=1, keepdims=True) * wk
        den = scale * jnp.sum(q * n_prev, axis=1, keepdims=True) + s_qk
        inv = 1.0 / jnp.maximum(jnp.abs(den), jnp.exp(-m_new))
        nn_ref[:, cs_:ce_] = scale * n_prev + wk * k
        mn_ref[:, hd:hd + 1] = m_new
        c_prev = [c_ref[b, hd] for b in range(bb)]
        qc = own_block(_dot(q_bf, jnp.concatenate([c.astype(BF16) for c in c_prev], axis=1)))
        hm_ref[:, cs_:ce_] = (scale * qc + s_qk * v) * inv
        outer = _dot((k * wk).T.astype(BF16), block_diag(v_bf))
        scale_w = _cumsum_dot(ones_rows, block_diag_f32(scale))
        for b in range(bb):
            blk = slice(b * M_DH, (b + 1) * M_DH)
            cn_ref[b, hd] = scale_w[:, blk] * c_prev[b] + outer[:, blk]
    la = la_ref[...]
    dec = jnp.exp(la)
    gq = gq_ref[...]
    gk = gk_ref[...]
    for g in range(G_HEADS):
        ks_, ke_ = g * G_DK, (g + 1) * G_DK
        vs_, ve_ = g * G_DV, (g + 1) * G_DV
        q = gq[:, ks_:ke_]
        k = gk[:, ks_:ke_]
        v_bf = gv_ref[:, vs_:ve_]
        a = jnp.sum(q * k, axis=1, keepdims=True)
        s_prev = [s_ref[b, g] for b in range(bb)]
        o = own_block(_dot((q * dec[:, ks_:ke_]).astype(BF16),
                           jnp.concatenate([s.astype(BF16) for s in s_prev], axis=1)))
        hg_ref[:, vs_:ve_] = o + a * v_bf.astype(F32)
        outer = _dot(k.T.astype(BF16), block_diag(v_bf))
        dec_w = _spread_dot(dec[:, ks_:ke_].T, block_ones)
        for b in range(bb):
            blk = slice(b * G_DV, (b + 1) * G_DV)
            sn_ref[b, g] = dec_w[:, blk] * s_prev[b] + outer[:, blk]


def _mix_tok_call(p, c0, n0, m0, s0):
    bb = TOK_BATCH
    n = c0.shape[0]
    row = lambda w: pl.BlockSpec((bb, w), lambda i: (i, 0))
    st4 = lambda a, b_: pl.BlockSpec((bb, M_HEADS, a, b_), lambda i: (i, 0, 0, 0))
    return pl.pallas_call(
        _mix_tok_kernel,
        grid=(n // bb,),
        in_specs=[row(M_WIDTH), row(M_WIDTH), row(M_WIDTH), row(LANES), row(G_KW), row(G_KW), row(G_VW),
                  row(G_KW), st4(M_DH, M_DH), row(M_WIDTH), row(M_HEADS), st4(G_DK, G_DV)],
        out_specs=[row(M_WIDTH), row(G_VW), st4(M_DH, M_DH), row(M_WIDTH), row(M_HEADS), st4(G_DK, G_DV)],
        out_shape=[jax.ShapeDtypeStruct((n, M_WIDTH), F32), jax.ShapeDtypeStruct((n, G_VW), F32),
                   jax.ShapeDtypeStruct(c0.shape, F32), jax.ShapeDtypeStruct((n, M_WIDTH), F32),
                   jax.ShapeDtypeStruct((n, M_HEADS), F32), jax.ShapeDtypeStruct(s0.shape, F32)],
        compiler_params=pltpu.CompilerParams(dimension_semantics=("arbitrary",), vmem_limit_bytes=VMEM_LIMIT),
        name="mix_tok",
    )(p["mq"], p["mk"], p["mv"], p["gates"], p["gq"], p["gk"], p["gv"], p["la"], c0, n0, m0, s0)


def _prep_weights(w_in, conv_w, conv_b, b_gate, w_a2, b_a, g_mhead, g_ghead, w_out, g_mix, g_mlp, w1, w2,
                  g_ple, w_ple, w_pg, g_final):
    offs = [0]
    for s in IN_SIZES:
        offs.append(offs[-1] + s)
    gates_lo, gates_hi = offs[3], offs[5]
    ga_lo = offs[9]
    w_big = jnp.concatenate([w_in[:, :gates_lo], w_in[:, gates_hi:ga_lo]], axis=1).astype(BF16)
    n_small = 2 * M_HEADS + G_RANK
    w_small = jnp.concatenate([w_in[:, gates_lo:gates_hi], w_in[:, ga_lo:],
                               jnp.zeros((D_MODEL, LANES - n_small), F32)], axis=1).astype(BF16)
    b_small = jnp.concatenate([b_gate, jnp.zeros((LANES - 2 * M_HEADS,), F32)])[None]
    w_a2p = jnp.concatenate([jnp.zeros((2 * M_HEADS, G_KW), F32), w_a2,
                             jnp.zeros((LANES - n_small, G_KW), F32)], axis=0).astype(BF16)
    return dict(
        w_big=w_big, w_small=w_small, b_small=b_small, w_a2p=w_a2p, b_a=b_a[None],
        conv_w=conv_w, conv_b=conv_b[None], g_mix=g_mix[None], g_mhead=g_mhead[None], g_ghead=g_ghead[None],
        w_out=w_out.astype(BF16), g_mlp=g_mlp[None], w1=w1.astype(BF16), w2=w2.astype(BF16),
        g_ple=g_ple[None], w_ple=w_ple.astype(BF16), w_pg=w_pg.astype(BF16), g_final=g_final[None])


def _mix_constants(ts):
    t = jnp.arange(ts)
    tril = (t[None, :] <= t[:, None]).astype(BF16)
    tri = ((t[:, None] // CHUNK == t[None, :] // CHUNK) & (t[None, :] <= t[:, None])).astype(BF16)
    rr = jnp.arange(G_KW)
    wred = ((rr[None, :, None] // G_DK == rr[None, None, :] // G_DK)
            & (rr[None, None, :] % SUB == jnp.arange(SUB)[:, None, None])).astype(BF16)
    return tril, tri, wred


def kernel(x_prompt, x_sample, p_prompt, p_sample, state_mlstm_C, state_mlstm_n, state_mlstm_m, state_conv,
           state_gla_S, w_in, conv_w, conv_b, b_gate, w_a2, b_a, g_mhead, g_ghead, w_out, g_mix, g_mlp, w1,
           w2, g_ple, w_ple, w_pg, g_final):
    assert w_in.shape[0] == 1, "single-layer trunk"
    n_seq, seq_len, _ = x_prompt.shape
    n_tok = x_sample.shape[0]
    assert x_sample.shape[1] == 1 and seq_len % SEQ_BLOCK == 0 and n_tok % TOK_BATCH == 0
    wts = _prep_weights(w_in[0], conv_w[0], conv_b[0], b_gate[0], w_a2[0], b_a[0], g_mhead[0], g_ghead[0],
                        w_out[0], g_mix[0], g_mlp[0], w1[0], w2[0], g_ple[0], w_ple[0], w_pg[0], g_final)

    y_p, tail_p, caug_p, m_p, s_p = _seq_call(
        x_prompt.reshape(n_seq * seq_len, D_MODEL), p_prompt[0].reshape(n_seq * seq_len, D_PLE), wts,
        _mix_constants(SEQ_BLOCK), n_seq=n_seq, seq_len=seq_len)

    xs = x_sample.reshape(n_tok, D_MODEL)
    buf = state_conv[0]
    outs = _proj_tok_call(xs, (buf[:, 0], buf[:, 1], buf[:, 2]), wts)
    ps = dict(zip(_PROJ_NAMES, outs[:-1]))
    raw_s = outs[-1]
    hm_s, hg_s, c_s, n_s, m_s, s_s = _mix_tok_call(
        ps, state_mlstm_C[0], state_mlstm_n[0].reshape(n_tok, M_WIDTH), state_mlstm_m[0], state_gla_S[0])
    y_s = _post_tok_call(xs, p_sample[0].reshape(n_tok, D_PLE), hm_s, hg_s, ps["mo"], ps["gr"], wts)

    return (y_p.reshape(n_seq, seq_len, D_MODEL),
            y_s.reshape(n_tok, 1, D_MODEL),
            caug_p[None, :, :, :, :M_DH],
            caug_p[None, :, :, :, M_DH],
            m_p[None, :, :M_HEADS, 0],
            tail_p[None, :, SUBLANES - (CONV_W - 1):, :],
            s_p[None],
            c_s[None],
            n_s.reshape(1, n_tok, M_HEADS, M_DH),
            m_s[None],
            jnp.stack([buf[:, 1], buf[:, 2], raw_s], axis=1)[None],
            s_s[None])
```

```python
import functools

import jax
import jax.numpy as jnp
from jax import lax
from jax.experimental import pallas as pl
from jax.experimental.pallas import tpu as pltpu

D_MODEL = 1024
M_HEADS = 4
M_DH = 128
M_WIDTH = M_HEADS * M_DH
G_HEADS = 4
G_DK = 64
G_DV = 128
G_KW = G_HEADS * G_DK
G_VW = G_HEADS * G_DV
G_RANK = 16
G_TAU = 16.0
CONV_W = 4
QK_CONV = 2 * M_WIDTH
D_FF = 4 * D_MODEL
D_PLE = 256
CHUNK = 64
SUB = 16
EPS = 1e-6
IN_SIZES = (QK_CONV, M_WIDTH, M_WIDTH, M_HEADS, M_HEADS, G_KW, G_KW, G_VW, G_VW, G_RANK)

LANES = 128
SUBLANES = 8
VMEM_LIMIT = 60 * 1024 * 1024
SEQ_BLOCK = 256
TOK_BATCH = 16

F32 = jnp.float32
BF16 = jnp.bfloat16
NEG = -1e30


def _rms(x, g):
    return x * lax.rsqrt(jnp.mean(x * x, axis=-1, keepdims=True) + EPS) * g


def _log_sigmoid(x):
    return jnp.minimum(x, 0.0) - jnp.log(1.0 + jnp.exp(-jnp.abs(x)))


def _sigmoid(x):
    return 0.5 * jnp.tanh(0.5 * x) + 0.5


def _div_pow2(idx, d):
    assert d & (d - 1) == 0
    return lax.shift_right_logical(idx, d.bit_length() - 1)


def _dot(a, b):
    return jnp.dot(a, b, preferred_element_type=F32)


def _dot_nt(a, b):
    return lax.dot_general(a, b, (((1,), (1,)), ((), ())), preferred_element_type=F32)


def _split3(x):
    hi = x.astype(BF16)
    r1 = x - hi.astype(F32)
    mid = r1.astype(BF16)
    lo = (r1 - mid.astype(F32)).astype(BF16)
    return hi, mid, lo


def _cumsum_dot(tri, x):
    hi, mid, lo = _split3(x)
    return _dot(tri, hi) + _dot(tri, mid) + _dot(tri, lo)


def _spread_dot(x, onehot):
    hi, mid, lo = _split3(x)
    return _dot(hi, onehot) + _dot(mid, onehot) + _dot(lo, onehot)


def _interleave(pattern, **streams):
    by_letter = {name[0]: gen for name, gen in streams.items()}
    for letter in pattern:
        next(by_letter[letter], None)
    for gen in by_letter.values():
        _run(gen)


def _run(gen):
    for _ in gen:
        pass


def _proj_stages(h, w, sinks, conv_piece):
    width = M_WIDTH

    def put_q(raw):
        sinks["mq"][...] = conv_piece(raw, 0).astype(sinks["mq"].dtype)

    def put_k(raw):
        sinks["mk"][...] = conv_piece(raw, M_WIDTH) * (M_DH ** -0.5)

    def put_mv(raw):
        sinks["mv"][...] = raw.astype(sinks["mv"].dtype)

    def put_mo(raw):
        sinks["mo"][...] = raw

    def put_gqk(raw):
        sinks["gq"][...] = raw[:, :G_KW] * (G_DK ** -0.5)
        sinks["gk"][...] = raw[:, G_KW:]

    def put_gv(raw):
        sinks["gv"][...] = raw.astype(sinks["gv"].dtype)

    def put_gr(raw):
        sinks["gr"][...] = raw

    def put_small(small):
        g = small + w["b_small"][...]
        lane = lax.broadcasted_iota(jnp.int32, g.shape, 1)
        sinks["gates"][...] = jnp.where(lane < M_HEADS, g, _log_sigmoid(g))
        z = _dot(small.astype(BF16), w["w_a2p"][...]) + w["b_a"][...]
        sinks["la"][...] = _log_sigmoid(z) * (1.0 / G_TAU)

    epilogues = [put_small, put_q, put_k, put_mv, put_mo, put_gqk, put_gv, put_gr]
    assert w["w_big"].shape[1] == width * (len(epilogues) - 1)
    pending = (_dot(h, w["w_small"][...]), epilogues[0])
    yield
    for n, epi in enumerate(epilogues[1:]):
        cur = _dot(h, w["w_big"][:, n * width:(n + 1) * width])
        pending[1](pending[0])
        pending = (cur, epi)
        yield
    pending[1](pending[0])
    yield


def _seq_conv(cbuf_ref, tail_ref, convw_ref, convb_ref, tb):
    cw = convw_ref[...]

    def conv_piece(raw, lo):
        cols = slice(lo, lo + raw.shape[1])
        cbuf_ref[SUBLANES:SUBLANES + tb, cols] = raw
        y = convb_ref[:, cols] + raw * cw[3:4, cols]
        for j in range(CONV_W - 1):
            off = SUBLANES - (CONV_W - 1) + j
            y = y + cbuf_ref[off:off + tb, cols] * cw[j:j + 1, cols]
        last = cbuf_ref[tb:tb + SUBLANES, cols]
        tail_ref[0, :, cols] = last
        cbuf_ref[0:SUBLANES, cols] = last
        return y * _sigmoid(y)

    return conv_piece


def _tok_conv(b0_ref, b1_ref, b2_ref, raw_ref, convw_ref, convb_ref):
    cw = convw_ref[...]

    def conv_piece(raw, lo):
        cols = slice(lo, lo + raw.shape[1])
        raw_ref[:, cols] = raw
        y = (convb_ref[:, cols] + b0_ref[:, cols] * cw[0:1, cols] + b1_ref[:, cols] * cw[1:2, cols]
             + b2_ref[:, cols] * cw[2:3, cols] + raw * cw[3:4, cols])
        return y * _sigmoid(y)

    return conv_piece


_PROJ_NAMES = ("mq", "mk", "mv", "mo", "gq", "gk", "gv", "gr", "gates", "la")
_PROJ_WIDTH = dict(mq=M_WIDTH, mk=M_WIDTH, mv=M_WIDTH, mo=M_WIDTH, gq=G_KW, gk=G_KW, gv=G_VW, gr=G_VW,
                   gates=LANES, la=G_KW)
_PROJ_DTYPE = dict(mq=BF16, mk=F32, mv=BF16, mo=F32, gq=F32, gk=F32, gv=BF16, gr=F32, gates=F32, la=F32)
_PROJ_WEIGHTS = ("w_big", "w_small", "b_small", "w_a2p", "b_a")


def _proj_tok_kernel(x_ref, b0_ref, b1_ref, b2_ref, gmix_ref, wbig_ref, wsmall_ref, bsmall_ref, wa2_ref,
                     ba_ref, convw_ref, convb_ref, *out_refs):
    sinks = dict(zip(_PROJ_NAMES, out_refs[:-1]))
    raw_ref = out_refs[-1]
    w = dict(zip(_PROJ_WEIGHTS, (wbig_ref, wsmall_ref, bsmall_ref, wa2_ref, ba_ref)))
    h = _rms(x_ref[...], gmix_ref[...]).astype(BF16)
    _run(_proj_stages(h, w, sinks, _tok_conv(b0_ref, b1_ref, b2_ref, raw_ref, convw_ref, convb_ref)))


def _proj_tok_call(x2d, conv_rows, wts):
    n = x2d.shape[0]
    row = lambda width: pl.BlockSpec((n, width), lambda i: (0, 0))
    whole = pl.BlockSpec(memory_space=pltpu.VMEM)
    names = list(_PROJ_NAMES)
    return pl.pallas_call(
        _proj_tok_kernel,
        grid=(1,),
        in_specs=[row(D_MODEL)] + [row(QK_CONV)] * 3 + [whole] * 8,
        out_specs=[row(_PROJ_WIDTH[k]) for k in names] + [row(QK_CONV)],
        out_shape=[jax.ShapeDtypeStruct((n, _PROJ_WIDTH[k]), _PROJ_DTYPE[k]) for k in names]
        + [jax.ShapeDtypeStruct((n, QK_CONV), F32)],
        compiler_params=pltpu.CompilerParams(dimension_semantics=("arbitrary",), vmem_limit_bytes=VMEM_LIMIT),
        name="proj_tok",
    )(x2d, *conv_rows, wts["g_mix"], wts["w_big"], wts["w_small"], wts["b_small"], wts["w_a2p"], wts["b_a"],
      wts["conv_w"], wts["conv_b"])


def _mix_stages(src, tril_ref, tri_ref, wred_ref, caug_ref, m_ref, sbd_ref, hm_ref, hg_ref):
    ts = src["mq"].shape[0]
    n_chunks = ts // CHUNK
    gates = src["gates"][...]
    bcum = _cumsum_dot(tril_ref[...], gates)
    gq = src["gq"][...]
    gk = src["gk"][...]
    gv_ref = src["gv"]
    bc = _cumsum_dot(tri_ref[...], src["la"][...])
    gates_t = gates.T
    bcum_t = bcum.T
    bc_t = bc.T
    yield

    nb = ts // SUB
    half = SUB // 2
    assert half == SUBLANES
    q3 = gq.reshape(nb, SUB, G_KW)
    k3 = gk.reshape(nb, SUB, G_KW)
    bc3 = bc.reshape(nb, SUB, G_KW)
    tl = lax.broadcasted_iota(jnp.int32, (nb, SUB, G_KW), 1)
    q3u = gq.reshape(nb, 2, half, G_KW)[:, 1]
    bc3u = bc.reshape(nb, 2, half, G_KW)[:, 1]
    tlu = lax.broadcasted_iota(jnp.int32, (nb, half, G_KW), 1) + half

    def exact_pass(j, acc):
        acc_all, acc_upper = acc
        if j < half:
            arg = jnp.where(tl >= j, bc3 - bc3[:, j:j + 1, :], NEG)
            e = (q3 * k3[:, j:j + 1, :] * jnp.exp(arg)).reshape(ts, G_KW)
            return acc_all + _dot(e.astype(BF16), wred_ref[j]), acc_upper
        arg = jnp.where(tlu >= j, bc3u - bc3[:, j:j + 1, :], NEG)
        e = (q3u * k3[:, j:j + 1, :] * jnp.exp(arg)).reshape(ts // 2, G_KW)
        return acc_all, acc_upper + _dot(e.astype(BF16), wred_ref[j])

    krow = lax.broadcasted_iota(jnp.int32, (CHUNK, G_KW), 0)
    same_head_kk = (_div_pow2(lax.broadcasted_iota(jnp.int32, (G_KW, G_KW), 0), G_DK)
                    == _div_pow2(lax.broadcasted_iota(jnp.int32, (G_KW, G_KW), 1), G_DK))
    same_head_kv = (_div_pow2(lax.broadcasted_iota(jnp.int32, (G_KW, G_VW), 0), G_DK)
                    == _div_pow2(lax.broadcasted_iota(jnp.int32, (G_KW, G_VW), 1), G_DV))

    def cross_block_scores(c):
        lo = c * CHUNK
        bc_c = bc[lo:lo + CHUNK]
        k_c = gk[lo:lo + CHUNK]
        out = [jnp.zeros((SUB, G_KW), F32)]
        for i in range(1, CHUNK // SUB):
            r0 = lo + i * SUB
            r_i = bc[r0 - 1:r0, :]
            qi = (gq[r0:r0 + SUB] * jnp.exp(bc[r0:r0 + SUB] - r_i)).astype(BF16)
            ki = (k_c * jnp.exp(jnp.where(krow < i * SUB, r_i - bc_c, NEG))).astype(BF16)
            kbd = jnp.where(same_head_kk, jnp.concatenate([ki] * G_HEADS, axis=0), jnp.zeros((), BF16))
            out.append(_dot_nt(qi, kbd))
        return out

    def state_update_term(c):
        lo, hi = c * CHUNK, (c + 1) * CHUNK
        b_end = bc[hi - 1:hi, :]
        k_out_t = (gk[lo:hi] * jnp.exp(b_end - bc[lo:hi])).T.astype(BF16)
        rows = []
        for g in range(G_HEADS):
            blk = _dot(k_out_t[g * G_DK:(g + 1) * G_DK], gv_ref[lo:hi, g * G_DV:(g + 1) * G_DV])
            zero = jnp.zeros((G_DK, G_DV), F32)
            rows.append(jnp.concatenate([zero] * g + [blk] + [zero] * (G_HEADS - 1 - g), axis=1))
        return jnp.concatenate(rows, axis=0)

    adiag = (jnp.zeros((ts, G_KW), F32), jnp.zeros((ts // 2, G_KW), F32))
    per_stage = SUB // 4
    assert n_chunks == 4
    offs = []
    s_terms = []

    causal = (lax.broadcasted_iota(jnp.int32, (ts, ts), 0) >= lax.broadcasted_iota(jnp.int32, (ts, ts), 1))
    ones_col = (lax.broadcasted_iota(jnp.int32, (ts, LANES), 1) == 0).astype(BF16)
    heads = range(M_HEADS)
    hsl = [slice(hd * M_DH, (hd + 1) * M_DH) for hd in heads]
    b_col = [bcum[:, M_HEADS + hd:M_HEADS + hd + 1] for hd in heads]
    b_row = [bcum_t[M_HEADS + hd:M_HEADS + hd + 1, :] for hd in heads]
    i_col = [gates[:, hd:hd + 1] for hd in heads]
    i_row = [gates_t[hd:hd + 1, :] for hd in heads]
    m_prev = [m_ref[hd:hd + 1, 0:1] for hd in heads]
    q = [src["mq"][:, hsl[hd]] for hd in heads]
    k = [src["mk"][:, hsl[hd]] for hd in heads]
    vaug = [jnp.concatenate([src["mv"][:, hsl[hd]], ones_col], axis=1) for hd in heads]
    caug = [caug_ref[hd] for hd in heads]
    s_qk = [_dot_nt(q[hd], k[hd].astype(BF16)) for hd in heads]
    qc = [_dot(q[hd], caug[hd].astype(BF16)) for hd in heads]
    for j in range(0, per_stage):
        adiag = exact_pass(j, adiag)
    offs += cross_block_scores(0)
    s_terms.append(state_update_term(0))
    yield
    b_last = [b_col[hd][ts - 1:ts, :] for hd in heads]
    dec = [b_last[hd] - b_col[hd] + i_col[hd] for hd in heads]
    m_new = [jnp.maximum(b_last[hd] + m_prev[hd], jnp.max(dec[hd], axis=0, keepdims=True)) for hd in heads]
    kw_t = [(k[hd] * jnp.exp(dec[hd] - m_new[hd])).T.astype(BF16) for hd in heads]
    upd = [_dot(kw_t[hd], vaug[hd]) for hd in heads]
    for j in range(per_stage, 2 * per_stage):
        adiag = exact_pass(j, adiag)
    offs += cross_block_scores(1)
    s_terms.append(state_update_term(1))
    yield
    dmat = [jnp.where(causal, b_col[hd] - b_row[hd] + i_row[hd], -jnp.inf) for hd in heads]
    inter = [b_col[hd] + m_prev[hd] for hd in heads]
    m_tok = [jnp.maximum(inter[hd], jnp.max(dmat[hd], axis=1, keepdims=True)) for hd in heads]
    for j in range(2 * per_stage, 3 * per_stage):
        adiag = exact_pass(j, adiag)
    offs += cross_block_scores(2)
    s_terms.append(state_update_term(2))
    yield
    p = [(s_qk[hd] * jnp.exp(dmat[hd] - m_tok[hd])).astype(BF16) for hd in heads]
    pv = [_dot(p[hd], vaug[hd]) for hd in heads]
    for j in range(3 * per_stage, SUB):
        adiag = exact_pass(j, adiag)
    offs += cross_block_scores(3)
    s_terms.append(state_update_term(3))
    yield
    for hd in heads:
        tot = jnp.exp(inter[hd] - m_tok[hd]) * qc[hd] + pv[hd]
        den = tot[:, M_DH:M_DH + 1]
        hm_ref[:, hsl[hd]] = tot[:, :M_DH] / jnp.maximum(jnp.abs(den), jnp.exp(-m_tok[hd]))
        caug_ref[hd] = jnp.exp(b_last[hd] + m_prev[hd] - m_new[hd]) * caug[hd] + upd[hd]
        m_ref[hd:hd + 1, :] = jnp.broadcast_to(m_new[hd], (1, LANES))
    yield

    sub_of = lambda idx: lax.shift_right_logical(idx & (CHUNK - 1), SUB.bit_length() - 1)
    rowb = sub_of(lax.broadcasted_iota(jnp.int32, (ts, G_KW), 0))
    colb = sub_of(lax.broadcasted_iota(jnp.int32, (ts, G_KW), 1))
    acc_all, acc_upper = adiag
    upper = acc_upper.reshape(nb, half, G_KW)
    acc_all = acc_all + jnp.stack([jnp.zeros_like(upper), upper], axis=1).reshape(ts, G_KW)
    adiag = jnp.where(rowb == colb, acc_all, 0.0)
    intra = (jnp.concatenate(offs, axis=0) + adiag).astype(BF16)
    q_in = (gq * jnp.exp(bc)).astype(BF16)
    o_intra = []
    for c in range(n_chunks):
        lo, hi = c * CHUNK, (c + 1) * CHUNK
        vbd = jnp.where(same_head_kv, jnp.concatenate([gv_ref[lo:hi, :]] * G_HEADS, axis=0),
                        jnp.zeros((), BF16))
        o_intra.append(_dot(intra[lo:hi], vbd))
    sbd = sbd_ref[...]
    for c in range(n_chunks):
        lo, hi = c * CHUNK, (c + 1) * CHUNK
        hg_ref[lo:hi, :] = o_intra[c] + _dot(q_in[lo:hi], sbd.astype(BF16))
        dcol = jnp.exp(bc_t[:, hi - 1:hi])
        sbd = dcol * sbd + s_terms[c]
    sbd_ref[...] = sbd
    yield


def _head_norm(hv, n_heads, width):
    parts = []
    for hd in range(n_heads):
        seg = hv[:, hd * width:(hd + 1) * width]
        parts.append(seg * lax.rsqrt(jnp.mean(seg * seg, axis=-1, keepdims=True) + EPS))
    return jnp.concatenate(parts, axis=1)


def _gate_heads(hm, hg, mo, gr, gmh, ggh):
    hm = _head_norm(hm, M_HEADS, M_DH) * gmh * _sigmoid(mo)
    hg = _head_norm(hg, G_HEADS, G_DV) * ggh * (gr * _sigmoid(gr))
    return jnp.concatenate([hm, hg], axis=1).astype(BF16)


_POST_WEIGHTS = ("w_out", "g_mlp", "w1", "w2", "g_ple", "w_ple", "w_pg", "g_final")


def _post_stages(x, p, mixed, w, y_ref):
    x1 = x + _dot(mixed, w["w_out"][...])
    n1 = _rms(x1, w["g_mlp"][...]).astype(BF16)
    yield
    n_ff = 4
    ff = D_FF // n_ff
    acts = []
    for j in range(n_ff):
        u = _dot(n1, w["w1"][:, j * ff:(j + 1) * ff])
        acts.append(jnp.square(jnp.maximum(u, 0.0)).astype(BF16))
        yield
    act = jnp.concatenate(acts, axis=1)
    half = D_MODEL // 2
    mlp = []
    for j in range(2):
        mlp.append(_dot(act, w["w2"][:, j * half:(j + 1) * half]))
        yield
    x2 = x1 + jnp.concatenate(mlp, axis=1)
    ple = _dot(p.astype(BF16), w["w_ple"][...])
    n2 = _rms(x2, w["g_ple"][...]).astype(BF16)
    yield
    gate = _sigmoid(_dot(n2, w["w_pg"][...]))
    x3 = x2 + ple * gate
    y_ref[...] = _rms(x3, w["g_final"][...])
    yield


def _post_tok_kernel(x_ref, p_ref, hm_ref, hg_ref, mo_ref, gr_ref, gmh_ref, ggh_ref, wout_ref, gmlp_ref,
                     w1_ref, w2_ref, gple_ref, wple_ref, wpg_ref, gfin_ref, y_ref):
    mixed = _gate_heads(hm_ref[...], hg_ref[...], mo_ref[...], gr_ref[...], gmh_ref[...], ggh_ref[...])
    w = dict(zip(_POST_WEIGHTS, (wout_ref, gmlp_ref, w1_ref, w2_ref, gple_ref, wple_ref, wpg_ref, gfin_ref)))
    _run(_post_stages(x_ref[...], p_ref[...], mixed, w, y_ref))


def _post_tok_call(x2d, p2d, hm, hg, mo, gr, wts):
    n = x2d.shape[0]
    row = lambda width: pl.BlockSpec((n, width), lambda i: (0, 0))
    whole = pl.BlockSpec(memory_space=pltpu.VMEM)
    return pl.pallas_call(
        _post_tok_kernel,
        grid=(1,),
        in_specs=[row(D_MODEL), row(D_PLE), row(M_WIDTH), row(G_VW), row(M_WIDTH), row(G_VW)] + [whole] * 10,
        out_specs=row(D_MODEL),
        out_shape=jax.ShapeDtypeStruct((n, D_MODEL), F32),
        compiler_params=pltpu.CompilerParams(dimension_semantics=("arbitrary",), vmem_limit_bytes=VMEM_LIMIT),
        name="post_tok",
    )(x2d, p2d, hm, hg, mo, gr, wts["g_mhead"], wts["g_ghead"], *[wts[k] for k in _POST_WEIGHTS])


def _seq_kernel(xa_ref, xb_ref, pb_ref, gmix_ref, wbig_ref, wsmall_ref, bsmall_ref, wa2_ref, ba_ref,
                convw_ref, convb_ref, tril_ref, tri_ref, wred_ref, gmh_ref, ggh_ref,
                wout_ref, gmlp_ref, w1_ref, w2_ref, gple_ref, wple_ref, wpg_ref, gfin_ref,
                y_ref, tail_ref, cout_ref, mout_ref, sout_ref,
                cbuf_ref, caug_ref, m_ref, sbd_ref, mixed_ref,
                mq_s, mk_s, mv_s, mo_s, gq_s, gk_s, gv_s, gr_s, gates_s, la_s, hm_s, hg_s,
                *, steps_per_seq, n_blocks):
    tb = xa_ref.shape[0]
    s = pl.program_id(0)
    r = lax.rem(jnp.minimum(s, n_blocks - 1), steps_per_seq)

    @pl.when(s == 0)
    def _():
        mixed_ref[...] = jnp.zeros(mixed_ref.shape, BF16)

    @pl.when(r == 0)
    def _():
        cbuf_ref[0:SUBLANES, :] = jnp.zeros((SUBLANES, QK_CONV), F32)
        caug_ref[...] = jnp.zeros(caug_ref.shape, F32)
        m_ref[...] = jnp.zeros(m_ref.shape, F32)
        sbd_ref[...] = jnp.zeros(sbd_ref.shape, F32)

    slot = lax.rem(s, 2)
    src = dict(zip(_PROJ_NAMES, (mq_s, mk_s, mv_s, mo_s, gq_s, gk_s, gv_s, gr_s, gates_s, la_s)))
    w_in = dict(zip(_PROJ_WEIGHTS, (wbig_ref, wsmall_ref, bsmall_ref, wa2_ref, ba_ref)))
    w_post = dict(zip(_POST_WEIGHTS, (wout_ref, gmlp_ref, w1_ref, w2_ref, gple_ref, wple_ref, wpg_ref, gfin_ref)))

    def front():
        h = _rms(xa_ref[...], gmix_ref[...]).astype(BF16)
        yield from _proj_stages(h, w_in, src, _seq_conv(cbuf_ref, tail_ref, convw_ref, convb_ref, tb))
        yield from _mix_stages(src, tril_ref, tri_ref, wred_ref, caug_ref, m_ref, sbd_ref, hm_s, hg_s)
        mixed_ref[slot] = _gate_heads(hm_s[...], hg_s[...], mo_s[...], gr_s[...], gmh_ref[...], ggh_ref[...])
        yield

    def back():
        yield from _post_stages(xb_ref[...], pb_ref[...], mixed_ref[1 - slot], w_post, y_ref)

    _interleave("bfff" "bfff" "bfff" "bff" "bf" "bf" "bf" "bf" "bff", front=front(), back=back())

    @pl.when(jnp.logical_and(s < n_blocks, r == steps_per_seq - 1))
    def _():
        cout_ref[0] = caug_ref[...]
        mout_ref[0] = m_ref[...]
        for g in range(G_HEADS):
            sout_ref[0, g] = sbd_ref[g * G_DK:(g + 1) * G_DK, g * G_DV:(g + 1) * G_DV]


def _seq_call(x2d, p2d, wts, consts, *, n_seq, seq_len):
    tb = SEQ_BLOCK
    steps_per_seq = seq_len // tb
    n_blocks = n_seq * steps_per_seq
    tril, tri, wred = consts
    front_blk = lambda s: jnp.minimum(s, n_blocks - 1)
    back_blk = lambda s: jnp.maximum(s - 1, 0)
    whole = pl.BlockSpec(memory_space=pltpu.VMEM)
    per_seq = lambda shape: pl.BlockSpec((1,) + shape, lambda s: (front_blk(s) // steps_per_seq,) + (0,) * len(shape))
    scratch = [
        pltpu.VMEM((tb + SUBLANES, QK_CONV), F32),
        pltpu.VMEM((M_HEADS, M_DH, 2 * M_DH), F32),
        pltpu.VMEM((SUBLANES, LANES), F32),
        pltpu.VMEM((G_KW, G_VW), F32),
        pltpu.VMEM((2, tb, D_MODEL), BF16),
    ] + [pltpu.VMEM((tb, _PROJ_WIDTH[k]), _PROJ_DTYPE[k]) for k in _PROJ_NAMES] + [
        pltpu.VMEM((tb, M_WIDTH), F32), pltpu.VMEM((tb, G_VW), F32)]
    return pl.pallas_call(
        functools.partial(_seq_kernel, steps_per_seq=steps_per_seq, n_blocks=n_blocks),
        grid=(n_blocks + 1,),
        in_specs=[pl.BlockSpec((tb, D_MODEL), lambda s: (front_blk(s), 0)),
                  pl.BlockSpec((tb, D_MODEL), lambda s: (back_blk(s), 0)),
                  pl.BlockSpec((tb, D_PLE), lambda s: (back_blk(s), 0))] + [whole] * 21,
        out_specs=[pl.BlockSpec((tb, D_MODEL), lambda s: (back_blk(s), 0)),
                   per_seq((SUBLANES, QK_CONV)), per_seq((M_HEADS, M_DH, 2 * M_DH)),
                   per_seq((SUBLANES, LANES)), per_seq((G_HEADS, G_DK, G_DV))],
        out_shape=[jax.ShapeDtypeStruct((n_blocks * tb, D_MODEL), F32),
                   jax.ShapeDtypeStruct((n_seq, SUBLANES, QK_CONV), F32),
                   jax.ShapeDtypeStruct((n_seq, M_HEADS, M_DH, 2 * M_DH), F32),
                   jax.ShapeDtypeStruct((n_seq, SUBLANES, LANES), F32),
                   jax.ShapeDtypeStruct((n_seq, G_HEADS, G_DK, G_DV), F32)],
        scratch_shapes=scratch,
        compiler_params=pltpu.CompilerParams(dimension_semantics=("arbitrary",), vmem_limit_bytes=VMEM_LIMIT),
        name="seq_fused",
    )(x2d, x2d, p2d, wts["g_mix"], wts["w_big"], wts["w_small"], wts["b_small"], wts["w_a2p"], wts["b_a"],
      wts["conv_w"], wts["conv_b"], tril, tri, wred, wts["g_mhead"], wts["g_ghead"],
      *[wts[k] for k in _POST_WEIGHTS])


def _mix_tok_kernel(mq_ref, mk_ref, mv_ref, gates_ref, gq_ref, gk_ref, gv_ref, la_ref,
                    c_ref, n_ref, m_ref, s_ref,
                    hm_ref, hg_ref, cn_ref, nn_ref, mn_ref, sn_ref):
    bb = mq_ref.shape[0]
    wide = bb * LANES
    diag = (lax.broadcasted_iota(jnp.int32, (bb, wide), 0)
            == _div_pow2(lax.broadcasted_iota(jnp.int32, (bb, wide), 1), LANES))
    block_ones = diag.astype(BF16)
    ones_rows = jnp.ones((M_DH, bb), BF16)

    def block_diag(rows):
        return jnp.where(diag, jnp.concatenate([rows] * bb, axis=1), jnp.zeros((), rows.dtype))

    def block_diag_f32(col):
        return jnp.where(diag, col, 0.0)

    def own_block(x):
        x = jnp.where(diag, x, 0.0)
        out = x[:, 0:LANES]
        for b in range(1, bb):
            out = out + x[:, b * LANES:(b + 1) * LANES]
        return out

    gates = gates_ref[...]
    m_all = m_ref[...]
    for hd in range(M_HEADS):
        cs_, ce_ = hd * M_DH, (hd + 1) * M_DH
        q_bf = mq_ref[:, cs_:ce_]
        q = q_bf.astype(F32)
        k = mk_ref[:, cs_:ce_]
        v_bf = mv_ref[:, cs_:ce_]
        v = v_bf.astype(F32)
        n_prev = n_ref[:, cs_:ce_]
        ig = gates[:, hd:hd + 1]
        lf = gates[:, M_HEADS + hd:M_HEADS + hd + 1]
        m_prev = m_all[:, hd:hd + 1]
        m_new = jnp.maximum(lf + m_prev, ig)
        scale = jnp.exp(lf + m_prev - m_new)
        wk = jnp.exp(ig - m_new)
        s_qk = jnp.sum(q * k, axis=1, keepdims=True) * wk
        den = scale * jnp.sum(q * n_prev, axis=1, keepdims=True) + s_qk
        inv = 1.0 / jnp.maximum(jnp.abs(den), jnp.exp(-m_new))
        nn_ref[:, cs_:ce_] = scale * n_prev + wk * k
        mn_ref[:, hd:hd + 1] = m_new
        c_prev = [c_ref[b, hd] for b in range(bb)]
        qc = own_block(_dot(q_bf, jnp.concatenate([c.astype(BF16) for c in c_prev], axis=1)))
        hm_ref[:, cs_:ce_] = (scale * qc + s_qk * v) * inv
        outer = _dot((k * wk).T.astype(BF16), block_diag(v_bf))
        scale_w = _cumsum_dot(ones_rows, block_diag_f32(scale))
        for b in range(bb):
            blk = slice(b * M_DH, (b + 1) * M_DH)
            cn_ref[b, hd] = scale_w[:, blk] * c_prev[b] + outer[:, blk]
    la = la_ref[...]
    dec = jnp.exp(la)
    gq = gq_ref[...]
    gk = gk_ref[...]
    for g in range(G_HEADS):
        ks_, ke_ = g * G_DK, (g + 1) * G_DK
        vs_, ve_ = g * G_DV, (g + 1) * G_DV
        q = gq[:, ks_:ke_]
        k = gk[:, ks_:ke_]
        v_bf = gv_ref[:, vs_:ve_]
        a = jnp.sum(q * k, axis=1, keepdims=True)
        s_prev = [s_ref[b, g] for b in range(bb)]
        o = own_block(_dot((q * dec[:, ks_:ke_]).astype(BF16),
                           jnp.concatenate([s.astype(BF16) for s in s_prev], axis=1)))
        hg_ref[:, vs_:ve_] = o + a * v_bf.astype(F32)
        outer = _dot(k.T.astype(BF16), block_diag(v_bf))
        dec_w = _spread_dot(dec[:, ks_:ke_].T, block_ones)
        for b in range(bb):
            blk = slice(b * G_DV, (b + 1) * G_DV)
            sn_ref[b, g] = dec_w[:, blk] * s_prev[b] + outer[:, blk]


def _mix_tok_call(p, c0, n0, m0, s0):
    bb = TOK_BATCH
    n = c0.shape[0]
    row = lambda w: pl.BlockSpec((bb, w), lambda i: (i, 0))
    st4 = lambda a, b_: pl.BlockSpec((bb, M_HEADS, a, b_), lambda i: (i, 0, 0, 0))
    return pl.pallas_call(
        _mix_tok_kernel,
        grid=(n // bb,),
        in_specs=[row(M_WIDTH), row(M_WIDTH), row(M_WIDTH), row(LANES), row(G_KW), row(G_KW), row(G_VW),
                  row(G_KW), st4(M_DH, M_DH), row(M_WIDTH), row(M_HEADS), st4(G_DK, G_DV)],
        out_specs=[row(M_WIDTH), row(G_VW), st4(M_DH, M_DH), row(M_WIDTH), row(M_HEADS), st4(G_DK, G_DV)],
        out_shape=[jax.ShapeDtypeStruct((n, M_WIDTH), F32), jax.ShapeDtypeStruct((n, G_VW), F32),
                   jax.ShapeDtypeStruct(c0.shape, F32), jax.ShapeDtypeStruct((n, M_WIDTH), F32),
                   jax.ShapeDtypeStruct((n, M_HEADS), F32), jax.ShapeDtypeStruct(s0.shape, F32)],
        compiler_params=pltpu.CompilerParams(dimension_semantics=("arbitrary",), vmem_limit_bytes=VMEM_LIMIT),
        name="mix_tok",
    )(p["mq"], p["mk"], p["mv"], p["gates"], p["gq"], p["gk"], p["gv"], p["la"], c0, n0, m0, s0)


_IN_OFFS = tuple(sum(IN_SIZES[:i]) for i in range(len(IN_SIZES) + 1))
_GATES_LO, _GATES_HI, _GA_LO = _IN_OFFS[3], _IN_OFFS[5], _IN_OFFS[9]
_N_SMALL = 2 * M_HEADS + G_RANK
_W_BIG_COLS = _GATES_LO + (_GA_LO - _GATES_HI)
_CAST_STEPS = 8


def _cast_kernel(win_ref, wout_ref, w1_ref, w2_ref, wple_ref, wpg_ref,
                 big_ref, small_ref, o_out, o_w1, o_w2, o_ple, o_pg):
    w = win_ref[...]
    big_ref[:, :_GATES_LO] = w[:, :_GATES_LO].astype(BF16)
    big_ref[:, _GATES_LO:] = w[:, _GATES_HI:_GA_LO].astype(BF16)
    small_ref[...] = jnp.zeros(small_ref.shape, BF16)
    small_ref[:, :2 * M_HEADS] = w[:, _GATES_LO:_GATES_HI].astype(BF16)
    small_ref[:, 2 * M_HEADS:_N_SMALL] = w[:, _GA_LO:].astype(BF16)
    o_out[...] = wout_ref[...].astype(BF16)
    o_w1[...] = w1_ref[...].astype(BF16)
    o_w2[...] = w2_ref[...].astype(BF16)
    o_ple[...] = wple_ref[...].astype(BF16)
    o_pg[...] = wpg_ref[...].astype(BF16)


def _cast_call(w_in, w_out, w1, w2, w_ple, w_pg):
    srcs = (w_in, w_out, w1, w2, w_ple, w_pg)
    rows = lambda a: pl.BlockSpec((a.shape[0] // _CAST_STEPS, a.shape[1]), lambda i: (i, 0))
    outs = [(D_MODEL, _W_BIG_COLS), (D_MODEL, LANES)] + [a.shape for a in srcs[1:]]
    return pl.pallas_call(
        _cast_kernel,
        grid=(_CAST_STEPS,),
        in_specs=[rows(a) for a in srcs],
        out_specs=[pl.BlockSpec((r // _CAST_STEPS, c), lambda i: (i, 0)) for r, c in outs],
        out_shape=[jax.ShapeDtypeStruct(o, BF16) for o in outs],
        compiler_params=pltpu.CompilerParams(dimension_semantics=("arbitrary",), vmem_limit_bytes=VMEM_LIMIT),
        name="cast_weights",
    )(*srcs)


def _prep_weights(w_in, conv_w, conv_b, b_gate, w_a2, b_a, g_mhead, g_ghead, w_out, g_mix, g_mlp, w1, w2,
                  g_ple, w_ple, w_pg, g_final):
    w_big, w_small, w_out_b, w1_b, w2_b, w_ple_b, w_pg_b = _cast_call(w_in, w_out, w1, w2, w_ple, w_pg)
    b_small = jnp.concatenate([b_gate, jnp.zeros((LANES - 2 * M_HEADS,), F32)])[None]
    w_a2p = jnp.concatenate([jnp.zeros((2 * M_HEADS, G_KW), F32), w_a2,
                             jnp.zeros((LANES - _N_SMALL, G_KW), F32)], axis=0).astype(BF16)
    return dict(
        w_big=w_big, w_small=w_small, b_small=b_small, w_a2p=w_a2p, b_a=b_a[None],
        conv_w=conv_w, conv_b=conv_b[None], g_mix=g_mix[None], g_mhead=g_mhead[None], g_ghead=g_ghead[None],
        w_out=w_out_b, g_mlp=g_mlp[None], w1=w1_b, w2=w2_b,
        g_ple=g_ple[None], w_ple=w_ple_b, w_pg=w_pg_b, g_final=g_final[None])


def _mix_constants(ts):
    t = jnp.arange(ts)
    tril = (t[None, :] <= t[:, None]).astype(BF16)
    tri = ((t[:, None] // CHUNK == t[None, :] // CHUNK) & (t[None, :] <= t[:, None])).astype(BF16)
    rr = jnp.arange(G_KW)
    wred = ((rr[None, :, None] // G_DK == rr[None, None, :] // G_DK)
            & (rr[None, None, :] % SUB == jnp.arange(SUB)[:, None, None])).astype(BF16)
    return tril, tri, wred


def kernel(x_prompt, x_sample, p_prompt, p_sample, state_mlstm_C, state_mlstm_n, state_mlstm_m, state_conv,
           state_gla_S, w_in, conv_w, conv_b, b_gate, w_a2, b_a, g_mhead, g_ghead, w_out, g_mix, g_mlp, w1,
           w2, g_ple, w_ple, w_pg, g_final):
    assert w_in.shape[0] == 1, "single-layer trunk"
    n_seq, seq_len, _ = x_prompt.shape
    n_tok = x_sample.shape[0]
    assert x_sample.shape[1] == 1 and seq_len % SEQ_BLOCK == 0 and n_tok % TOK_BATCH == 0
    wts = _prep_weights(w_in[0], conv_w[0], conv_b[0], b_gate[0], w_a2[0], b_a[0], g_mhead[0], g_ghead[0],
                        w_out[0], g_mix[0], g_mlp[0], w1[0], w2[0], g_ple[0], w_ple[0], w_pg[0], g_final)

    y_p, tail_p, caug_p, m_p, s_p = _seq_call(
        x_prompt.reshape(n_seq * seq_len, D_MODEL), p_prompt[0].reshape(n_seq * seq_len, D_PLE), wts,
        _mix_constants(SEQ_BLOCK), n_seq=n_seq, seq_len=seq_len)

    xs = x_sample.reshape(n_tok, D_MODEL)
    buf = state_conv[0]
    outs = _proj_tok_call(xs, (buf[:, 0], buf[:, 1], buf[:, 2]), wts)
    ps = dict(zip(_PROJ_NAMES, outs[:-1]))
    raw_s = outs[-1]
    hm_s, hg_s, c_s, n_s, m_s, s_s = _mix_tok_call(
        ps, state_mlstm_C[0], state_mlstm_n[0].reshape(n_tok, M_WIDTH), state_mlstm_m[0], state_gla_S[0])
    y_s = _post_tok_call(xs, p_sample[0].reshape(n_tok, D_PLE), hm_s, hg_s, ps["mo"], ps["gr"], wts)

    return (y_p.reshape(n_seq, seq_len, D_MODEL),
            y_s.reshape(n_tok, 1, D_MODEL),
            caug_p[None, :, :, :, :M_DH],
            caug_p[None, :, :, :, M_DH],
            m_p[None, :, :M_HEADS, 0],
            tail_p[None, :, SUBLANES - (CONV_W - 1):, :],
            s_p[None],
            c_s[None],
            n_s.reshape(1, n_tok, M_HEADS, M_DH),
            m_s[None],
            jnp.stack([buf[:, 1], buf[:, 2], raw_s], axis=1)[None],
            s_s[None])
```

```python
import functools

import jax
import jax.numpy as jnp
from jax import lax
from jax.experimental import pallas as pl
from jax.experimental.pallas import tpu as pltpu

D_MODEL = 1024
M_HEADS = 4
M_DH = 128
M_WIDTH = M_HEADS * M_DH
G_HEADS = 4
G_DK = 64
G_DV = 128
G_KW = G_HEADS * G_DK
G_VW = G_HEADS * G_DV
G_RANK = 16
G_TAU = 16.0
CONV_W = 4
QK_CONV = 2 * M_WIDTH
D_FF = 4 * D_MODEL
D_PLE = 256
CHUNK = 64
SUB = 16
EPS = 1e-6
IN_SIZES = (QK_CONV, M_WIDTH, M_WIDTH, M_HEADS, M_HEADS, G_KW, G_KW, G_VW, G_VW, G_RANK)

LANES = 128
SUBLANES = 8
VMEM_LIMIT = 60 * 1024 * 1024
SEQ_BLOCK = 256
TOK_BATCH = 16

F32 = jnp.float32
BF16 = jnp.bfloat16
NEG = -1e30


def _rms(x, g):
    return x * lax.rsqrt(jnp.mean(x * x, axis=-1, keepdims=True) + EPS) * g


def _log_sigmoid(x):
    return jnp.minimum(x, 0.0) - jnp.log(1.0 + jnp.exp(-jnp.abs(x)))


def _sigmoid(x):
    return 0.5 * jnp.tanh(0.5 * x) + 0.5


def _div_pow2(idx, d):
    assert d & (d - 1) == 0
    return lax.shift_right_logical(idx, d.bit_length() - 1)


def _dot(a, b):
    return jnp.dot(a, b, preferred_element_type=F32)


def _dot_nt(a, b):
    return lax.dot_general(a, b, (((1,), (1,)), ((), ())), preferred_element_type=F32)


def _split3(x):
    hi = x.astype(BF16)
    r1 = x - hi.astype(F32)
    mid = r1.astype(BF16)
    lo = (r1 - mid.astype(F32)).astype(BF16)
    return hi, mid, lo


def _cumsum_dot(tri, x):
    hi, mid, lo = _split3(x)
    return _dot(tri, hi) + _dot(tri, mid) + _dot(tri, lo)


def _spread_dot(x, onehot):
    hi, mid, lo = _split3(x)
    return _dot(hi, onehot) + _dot(mid, onehot) + _dot(lo, onehot)


def _interleave(pattern, **streams):
    by_letter = {name[0]: gen for name, gen in streams.items()}
    for letter in pattern:
        next(by_letter[letter], None)
    for gen in by_letter.values():
        _run(gen)


def _run(gen):
    for _ in gen:
        pass


def _proj_stages(h, w, sinks, conv_piece):
    width = M_WIDTH

    def put_q(raw):
        sinks["mq"][...] = conv_piece(raw, 0).astype(sinks["mq"].dtype)

    def put_k(raw):
        sinks["mk"][...] = conv_piece(raw, M_WIDTH) * (M_DH ** -0.5)

    def put_mv(raw):
        sinks["mv"][...] = raw.astype(sinks["mv"].dtype)

    def put_mo(raw):
        sinks["mo"][...] = raw

    def put_gqk(raw):
        sinks["gq"][...] = raw[:, :G_KW] * (G_DK ** -0.5)
        sinks["gk"][...] = raw[:, G_KW:]

    def put_gv(raw):
        sinks["gv"][...] = raw.astype(sinks["gv"].dtype)

    def put_gr(raw):
        sinks["gr"][...] = raw

    def put_small(small):
        g = small + w["b_small"][...]
        lane = lax.broadcasted_iota(jnp.int32, g.shape, 1)
        sinks["gates"][...] = jnp.where(lane < M_HEADS, g, _log_sigmoid(g))
        z = _dot(small.astype(BF16), w["w_a2p"][...]) + w["b_a"][...]
        sinks["la"][...] = _log_sigmoid(z) * (1.0 / G_TAU)

    epilogues = [put_small, put_q, put_k, put_mv, put_mo, put_gqk, put_gv, put_gr]
    assert w["w_big"].shape[0] == width * (len(epilogues) - 1)
    pending = (_dot_nt(h, w["w_small"][...]), epilogues[0])
    yield
    for n, epi in enumerate(epilogues[1:]):
        cur = _dot_nt(h, w["w_big"][n * width:(n + 1) * width, :])
        pending[1](pending[0])
        pending = (cur, epi)
        yield
    pending[1](pending[0])
    yield


def _seq_conv(cbuf_ref, tail_ref, convw_ref, convb_ref, tb):
    cw = convw_ref[...]

    def conv_piece(raw, lo):
        cols = slice(lo, lo + raw.shape[1])
        cbuf_ref[SUBLANES:SUBLANES + tb, cols] = raw
        y = convb_ref[:, cols] + raw * cw[3:4, cols]
        for j in range(CONV_W - 1):
            off = SUBLANES - (CONV_W - 1) + j
            y = y + cbuf_ref[off:off + tb, cols] * cw[j:j + 1, cols]
        last = cbuf_ref[tb:tb + SUBLANES, cols]
        tail_ref[0, :, cols] = last
        cbuf_ref[0:SUBLANES, cols] = last
        return y * _sigmoid(y)

    return conv_piece


def _tok_conv(b0_ref, b1_ref, b2_ref, raw_ref, convw_ref, convb_ref):
    cw = convw_ref[...]

    def conv_piece(raw, lo):
        cols = slice(lo, lo + raw.shape[1])
        raw_ref[:, cols] = raw
        y = (convb_ref[:, cols] + b0_ref[:, cols] * cw[0:1, cols] + b1_ref[:, cols] * cw[1:2, cols]
             + b2_ref[:, cols] * cw[2:3, cols] + raw * cw[3:4, cols])
        return y * _sigmoid(y)

    return conv_piece


_PROJ_NAMES = ("mq", "mk", "mv", "mo", "gq", "gk", "gv", "gr", "gates", "la")
_PROJ_WIDTH = dict(mq=M_WIDTH, mk=M_WIDTH, mv=M_WIDTH, mo=M_WIDTH, gq=G_KW, gk=G_KW, gv=G_VW, gr=G_VW,
                   gates=LANES, la=G_KW)
_PROJ_DTYPE = dict(mq=BF16, mk=F32, mv=BF16, mo=F32, gq=F32, gk=F32, gv=BF16, gr=F32, gates=F32, la=F32)
_PROJ_WEIGHTS = ("w_big", "w_small", "b_small", "w_a2p", "b_a")


def _proj_tok_kernel(x_ref, b0_ref, b1_ref, b2_ref, gmix_ref, wbig_ref, wsmall_ref, bsmall_ref, wa2_ref,
                     ba_ref, convw_ref, convb_ref, *out_refs):
    sinks = dict(zip(_PROJ_NAMES, out_refs[:-1]))
    raw_ref = out_refs[-1]
    w = dict(zip(_PROJ_WEIGHTS, (wbig_ref, wsmall_ref, bsmall_ref, wa2_ref, ba_ref)))
    h = _rms(x_ref[...], gmix_ref[...]).astype(BF16)
    _run(_proj_stages(h, w, sinks, _tok_conv(b0_ref, b1_ref, b2_ref, raw_ref, convw_ref, convb_ref)))


def _proj_tok_call(x2d, conv_rows, wts):
    n = x2d.shape[0]
    row = lambda width: pl.BlockSpec((n, width), lambda i: (0, 0))
    whole = pl.BlockSpec(memory_space=pltpu.VMEM)
    names = list(_PROJ_NAMES)
    return pl.pallas_call(
        _proj_tok_kernel,
        grid=(1,),
        in_specs=[row(D_MODEL)] + [row(QK_CONV)] * 3 + [whole] * 8,
        out_specs=[row(_PROJ_WIDTH[k]) for k in names] + [row(QK_CONV)],
        out_shape=[jax.ShapeDtypeStruct((n, _PROJ_WIDTH[k]), _PROJ_DTYPE[k]) for k in names]
        + [jax.ShapeDtypeStruct((n, QK_CONV), F32)],
        compiler_params=pltpu.CompilerParams(dimension_semantics=("arbitrary",), vmem_limit_bytes=VMEM_LIMIT),
        name="proj_tok",
    )(x2d, *conv_rows, wts["g_mix"], wts["w_big"], wts["w_small"], wts["b_small"], wts["w_a2p"], wts["b_a"],
      wts["conv_w"], wts["conv_b"])


def _mix_stages(src, tril_ref, tri_ref, wred_ref, caug_ref, m_ref, sbd_ref, hm_ref, hg_ref):
    ts = src["mq"].shape[0]
    n_chunks = ts // CHUNK
    gates = src["gates"][...]
    bcum = _cumsum_dot(tril_ref[...], gates)
    gq = src["gq"][...]
    gk = src["gk"][...]
    gv_ref = src["gv"]
    bc = _cumsum_dot(tri_ref[...], src["la"][...])
    gates_t = gates.T
    bcum_t = bcum.T
    bc_t = bc.T
    yield

    nb = ts // SUB
    half = SUB // 2
    assert half == SUBLANES
    q3 = gq.reshape(nb, SUB, G_KW)
    k3 = gk.reshape(nb, SUB, G_KW)
    bc3 = bc.reshape(nb, SUB, G_KW)
    tl = lax.broadcasted_iota(jnp.int32, (nb, SUB, G_KW), 1)
    q3u = gq.reshape(nb, 2, half, G_KW)[:, 1]
    bc3u = bc.reshape(nb, 2, half, G_KW)[:, 1]
    tlu = lax.broadcasted_iota(jnp.int32, (nb, half, G_KW), 1) + half

    def exact_pass(j, acc):
        acc_all, acc_upper = acc
        if j < half:
            arg = jnp.where(tl >= j, bc3 - bc3[:, j:j + 1, :], NEG)
            e = (q3 * k3[:, j:j + 1, :] * jnp.exp(arg)).reshape(ts, G_KW)
            return acc_all + _dot(e.astype(BF16), wred_ref[j]), acc_upper
        arg = jnp.where(tlu >= j, bc3u - bc3[:, j:j + 1, :], NEG)
        e = (q3u * k3[:, j:j + 1, :] * jnp.exp(arg)).reshape(ts // 2, G_KW)
        return acc_all, acc_upper + _dot(e.astype(BF16), wred_ref[j])

    krow = lax.broadcasted_iota(jnp.int32, (CHUNK, G_KW), 0)
    same_head_kk = (_div_pow2(lax.broadcasted_iota(jnp.int32, (G_KW, G_KW), 0), G_DK)
                    == _div_pow2(lax.broadcasted_iota(jnp.int32, (G_KW, G_KW), 1), G_DK))
    same_head_kv = (_div_pow2(lax.broadcasted_iota(jnp.int32, (G_KW, G_VW), 0), G_DK)
                    == _div_pow2(lax.broadcasted_iota(jnp.int32, (G_KW, G_VW), 1), G_DV))

    def cross_block_scores(c):
        lo = c * CHUNK
        bc_c = bc[lo:lo + CHUNK]
        k_c = gk[lo:lo + CHUNK]
        out = [jnp.zeros((SUB, G_KW), F32)]
        for i in range(1, CHUNK // SUB):
            r0 = lo + i * SUB
            r_i = bc[r0 - 1:r0, :]
            qi = (gq[r0:r0 + SUB] * jnp.exp(bc[r0:r0 + SUB] - r_i)).astype(BF16)
            ki = (k_c * jnp.exp(jnp.where(krow < i * SUB, r_i - bc_c, NEG))).astype(BF16)
            kbd = jnp.where(same_head_kk, jnp.concatenate([ki] * G_HEADS, axis=0), jnp.zeros((), BF16))
            out.append(_dot_nt(qi, kbd))
        return out

    def state_update_term(c):
        lo, hi = c * CHUNK, (c + 1) * CHUNK
        b_end = bc[hi - 1:hi, :]
        k_out_t = (gk[lo:hi] * jnp.exp(b_end - bc[lo:hi])).T.astype(BF16)
        rows = []
        for g in range(G_HEADS):
            blk = _dot(k_out_t[g * G_DK:(g + 1) * G_DK], gv_ref[lo:hi, g * G_DV:(g + 1) * G_DV])
            zero = jnp.zeros((G_DK, G_DV), F32)
            rows.append(jnp.concatenate([zero] * g + [blk] + [zero] * (G_HEADS - 1 - g), axis=1))
        return jnp.concatenate(rows, axis=0)

    adiag = (jnp.zeros((ts, G_KW), F32), jnp.zeros((ts // 2, G_KW), F32))
    per_stage = SUB // 4
    assert n_chunks == 4
    offs = []
    s_terms = []

    causal = (lax.broadcasted_iota(jnp.int32, (ts, ts), 0) >= lax.broadcasted_iota(jnp.int32, (ts, ts), 1))
    ones_col = (lax.broadcasted_iota(jnp.int32, (ts, LANES), 1) == 0).astype(BF16)
    heads = range(M_HEADS)
    hsl = [slice(hd * M_DH, (hd + 1) * M_DH) for hd in heads]
    b_col = [bcum[:, M_HEADS + hd:M_HEADS + hd + 1] for hd in heads]
    b_row = [bcum_t[M_HEADS + hd:M_HEADS + hd + 1, :] for hd in heads]
    i_col = [gates[:, hd:hd + 1] for hd in heads]
    i_row = [gates_t[hd:hd + 1, :] for hd in heads]
    m_prev = [m_ref[hd:hd + 1, 0:1] for hd in heads]
    q = [src["mq"][:, hsl[hd]] for hd in heads]
    k = [src["mk"][:, hsl[hd]] for hd in heads]
    vaug = [jnp.concatenate([src["mv"][:, hsl[hd]], ones_col], axis=1) for hd in heads]
    caug = [caug_ref[hd] for hd in heads]
    s_qk = [_dot_nt(q[hd], k[hd].astype(BF16)) for hd in heads]
    qc = [_dot(q[hd], caug[hd].astype(BF16)) for hd in heads]
    for j in range(0, per_stage):
        adiag = exact_pass(j, adiag)
    offs += cross_block_scores(0)
    s_terms.append(state_update_term(0))
    yield
    b_last = [b_col[hd][ts - 1:ts, :] for hd in heads]
    dec = [b_last[hd] - b_col[hd] + i_col[hd] for hd in heads]
    m_new = [jnp.maximum(b_last[hd] + m_prev[hd], jnp.max(dec[hd], axis=0, keepdims=True)) for hd in heads]
    kw_t = [(k[hd] * jnp.exp(dec[hd] - m_new[hd])).T.astype(BF16) for hd in heads]
    upd = [_dot(kw_t[hd], vaug[hd]) for hd in heads]
    for j in range(per_stage, 2 * per_stage):
        adiag = exact_pass(j, adiag)
    offs += cross_block_scores(1)
    s_terms.append(state_update_term(1))
    yield
    dmat = [jnp.where(causal, b_col[hd] - b_row[hd] + i_row[hd], -jnp.inf) for hd in heads]
    inter = [b_col[hd] + m_prev[hd] for hd in heads]
    m_tok = [jnp.maximum(inter[hd], jnp.max(dmat[hd], axis=1, keepdims=True)) for hd in heads]
    for j in range(2 * per_stage, 3 * per_stage):
        adiag = exact_pass(j, adiag)
    offs += cross_block_scores(2)
    s_terms.append(state_update_term(2))
    yield
    p = [(s_qk[hd] * jnp.exp(dmat[hd] - m_tok[hd])).astype(BF16) for hd in heads]
    pv = [_dot(p[hd], vaug[hd]) for hd in heads]
    for j in range(3 * per_stage, SUB):
        adiag = exact_pass(j, adiag)
    offs += cross_block_scores(3)
    s_terms.append(state_update_term(3))
    yield
    for hd in heads:
        tot = jnp.exp(inter[hd] - m_tok[hd]) * qc[hd] + pv[hd]
        den = tot[:, M_DH:M_DH + 1]
        hm_ref[:, hsl[hd]] = tot[:, :M_DH] / jnp.maximum(jnp.abs(den), jnp.exp(-m_tok[hd]))
        caug_ref[hd] = jnp.exp(b_last[hd] + m_prev[hd] - m_new[hd]) * caug[hd] + upd[hd]
        m_ref[hd:hd + 1, :] = jnp.broadcast_to(m_new[hd], (1, LANES))
    yield

    sub_of = lambda idx: lax.shift_right_logical(idx & (CHUNK - 1), SUB.bit_length() - 1)
    rowb = sub_of(lax.broadcasted_iota(jnp.int32, (ts, G_KW), 0))
    colb = sub_of(lax.broadcasted_iota(jnp.int32, (ts, G_KW), 1))
    acc_all, acc_upper = adiag
    upper = acc_upper.reshape(nb, half, G_KW)
    acc_all = acc_all + jnp.stack([jnp.zeros_like(upper), upper], axis=1).reshape(ts, G_KW)
    adiag = jnp.where(rowb == colb, acc_all, 0.0)
    intra = (jnp.concatenate(offs, axis=0) + adiag).astype(BF16)
    q_in = (gq * jnp.exp(bc)).astype(BF16)
    o_intra = []
    for c in range(n_chunks):
        lo, hi = c * CHUNK, (c + 1) * CHUNK
        vbd = jnp.where(same_head_kv, jnp.concatenate([gv_ref[lo:hi, :]] * G_HEADS, axis=0),
                        jnp.zeros((), BF16))
        o_intra.append(_dot(intra[lo:hi], vbd))
    sbd = sbd_ref[...]
    for c in range(n_chunks):
        lo, hi = c * CHUNK, (c + 1) * CHUNK
        hg_ref[lo:hi, :] = o_intra[c] + _dot(q_in[lo:hi], sbd.astype(BF16))
        dcol = jnp.exp(bc_t[:, hi - 1:hi])
        sbd = dcol * sbd + s_terms[c]
    sbd_ref[...] = sbd
    yield


def _head_norm(hv, n_heads, width):
    parts = []
    for hd in range(n_heads):
        seg = hv[:, hd * width:(hd + 1) * width]
        parts.append(seg * lax.rsqrt(jnp.mean(seg * seg, axis=-1, keepdims=True) + EPS))
    return jnp.concatenate(parts, axis=1)


def _gate_heads(hm, hg, mo, gr, gmh, ggh):
    hm = _head_norm(hm, M_HEADS, M_DH) * gmh * _sigmoid(mo)
    hg = _head_norm(hg, G_HEADS, G_DV) * ggh * (gr * _sigmoid(gr))
    return jnp.concatenate([hm, hg], axis=1).astype(BF16)


_POST_WEIGHTS = ("w_out", "g_mlp", "w1", "w2", "g_ple", "w_ple", "w_pg", "g_final")


def _post_stages(x, p, mixed, w, y_ref):
    x1 = x + _dot(mixed, w["w_out"][...])
    n1 = _rms(x1, w["g_mlp"][...]).astype(BF16)
    yield
    n_ff = 4
    ff = D_FF // n_ff
    acts = []
    for j in range(n_ff):
        u = _dot(n1, w["w1"][:, j * ff:(j + 1) * ff])
        acts.append(jnp.square(jnp.maximum(u, 0.0)).astype(BF16))
        yield
    act = jnp.concatenate(acts, axis=1)
    half = D_MODEL // 2
    mlp = []
    for j in range(2):
        mlp.append(_dot(act, w["w2"][:, j * half:(j + 1) * half]))
        yield
    x2 = x1 + jnp.concatenate(mlp, axis=1)
    ple = _dot(p.astype(BF16), w["w_ple"][...])
    n2 = _rms(x2, w["g_ple"][...]).astype(BF16)
    yield
    gate = _sigmoid(_dot(n2, w["w_pg"][...]))
    x3 = x2 + ple * gate
    y_ref[...] = _rms(x3, w["g_final"][...])
    yield


def _post_tok_kernel(x_ref, p_ref, hm_ref, hg_ref, mo_ref, gr_ref, gmh_ref, ggh_ref, wout_ref, gmlp_ref,
                     w1_ref, w2_ref, gple_ref, wple_ref, wpg_ref, gfin_ref, y_ref):
    mixed = _gate_heads(hm_ref[...], hg_ref[...], mo_ref[...], gr_ref[...], gmh_ref[...], ggh_ref[...])
    w = dict(zip(_POST_WEIGHTS, (wout_ref, gmlp_ref, w1_ref, w2_ref, gple_ref, wple_ref, wpg_ref, gfin_ref)))
    _run(_post_stages(x_ref[...], p_ref[...], mixed, w, y_ref))


def _post_tok_call(x2d, p2d, hm, hg, mo, gr, wts):
    n = x2d.shape[0]
    row = lambda width: pl.BlockSpec((n, width), lambda i: (0, 0))
    whole = pl.BlockSpec(memory_space=pltpu.VMEM)
    return pl.pallas_call(
        _post_tok_kernel,
        grid=(1,),
        in_specs=[row(D_MODEL), row(D_PLE), row(M_WIDTH), row(G_VW), row(M_WIDTH), row(G_VW)] + [whole] * 10,
        out_specs=row(D_MODEL),
        out_shape=jax.ShapeDtypeStruct((n, D_MODEL), F32),
        compiler_params=pltpu.CompilerParams(dimension_semantics=("arbitrary",), vmem_limit_bytes=VMEM_LIMIT),
        name="post_tok",
    )(x2d, p2d, hm, hg, mo, gr, wts["g_mhead"], wts["g_ghead"], *[wts[k] for k in _POST_WEIGHTS])


def _seq_kernel(xa_ref, xb_ref, pb_ref, gmix_ref, wbig_ref, wsmall_ref, bsmall_ref, wa2_ref, ba_ref,
                convw_ref, convb_ref, tril_ref, tri_ref, wred_ref, gmh_ref, ggh_ref,
                wout_ref, gmlp_ref, w1_ref, w2_ref, gple_ref, wple_ref, wpg_ref, gfin_ref,
                y_ref, tail_ref, cout_ref, mout_ref, sout_ref,
                cbuf_ref, caug_ref, m_ref, sbd_ref, mixed_ref,
                mq_s, mk_s, mv_s, mo_s, gq_s, gk_s, gv_s, gr_s, gates_s, la_s, hm_s, hg_s,
                *, steps_per_seq, n_blocks):
    tb = xa_ref.shape[0]
    s = pl.program_id(0)
    r = lax.rem(jnp.minimum(s, n_blocks - 1), steps_per_seq)

    @pl.when(s == 0)
    def _():
        mixed_ref[...] = jnp.zeros(mixed_ref.shape, BF16)

    @pl.when(r == 0)
    def _():
        cbuf_ref[0:SUBLANES, :] = jnp.zeros((SUBLANES, QK_CONV), F32)
        caug_ref[...] = jnp.zeros(caug_ref.shape, F32)
        m_ref[...] = jnp.zeros(m_ref.shape, F32)
        sbd_ref[...] = jnp.zeros(sbd_ref.shape, F32)

    slot = lax.rem(s, 2)
    src = dict(zip(_PROJ_NAMES, (mq_s, mk_s, mv_s, mo_s, gq_s, gk_s, gv_s, gr_s, gates_s, la_s)))
    w_in = dict(zip(_PROJ_WEIGHTS, (wbig_ref, wsmall_ref, bsmall_ref, wa2_ref, ba_ref)))
    w_post = dict(zip(_POST_WEIGHTS, (wout_ref, gmlp_ref, w1_ref, w2_ref, gple_ref, wple_ref, wpg_ref, gfin_ref)))

    def front():
        h = _rms(xa_ref[...], gmix_ref[...]).astype(BF16)
        yield from _proj_stages(h, w_in, src, _seq_conv(cbuf_ref, tail_ref, convw_ref, convb_ref, tb))
        yield from _mix_stages(src, tril_ref, tri_ref, wred_ref, caug_ref, m_ref, sbd_ref, hm_s, hg_s)
        mixed_ref[slot] = _gate_heads(hm_s[...], hg_s[...], mo_s[...], gr_s[...], gmh_ref[...], ggh_ref[...])
        yield

    def back():
        yield from _post_stages(xb_ref[...], pb_ref[...], mixed_ref[1 - slot], w_post, y_ref)

    _interleave("bfff" "bfff" "bfff" "bff" "bf" "bf" "bf" "bf" "bff", front=front(), back=back())

    @pl.when(jnp.logical_and(s < n_blocks, r == steps_per_seq - 1))
    def _():
        cout_ref[0] = caug_ref[...]
        mout_ref[0] = m_ref[...]
        for g in range(G_HEADS):
            sout_ref[0, g] = sbd_ref[g * G_DK:(g + 1) * G_DK, g * G_DV:(g + 1) * G_DV]


def _seq_call(x2d, p2d, wts, consts, *, n_seq, seq_len):
    tb = SEQ_BLOCK
    steps_per_seq = seq_len // tb
    n_blocks = n_seq * steps_per_seq
    tril, tri, wred = consts
    front_blk = lambda s: jnp.minimum(s, n_blocks - 1)
    back_blk = lambda s: jnp.maximum(s - 1, 0)
    whole = pl.BlockSpec(memory_space=pltpu.VMEM)
    per_seq = lambda shape: pl.BlockSpec((1,) + shape, lambda s: (front_blk(s) // steps_per_seq,) + (0,) * len(shape))
    scratch = [
        pltpu.VMEM((tb + SUBLANES, QK_CONV), F32),
        pltpu.VMEM((M_HEADS, M_DH, 2 * M_DH), F32),
        pltpu.VMEM((SUBLANES, LANES), F32),
        pltpu.VMEM((G_KW, G_VW), F32),
        pltpu.VMEM((2, tb, D_MODEL), BF16),
    ] + [pltpu.VMEM((tb, _PROJ_WIDTH[k]), _PROJ_DTYPE[k]) for k in _PROJ_NAMES] + [
        pltpu.VMEM((tb, M_WIDTH), F32), pltpu.VMEM((tb, G_VW), F32)]
    return pl.pallas_call(
        functools.partial(_seq_kernel, steps_per_seq=steps_per_seq, n_blocks=n_blocks),
        grid=(n_blocks + 1,),
        in_specs=[pl.BlockSpec((tb, D_MODEL), lambda s: (front_blk(s), 0)),
                  pl.BlockSpec((tb, D_MODEL), lambda s: (back_blk(s), 0)),
                  pl.BlockSpec((tb, D_PLE), lambda s: (back_blk(s), 0))] + [whole] * 21,
        out_specs=[pl.BlockSpec((tb, D_MODEL), lambda s: (back_blk(s), 0)),
                   per_seq((SUBLANES, QK_CONV)), per_seq((M_HEADS, M_DH, 2 * M_DH)),
                   per_seq((SUBLANES, LANES)), per_seq((G_HEADS, G_DK, G_DV))],
        out_shape=[jax.ShapeDtypeStruct((n_blocks * tb, D_MODEL), F32),
                   jax.ShapeDtypeStruct((n_seq, SUBLANES, QK_CONV), F32),
                   jax.ShapeDtypeStruct((n_seq, M_HEADS, M_DH, 2 * M_DH), F32),
                   jax.ShapeDtypeStruct((n_seq, SUBLANES, LANES), F32),
                   jax.ShapeDtypeStruct((n_seq, G_HEADS, G_DK, G_DV), F32)],
        scratch_shapes=scratch,
        compiler_params=pltpu.CompilerParams(dimension_semantics=("arbitrary",), vmem_limit_bytes=VMEM_LIMIT),
        name="seq_fused",
    )(x2d, x2d, p2d, wts["g_mix"], wts["w_big"], wts["w_small"], wts["b_small"], wts["w_a2p"], wts["b_a"],
      wts["conv_w"], wts["conv_b"], tril, tri, wred, wts["g_mhead"], wts["g_ghead"],
      *[wts[k] for k in _POST_WEIGHTS])


def _mix_tok_kernel(mq_ref, mk_ref, mv_ref, gates_ref, gq_ref, gk_ref, gv_ref, la_ref,
                    c_ref, n_ref, m_ref, s_ref,
                    hm_ref, hg_ref, cn_ref, nn_ref, mn_ref, sn_ref):
    bb = mq_ref.shape[0]
    wide = bb * LANES
    diag = (lax.broadcasted_iota(jnp.int32, (bb, wide), 0)
            == _div_pow2(lax.broadcasted_iota(jnp.int32, (bb, wide), 1), LANES))
    block_ones = diag.astype(BF16)
    ones_rows = jnp.ones((M_DH, bb), BF16)

    def block_diag(rows):
        return jnp.where(diag, jnp.concatenate([rows] * bb, axis=1), jnp.zeros((), rows.dtype))

    def block_diag_f32(col):
        return jnp.where(diag, col, 0.0)

    def own_block(x):
        x = jnp.where(diag, x, 0.0)
        out = x[:, 0:LANES]
        for b in range(1, bb):
            out = out + x[:, b * LANES:(b + 1) * LANES]
        return out

    gates = gates_ref[...]
    m_all = m_ref[...]
    for hd in range(M_HEADS):
        cs_, ce_ = hd * M_DH, (hd + 1) * M_DH
        q_bf = mq_ref[:, cs_:ce_]
        q = q_bf.astype(F32)
        k = mk_ref[:, cs_:ce_]
        v_bf = mv_ref[:, cs_:ce_]
        v = v_bf.astype(F32)
        n_prev = n_ref[:, cs_:ce_]
        ig = gates[:, hd:hd + 1]
        lf = gates[:, M_HEADS + hd:M_HEADS + hd + 1]
        m_prev = m_all[:, hd:hd + 1]
        m_new = jnp.maximum(lf + m_prev, ig)
        scale = jnp.exp(lf + m_prev - m_new)
        wk = jnp.exp(ig - m_new)
        s_qk = jnp.sum(q * k, axis=1, keepdims=True) * wk
        den = scale * jnp.sum(q * n_prev, axis=1, keepdims=True) + s_qk
        inv = 1.0 / jnp.maximum(jnp.abs(den), jnp.exp(-m_new))
        nn_ref[:, cs_:ce_] = scale * n_prev + wk * k
        mn_ref[:, hd:hd + 1] = m_new
        c_prev = [c_ref[b, hd] for b in range(bb)]
        qc = own_block(_dot(q_bf, jnp.concatenate([c.astype(BF16) for c in c_prev], axis=1)))
        hm_ref[:, cs_:ce_] = (scale * qc + s_qk * v) * inv
        outer = _dot((k * wk).T.astype(BF16), block_diag(v_bf))
        scale_w = _cumsum_dot(ones_rows, block_diag_f32(scale))
        for b in range(bb):
            blk = slice(b * M_DH, (b + 1) * M_DH)
            cn_ref[b, hd] = scale_w[:, blk] * c_prev[b] + outer[:, blk]
    la = la_ref[...]
    dec = jnp.exp(la)
    gq = gq_ref[...]
    gk = gk_ref[...]
    for g in range(G_HEADS):
        ks_, ke_ = g * G_DK, (g + 1) * G_DK
        vs_, ve_ = g * G_DV, (g + 1) * G_DV
        q = gq[:, ks_:ke_]
        k = gk[:, ks_:ke_]
        v_bf = gv_ref[:, vs_:ve_]
        a = jnp.sum(q * k, axis=1, keepdims=True)
        s_prev = [s_ref[b, g] for b in range(bb)]
        o = own_block(_dot((q * dec[:, ks_:ke_]).astype(BF16),
                           jnp.concatenate([s.astype(BF16) for s in s_prev], axis=1)))
        hg_ref[:, vs_:ve_] = o + a * v_bf.astype(F32)
        outer = _dot(k.T.astype(BF16), block_diag(v_bf))
        dec_w = _spread_dot(dec[:, ks_:ke_].T, block_ones)
        for b in range(bb):
            blk = slice(b * G_DV, (b + 1) * G_DV)
            sn_ref[b, g] = dec_w[:, blk] * s_prev[b] + outer[:, blk]


def _mix_tok_call(p, c0, n0, m0, s0):
    bb = TOK_BATCH
    n = c0.shape[0]
    row = lambda w: pl.BlockSpec((bb, w), lambda i: (i, 0))
    st4 = lambda a, b_: pl.BlockSpec((bb, M_HEADS, a, b_), lambda i: (i, 0, 0, 0))
    return pl.pallas_call(
        _mix_tok_kernel,
        grid=(n // bb,),
        in_specs=[row(M_WIDTH), row(M_WIDTH), row(M_WIDTH), row(LANES), row(G_KW), row(G_KW), row(G_VW),
                  row(G_KW), st4(M_DH, M_DH), row(M_WIDTH), row(M_HEADS), st4(G_DK, G_DV)],
        out_specs=[row(M_WIDTH), row(G_VW), st4(M_DH, M_DH), row(M_WIDTH), row(M_HEADS), st4(G_DK, G_DV)],
        out_shape=[jax.ShapeDtypeStruct((n, M_WIDTH), F32), jax.ShapeDtypeStruct((n, G_VW), F32),
                   jax.ShapeDtypeStruct(c0.shape, F32), jax.ShapeDtypeStruct((n, M_WIDTH), F32),
                   jax.ShapeDtypeStruct((n, M_HEADS), F32), jax.ShapeDtypeStruct(s0.shape, F32)],
        compiler_params=pltpu.CompilerParams(dimension_semantics=("arbitrary",), vmem_limit_bytes=VMEM_LIMIT),
        name="mix_tok",
    )(p["mq"], p["mk"], p["mv"], p["gates"], p["gq"], p["gk"], p["gv"], p["la"], c0, n0, m0, s0)


_IN_OFFS = tuple(sum(IN_SIZES[:i]) for i in range(len(IN_SIZES) + 1))
_GATES_LO, _GATES_HI, _GA_LO = _IN_OFFS[3], _IN_OFFS[5], _IN_OFFS[9]
_N_SMALL = 2 * M_HEADS + G_RANK
_W_BIG_COLS = _GATES_LO + (_GA_LO - _GATES_HI)
_CAST_STEPS = 8


def _cast_kernel(wout_ref, w1_ref, w2_ref, wple_ref, wpg_ref, o_out, o_w1, o_w2, o_ple, o_pg):
    o_out[...] = wout_ref[...].astype(BF16)
    o_w1[...] = w1_ref[...].astype(BF16)
    o_w2[...] = w2_ref[...].astype(BF16)
    o_ple[...] = wple_ref[...].astype(BF16)
    o_pg[...] = wpg_ref[...].astype(BF16)


def _cast_call(*srcs):
    rows = lambda a: pl.BlockSpec((a.shape[0] // _CAST_STEPS, a.shape[1]), lambda i: (i, 0))
    return pl.pallas_call(
        _cast_kernel,
        grid=(_CAST_STEPS,),
        in_specs=[rows(a) for a in srcs],
        out_specs=[rows(a) for a in srcs],
        out_shape=[jax.ShapeDtypeStruct(a.shape, BF16) for a in srcs],
        compiler_params=pltpu.CompilerParams(dimension_semantics=("arbitrary",), vmem_limit_bytes=VMEM_LIMIT),
        name="cast_weights",
    )(*srcs)


def _cast_in_kernel(wt_ref, big_ref, small_ref):
    big_ref[:_GATES_LO, :] = wt_ref[:_GATES_LO, :].astype(BF16)
    big_ref[_GATES_LO:, :] = wt_ref[_GATES_HI:_GA_LO, :].astype(BF16)
    small = jnp.concatenate([wt_ref[_GATES_LO:_GATES_HI, :], wt_ref[_GA_LO:, :],
                             jnp.zeros((LANES - _N_SMALL, D_MODEL), F32)], axis=0)
    small_ref[...] = small.astype(BF16)


def _cast_in_call(w_in_t):
    whole = pl.BlockSpec(memory_space=pltpu.VMEM)
    return pl.pallas_call(
        _cast_in_kernel,
        in_specs=[whole], out_specs=[whole, whole],
        out_shape=[jax.ShapeDtypeStruct((_W_BIG_COLS, D_MODEL), BF16),
                   jax.ShapeDtypeStruct((LANES, D_MODEL), BF16)],
        compiler_params=pltpu.CompilerParams(vmem_limit_bytes=VMEM_LIMIT),
        name="cast_w_in",
    )(w_in_t)


def _prep_weights(w_in, conv_w, conv_b, b_gate, w_a2, b_a, g_mhead, g_ghead, w_out, g_mix, g_mlp, w1, w2,
                  g_ple, w_ple, w_pg, g_final):
    w_big, w_small = _cast_in_call(jnp.swapaxes(w_in, 0, 1))
    w_out_b, w1_b, w2_b, w_ple_b, w_pg_b = _cast_call(w_out, w1, w2, w_ple, w_pg)
    b_small = jnp.concatenate([b_gate, jnp.zeros((LANES - 2 * M_HEADS,), F32)])[None]
    w_a2p = jnp.concatenate([jnp.zeros((2 * M_HEADS, G_KW), F32), w_a2,
                             jnp.zeros((LANES - _N_SMALL, G_KW), F32)], axis=0).astype(BF16)
    return dict(
        w_big=w_big, w_small=w_small, b_small=b_small, w_a2p=w_a2p, b_a=b_a[None],
        conv_w=conv_w, conv_b=conv_b[None], g_mix=g_mix[None], g_mhead=g_mhead[None], g_ghead=g_ghead[None],
        w_out=w_out_b, g_mlp=g_mlp[None], w1=w1_b, w2=w2_b,
        g_ple=g_ple[None], w_ple=w_ple_b, w_pg=w_pg_b, g_final=g_final[None])


def _mix_constants(ts):
    t = jnp.arange(ts)
    tril = (t[None, :] <= t[:, None]).astype(BF16)
    tri = ((t[:, None] // CHUNK == t[None, :] // CHUNK) & (t[None, :] <= t[:, None])).astype(BF16)
    rr = jnp.arange(G_KW)
    wred = ((rr[None, :, None] // G_DK == rr[None, None, :] // G_DK)
            & (rr[None, None, :] % SUB == jnp.arange(SUB)[:, None, None])).astype(BF16)
    return tril, tri, wred


def kernel(x_prompt, x_sample, p_prompt, p_sample, state_mlstm_C, state_mlstm_n, state_mlstm_m, state_conv,
           state_gla_S, w_in, conv_w, conv_b, b_gate, w_a2, b_a, g_mhead, g_ghead, w_out, g_mix, g_mlp, w1,
           w2, g_ple, w_ple, w_pg, g_final):
    assert w_in.shape[0] == 1, "single-layer trunk"
    n_seq, seq_len, _ = x_prompt.shape
    n_tok = x_sample.shape[0]
    assert x_sample.shape[1] == 1 and seq_len % SEQ_BLOCK == 0 and n_tok % TOK_BATCH == 0
    wts = _prep_weights(w_in[0], conv_w[0], conv_b[0], b_gate[0], w_a2[0], b_a[0], g_mhead[0], g_ghead[0],
                        w_out[0], g_mix[0], g_mlp[0], w1[0], w2[0], g_ple[0], w_ple[0], w_pg[0], g_final)

    y_p, tail_p, caug_p, m_p, s_p = _seq_call(
        x_prompt.reshape(n_seq * seq_len, D_MODEL), p_prompt[0].reshape(n_seq * seq_len, D_PLE), wts,
        _mix_constants(SEQ_BLOCK), n_seq=n_seq, seq_len=seq_len)

    xs = x_sample.reshape(n_tok, D_MODEL)
    buf = state_conv[0]
    outs = _proj_tok_call(xs, (buf[:, 0], buf[:, 1], buf[:, 2]), wts)
    ps = dict(zip(_PROJ_NAMES, outs[:-1]))
    raw_s = outs[-1]
    hm_s, hg_s, c_s, n_s, m_s, s_s = _mix_tok_call(
        ps, state_mlstm_C[0], state_mlstm_n[0].reshape(n_tok, M_WIDTH), state_mlstm_m[0], state_gla_S[0])
    y_s = _post_tok_call(xs, p_sample[0].reshape(n_tok, D_PLE), hm_s, hg_s, ps["mo"], ps["gr"], wts)

    return (y_p.reshape(n_seq, seq_len, D_MODEL),
            y_s.reshape(n_tok, 1, D_MODEL),
            caug_p[None, :, :, :, :M_DH],
            caug_p[None, :, :, :, M_DH],
            m_p[None, :, :M_HEADS, 0],
            tail_p[None, :, SUBLANES - (CONV_W - 1):, :],
            s_p[None],
            c_s[None],
            n_s.reshape(1, n_tok, M_HEADS, M_DH),
            m_s[None],
            jnp.stack([buf[:, 1], buf[:, 2], raw_s], axis=1)[None],
            s_s[None])
```

```python
import functools

import jax
import jax.numpy as jnp
from jax import lax
from jax.experimental import pallas as pl
from jax.experimental.pallas import tpu as pltpu

D_MODEL = 1024
M_HEADS = 4
M_DH = 128
M_WIDTH = M_HEADS * M_DH
G_HEADS = 4
G_DK = 64
G_DV = 128
G_KW = G_HEADS * G_DK
G_VW = G_HEADS * G_DV
G_RANK = 16
G_TAU = 16.0
CONV_W = 4
QK_CONV = 2 * M_WIDTH
D_FF = 4 * D_MODEL
D_PLE = 256
CHUNK = 64
SUB = 16
EPS = 1e-6
IN_SIZES = (QK_CONV, M_WIDTH, M_WIDTH, M_HEADS, M_HEADS, G_KW, G_KW, G_VW, G_VW, G_RANK)

LANES = 128
SUBLANES = 8
VMEM_LIMIT = 60 * 1024 * 1024
SEQ_BLOCK = 256
TOK_BATCH = 16

F32 = jnp.float32
BF16 = jnp.bfloat16
NEG = -1e30


def _rms(x, g):
    return x * lax.rsqrt(jnp.mean(x * x, axis=-1, keepdims=True) + EPS) * g


def _log_sigmoid(x):
    return jnp.minimum(x, 0.0) - jnp.log(1.0 + jnp.exp(-jnp.abs(x)))


def _sigmoid(x):
    return 0.5 * jnp.tanh(0.5 * x) + 0.5


def _div_pow2(idx, d):
    assert d & (d - 1) == 0
    return lax.shift_right_logical(idx, d.bit_length() - 1)


def _dot(a, b):
    return jnp.dot(a, b, preferred_element_type=F32)


def _dot_nt(a, b):
    return lax.dot_general(a, b, (((1,), (1,)), ((), ())), preferred_element_type=F32)


def _split3(x):
    hi = x.astype(BF16)
    r1 = x - hi.astype(F32)
    mid = r1.astype(BF16)
    lo = (r1 - mid.astype(F32)).astype(BF16)
    return hi, mid, lo


def _cumsum_dot(tri, x):
    hi, mid, lo = _split3(x)
    return _dot(tri, hi) + _dot(tri, mid) + _dot(tri, lo)


def _spread_dot(x, onehot):
    hi, mid, lo = _split3(x)
    return _dot(hi, onehot) + _dot(mid, onehot) + _dot(lo, onehot)


def _interleave(pattern, **streams):
    by_letter = {name[0]: gen for name, gen in streams.items()}
    for letter in pattern:
        next(by_letter[letter], None)
    for gen in by_letter.values():
        _run(gen)


def _run(gen):
    for _ in gen:
        pass


def _proj_stages(h, w, sinks, conv_piece):
    width = M_WIDTH

    def put_q(raw):
        sinks["mq"][...] = conv_piece(raw, 0).astype(sinks["mq"].dtype)

    def put_k(raw):
        sinks["mk"][...] = conv_piece(raw, M_WIDTH) * (M_DH ** -0.5)

    def put_mv(raw):
        sinks["mv"][...] = raw.astype(sinks["mv"].dtype)

    def put_mo(raw):
        sinks["mo"][...] = raw

    def put_gqk(raw):
        sinks["gq"][...] = raw[:, :G_KW] * (G_DK ** -0.5)
        sinks["gk"][...] = raw[:, G_KW:]

    def put_gv(raw):
        sinks["gv"][...] = raw.astype(sinks["gv"].dtype)

    def put_gr(raw):
        sinks["gr"][...] = raw

    def put_small(small):
        g = small + w["b_small"][...]
        lane = lax.broadcasted_iota(jnp.int32, g.shape, 1)
        sinks["gates"][...] = jnp.where(lane < M_HEADS, g, _log_sigmoid(g))
        z = _dot(small.astype(BF16), w["w_a2p"][...]) + w["b_a"][...]
        sinks["la"][...] = _log_sigmoid(z) * (1.0 / G_TAU)

    epilogues = [put_small, put_q, put_k, put_mv, put_mo, put_gqk, put_gv, put_gr]
    assert w["w_big"].shape[1] == width * (len(epilogues) - 1)
    pending = (_dot(h, w["w_small"][...]), epilogues[0])
    yield
    for n, epi in enumerate(epilogues[1:]):
        cur = _dot(h, w["w_big"][:, n * width:(n + 1) * width])
        pending[1](pending[0])
        pending = (cur, epi)
        yield
    pending[1](pending[0])
    yield


def _seq_conv(cbuf_ref, tail_ref, convw_ref, convb_ref, tb):
    cw = convw_ref[...]

    def conv_piece(raw, lo):
        cols = slice(lo, lo + raw.shape[1])
        cbuf_ref[SUBLANES:SUBLANES + tb, cols] = raw
        y = convb_ref[:, cols] + raw * cw[3:4, cols]
        for j in range(CONV_W - 1):
            off = SUBLANES - (CONV_W - 1) + j
            y = y + cbuf_ref[off:off + tb, cols] * cw[j:j + 1, cols]
        last = cbuf_ref[tb:tb + SUBLANES, cols]
        tail_ref[0, :, cols] = last
        cbuf_ref[0:SUBLANES, cols] = last
        return y * _sigmoid(y)

    return conv_piece


def _tok_conv(b0_ref, b1_ref, b2_ref, raw_ref, convw_ref, convb_ref):
    cw = convw_ref[...]

    def conv_piece(raw, lo):
        cols = slice(lo, lo + raw.shape[1])
        raw_ref[:, cols] = raw
        y = (convb_ref[:, cols] + b0_ref[:, cols] * cw[0:1, cols] + b1_ref[:, cols] * cw[1:2, cols]
             + b2_ref[:, cols] * cw[2:3, cols] + raw * cw[3:4, cols])
        return y * _sigmoid(y)

    return conv_piece


_PROJ_NAMES = ("mq", "mk", "mv", "mo", "gq", "gk", "gv", "gr", "gates", "la")
_PROJ_WIDTH = dict(mq=M_WIDTH, mk=M_WIDTH, mv=M_WIDTH, mo=M_WIDTH, gq=G_KW, gk=G_KW, gv=G_VW, gr=G_VW,
                   gates=LANES, la=G_KW)
_PROJ_DTYPE = dict(mq=BF16, mk=F32, mv=BF16, mo=F32, gq=F32, gk=F32, gv=BF16, gr=F32, gates=F32, la=F32)
_PROJ_WEIGHTS = ("w_big", "w_small", "b_small", "w_a2p", "b_a")


def _proj_tok_kernel(x_ref, b0_ref, b1_ref, b2_ref, gmix_ref, wbig_ref, wsmall_ref, bsmall_ref, wa2_ref,
                     ba_ref, convw_ref, convb_ref, *out_refs):
    sinks = dict(zip(_PROJ_NAMES, out_refs[:-1]))
    raw_ref = out_refs[-1]
    w = dict(zip(_PROJ_WEIGHTS, (wbig_ref, wsmall_ref, bsmall_ref, wa2_ref, ba_ref)))
    h = _rms(x_ref[...], gmix_ref[...]).astype(BF16)
    _run(_proj_stages(h, w, sinks, _tok_conv(b0_ref, b1_ref, b2_ref, raw_ref, convw_ref, convb_ref)))


def _proj_tok_call(x2d, conv_rows, wts):
    n = x2d.shape[0]
    row = lambda width: pl.BlockSpec((n, width), lambda i: (0, 0))
    whole = pl.BlockSpec(memory_space=pltpu.VMEM)
    names = list(_PROJ_NAMES)
    return pl.pallas_call(
        _proj_tok_kernel,
        grid=(1,),
        in_specs=[row(D_MODEL)] + [row(QK_CONV)] * 3 + [whole] * 8,
        out_specs=[row(_PROJ_WIDTH[k]) for k in names] + [row(QK_CONV)],
        out_shape=[jax.ShapeDtypeStruct((n, _PROJ_WIDTH[k]), _PROJ_DTYPE[k]) for k in names]
        + [jax.ShapeDtypeStruct((n, QK_CONV), F32)],
        compiler_params=pltpu.CompilerParams(dimension_semantics=("arbitrary",), vmem_limit_bytes=VMEM_LIMIT),
        name="proj_tok",
    )(x2d, *conv_rows, wts["g_mix"], wts["w_big"], wts["w_small"], wts["b_small"], wts["w_a2p"], wts["b_a"],
      wts["conv_w"], wts["conv_b"])


def _mix_stages(src, tril_ref, tri_ref, wred_ref, caug_ref, m_ref, sbd_ref, hm_ref, hg_ref):
    ts = src["mq"].shape[0]
    n_chunks = ts // CHUNK
    gates = src["gates"][...]
    bcum = _cumsum_dot(tril_ref[...], gates)
    gq = src["gq"][...]
    gk = src["gk"][...]
    gv_ref = src["gv"]
    bc = _cumsum_dot(tri_ref[...], src["la"][...])
    gates_t = gates.T
    bcum_t = bcum.T
    bc_t = bc.T
    yield

    nb = ts // SUB
    half = SUB // 2
    assert half == SUBLANES
    q3 = gq.reshape(nb, SUB, G_KW)
    k3 = gk.reshape(nb, SUB, G_KW)
    bc3 = bc.reshape(nb, SUB, G_KW)
    tl = lax.broadcasted_iota(jnp.int32, (nb, SUB, G_KW), 1)
    q3u = gq.reshape(nb, 2, half, G_KW)[:, 1]
    bc3u = bc.reshape(nb, 2, half, G_KW)[:, 1]
    tlu = lax.broadcasted_iota(jnp.int32, (nb, half, G_KW), 1) + half

    def exact_pass(j, acc):
        acc_all, acc_upper = acc
        if j < half:
            arg = jnp.where(tl >= j, bc3 - bc3[:, j:j + 1, :], NEG)
            e = (q3 * k3[:, j:j + 1, :] * jnp.exp(arg)).reshape(ts, G_KW)
            return acc_all + _dot(e.astype(BF16), wred_ref[j]), acc_upper
        arg = jnp.where(tlu >= j, bc3u - bc3[:, j:j + 1, :], NEG)
        e = (q3u * k3[:, j:j + 1, :] * jnp.exp(arg)).reshape(ts // 2, G_KW)
        return acc_all, acc_upper + _dot(e.astype(BF16), wred_ref[j])

    krow = lax.broadcasted_iota(jnp.int32, (CHUNK, G_KW), 0)
    same_head_kk = (_div_pow2(lax.broadcasted_iota(jnp.int32, (G_KW, G_KW), 0), G_DK)
                    == _div_pow2(lax.broadcasted_iota(jnp.int32, (G_KW, G_KW), 1), G_DK))
    same_head_kv = (_div_pow2(lax.broadcasted_iota(jnp.int32, (G_KW, G_VW), 0), G_DK)
                    == _div_pow2(lax.broadcasted_iota(jnp.int32, (G_KW, G_VW), 1), G_DV))

    def cross_block_scores(c):
        lo = c * CHUNK
        bc_c = bc[lo:lo + CHUNK]
        k_c = gk[lo:lo + CHUNK]
        out = [jnp.zeros((SUB, G_KW), F32)]
        for i in range(1, CHUNK // SUB):
            r0 = lo + i * SUB
            r_i = bc[r0 - 1:r0, :]
            qi = (gq[r0:r0 + SUB] * jnp.exp(bc[r0:r0 + SUB] - r_i)).astype(BF16)
            ki = (k_c * jnp.exp(jnp.where(krow < i * SUB, r_i - bc_c, NEG))).astype(BF16)
            kbd = jnp.where(same_head_kk, jnp.concatenate([ki] * G_HEADS, axis=0), jnp.zeros((), BF16))
            out.append(_dot_nt(qi, kbd))
        return out

    def state_update_term(c):
        lo, hi = c * CHUNK, (c + 1) * CHUNK
        b_end = bc[hi - 1:hi, :]
        k_out_t = (gk[lo:hi] * jnp.exp(b_end - bc[lo:hi])).T.astype(BF16)
        rows = []
        for g in range(G_HEADS):
            blk = _dot(k_out_t[g * G_DK:(g + 1) * G_DK], gv_ref[lo:hi, g * G_DV:(g + 1) * G_DV])
            zero = jnp.zeros((G_DK, G_DV), F32)
            rows.append(jnp.concatenate([zero] * g + [blk] + [zero] * (G_HEADS - 1 - g), axis=1))
        return jnp.concatenate(rows, axis=0)

    adiag = (jnp.zeros((ts, G_KW), F32), jnp.zeros((ts // 2, G_KW), F32))
    per_stage = SUB // 4
    assert n_chunks == 4
    offs = []
    s_terms = []

    causal = (lax.broadcasted_iota(jnp.int32, (ts, ts), 0) >= lax.broadcasted_iota(jnp.int32, (ts, ts), 1))
    ones_col = (lax.broadcasted_iota(jnp.int32, (ts, LANES), 1) == 0).astype(BF16)
    heads = range(M_HEADS)
    hsl = [slice(hd * M_DH, (hd + 1) * M_DH) for hd in heads]
    b_col = [bcum[:, M_HEADS + hd:M_HEADS + hd + 1] for hd in heads]
    b_row = [bcum_t[M_HEADS + hd:M_HEADS + hd + 1, :] for hd in heads]
    i_col = [gates[:, hd:hd + 1] for hd in heads]
    i_row = [gates_t[hd:hd + 1, :] for hd in heads]
    m_prev = [m_ref[hd:hd + 1, 0:1] for hd in heads]
    q = [src["mq"][:, hsl[hd]] for hd in heads]
    k = [src["mk"][:, hsl[hd]] for hd in heads]
    vaug = [jnp.concatenate([src["mv"][:, hsl[hd]], ones_col], axis=1) for hd in heads]
    caug = [caug_ref[hd] for hd in heads]
    s_qk = [_dot_nt(q[hd], k[hd].astype(BF16)) for hd in heads]
    qc = [_dot(q[hd], caug[hd].astype(BF16)) for hd in heads]
    for j in range(0, per_stage):
        adiag = exact_pass(j, adiag)
    offs += cross_block_scores(0)
    s_terms.append(state_update_term(0))
    yield
    b_last = [b_col[hd][ts - 1:ts, :] for hd in heads]
    dec = [b_last[hd] - b_col[hd] + i_col[hd] for hd in heads]
    m_new = [jnp.maximum(b_last[hd] + m_prev[hd], jnp.max(dec[hd], axis=0, keepdims=True)) for hd in heads]
    kw_t = [(k[hd] * jnp.exp(dec[hd] - m_new[hd])).T.astype(BF16) for hd in heads]
    upd = [_dot(kw_t[hd], vaug[hd]) for hd in heads]
    for j in range(per_stage, 2 * per_stage):
        adiag = exact_pass(j, adiag)
    offs += cross_block_scores(1)
    s_terms.append(state_update_term(1))
    yield
    dmat = [jnp.where(causal, b_col[hd] - b_row[hd] + i_row[hd], -jnp.inf) for hd in heads]
    inter = [b_col[hd] + m_prev[hd] for hd in heads]
    m_tok = [jnp.maximum(inter[hd], jnp.max(dmat[hd], axis=1, keepdims=True)) for hd in heads]
    for j in range(2 * per_stage, 3 * per_stage):
        adiag = exact_pass(j, adiag)
    offs += cross_block_scores(2)
    s_terms.append(state_update_term(2))
    yield
    p = [(s_qk[hd] * jnp.exp(dmat[hd] - m_tok[hd])).astype(BF16) for hd in heads]
    pv = [_dot(p[hd], vaug[hd]) for hd in heads]
    for j in range(3 * per_stage, SUB):
        adiag = exact_pass(j, adiag)
    offs += cross_block_scores(3)
    s_terms.append(state_update_term(3))
    yield
    for hd in heads:
        tot = jnp.exp(inter[hd] - m_tok[hd]) * qc[hd] + pv[hd]
        den = tot[:, M_DH:M_DH + 1]
        hm_ref[:, hsl[hd]] = tot[:, :M_DH] / jnp.maximum(jnp.abs(den), jnp.exp(-m_tok[hd]))
        caug_ref[hd] = jnp.exp(b_last[hd] + m_prev[hd] - m_new[hd]) * caug[hd] + upd[hd]
        m_ref[hd:hd + 1, :] = jnp.broadcast_to(m_new[hd], (1, LANES))
    yield

    sub_of = lambda idx: lax.shift_right_logical(idx & (CHUNK - 1), SUB.bit_length() - 1)
    rowb = sub_of(lax.broadcasted_iota(jnp.int32, (ts, G_KW), 0))
    colb = sub_of(lax.broadcasted_iota(jnp.int32, (ts, G_KW), 1))
    acc_all, acc_upper = adiag
    upper = acc_upper.reshape(nb, half, G_KW)
    acc_all = acc_all + jnp.stack([jnp.zeros_like(upper), upper], axis=1).reshape(ts, G_KW)
    adiag = jnp.where(rowb == colb, acc_all, 0.0)
    intra = (jnp.concatenate(offs, axis=0) + adiag).astype(BF16)
    q_in = (gq * jnp.exp(bc)).astype(BF16)
    o_intra = []
    for c in range(n_chunks):
        lo, hi = c * CHUNK, (c + 1) * CHUNK
        vbd = jnp.where(same_head_kv, jnp.concatenate([gv_ref[lo:hi, :]] * G_HEADS, axis=0),
                        jnp.zeros((), BF16))
        o_intra.append(_dot(intra[lo:hi], vbd))
    sbd = sbd_ref[...]
    for c in range(n_chunks):
        lo, hi = c * CHUNK, (c + 1) * CHUNK
        hg_ref[lo:hi, :] = o_intra[c] + _dot(q_in[lo:hi], sbd.astype(BF16))
        dcol = jnp.exp(bc_t[:, hi - 1:hi])
        sbd = dcol * sbd + s_terms[c]
    sbd_ref[...] = sbd
    yield


def _head_norm(hv, n_heads, width):
    parts = []
    for hd in range(n_heads):
        seg = hv[:, hd * width:(hd + 1) * width]
        parts.append(seg * lax.rsqrt(jnp.mean(seg * seg, axis=-1, keepdims=True) + EPS))
    return jnp.concatenate(parts, axis=1)


def _gate_heads(hm, hg, mo, gr, gmh, ggh):
    hm = _head_norm(hm, M_HEADS, M_DH) * gmh * _sigmoid(mo)
    hg = _head_norm(hg, G_HEADS, G_DV) * ggh * (gr * _sigmoid(gr))
    return jnp.concatenate([hm, hg], axis=1).astype(BF16)


_POST_WEIGHTS = ("w_out", "g_mlp", "w1", "w2", "g_ple", "w_ple", "w_pg", "g_final")


def _post_stages(x, p, mixed, w, y_ref):
    x1 = x + _dot(mixed, w["w_out"][...])
    n1 = _rms(x1, w["g_mlp"][...]).astype(BF16)
    yield
    n_ff = 4
    ff = D_FF // n_ff
    acts = []
    for j in range(n_ff):
        u = _dot(n1, w["w1"][:, j * ff:(j + 1) * ff])
        acts.append(jnp.square(jnp.maximum(u, 0.0)).astype(BF16))
        yield
    act = jnp.concatenate(acts, axis=1)
    half = D_MODEL // 2
    mlp = []
    for j in range(2):
        mlp.append(_dot(act, w["w2"][:, j * half:(j + 1) * half]))
        yield
    x2 = x1 + jnp.concatenate(mlp, axis=1)
    ple = _dot(p.astype(BF16), w["w_ple"][...])
    n2 = _rms(x2, w["g_ple"][...]).astype(BF16)
    yield
    gate = _sigmoid(_dot(n2, w["w_pg"][...]))
    x3 = x2 + ple * gate
    y_ref[...] = _rms(x3, w["g_final"][...])
    yield


def _post_tok_kernel(x_ref, p_ref, hm_ref, hg_ref, mo_ref, gr_ref, gmh_ref, ggh_ref, wout_ref, gmlp_ref,
                     w1_ref, w2_ref, gple_ref, wple_ref, wpg_ref, gfin_ref, y_ref):
    mixed = _gate_heads(hm_ref[...], hg_ref[...], mo_ref[...], gr_ref[...], gmh_ref[...], ggh_ref[...])
    w = dict(zip(_POST_WEIGHTS, (wout_ref, gmlp_ref, w1_ref, w2_ref, gple_ref, wple_ref, wpg_ref, gfin_ref)))
    _run(_post_stages(x_ref[...], p_ref[...], mixed, w, y_ref))


def _post_tok_call(x2d, p2d, hm, hg, mo, gr, wts):
    n = x2d.shape[0]
    row = lambda width: pl.BlockSpec((n, width), lambda i: (0, 0))
    whole = pl.BlockSpec(memory_space=pltpu.VMEM)
    return pl.pallas_call(
        _post_tok_kernel,
        grid=(1,),
        in_specs=[row(D_MODEL), row(D_PLE), row(M_WIDTH), row(G_VW), row(M_WIDTH), row(G_VW)] + [whole] * 10,
        out_specs=row(D_MODEL),
        out_shape=jax.ShapeDtypeStruct((n, D_MODEL), F32),
        compiler_params=pltpu.CompilerParams(dimension_semantics=("arbitrary",), vmem_limit_bytes=VMEM_LIMIT),
        name="post_tok",
    )(x2d, p2d, hm, hg, mo, gr, wts["g_mhead"], wts["g_ghead"], *[wts[k] for k in _POST_WEIGHTS])


def _seq_kernel(xa_ref, xb_ref, pb_ref, gmix_ref, wbig_ref, wsmall_ref, bsmall_ref, wa2_ref, ba_ref,
                convw_ref, convb_ref, tril_ref, tri_ref, wred_ref, gmh_ref, ggh_ref,
                wout_ref, gmlp_ref, w1_ref, w2_ref, gple_ref, wple_ref, wpg_ref, gfin_ref,
                y_ref, tail_ref, cout_ref, mout_ref, sout_ref,
                cbuf_ref, caug_ref, m_ref, sbd_ref, mixed_ref,
                mq_s, mk_s, mv_s, mo_s, gq_s, gk_s, gv_s, gr_s, gates_s, la_s, hm_s, hg_s,
                *, steps_per_seq, n_blocks):
    tb = xa_ref.shape[0]
    s = pl.program_id(0)
    r = lax.rem(jnp.minimum(s, n_blocks - 1), steps_per_seq)

    @pl.when(s == 0)
    def _():
        mixed_ref[...] = jnp.zeros(mixed_ref.shape, BF16)

    @pl.when(r == 0)
    def _():
        cbuf_ref[0:SUBLANES, :] = jnp.zeros((SUBLANES, QK_CONV), F32)
        caug_ref[...] = jnp.zeros(caug_ref.shape, F32)
        m_ref[...] = jnp.zeros(m_ref.shape, F32)
        sbd_ref[...] = jnp.zeros(sbd_ref.shape, F32)

    slot = lax.rem(s, 2)
    src = dict(zip(_PROJ_NAMES, (mq_s, mk_s, mv_s, mo_s, gq_s, gk_s, gv_s, gr_s, gates_s, la_s)))
    w_in = dict(zip(_PROJ_WEIGHTS, (wbig_ref, wsmall_ref, bsmall_ref, wa2_ref, ba_ref)))
    w_post = dict(zip(_POST_WEIGHTS, (wout_ref, gmlp_ref, w1_ref, w2_ref, gple_ref, wple_ref, wpg_ref, gfin_ref)))

    def front():
        h = _rms(xa_ref[...], gmix_ref[...]).astype(BF16)
        yield from _proj_stages(h, w_in, src, _seq_conv(cbuf_ref, tail_ref, convw_ref, convb_ref, tb))
        yield from _mix_stages(src, tril_ref, tri_ref, wred_ref, caug_ref, m_ref, sbd_ref, hm_s, hg_s)
        mixed_ref[slot] = _gate_heads(hm_s[...], hg_s[...], mo_s[...], gr_s[...], gmh_ref[...], ggh_ref[...])
        yield

    def back():
        yield from _post_stages(xb_ref[...], pb_ref[...], mixed_ref[1 - slot], w_post, y_ref)

    _interleave("bfff" "bfff" "bfff" "bff" "bf" "bf" "bf" "bf" "bff", front=front(), back=back())

    @pl.when(jnp.logical_and(s < n_blocks, r == steps_per_seq - 1))
    def _():
        cout_ref[0] = caug_ref[...]
        mout_ref[0] = m_ref[...]
        for g in range(G_HEADS):
            sout_ref[0, g] = sbd_ref[g * G_DK:(g + 1) * G_DK, g * G_DV:(g + 1) * G_DV]


def _seq_call(x2d, p2d, wts, consts, *, n_seq, seq_len):
    tb = SEQ_BLOCK
    steps_per_seq = seq_len // tb
    n_blocks = n_seq * steps_per_seq
    tril, tri, wred = consts
    front_blk = lambda s: jnp.minimum(s, n_blocks - 1)
    back_blk = lambda s: jnp.maximum(s - 1, 0)
    whole = pl.BlockSpec(memory_space=pltpu.VMEM)
    per_seq = lambda shape: pl.BlockSpec((1,) + shape, lambda s: (front_blk(s) // steps_per_seq,) + (0,) * len(shape))
    scratch = [
        pltpu.VMEM((tb + SUBLANES, QK_CONV), F32),
        pltpu.VMEM((M_HEADS, M_DH, 2 * M_DH), F32),
        pltpu.VMEM((SUBLANES, LANES), F32),
        pltpu.VMEM((G_KW, G_VW), F32),
        pltpu.VMEM((2, tb, D_MODEL), BF16),
    ] + [pltpu.VMEM((tb, _PROJ_WIDTH[k]), _PROJ_DTYPE[k]) for k in _PROJ_NAMES] + [
        pltpu.VMEM((tb, M_WIDTH), F32), pltpu.VMEM((tb, G_VW), F32)]
    return pl.pallas_call(
        functools.partial(_seq_kernel, steps_per_seq=steps_per_seq, n_blocks=n_blocks),
        grid=(n_blocks + 1,),
        in_specs=[pl.BlockSpec((tb, D_MODEL), lambda s: (front_blk(s), 0)),
                  pl.BlockSpec((tb, D_MODEL), lambda s: (back_blk(s), 0)),
                  pl.BlockSpec((tb, D_PLE), lambda s: (back_blk(s), 0))] + [whole] * 21,
        out_specs=[pl.BlockSpec((tb, D_MODEL), lambda s: (back_blk(s), 0)),
                   per_seq((SUBLANES, QK_CONV)), per_seq((M_HEADS, M_DH, 2 * M_DH)),
                   per_seq((SUBLANES, LANES)), per_seq((G_HEADS, G_DK, G_DV))],
        out_shape=[jax.ShapeDtypeStruct((n_blocks * tb, D_MODEL), F32),
                   jax.ShapeDtypeStruct((n_seq, SUBLANES, QK_CONV), F32),
                   jax.ShapeDtypeStruct((n_seq, M_HEADS, M_DH, 2 * M_DH), F32),
                   jax.ShapeDtypeStruct((n_seq, SUBLANES, LANES), F32),
                   jax.ShapeDtypeStruct((n_seq, G_HEADS, G_DK, G_DV), F32)],
        scratch_shapes=scratch,
        compiler_params=pltpu.CompilerParams(dimension_semantics=("arbitrary",), vmem_limit_bytes=VMEM_LIMIT),
        name="seq_fused",
    )(x2d, x2d, p2d, wts["g_mix"], wts["w_big"], wts["w_small"], wts["b_small"], wts["w_a2p"], wts["b_a"],
      wts["conv_w"], wts["conv_b"], tril, tri, wred, wts["g_mhead"], wts["g_ghead"],
      *[wts[k] for k in _POST_WEIGHTS])


def _mix_tok_kernel(mq_ref, mk_ref, mv_ref, gates_ref, gq_ref, gk_ref, gv_ref, la_ref,
                    c_ref, n_ref, m_ref, s_ref,
                    hm_ref, hg_ref, cn_ref, nn_ref, mn_ref, sn_ref):
    bb = mq_ref.shape[0]
    wide = bb * LANES
    diag = (lax.broadcasted_iota(jnp.int32, (bb, wide), 0)
            == _div_pow2(lax.broadcasted_iota(jnp.int32, (bb, wide), 1), LANES))
    block_ones = diag.astype(BF16)
    ones_rows = jnp.ones((M_DH, bb), BF16)

    def block_diag(rows):
        return jnp.where(diag, jnp.concatenate([rows] * bb, axis=1), jnp.zeros((), rows.dtype))

    def block_diag_f32(col):
        return jnp.where(diag, col, 0.0)

    def own_block(x):
        x = jnp.where(diag, x, 0.0)
        out = x[:, 0:LANES]
        for b in range(1, bb):
            out = out + x[:, b * LANES:(b + 1) * LANES]
        return out

    gates = gates_ref[...]
    m_all = m_ref[...]
    for hd in range(M_HEADS):
        cs_, ce_ = hd * M_DH, (hd + 1) * M_DH
        q_bf = mq_ref[:, cs_:ce_]
        q = q_bf.astype(F32)
        k = mk_ref[:, cs_:ce_]
        v_bf = mv_ref[:, cs_:ce_]
        v = v_bf.astype(F32)
        n_prev = n_ref[:, cs_:ce_]
        ig = gates[:, hd:hd + 1]
        lf = gates[:, M_HEADS + hd:M_HEADS + hd + 1]
        m_prev = m_all[:, hd:hd + 1]
        m_new = jnp.maximum(lf + m_prev, ig)
        scale = jnp.exp(lf + m_prev - m_new)
        wk = jnp.exp(ig - m_new)
        s_qk = jnp.sum(q * k, axis=1, keepdims=True) * wk
        den = scale * jnp.sum(q * n_prev, axis=1, keepdims=True) + s_qk
        inv = 1.0 / jnp.maximum(jnp.abs(den), jnp.exp(-m_new))
        nn_ref[:, cs_:ce_] = scale * n_prev + wk * k
        mn_ref[:, hd:hd + 1] = m_new
        c_prev = [c_ref[b, hd] for b in range(bb)]
        qc = own_block(_dot(q_bf, jnp.concatenate([c.astype(BF16) for c in c_prev], axis=1)))
        hm_ref[:, cs_:ce_] = (scale * qc + s_qk * v) * inv
        outer = _dot((k * wk).T.astype(BF16), block_diag(v_bf))
        scale_w = _cumsum_dot(ones_rows, block_diag_f32(scale))
        for b in range(bb):
            blk = slice(b * M_DH, (b + 1) * M_DH)
            cn_ref[b, hd] = scale_w[:, blk] * c_prev[b] + outer[:, blk]
    la = la_ref[...]
    dec = jnp.exp(la)
    gq = gq_ref[...]
    gk = gk_ref[...]
    for g in range(G_HEADS):
        ks_, ke_ = g * G_DK, (g + 1) * G_DK
        vs_, ve_ = g * G_DV, (g + 1) * G_DV
        q = gq[:, ks_:ke_]
        k = gk[:, ks_:ke_]
        v_bf = gv_ref[:, vs_:ve_]
        a = jnp.sum(q * k, axis=1, keepdims=True)
        s_prev = [s_ref[b, g] for b in range(bb)]
        o = own_block(_dot((q * dec[:, ks_:ke_]).astype(BF16),
                           jnp.concatenate([s.astype(BF16) for s in s_prev], axis=1)))
        hg_ref[:, vs_:ve_] = o + a * v_bf.astype(F32)
        outer = _dot(k.T.astype(BF16), block_diag(v_bf))
        dec_w = _spread_dot(dec[:, ks_:ke_].T, block_ones)
        for b in range(bb):
            blk = slice(b * G_DV, (b + 1) * G_DV)
            sn_ref[b, g] = dec_w[:, blk] * s_prev[b] + outer[:, blk]


def _mix_tok_call(p, c0, n0, m0, s0):
    bb = TOK_BATCH
    n = c0.shape[0]
    row = lambda w: pl.BlockSpec((bb, w), lambda i: (i, 0))
    st4 = lambda a, b_: pl.BlockSpec((bb, M_HEADS, a, b_), lambda i: (i, 0, 0, 0))
    return pl.pallas_call(
        _mix_tok_kernel,
        grid=(n // bb,),
        in_specs=[row(M_WIDTH), row(M_WIDTH), row(M_WIDTH), row(LANES), row(G_KW), row(G_KW), row(G_VW),
                  row(G_KW), st4(M_DH, M_DH), row(M_WIDTH), row(M_HEADS), st4(G_DK, G_DV)],
        out_specs=[row(M_WIDTH), row(G_VW), st4(M_DH, M_DH), row(M_WIDTH), row(M_HEADS), st4(G_DK, G_DV)],
        out_shape=[jax.ShapeDtypeStruct((n, M_WIDTH), F32), jax.ShapeDtypeStruct((n, G_VW), F32),
                   jax.ShapeDtypeStruct(c0.shape, F32), jax.ShapeDtypeStruct((n, M_WIDTH), F32),
                   jax.ShapeDtypeStruct((n, M_HEADS), F32), jax.ShapeDtypeStruct(s0.shape, F32)],
        compiler_params=pltpu.CompilerParams(dimension_semantics=("arbitrary",), vmem_limit_bytes=VMEM_LIMIT),
        name="mix_tok",
    )(p["mq"], p["mk"], p["mv"], p["gates"], p["gq"], p["gk"], p["gv"], p["la"], c0, n0, m0, s0)


_IN_OFFS = tuple(sum(IN_SIZES[:i]) for i in range(len(IN_SIZES) + 1))
_GATES_LO, _GATES_HI, _GA_LO = _IN_OFFS[3], _IN_OFFS[5], _IN_OFFS[9]
_N_SMALL = 2 * M_HEADS + G_RANK
_W_BIG_COLS = _GATES_LO + (_GA_LO - _GATES_HI)
_CAST_STEPS = 8


def _cast_kernel(wout_ref, w1_ref, w2_ref, wple_ref, wpg_ref, o_out, o_w1, o_w2, o_ple, o_pg):
    o_out[...] = wout_ref[...].astype(BF16)
    o_w1[...] = w1_ref[...].astype(BF16)
    o_w2[...] = w2_ref[...].astype(BF16)
    o_ple[...] = wple_ref[...].astype(BF16)
    o_pg[...] = wpg_ref[...].astype(BF16)


def _cast_call(*srcs):
    rows = lambda a: pl.BlockSpec((a.shape[0] // _CAST_STEPS, a.shape[1]), lambda i: (i, 0))
    return pl.pallas_call(
        _cast_kernel,
        grid=(_CAST_STEPS,),
        in_specs=[rows(a) for a in srcs],
        out_specs=[rows(a) for a in srcs],
        out_shape=[jax.ShapeDtypeStruct(a.shape, BF16) for a in srcs],
        compiler_params=pltpu.CompilerParams(dimension_semantics=("arbitrary",), vmem_limit_bytes=VMEM_LIMIT),
        name="cast_weights",
    )(*srcs)


def _cast_in_kernel(wt_ref, big_ref, small_ref):
    piece = M_WIDTH
    for n in range(_W_BIG_COLS // piece):
        dst = n * piece
        src = dst if dst < _GATES_LO else dst + (_GATES_HI - _GATES_LO)
        big_ref[:, dst:dst + piece] = wt_ref[src:src + piece, :].T.astype(BF16)
    small = jnp.concatenate([wt_ref[_GATES_LO:_GATES_HI, :], wt_ref[_GA_LO:, :],
                             jnp.zeros((LANES - _N_SMALL, D_MODEL), F32)], axis=0)
    small_ref[...] = small.T.astype(BF16)


def _cast_in_call(w_in_t):
    whole = pl.BlockSpec(memory_space=pltpu.VMEM)
    assert _GATES_LO % M_WIDTH == 0
    return pl.pallas_call(
        _cast_in_kernel,
        in_specs=[whole], out_specs=[whole, whole],
        out_shape=[jax.ShapeDtypeStruct((D_MODEL, _W_BIG_COLS), BF16),
                   jax.ShapeDtypeStruct((D_MODEL, LANES), BF16)],
        compiler_params=pltpu.CompilerParams(vmem_limit_bytes=VMEM_LIMIT),
        name="cast_w_in",
    )(w_in_t)


def _prep_weights(w_in, conv_w, conv_b, b_gate, w_a2, b_a, g_mhead, g_ghead, w_out, g_mix, g_mlp, w1, w2,
                  g_ple, w_ple, w_pg, g_final):
    w_big, w_small = _cast_in_call(jnp.swapaxes(w_in, 0, 1))
    w_out_b, w1_b, w2_b, w_ple_b, w_pg_b = _cast_call(w_out, w1, w2, w_ple, w_pg)
    b_small = jnp.concatenate([b_gate, jnp.zeros((LANES - 2 * M_HEADS,), F32)])[None]
    w_a2p = jnp.concatenate([jnp.zeros((2 * M_HEADS, G_KW), F32), w_a2,
                             jnp.zeros((LANES - _N_SMALL, G_KW), F32)], axis=0).astype(BF16)
    return dict(
        w_big=w_big, w_small=w_small, b_small=b_small, w_a2p=w_a2p, b_a=b_a[None],
        conv_w=conv_w, conv_b=conv_b[None], g_mix=g_mix[None], g_mhead=g_mhead[None], g_ghead=g_ghead[None],
        w_out=w_out_b, g_mlp=g_mlp[None], w1=w1_b, w2=w2_b,
        g_ple=g_ple[None], w_ple=w_ple_b, w_pg=w_pg_b, g_final=g_final[None])


def _mix_constants(ts):
    t = jnp.arange(ts)
    tril = (t[None, :] <= t[:, None]).astype(BF16)
    tri = ((t[:, None] // CHUNK == t[None, :] // CHUNK) & (t[None, :] <= t[:, None])).astype(BF16)
    rr = jnp.arange(G_KW)
    wred = ((rr[None, :, None] // G_DK == rr[None, None, :] // G_DK)
            & (rr[None, None, :] % SUB == jnp.arange(SUB)[:, None, None])).astype(BF16)
    return tril, tri, wred


def kernel(x_prompt, x_sample, p_prompt, p_sample, state_mlstm_C, state_mlstm_n, state_mlstm_m, state_conv,
           state_gla_S, w_in, conv_w, conv_b, b_gate, w_a2, b_a, g_mhead, g_ghead, w_out, g_mix, g_mlp, w1,
           w2, g_ple, w_ple, w_pg, g_final):
    assert w_in.shape[0] == 1, "single-layer trunk"
    n_seq, seq_len, _ = x_prompt.shape
    n_tok = x_sample.shape[0]
    assert x_sample.shape[1] == 1 and seq_len % SEQ_BLOCK == 0 and n_tok % TOK_BATCH == 0
    wts = _prep_weights(w_in[0], conv_w[0], conv_b[0], b_gate[0], w_a2[0], b_a[0], g_mhead[0], g_ghead[0],
                        w_out[0], g_mix[0], g_mlp[0], w1[0], w2[0], g_ple[0], w_ple[0], w_pg[0], g_final)

    y_p, tail_p, caug_p, m_p, s_p = _seq_call(
        x_prompt.reshape(n_seq * seq_len, D_MODEL), p_prompt[0].reshape(n_seq * seq_len, D_PLE), wts,
        _mix_constants(SEQ_BLOCK), n_seq=n_seq, seq_len=seq_len)

    xs = x_sample.reshape(n_tok, D_MODEL)
    buf = state_conv[0]
    outs = _proj_tok_call(xs, (buf[:, 0], buf[:, 1], buf[:, 2]), wts)
    ps = dict(zip(_PROJ_NAMES, outs[:-1]))
    raw_s = outs[-1]
    hm_s, hg_s, c_s, n_s, m_s, s_s = _mix_tok_call(
        ps, state_mlstm_C[0], state_mlstm_n[0].reshape(n_tok, M_WIDTH), state_mlstm_m[0], state_gla_S[0])
    y_s = _post_tok_call(xs, p_sample[0].reshape(n_tok, D_PLE), hm_s, hg_s, ps["mo"], ps["gr"], wts)

    return (y_p.reshape(n_seq, seq_len, D_MODEL),
            y_s.reshape(n_tok, 1, D_MODEL),
            caug_p[None, :, :, :, :M_DH],
            caug_p[None, :, :, :, M_DH],
            m_p[None, :, :M_HEADS, 0],
            tail_p[None, :, SUBLANES - (CONV_W - 1):, :],
            s_p[None],
            c_s[None],
            n_s.reshape(1, n_tok, M_HEADS, M_DH),
            m_s[None],
            jnp.stack([buf[:, 1], buf[:, 2], raw_s], axis=1)[None],
            s_s[None])
```

```python
import functools

import jax
import jax.numpy as jnp
from jax import lax
from jax.experimental import pallas as pl
from jax.experimental.pallas import tpu as pltpu

D_MODEL = 1024
M_HEADS = 4
M_DH = 128
M_WIDTH = M_HEADS * M_DH
G_HEADS = 4
G_DK = 64
G_DV = 128
G_KW = G_HEADS * G_DK
G_VW = G_HEADS * G_DV
G_RANK = 16
G_TAU = 16.0
CONV_W = 4
QK_CONV = 2 * M_WIDTH
D_FF = 4 * D_MODEL
D_PLE = 256
CHUNK = 64
SUB = 16
EPS = 1e-6
IN_SIZES = (QK_CONV, M_WIDTH, M_WIDTH, M_HEADS, M_HEADS, G_KW, G_KW, G_VW, G_VW, G_RANK)

LANES = 128
SUBLANES = 8
VMEM_LIMIT = 60 * 1024 * 1024
SEQ_BLOCK = 256
TOK_BATCH = 16

F32 = jnp.float32
BF16 = jnp.bfloat16
NEG = -1e30


def _rms(x, g):
    return x * lax.rsqrt(jnp.mean(x * x, axis=-1, keepdims=True) + EPS) * g


def _log_sigmoid(x):
    return jnp.minimum(x, 0.0) - jnp.log(1.0 + jnp.exp(-jnp.abs(x)))


def _sigmoid(x):
    return 0.5 * jnp.tanh(0.5 * x) + 0.5


def _div_pow2(idx, d):
    assert d & (d - 1) == 0
    return lax.shift_right_logical(idx, d.bit_length() - 1)


def _dot(a, b):
    return jnp.dot(a, b, preferred_element_type=F32)


def _dot_nt(a, b):
    return lax.dot_general(a, b, (((1,), (1,)), ((), ())), preferred_element_type=F32)


def _split3(x):
    hi = x.astype(BF16)
    r1 = x - hi.astype(F32)
    mid = r1.astype(BF16)
    lo = (r1 - mid.astype(F32)).astype(BF16)
    return hi, mid, lo


def _cumsum_dot(tri, x):
    hi, mid, lo = _split3(x)
    return _dot(tri, hi) + _dot(tri, mid) + _dot(tri, lo)


def _spread_dot(x, onehot):
    hi, mid, lo = _split3(x)
    return _dot(hi, onehot) + _dot(mid, onehot) + _dot(lo, onehot)


def _interleave(pattern, **streams):
    by_letter = {name[0]: gen for name, gen in streams.items()}
    for letter in pattern:
        next(by_letter[letter], None)
    for gen in by_letter.values():
        _run(gen)


def _run(gen):
    for _ in gen:
        pass


def _proj_stages(h, w, sinks, conv_piece):
    width = M_WIDTH

    def put_q(raw):
        sinks["mq"][...] = conv_piece(raw, 0).astype(sinks["mq"].dtype)

    def put_k(raw):
        kk = conv_piece(raw, M_WIDTH) * (M_DH ** -0.5)
        if "mk_t" in sinks:
            sinks["mk_t"][...] = kk.T
        else:
            sinks["mk"][...] = kk

    def put_mv(raw):
        sinks["mv"][...] = raw.astype(sinks["mv"].dtype)

    def put_mo(raw):
        sinks["mo"][...] = raw

    def put_gqk(raw):
        sinks["gq"][...] = raw[:, :G_KW] * (G_DK ** -0.5)
        sinks["gk"][...] = raw[:, G_KW:]

    def put_gv(raw):
        sinks["gv"][...] = raw.astype(sinks["gv"].dtype)

    def put_gr(raw):
        sinks["gr"][...] = raw

    def put_small(small):
        g = small + w["b_small"][...]
        lane = lax.broadcasted_iota(jnp.int32, g.shape, 1)
        sinks["gates"][...] = jnp.where(lane < M_HEADS, g, _log_sigmoid(g))
        z = _dot(small.astype(BF16), w["w_a2p"][...]) + w["b_a"][...]
        sinks["la"][...] = _log_sigmoid(z) * (1.0 / G_TAU)

    epilogues = [put_small, put_q, put_k, put_mv, put_mo, put_gqk, put_gv, put_gr]
    assert w["w_big"].shape[1] == width * (len(epilogues) - 1)
    pending = (_dot(h, w["w_small"][...]), epilogues[0])
    yield
    for n, epi in enumerate(epilogues[1:]):
        cur = _dot(h, w["w_big"][:, n * width:(n + 1) * width])
        pending[1](pending[0])
        pending = (cur, epi)
        yield
    pending[1](pending[0])
    yield


def _seq_conv(cbuf_ref, tail_ref, convw_ref, convb_ref, tb):
    cw = convw_ref[...]

    def conv_piece(raw, lo):
        cols = slice(lo, lo + raw.shape[1])
        cbuf_ref[SUBLANES:SUBLANES + tb, cols] = raw
        y = convb_ref[:, cols] + raw * cw[3:4, cols]
        for j in range(CONV_W - 1):
            off = SUBLANES - (CONV_W - 1) + j
            y = y + cbuf_ref[off:off + tb, cols] * cw[j:j + 1, cols]
        last = cbuf_ref[tb:tb + SUBLANES, cols]
        tail_ref[0, :, cols] = last
        cbuf_ref[0:SUBLANES, cols] = last
        return y * _sigmoid(y)

    return conv_piece


def _tok_conv(b0_ref, b1_ref, b2_ref, raw_ref, convw_ref, convb_ref):
    cw = convw_ref[...]

    def conv_piece(raw, lo):
        cols = slice(lo, lo + raw.shape[1])
        raw_ref[:, cols] = raw
        y = (convb_ref[:, cols] + b0_ref[:, cols] * cw[0:1, cols] + b1_ref[:, cols] * cw[1:2, cols]
             + b2_ref[:, cols] * cw[2:3, cols] + raw * cw[3:4, cols])
        return y * _sigmoid(y)

    return conv_piece


_PROJ_NAMES = ("mq", "mk", "mv", "mo", "gq", "gk", "gv", "gr", "gates", "la")
_PROJ_WIDTH = dict(mq=M_WIDTH, mk=M_WIDTH, mv=M_WIDTH, mo=M_WIDTH, gq=G_KW, gk=G_KW, gv=G_VW, gr=G_VW,
                   gates=LANES, la=G_KW)
_PROJ_DTYPE = dict(mq=BF16, mk=F32, mv=BF16, mo=F32, gq=F32, gk=F32, gv=BF16, gr=F32, gates=F32, la=F32)
_PROJ_WEIGHTS = ("w_big", "w_small", "b_small", "w_a2p", "b_a")


def _proj_tok_kernel(x_ref, b0_ref, b1_ref, b2_ref, gmix_ref, wbig_ref, wsmall_ref, bsmall_ref, wa2_ref,
                     ba_ref, convw_ref, convb_ref, *out_refs):
    sinks = dict(zip(_PROJ_NAMES, out_refs[:-1]))
    raw_ref = out_refs[-1]
    w = dict(zip(_PROJ_WEIGHTS, (wbig_ref, wsmall_ref, bsmall_ref, wa2_ref, ba_ref)))
    h = _rms(x_ref[...], gmix_ref[...]).astype(BF16)
    _run(_proj_stages(h, w, sinks, _tok_conv(b0_ref, b1_ref, b2_ref, raw_ref, convw_ref, convb_ref)))


def _proj_tok_call(x2d, conv_rows, wts):
    n = x2d.shape[0]
    row = lambda width: pl.BlockSpec((n, width), lambda i: (0, 0))
    whole = pl.BlockSpec(memory_space=pltpu.VMEM)
    names = list(_PROJ_NAMES)
    return pl.pallas_call(
        _proj_tok_kernel,
        grid=(1,),
        in_specs=[row(D_MODEL)] + [row(QK_CONV)] * 3 + [whole] * 8,
        out_specs=[row(_PROJ_WIDTH[k]) for k in names] + [row(QK_CONV)],
        out_shape=[jax.ShapeDtypeStruct((n, _PROJ_WIDTH[k]), _PROJ_DTYPE[k]) for k in names]
        + [jax.ShapeDtypeStruct((n, QK_CONV), F32)],
        compiler_params=pltpu.CompilerParams(dimension_semantics=("arbitrary",), vmem_limit_bytes=VMEM_LIMIT),
        name="proj_tok",
    )(x2d, *conv_rows, wts["g_mix"], wts["w_big"], wts["w_small"], wts["b_small"], wts["w_a2p"], wts["b_a"],
      wts["conv_w"], wts["conv_b"])


def _mix_stages(src, tril_ref, tri_ref, wred_ref, caug_ref, m_ref, sbd_ref, hm_ref, hg_ref):
    ts = src["mq"].shape[0]
    n_chunks = ts // CHUNK
    gates = src["gates"][...]
    bcum = _cumsum_dot(tril_ref[...], gates)
    gq = src["gq"][...]
    gk = src["gk"][...]
    gv_ref = src["gv"]
    bc = _cumsum_dot(tri_ref[...], src["la"][...])
    gates_t = gates.T
    bcum_t = bcum.T
    bc_t = bc.T
    yield

    nb = ts // SUB
    half = SUB // 2
    assert half == SUBLANES
    q3 = gq.reshape(nb, SUB, G_KW)
    k3 = gk.reshape(nb, SUB, G_KW)
    bc3 = bc.reshape(nb, SUB, G_KW)
    tl = lax.broadcasted_iota(jnp.int32, (nb, SUB, G_KW), 1)
    q3u = gq.reshape(nb, 2, half, G_KW)[:, 1]
    bc3u = bc.reshape(nb, 2, half, G_KW)[:, 1]
    tlu = lax.broadcasted_iota(jnp.int32, (nb, half, G_KW), 1) + half

    def exact_pass(j, acc):
        acc_all, acc_upper = acc
        if j < half:
            arg = jnp.where(tl >= j, bc3 - bc3[:, j:j + 1, :], NEG)
            e = (q3 * k3[:, j:j + 1, :] * jnp.exp(arg)).reshape(ts, G_KW)
            return acc_all + _dot(e.astype(BF16), wred_ref[j]), acc_upper
        arg = jnp.where(tlu >= j, bc3u - bc3[:, j:j + 1, :], NEG)
        e = (q3u * k3[:, j:j + 1, :] * jnp.exp(arg)).reshape(ts // 2, G_KW)
        return acc_all, acc_upper + _dot(e.astype(BF16), wred_ref[j])

    krow = lax.broadcasted_iota(jnp.int32, (CHUNK, G_KW), 0)
    same_head_kk = (_div_pow2(lax.broadcasted_iota(jnp.int32, (G_KW, G_KW), 0), G_DK)
                    == _div_pow2(lax.broadcasted_iota(jnp.int32, (G_KW, G_KW), 1), G_DK))
    same_head_kv = (_div_pow2(lax.broadcasted_iota(jnp.int32, (G_KW, G_VW), 0), G_DK)
                    == _div_pow2(lax.broadcasted_iota(jnp.int32, (G_KW, G_VW), 1), G_DV))

    def cross_block_scores(c):
        lo = c * CHUNK
        bc_c = bc[lo:lo + CHUNK]
        k_c = gk[lo:lo + CHUNK]
        out = [jnp.zeros((SUB, G_KW), F32)]
        for i in range(1, CHUNK // SUB):
            r0 = lo + i * SUB
            r_i = bc[r0 - 1:r0, :]
            qi = (gq[r0:r0 + SUB] * jnp.exp(bc[r0:r0 + SUB] - r_i)).astype(BF16)
            ki = (k_c * jnp.exp(jnp.where(krow < i * SUB, r_i - bc_c, NEG))).astype(BF16)
            kbd = jnp.where(same_head_kk, jnp.concatenate([ki] * G_HEADS, axis=0), jnp.zeros((), BF16))
            out.append(_dot_nt(qi, kbd))
        return out

    def state_update_term(c):
        lo, hi = c * CHUNK, (c + 1) * CHUNK
        b_end = bc[hi - 1:hi, :]
        k_out_t = (gk[lo:hi] * jnp.exp(b_end - bc[lo:hi])).T.astype(BF16)
        rows = []
        for g in range(G_HEADS):
            blk = _dot(k_out_t[g * G_DK:(g + 1) * G_DK], gv_ref[lo:hi, g * G_DV:(g + 1) * G_DV])
            zero = jnp.zeros((G_DK, G_DV), F32)
            rows.append(jnp.concatenate([zero] * g + [blk] + [zero] * (G_HEADS - 1 - g), axis=1))
        return jnp.concatenate(rows, axis=0)

    adiag = (jnp.zeros((ts, G_KW), F32), jnp.zeros((ts // 2, G_KW), F32))
    per_stage = SUB // 4
    assert n_chunks == 4
    offs = []
    s_terms = []

    causal = (lax.broadcasted_iota(jnp.int32, (ts, ts), 0) >= lax.broadcasted_iota(jnp.int32, (ts, ts), 1))
    ones_col = (lax.broadcasted_iota(jnp.int32, (ts, LANES), 1) == 0).astype(BF16)
    heads = range(M_HEADS)
    hsl = [slice(hd * M_DH, (hd + 1) * M_DH) for hd in heads]
    b_col = [bcum[:, M_HEADS + hd:M_HEADS + hd + 1] for hd in heads]
    b_row = [bcum_t[M_HEADS + hd:M_HEADS + hd + 1, :] for hd in heads]
    i_row = [gates_t[hd:hd + 1, :] for hd in heads]
    m_prev = [m_ref[hd:hd + 1, 0:1] for hd in heads]
    q = [src["mq"][:, hsl[hd]] for hd in heads]
    k_t = [src["mk_t"][hsl[hd], :] for hd in heads]
    vaug = [jnp.concatenate([src["mv"][:, hsl[hd]], ones_col], axis=1) for hd in heads]
    caug = [caug_ref[hd] for hd in heads]
    s_qk = [_dot(q[hd], k_t[hd].astype(BF16)) for hd in heads]
    qc = [_dot(q[hd], caug[hd].astype(BF16)) for hd in heads]
    for j in range(0, per_stage):
        adiag = exact_pass(j, adiag)
    offs += cross_block_scores(0)
    s_terms.append(state_update_term(0))
    yield
    b_last = [b_col[hd][ts - 1:ts, :] for hd in heads]
    dec = [b_last[hd] - b_row[hd] + i_row[hd] for hd in heads]
    m_new = [jnp.maximum(b_last[hd] + m_prev[hd], jnp.max(dec[hd], axis=1, keepdims=True)) for hd in heads]
    kw_t = [(k_t[hd] * jnp.exp(dec[hd] - m_new[hd])).astype(BF16) for hd in heads]
    upd = [_dot(kw_t[hd], vaug[hd]) for hd in heads]
    for j in range(per_stage, 2 * per_stage):
        adiag = exact_pass(j, adiag)
    offs += cross_block_scores(1)
    s_terms.append(state_update_term(1))
    yield
    dmat = [jnp.where(causal, b_col[hd] - b_row[hd] + i_row[hd], -jnp.inf) for hd in heads]
    inter = [b_col[hd] + m_prev[hd] for hd in heads]
    m_tok = [jnp.maximum(inter[hd], jnp.max(dmat[hd], axis=1, keepdims=True)) for hd in heads]
    for j in range(2 * per_stage, 3 * per_stage):
        adiag = exact_pass(j, adiag)
    offs += cross_block_scores(2)
    s_terms.append(state_update_term(2))
    yield
    p = [(s_qk[hd] * jnp.exp(dmat[hd] - m_tok[hd])).astype(BF16) for hd in heads]
    pv = [_dot(p[hd], vaug[hd]) for hd in heads]
    for j in range(3 * per_stage, SUB):
        adiag = exact_pass(j, adiag)
    offs += cross_block_scores(3)
    s_terms.append(state_update_term(3))
    yield
    for hd in heads:
        tot = jnp.exp(inter[hd] - m_tok[hd]) * qc[hd] + pv[hd]
        den = tot[:, M_DH:M_DH + 1]
        hm_ref[:, hsl[hd]] = tot[:, :M_DH] / jnp.maximum(jnp.abs(den), jnp.exp(-m_tok[hd]))
        caug_ref[hd] = jnp.exp(b_last[hd] + m_prev[hd] - m_new[hd]) * caug[hd] + upd[hd]
        m_ref[hd:hd + 1, :] = jnp.broadcast_to(m_new[hd], (1, LANES))
    yield

    sub_of = lambda idx: lax.shift_right_logical(idx & (CHUNK - 1), SUB.bit_length() - 1)
    rowb = sub_of(lax.broadcasted_iota(jnp.int32, (ts, G_KW), 0))
    colb = sub_of(lax.broadcasted_iota(jnp.int32, (ts, G_KW), 1))
    acc_all, acc_upper = adiag
    upper = acc_upper.reshape(nb, half, G_KW)
    acc_all = acc_all + jnp.stack([jnp.zeros_like(upper), upper], axis=1).reshape(ts, G_KW)
    adiag = jnp.where(rowb == colb, acc_all, 0.0)
    intra = (jnp.concatenate(offs, axis=0) + adiag).astype(BF16)
    q_in = (gq * jnp.exp(bc)).astype(BF16)
    o_intra = []
    for c in range(n_chunks):
        lo, hi = c * CHUNK, (c + 1) * CHUNK
        vbd = jnp.where(same_head_kv, jnp.concatenate([gv_ref[lo:hi, :]] * G_HEADS, axis=0),
                        jnp.zeros((), BF16))
        o_intra.append(_dot(intra[lo:hi], vbd))
    sbd = sbd_ref[...]
    for c in range(n_chunks):
        lo, hi = c * CHUNK, (c + 1) * CHUNK
        hg_ref[lo:hi, :] = o_intra[c] + _dot(q_in[lo:hi], sbd.astype(BF16))
        dcol = jnp.exp(bc_t[:, hi - 1:hi])
        sbd = dcol * sbd + s_terms[c]
    sbd_ref[...] = sbd
    yield


def _head_norm(hv, n_heads, width):
    parts = []
    for hd in range(n_heads):
        seg = hv[:, hd * width:(hd + 1) * width]
        parts.append(seg * lax.rsqrt(jnp.mean(seg * seg, axis=-1, keepdims=True) + EPS))
    return jnp.concatenate(parts, axis=1)


def _gate_heads(hm, hg, mo, gr, gmh, ggh):
    hm = _head_norm(hm, M_HEADS, M_DH) * gmh * _sigmoid(mo)
    hg = _head_norm(hg, G_HEADS, G_DV) * ggh * (gr * _sigmoid(gr))
    return jnp.concatenate([hm, hg], axis=1).astype(BF16)


_POST_WEIGHTS = ("w_out", "g_mlp", "w1", "w2", "g_ple", "w_ple", "w_pg", "g_final")


def _post_stages(x, p, mixed, w, y_ref):
    x1 = x + _dot(mixed, w["w_out"][...])
    n1 = _rms(x1, w["g_mlp"][...]).astype(BF16)
    yield
    n_ff = 4
    ff = D_FF // n_ff
    acts = []
    for j in range(n_ff):
        u = _dot(n1, w["w1"][:, j * ff:(j + 1) * ff])
        acts.append(jnp.square(jnp.maximum(u, 0.0)).astype(BF16))
        yield
    act = jnp.concatenate(acts, axis=1)
    half = D_MODEL // 2
    mlp = []
    for j in range(2):
        mlp.append(_dot(act, w["w2"][:, j * half:(j + 1) * half]))
        yield
    x2 = x1 + jnp.concatenate(mlp, axis=1)
    ple = _dot(p.astype(BF16), w["w_ple"][...])
    n2 = _rms(x2, w["g_ple"][...]).astype(BF16)
    yield
    gate = _sigmoid(_dot(n2, w["w_pg"][...]))
    x3 = x2 + ple * gate
    y_ref[...] = _rms(x3, w["g_final"][...])
    yield


def _post_tok_kernel(x_ref, p_ref, hm_ref, hg_ref, mo_ref, gr_ref, gmh_ref, ggh_ref, wout_ref, gmlp_ref,
                     w1_ref, w2_ref, gple_ref, wple_ref, wpg_ref, gfin_ref, y_ref):
    mixed = _gate_heads(hm_ref[...], hg_ref[...], mo_ref[...], gr_ref[...], gmh_ref[...], ggh_ref[...])
    w = dict(zip(_POST_WEIGHTS, (wout_ref, gmlp_ref, w1_ref, w2_ref, gple_ref, wple_ref, wpg_ref, gfin_ref)))
    _run(_post_stages(x_ref[...], p_ref[...], mixed, w, y_ref))


def _post_tok_call(x2d, p2d, hm, hg, mo, gr, wts):
    n = x2d.shape[0]
    row = lambda width: pl.BlockSpec((n, width), lambda i: (0, 0))
    whole = pl.BlockSpec(memory_space=pltpu.VMEM)
    return pl.pallas_call(
        _post_tok_kernel,
        grid=(1,),
        in_specs=[row(D_MODEL), row(D_PLE), row(M_WIDTH), row(G_VW), row(M_WIDTH), row(G_VW)] + [whole] * 10,
        out_specs=row(D_MODEL),
        out_shape=jax.ShapeDtypeStruct((n, D_MODEL), F32),
        compiler_params=pltpu.CompilerParams(dimension_semantics=("arbitrary",), vmem_limit_bytes=VMEM_LIMIT),
        name="post_tok",
    )(x2d, p2d, hm, hg, mo, gr, wts["g_mhead"], wts["g_ghead"], *[wts[k] for k in _POST_WEIGHTS])


def _seq_kernel(xa_ref, xb_ref, pb_ref, gmix_ref, wbig_ref, wsmall_ref, bsmall_ref, wa2_ref, ba_ref,
                convw_ref, convb_ref, tril_ref, tri_ref, wred_ref, gmh_ref, ggh_ref,
                wout_ref, gmlp_ref, w1_ref, w2_ref, gple_ref, wple_ref, wpg_ref, gfin_ref,
                y_ref, tail_ref, cout_ref, mout_ref, sout_ref,
                cbuf_ref, caug_ref, m_ref, sbd_ref, mixed_ref,
                mq_s, mk_s, mv_s, mo_s, gq_s, gk_s, gv_s, gr_s, gates_s, la_s, hm_s, hg_s,
                *, steps_per_seq, n_blocks):
    tb = xa_ref.shape[0]
    s = pl.program_id(0)
    r = lax.rem(jnp.minimum(s, n_blocks - 1), steps_per_seq)

    @pl.when(s == 0)
    def _():
        mixed_ref[...] = jnp.zeros(mixed_ref.shape, BF16)

    @pl.when(r == 0)
    def _():
        cbuf_ref[0:SUBLANES, :] = jnp.zeros((SUBLANES, QK_CONV), F32)
        caug_ref[...] = jnp.zeros(caug_ref.shape, F32)
        m_ref[...] = jnp.zeros(m_ref.shape, F32)
        sbd_ref[...] = jnp.zeros(sbd_ref.shape, F32)

    slot = lax.rem(s, 2)
    src = dict(zip(_PROJ_NAMES, (mq_s, mk_s, mv_s, mo_s, gq_s, gk_s, gv_s, gr_s, gates_s, la_s)))
    src["mk_t"] = src.pop("mk")
    w_in = dict(zip(_PROJ_WEIGHTS, (wbig_ref, wsmall_ref, bsmall_ref, wa2_ref, ba_ref)))
    w_post = dict(zip(_POST_WEIGHTS, (wout_ref, gmlp_ref, w1_ref, w2_ref, gple_ref, wple_ref, wpg_ref, gfin_ref)))

    def front():
        h = _rms(xa_ref[...], gmix_ref[...]).astype(BF16)
        yield from _proj_stages(h, w_in, src, _seq_conv(cbuf_ref, tail_ref, convw_ref, convb_ref, tb))
        yield from _mix_stages(src, tril_ref, tri_ref, wred_ref, caug_ref, m_ref, sbd_ref, hm_s, hg_s)
        mixed_ref[slot] = _gate_heads(hm_s[...], hg_s[...], mo_s[...], gr_s[...], gmh_ref[...], ggh_ref[...])
        yield

    def back():
        yield from _post_stages(xb_ref[...], pb_ref[...], mixed_ref[1 - slot], w_post, y_ref)

    _interleave("bfff" "bfff" "bfff" "bff" "bf" "bf" "bf" "bf" "bff", front=front(), back=back())

    @pl.when(jnp.logical_and(s < n_blocks, r == steps_per_seq - 1))
    def _():
        cout_ref[0] = caug_ref[...]
        mout_ref[0] = m_ref[...]
        for g in range(G_HEADS):
            sout_ref[0, g] = sbd_ref[g * G_DK:(g + 1) * G_DK, g * G_DV:(g + 1) * G_DV]


def _seq_call(x2d, p2d, wts, consts, *, n_seq, seq_len):
    tb = SEQ_BLOCK
    steps_per_seq = seq_len // tb
    n_blocks = n_seq * steps_per_seq
    tril, tri, wred = consts
    front_blk = lambda s: jnp.minimum(s, n_blocks - 1)
    back_blk = lambda s: jnp.maximum(s - 1, 0)
    whole = pl.BlockSpec(memory_space=pltpu.VMEM)
    per_seq = lambda shape: pl.BlockSpec((1,) + shape, lambda s: (front_blk(s) // steps_per_seq,) + (0,) * len(shape))
    scratch = [
        pltpu.VMEM((tb + SUBLANES, QK_CONV), F32),
        pltpu.VMEM((M_HEADS, M_DH, 2 * M_DH), F32),
        pltpu.VMEM((SUBLANES, LANES), F32),
        pltpu.VMEM((G_KW, G_VW), F32),
        pltpu.VMEM((2, tb, D_MODEL), BF16),
    ] + [pltpu.VMEM((_PROJ_WIDTH[k], tb) if k == "mk" else (tb, _PROJ_WIDTH[k]), _PROJ_DTYPE[k])
         for k in _PROJ_NAMES] + [
        pltpu.VMEM((tb, M_WIDTH), F32), pltpu.VMEM((tb, G_VW), F32)]
    return pl.pallas_call(
        functools.partial(_seq_kernel, steps_per_seq=steps_per_seq, n_blocks=n_blocks),
        grid=(n_blocks + 1,),
        in_specs=[pl.BlockSpec((tb, D_MODEL), lambda s: (front_blk(s), 0)),
                  pl.BlockSpec((tb, D_MODEL), lambda s: (back_blk(s), 0)),
                  pl.BlockSpec((tb, D_PLE), lambda s: (back_blk(s), 0))] + [whole] * 21,
        out_specs=[pl.BlockSpec((tb, D_MODEL), lambda s: (back_blk(s), 0)),
                   per_seq((SUBLANES, QK_CONV)), per_seq((M_HEADS, M_DH, 2 * M_DH)),
                   per_seq((SUBLANES, LANES)), per_seq((G_HEADS, G_DK, G_DV))],
        out_shape=[jax.ShapeDtypeStruct((n_blocks * tb, D_MODEL), F32),
                   jax.ShapeDtypeStruct((n_seq, SUBLANES, QK_CONV), F32),
                   jax.ShapeDtypeStruct((n_seq, M_HEADS, M_DH, 2 * M_DH), F32),
                   jax.ShapeDtypeStruct((n_seq, SUBLANES, LANES), F32),
                   jax.ShapeDtypeStruct((n_seq, G_HEADS, G_DK, G_DV), F32)],
        scratch_shapes=scratch,
        compiler_params=pltpu.CompilerParams(dimension_semantics=("arbitrary",), vmem_limit_bytes=VMEM_LIMIT),
        name="seq_fused",
    )(x2d, x2d, p2d, wts["g_mix"], wts["w_big"], wts["w_small"], wts["b_small"], wts["w_a2p"], wts["b_a"],
      wts["conv_w"], wts["conv_b"], tril, tri, wred, wts["g_mhead"], wts["g_ghead"],
      *[wts[k] for k in _POST_WEIGHTS])


def _mix_tok_kernel(mq_ref, mk_ref, mv_ref, gates_ref, gq_ref, gk_ref, gv_ref, la_ref,
                    c_ref, n_ref, m_ref, s_ref,
                    hm_ref, hg_ref, cn_ref, nn_ref, mn_ref, sn_ref):
    bb = mq_ref.shape[0]
    wide = bb * LANES
    diag = (lax.broadcasted_iota(jnp.int32, (bb, wide), 0)
            == _div_pow2(lax.broadcasted_iota(jnp.int32, (bb, wide), 1), LANES))
    block_ones = diag.astype(BF16)

    def block_diag(rows):
        return jnp.where(diag, jnp.concatenate([rows] * bb, axis=1), jnp.zeros((), rows.dtype))

    gates = gates_ref[...]
    m_all = m_ref[...]
    for hd in range(M_HEADS):
        cs_, ce_ = hd * M_DH, (hd + 1) * M_DH
        q_bf = mq_ref[:, cs_:ce_]
        q = q_bf.astype(F32)
        k = mk_ref[:, cs_:ce_]
        v_bf = mv_ref[:, cs_:ce_]
        v = v_bf.astype(F32)
        n_prev = n_ref[:, cs_:ce_]
        ig = gates[:, hd:hd + 1]
        lf = gates[:, M_HEADS + hd:M_HEADS + hd + 1]
        m_prev = m_all[:, hd:hd + 1]
        m_new = jnp.maximum(lf + m_prev, ig)
        scale = jnp.exp(lf + m_prev - m_new)
        wk = jnp.exp(ig - m_new)
        s_qk = jnp.sum(q * k, axis=1, keepdims=True) * wk
        den = scale * jnp.sum(q * n_prev, axis=1, keepdims=True) + s_qk
        inv = 1.0 / jnp.maximum(jnp.abs(den), jnp.exp(-m_new))
        nn_ref[:, cs_:ce_] = scale * n_prev + wk * k
        mn_ref[:, hd:hd + 1] = m_new
        q_w = _dot(q.T.astype(BF16), block_ones)
        outer = _dot((k * wk).T.astype(BF16), block_diag(v_bf))
        num_w = s_qk * v
        for b in range(bb):
            blk = slice(b * M_DH, (b + 1) * M_DH)
            c_prev = c_ref[b, hd]
            qc = jnp.sum(q_w[:, blk] * c_prev, axis=0, keepdims=True)
            hm_ref[b:b + 1, cs_:ce_] = (scale[b:b + 1] * qc + num_w[b:b + 1]) * inv[b:b + 1]
            cn_ref[b, hd] = scale[b:b + 1] * c_prev + outer[:, blk]
    la = la_ref[...]
    dec = jnp.exp(la)
    gq = gq_ref[...]
    gk = gk_ref[...]
    for g in range(G_HEADS):
        ks_, ke_ = g * G_DK, (g + 1) * G_DK
        vs_, ve_ = g * G_DV, (g + 1) * G_DV
        q = gq[:, ks_:ke_]
        k = gk[:, ks_:ke_]
        v_bf = gv_ref[:, vs_:ve_]
        av = jnp.sum(q * k, axis=1, keepdims=True) * v_bf.astype(F32)
        qd_w = _dot((q * dec[:, ks_:ke_]).T.astype(BF16), block_ones)
        outer = _dot(k.T.astype(BF16), block_diag(v_bf))
        dec_w = _spread_dot(dec[:, ks_:ke_].T, block_ones)
        for b in range(bb):
            blk = slice(b * G_DV, (b + 1) * G_DV)
            s_prev = s_ref[b, g]
            hg_ref[b:b + 1, vs_:ve_] = jnp.sum(qd_w[:, blk] * s_prev, axis=0, keepdims=True) + av[b:b + 1]
            sn_ref[b, g] = dec_w[:, blk] * s_prev + outer[:, blk]


def _mix_tok_call(p, c0, n0, m0, s0):
    bb = TOK_BATCH
    n = c0.shape[0]
    row = lambda w: pl.BlockSpec((bb, w), lambda i: (i, 0))
    st4 = lambda a, b_: pl.BlockSpec((bb, M_HEADS, a, b_), lambda i: (i, 0, 0, 0))
    return pl.pallas_call(
        _mix_tok_kernel,
        grid=(n // bb,),
        in_specs=[row(M_WIDTH), row(M_WIDTH), row(M_WIDTH), row(LANES), row(G_KW), row(G_KW), row(G_VW),
                  row(G_KW), st4(M_DH, M_DH), row(M_WIDTH), row(M_HEADS), st4(G_DK, G_DV)],
        out_specs=[row(M_WIDTH), row(G_VW), st4(M_DH, M_DH), row(M_WIDTH), row(M_HEADS), st4(G_DK, G_DV)],
        out_shape=[jax.ShapeDtypeStruct((n, M_WIDTH), F32), jax.ShapeDtypeStruct((n, G_VW), F32),
                   jax.ShapeDtypeStruct(c0.shape, F32), jax.ShapeDtypeStruct((n, M_WIDTH), F32),
                   jax.ShapeDtypeStruct((n, M_HEADS), F32), jax.ShapeDtypeStruct(s0.shape, F32)],
        compiler_params=pltpu.CompilerParams(dimension_semantics=("arbitrary",), vmem_limit_bytes=VMEM_LIMIT),
        name="mix_tok",
    )(p["mq"], p["mk"], p["mv"], p["gates"], p["gq"], p["gk"], p["gv"], p["la"], c0, n0, m0, s0)


_IN_OFFS = tuple(sum(IN_SIZES[:i]) for i in range(len(IN_SIZES) + 1))
_GATES_LO, _GATES_HI, _GA_LO = _IN_OFFS[3], _IN_OFFS[5], _IN_OFFS[9]
_N_SMALL = 2 * M_HEADS + G_RANK
_W_BIG_COLS = _GATES_LO + (_GA_LO - _GATES_HI)
_CAST_STEPS = 8


def _cast_kernel(wout_ref, w1_ref, w2_ref, wple_ref, wpg_ref, o_out, o_w1, o_w2, o_ple, o_pg):
    o_out[...] = wout_ref[...].astype(BF16)
    o_w1[...] = w1_ref[...].astype(BF16)
    o_w2[...] = w2_ref[...].astype(BF16)
    o_ple[...] = wple_ref[...].astype(BF16)
    o_pg[...] = wpg_ref[...].astype(BF16)


def _cast_call(*srcs):
    rows = lambda a: pl.BlockSpec((a.shape[0] // _CAST_STEPS, a.shape[1]), lambda i: (i, 0))
    return pl.pallas_call(
        _cast_kernel,
        grid=(_CAST_STEPS,),
        in_specs=[rows(a) for a in srcs],
        out_specs=[rows(a) for a in srcs],
        out_shape=[jax.ShapeDtypeStruct(a.shape, BF16) for a in srcs],
        compiler_params=pltpu.CompilerParams(dimension_semantics=("arbitrary",), vmem_limit_bytes=VMEM_LIMIT),
        name="cast_weights",
    )(*srcs)


def _cast_in_kernel(wt_ref, big_ref, small_ref):
    piece = M_WIDTH
    for n in range(_W_BIG_COLS // piece):
        dst = n * piece
        src = dst if dst < _GATES_LO else dst + (_GATES_HI - _GATES_LO)
        big_ref[:, dst:dst + piece] = wt_ref[src:src + piece, :].T.astype(BF16)
    small = jnp.concatenate([wt_ref[_GATES_LO:_GATES_HI, :], wt_ref[_GA_LO:, :],
                             jnp.zeros((LANES - _N_SMALL, D_MODEL), F32)], axis=0)
    small_ref[...] = small.T.astype(BF16)


def _cast_in_call(w_in_t):
    whole = pl.BlockSpec(memory_space=pltpu.VMEM)
    assert _GATES_LO % M_WIDTH == 0
    return pl.pallas_call(
        _cast_in_kernel,
        in_specs=[whole], out_specs=[whole, whole],
        out_shape=[jax.ShapeDtypeStruct((D_MODEL, _W_BIG_COLS), BF16),
                   jax.ShapeDtypeStruct((D_MODEL, LANES), BF16)],
        compiler_params=pltpu.CompilerParams(vmem_limit_bytes=VMEM_LIMIT),
        name="cast_w_in",
    )(w_in_t)


def _prep_weights(w_in, conv_w, conv_b, b_gate, w_a2, b_a, g_mhead, g_ghead, w_out, g_mix, g_mlp, w1, w2,
                  g_ple, w_ple, w_pg, g_final):
    w_big, w_small = _cast_in_call(jnp.swapaxes(w_in, 0, 1))
    w_out_b, w1_b, w2_b, w_ple_b, w_pg_b = _cast_call(w_out, w1, w2, w_ple, w_pg)
    b_small = jnp.concatenate([b_gate, jnp.zeros((LANES - 2 * M_HEADS,), F32)])[None]
    w_a2p = jnp.concatenate([jnp.zeros((2 * M_HEADS, G_KW), F32), w_a2,
                             jnp.zeros((LANES - _N_SMALL, G_KW), F32)], axis=0).astype(BF16)
    return dict(
        w_big=w_big, w_small=w_small, b_small=b_small, w_a2p=w_a2p, b_a=b_a[None],
        conv_w=conv_w, conv_b=conv_b[None], g_mix=g_mix[None], g_mhead=g_mhead[None], g_ghead=g_ghead[None],
        w_out=w_out_b, g_mlp=g_mlp[None], w1=w1_b, w2=w2_b,
        g_ple=g_ple[None], w_ple=w_ple_b, w_pg=w_pg_b, g_final=g_final[None])


def _mix_constants(ts):
    t = jnp.arange(ts)
    tril = (t[None, :] <= t[:, None]).astype(BF16)
    tri = ((t[:, None] // CHUNK == t[None, :] // CHUNK) & (t[None, :] <= t[:, None])).astype(BF16)
    rr = jnp.arange(G_KW)
    wred = ((rr[None, :, None] // G_DK == rr[None, None, :] // G_DK)
            & (rr[None, None, :] % SUB == jnp.arange(SUB)[:, None, None])).astype(BF16)
    return tril, tri, wred


def kernel(x_prompt, x_sample, p_prompt, p_sample, state_mlstm_C, state_mlstm_n, state_mlstm_m, state_conv,
           state_gla_S, w_in, conv_w, conv_b, b_gate, w_a2, b_a, g_mhead, g_ghead, w_out, g_mix, g_mlp, w1,
           w2, g_ple, w_ple, w_pg, g_final):
    assert w_in.shape[0] == 1, "single-layer trunk"
    n_seq, seq_len, _ = x_prompt.shape
    n_tok = x_sample.shape[0]
    assert x_sample.shape[1] == 1 and seq_len % SEQ_BLOCK == 0 and n_tok % TOK_BATCH == 0
    wts = _prep_weights(w_in[0], conv_w[0], conv_b[0], b_gate[0], w_a2[0], b_a[0], g_mhead[0], g_ghead[0],
                        w_out[0], g_mix[0], g_mlp[0], w1[0], w2[0], g_ple[0], w_ple[0], w_pg[0], g_final)

    y_p, tail_p, caug_p, m_p, s_p = _seq_call(
        x_prompt.reshape(n_seq * seq_len, D_MODEL), p_prompt[0].reshape(n_seq * seq_len, D_PLE), wts,
        _mix_constants(SEQ_BLOCK), n_seq=n_seq, seq_len=seq_len)

    xs = x_sample.reshape(n_tok, D_MODEL)
    buf = state_conv[0]
    outs = _proj_tok_call(xs, (buf[:, 0], buf[:, 1], buf[:, 2]), wts)
    ps = dict(zip(_PROJ_NAMES, outs[:-1]))
    raw_s = outs[-1]
    hm_s, hg_s, c_s, n_s, m_s, s_s = _mix_tok_call(
        ps, state_mlstm_C[0], state_mlstm_n[0].reshape(n_tok, M_WIDTH), state_mlstm_m[0], state_gla_S[0])
    y_s = _post_tok_call(xs, p_sample[0].reshape(n_tok, D_PLE), hm_s, hg_s, ps["mo"], ps["gr"], wts)

    return (y_p.reshape(n_seq, seq_len, D_MODEL),
            y_s.reshape(n_tok, 1, D_MODEL),
            caug_p[None, :, :, :, :M_DH],
            caug_p[None, :, :, :, M_DH],
            m_p[None, :, :M_HEADS, 0],
            tail_p[None, :, SUBLANES - (CONV_W - 1):, :],
            s_p[None],
            c_s[None],
            n_s.reshape(1, n_tok, M_HEADS, M_DH),
            m_s[None],
            jnp.stack([buf[:, 1], buf[:, 2], raw_s], axis=1)[None],
            s_s[None])
```

```python
import functools

import jax
import jax.numpy as jnp
import numpy as np
from jax import lax
from jax.experimental import pallas as pl
from jax.experimental.pallas import tpu as pltpu

D_MODEL = 1024
M_HEADS = 4
M_DH = 128
M_WIDTH = M_HEADS * M_DH
G_HEADS = 4
G_DK = 64
G_DV = 128
G_KW = G_HEADS * G_DK
G_VW = G_HEADS * G_DV
G_RANK = 16
G_TAU = 16.0
CONV_W = 4
QK_CONV = 2 * M_WIDTH
D_FF = 4 * D_MODEL
D_PLE = 256
CHUNK = 64
SUB = 16
EPS = 1e-6
IN_SIZES = (QK_CONV, M_WIDTH, M_WIDTH, M_HEADS, M_HEADS, G_KW, G_KW, G_VW, G_VW, G_RANK)

LANES = 128
SUBLANES = 8
VMEM_LIMIT = 60 * 1024 * 1024
SEQ_BLOCK = 256
TOK_BATCH = 16

F32 = jnp.float32
BF16 = jnp.bfloat16
NEG = -1e30


def _rms(x, g):
    return x * lax.rsqrt(jnp.mean(x * x, axis=-1, keepdims=True) + EPS) * g


def _log_sigmoid(x):
    return jnp.minimum(x, 0.0) - jnp.log(1.0 + jnp.exp(-jnp.abs(x)))


def _sigmoid(x):
    return 0.5 * jnp.tanh(0.5 * x) + 0.5


def _div_pow2(idx, d):
    assert d & (d - 1) == 0
    return lax.shift_right_logical(idx, d.bit_length() - 1)


def _dot(a, b):
    return jnp.dot(a, b, preferred_element_type=F32)


def _dot_nt(a, b):
    return lax.dot_general(a, b, (((1,), (1,)), ((), ())), preferred_element_type=F32)


def _split3(x):
    hi = x.astype(BF16)
    r1 = x - hi.astype(F32)
    mid = r1.astype(BF16)
    lo = (r1 - mid.astype(F32)).astype(BF16)
    return hi, mid, lo


def _cumsum_dot(tri, x):
    hi, mid, lo = _split3(x)
    return _dot(tri, hi) + _dot(tri, mid) + _dot(tri, lo)


def _spread_dot(x, onehot):
    hi, mid, lo = _split3(x)
    return _dot(hi, onehot) + _dot(mid, onehot) + _dot(lo, onehot)


def _interleave(pattern, **streams):
    by_letter = {name[0]: gen for name, gen in streams.items()}
    for letter in pattern:
        next(by_letter[letter], None)
    for gen in by_letter.values():
        _run(gen)


def _run(gen):
    for _ in gen:
        pass


def _proj_stages(h, w, sinks, conv_piece):
    width = M_WIDTH

    def put_q(raw):
        sinks["mq"][...] = conv_piece(raw, 0).astype(sinks["mq"].dtype)

    def put_k(raw):
        sinks["mk"][...] = conv_piece(raw, M_WIDTH) * (M_DH ** -0.5)

    def put_mv(raw):
        sinks["mv"][...] = raw.astype(sinks["mv"].dtype)

    def put_mo(raw):
        sinks["mo"][...] = raw

    def put_gqk(raw):
        sinks["gq"][...] = raw[:, :G_KW] * (G_DK ** -0.5)
        sinks["gk"][...] = raw[:, G_KW:]

    def put_gv(raw):
        sinks["gv"][...] = raw.astype(sinks["gv"].dtype)

    def put_gr(raw):
        sinks["gr"][...] = raw

    def put_small(small):
        g = small + w["b_small"][...]
        lane = lax.broadcasted_iota(jnp.int32, g.shape, 1)
        sinks["gates"][...] = jnp.where(lane < M_HEADS, g, _log_sigmoid(g))
        z = _dot(small.astype(BF16), w["w_a2p"][...]) + w["b_a"][...]
        sinks["la"][...] = _log_sigmoid(z) * (1.0 / G_TAU)

    epilogues = [put_small, put_q, put_k, put_mv, put_mo, put_gqk, put_gv, put_gr]
    assert w["w_big"].shape[1] == width * (len(epilogues) - 1)
    pending = (_dot(h, w["w_small"][...]), epilogues[0])
    yield
    for n, epi in enumerate(epilogues[1:]):
        cur = _dot(h, w["w_big"][:, n * width:(n + 1) * width])
        pending[1](pending[0])
        pending = (cur, epi)
        yield
    pending[1](pending[0])
    yield


def _seq_conv(cbuf_ref, tail_ref, convw_ref, convb_ref, tb):
    cw = convw_ref[...]

    def conv_piece(raw, lo):
        cols = slice(lo, lo + raw.shape[1])
        cbuf_ref[SUBLANES:SUBLANES + tb, cols] = raw
        y = convb_ref[:, cols] + raw * cw[3:4, cols]
        for j in range(CONV_W - 1):
            off = SUBLANES - (CONV_W - 1) + j
            y = y + cbuf_ref[off:off + tb, cols] * cw[j:j + 1, cols]
        last = cbuf_ref[tb:tb + SUBLANES, cols]
        tail_ref[0, :, cols] = last
        cbuf_ref[0:SUBLANES, cols] = last
        return y * _sigmoid(y)

    return conv_piece


def _tok_conv(b0_ref, b1_ref, b2_ref, raw_ref, convw_ref, convb_ref):
    cw = convw_ref[...]

    def conv_piece(raw, lo):
        cols = slice(lo, lo + raw.shape[1])
        raw_ref[:, cols] = raw
        y = (convb_ref[:, cols] + b0_ref[:, cols] * cw[0:1, cols] + b1_ref[:, cols] * cw[1:2, cols]
             + b2_ref[:, cols] * cw[2:3, cols] + raw * cw[3:4, cols])
        return y * _sigmoid(y)

    return conv_piece


_PROJ_NAMES = ("mq", "mk", "mv", "mo", "gq", "gk", "gv", "gr", "gates", "la")
_PROJ_WIDTH = dict(mq=M_WIDTH, mk=M_WIDTH, mv=M_WIDTH, mo=M_WIDTH, gq=G_KW, gk=G_KW, gv=G_VW, gr=G_VW,
                   gates=LANES, la=G_KW)
_PROJ_DTYPE = dict(mq=BF16, mk=F32, mv=BF16, mo=F32, gq=F32, gk=F32, gv=BF16, gr=F32, gates=F32, la=F32)
_PROJ_WEIGHTS = ("w_big", "w_small", "b_small", "w_a2p", "b_a")


def _proj_tok_kernel(x_ref, b0_ref, b1_ref, b2_ref, gmix_ref, wbig_ref, wsmall_ref, bsmall_ref, wa2_ref,
                     ba_ref, convw_ref, convb_ref, *out_refs):
    sinks = dict(zip(_PROJ_NAMES, out_refs[:-1]))
    raw_ref = out_refs[-1]
    w = dict(zip(_PROJ_WEIGHTS, (wbig_ref, wsmall_ref, bsmall_ref, wa2_ref, ba_ref)))
    h = _rms(x_ref[...], gmix_ref[...]).astype(BF16)
    _run(_proj_stages(h, w, sinks, _tok_conv(b0_ref, b1_ref, b2_ref, raw_ref, convw_ref, convb_ref)))


def _proj_tok_call(x2d, conv_rows, wts):
    n = x2d.shape[0]
    row = lambda width: pl.BlockSpec((n, width), lambda i: (0, 0))
    whole = pl.BlockSpec(memory_space=pltpu.VMEM)
    names = list(_PROJ_NAMES)
    return pl.pallas_call(
        _proj_tok_kernel,
        grid=(1,),
        in_specs=[row(D_MODEL)] + [row(QK_CONV)] * 3 + [whole] * 8,
        out_specs=[row(_PROJ_WIDTH[k]) for k in names] + [row(QK_CONV)],
        out_shape=[jax.ShapeDtypeStruct((n, _PROJ_WIDTH[k]), _PROJ_DTYPE[k]) for k in names]
        + [jax.ShapeDtypeStruct((n, QK_CONV), F32)],
        compiler_params=pltpu.CompilerParams(dimension_semantics=("arbitrary",), vmem_limit_bytes=VMEM_LIMIT),
        name="proj_tok",
    )(x2d, *conv_rows, wts["g_mix"], wts["w_big"], wts["w_small"], wts["b_small"], wts["w_a2p"], wts["b_a"],
      wts["conv_w"], wts["conv_b"])


def _mix_gate_sums(src, tril_ref, tri_ref):
    gates = src["gates"][...]
    bcum = _cumsum_dot(tril_ref[...], gates)
    bc = _cumsum_dot(tri_ref[...], src["la"][...])
    return gates, bcum, bc, gates.T, bcum.T, bc.T


def _mix_stages(gate_sums, src, wred_ref, caug_ref, m_ref, sbd_ref, hm_ref, hg_ref):
    ts = src["mq"].shape[0]
    n_chunks = ts // CHUNK
    gates, bcum, bc, gates_t, bcum_t, bc_t = gate_sums
    gq = src["gq"][...]
    gk = src["gk"][...]
    gv_ref = src["gv"]

    nb = ts // SUB
    half = SUB // 2
    assert half == SUBLANES
    q3 = gq.reshape(nb, SUB, G_KW)
    k3 = gk.reshape(nb, SUB, G_KW)
    bc3 = bc.reshape(nb, SUB, G_KW)
    tl = lax.broadcasted_iota(jnp.int32, (nb, SUB, G_KW), 1)
    q3u = gq.reshape(nb, 2, half, G_KW)[:, 1]
    bc3u = bc.reshape(nb, 2, half, G_KW)[:, 1]
    tlu = lax.broadcasted_iota(jnp.int32, (nb, half, G_KW), 1) + half

    def exact_pass(j, acc):
        acc_all, acc_upper = acc
        if j < half:
            arg = jnp.where(tl >= j, bc3 - bc3[:, j:j + 1, :], NEG)
            e = (q3 * k3[:, j:j + 1, :] * jnp.exp(arg)).reshape(ts, G_KW)
            return acc_all + _dot(e.astype(BF16), wred_ref[j]), acc_upper
        arg = jnp.where(tlu >= j, bc3u - bc3[:, j:j + 1, :], NEG)
        e = (q3u * k3[:, j:j + 1, :] * jnp.exp(arg)).reshape(ts // 2, G_KW)
        return acc_all, acc_upper + _dot(e.astype(BF16), wred_ref[j])

    krow = lax.broadcasted_iota(jnp.int32, (CHUNK, G_KW), 0)
    same_head_kk = (_div_pow2(lax.broadcasted_iota(jnp.int32, (G_KW, G_KW), 0), G_DK)
                    == _div_pow2(lax.broadcasted_iota(jnp.int32, (G_KW, G_KW), 1), G_DK))
    same_head_kv = (_div_pow2(lax.broadcasted_iota(jnp.int32, (G_KW, G_VW), 0), G_DK)
                    == _div_pow2(lax.broadcasted_iota(jnp.int32, (G_KW, G_VW), 1), G_DV))

    def cross_block_scores(c):
        lo = c * CHUNK
        bc_c = bc[lo:lo + CHUNK]
        k_c = gk[lo:lo + CHUNK]
        out = [jnp.zeros((SUB, G_KW), F32)]
        for i in range(1, CHUNK // SUB):
            r0 = lo + i * SUB
            r_i = bc[r0 - 1:r0, :]
            qi = (gq[r0:r0 + SUB] * jnp.exp(bc[r0:r0 + SUB] - r_i)).astype(BF16)
            ki = (k_c * jnp.exp(jnp.where(krow < i * SUB, r_i - bc_c, NEG))).astype(BF16)
            kbd = jnp.where(same_head_kk, jnp.concatenate([ki] * G_HEADS, axis=0), jnp.zeros((), BF16))
            out.append(_dot_nt(qi, kbd))
        return out

    def state_update_term(c):
        lo, hi = c * CHUNK, (c + 1) * CHUNK
        b_end = bc[hi - 1:hi, :]
        k_out_t = (gk[lo:hi] * jnp.exp(b_end - bc[lo:hi])).T.astype(BF16)
        rows = []
        for g in range(G_HEADS):
            blk = _dot(k_out_t[g * G_DK:(g + 1) * G_DK], gv_ref[lo:hi, g * G_DV:(g + 1) * G_DV])
            zero = jnp.zeros((G_DK, G_DV), F32)
            rows.append(jnp.concatenate([zero] * g + [blk] + [zero] * (G_HEADS - 1 - g), axis=1))
        return jnp.concatenate(rows, axis=0)

    adiag = (jnp.zeros((ts, G_KW), F32), jnp.zeros((ts // 2, G_KW), F32))
    per_stage = SUB // 4
    assert n_chunks == 4
    offs = []
    s_terms = []

    causal = (lax.broadcasted_iota(jnp.int32, (ts, ts), 0) >= lax.broadcasted_iota(jnp.int32, (ts, ts), 1))
    ones_col = (lax.broadcasted_iota(jnp.int32, (ts, LANES), 1) == 0).astype(BF16)
    heads = range(M_HEADS)
    hsl = [slice(hd * M_DH, (hd + 1) * M_DH) for hd in heads]
    b_col = [bcum[:, M_HEADS + hd:M_HEADS + hd + 1] for hd in heads]
    b_row = [bcum_t[M_HEADS + hd:M_HEADS + hd + 1, :] for hd in heads]
    i_row = [gates_t[hd:hd + 1, :] for hd in heads]
    m_prev = [m_ref[hd:hd + 1, 0:1] for hd in heads]
    q = [src["mq"][:, hsl[hd]] for hd in heads]
    i_col = [gates[:, hd:hd + 1] for hd in heads]
    k = [src["mk"][:, hsl[hd]] for hd in heads]
    vaug = [jnp.concatenate([src["mv"][:, hsl[hd]], ones_col], axis=1) for hd in heads]
    caug = [caug_ref[hd] for hd in heads]
    s_qk = [_dot_nt(q[hd], k[hd].astype(BF16)) for hd in heads]
    qc = [_dot(q[hd], caug[hd].astype(BF16)) for hd in heads]
    for j in range(0, per_stage):
        adiag = exact_pass(j, adiag)
    offs += cross_block_scores(0)
    s_terms.append(state_update_term(0))
    yield
    b_last = [b_col[hd][ts - 1:ts, :] for hd in heads]
    dec = [b_last[hd] - b_col[hd] + i_col[hd] for hd in heads]
    m_new = [jnp.maximum(b_last[hd] + m_prev[hd], jnp.max(dec[hd], axis=0, keepdims=True)) for hd in heads]
    kw_t = [(k[hd] * jnp.exp(dec[hd] - m_new[hd])).T.astype(BF16) for hd in heads]
    upd = [_dot(kw_t[hd], vaug[hd]) for hd in heads]
    for j in range(per_stage, 2 * per_stage):
        adiag = exact_pass(j, adiag)
    offs += cross_block_scores(1)
    s_terms.append(state_update_term(1))
    yield
    dmat = [jnp.where(causal, b_col[hd] - b_row[hd] + i_row[hd], -jnp.inf) for hd in heads]
    inter = [b_col[hd] + m_prev[hd] for hd in heads]
    m_tok = [jnp.maximum(inter[hd], jnp.max(dmat[hd], axis=1, keepdims=True)) for hd in heads]
    for j in range(2 * per_stage, 3 * per_stage):
        adiag = exact_pass(j, adiag)
    offs += cross_block_scores(2)
    s_terms.append(state_update_term(2))
    yield
    p = [(s_qk[hd] * jnp.exp(dmat[hd] - m_tok[hd])).astype(BF16) for hd in heads]
    pv = [_dot(p[hd], vaug[hd]) for hd in heads]
    for j in range(3 * per_stage, SUB):
        adiag = exact_pass(j, adiag)
    offs += cross_block_scores(3)
    s_terms.append(state_update_term(3))
    yield
    for hd in heads:
        tot = jnp.exp(inter[hd] - m_tok[hd]) * qc[hd] + pv[hd]
        den = tot[:, M_DH:M_DH + 1]
        hm_ref[:, hsl[hd]] = tot[:, :M_DH] / jnp.maximum(jnp.abs(den), jnp.exp(-m_tok[hd]))
        caug_ref[hd] = jnp.exp(b_last[hd] + m_prev[hd] - m_new[hd]) * caug[hd] + upd[hd]
        m_ref[hd:hd + 1, :] = jnp.broadcast_to(m_new[hd], (1, LANES))
    yield

    sub_of = lambda idx: lax.shift_right_logical(idx & (CHUNK - 1), SUB.bit_length() - 1)
    rowb = sub_of(lax.broadcasted_iota(jnp.int32, (ts, G_KW), 0))
    colb = sub_of(lax.broadcasted_iota(jnp.int32, (ts, G_KW), 1))
    acc_all, acc_upper = adiag
    upper = acc_upper.reshape(nb, half, G_KW)
    acc_all = acc_all + jnp.stack([jnp.zeros_like(upper), upper], axis=1).reshape(ts, G_KW)
    adiag = jnp.where(rowb == colb, acc_all, 0.0)
    intra = (jnp.concatenate(offs, axis=0) + adiag).astype(BF16)
    q_in = (gq * jnp.exp(bc)).astype(BF16)
    o_intra = []
    for c in range(n_chunks):
        lo, hi = c * CHUNK, (c + 1) * CHUNK
        vbd = jnp.where(same_head_kv, jnp.concatenate([gv_ref[lo:hi, :]] * G_HEADS, axis=0),
                        jnp.zeros((), BF16))
        o_intra.append(_dot(intra[lo:hi], vbd))
    sbd = sbd_ref[...]
    for c in range(n_chunks):
        lo, hi = c * CHUNK, (c + 1) * CHUNK
        hg_ref[lo:hi, :] = o_intra[c] + _dot(q_in[lo:hi], sbd.astype(BF16))
        dcol = jnp.exp(bc_t[:, hi - 1:hi])
        sbd = dcol * sbd + s_terms[c]
    sbd_ref[...] = sbd
    yield


def _head_norm(hv, n_heads, width):
    parts = []
    for hd in range(n_heads):
        seg = hv[:, hd * width:(hd + 1) * width]
        parts.append(seg * lax.rsqrt(jnp.mean(seg * seg, axis=-1, keepdims=True) + EPS))
    return jnp.concatenate(parts, axis=1)


def _gate_heads(hm, hg, mo, gr, gmh, ggh):
    hm = _head_norm(hm, M_HEADS, M_DH) * gmh * _sigmoid(mo)
    hg = _head_norm(hg, G_HEADS, G_DV) * ggh * (gr * _sigmoid(gr))
    return jnp.concatenate([hm, hg], axis=1).astype(BF16)


_POST_WEIGHTS = ("w_out", "g_mlp", "w1", "w2", "g_ple", "w_ple", "w_pg", "g_final")


def _post_stages(x, p, mixed, w, y_ref):
    x1 = x + _dot(mixed, w["w_out"][...])
    n1 = _rms(x1, w["g_mlp"][...]).astype(BF16)
    yield
    n_ff = 2
    ff = D_FF // n_ff
    acts = []
    for j in range(n_ff):
        u = _dot(n1, w["w1"][:, j * ff:(j + 1) * ff])
        acts.append(jnp.square(jnp.maximum(u, 0.0)).astype(BF16))
        yield
    act = jnp.concatenate(acts, axis=1)
    half = D_MODEL // 2
    mlp = []
    for j in range(2):
        mlp.append(_dot(act, w["w2"][:, j * half:(j + 1) * half]))
        yield
    x2 = x1 + jnp.concatenate(mlp, axis=1)
    ple = _dot(p.astype(BF16), w["w_ple"][...])
    n2 = _rms(x2, w["g_ple"][...]).astype(BF16)
    yield
    gate = _sigmoid(_dot(n2, w["w_pg"][...]))
    x3 = x2 + ple * gate
    y_ref[...] = _rms(x3, w["g_final"][...])
    yield


def _post_tok_kernel(x_ref, p_ref, hm_ref, hg_ref, mo_ref, gr_ref, gmh_ref, ggh_ref, wout_ref, gmlp_ref,
                     w1_ref, w2_ref, gple_ref, wple_ref, wpg_ref, gfin_ref, y_ref):
    mixed = _gate_heads(hm_ref[...], hg_ref[...], mo_ref[...], gr_ref[...], gmh_ref[...], ggh_ref[...])
    w = dict(zip(_POST_WEIGHTS, (wout_ref, gmlp_ref, w1_ref, w2_ref, gple_ref, wple_ref, wpg_ref, gfin_ref)))
    _run(_post_stages(x_ref[...], p_ref[...], mixed, w, y_ref))


def _post_tok_call(x2d, p2d, hm, hg, mo, gr, wts):
    n = x2d.shape[0]
    row = lambda width: pl.BlockSpec((n, width), lambda i: (0, 0))
    whole = pl.BlockSpec(memory_space=pltpu.VMEM)
    return pl.pallas_call(
        _post_tok_kernel,
        grid=(1,),
        in_specs=[row(D_MODEL), row(D_PLE), row(M_WIDTH), row(G_VW), row(M_WIDTH), row(G_VW)] + [whole] * 10,
        out_specs=row(D_MODEL),
        out_shape=jax.ShapeDtypeStruct((n, D_MODEL), F32),
        compiler_params=pltpu.CompilerParams(dimension_semantics=("arbitrary",), vmem_limit_bytes=VMEM_LIMIT),
        name="post_tok",
    )(x2d, p2d, hm, hg, mo, gr, wts["g_mhead"], wts["g_ghead"], *[wts[k] for k in _POST_WEIGHTS])


def _seq_kernel(xa_ref, xb_ref, pb_ref, gmix_ref, wbig_ref, wsmall_ref, bsmall_ref, wa2_ref, ba_ref,
                convw_ref, convb_ref, tril_ref, tri_ref, wred_ref, gmh_ref, ggh_ref,
                wout_ref, gmlp_ref, w1_ref, w2_ref, gple_ref, wple_ref, wpg_ref, gfin_ref,
                y_ref, tail_ref, cout_ref, mout_ref, sout_ref,
                cbuf_ref, caug_ref, m_ref, sbd_ref, mixed_ref,
                mq_s, mk_s, mv_s, mo_s, gq_s, gk_s, gv_s, gr_s, gates_s, la_s, hm_s, hg_s,
                *, steps_per_seq, n_blocks):
    tb = xa_ref.shape[0]
    s = pl.program_id(0)
    r = lax.rem(jnp.minimum(s, n_blocks - 1), steps_per_seq)

    @pl.when(s == 0)
    def _():
        mixed_ref[...] = jnp.zeros(mixed_ref.shape, BF16)

    @pl.when(r == 0)
    def _():
        cbuf_ref[0:SUBLANES, :] = jnp.zeros((SUBLANES, QK_CONV), F32)
        caug_ref[...] = jnp.zeros(caug_ref.shape, F32)
        m_ref[...] = jnp.zeros(m_ref.shape, F32)
        sbd_ref[...] = jnp.zeros(sbd_ref.shape, F32)

    slot = lax.rem(s, 2)
    src = dict(zip(_PROJ_NAMES, (mq_s, mk_s, mv_s, mo_s, gq_s, gk_s, gv_s, gr_s, gates_s, la_s)))
    w_in = dict(zip(_PROJ_WEIGHTS, (wbig_ref, wsmall_ref, bsmall_ref, wa2_ref, ba_ref)))
    w_post = dict(zip(_POST_WEIGHTS, (wout_ref, gmlp_ref, w1_ref, w2_ref, gple_ref, wple_ref, wpg_ref, gfin_ref)))

    def front():
        h = _rms(xa_ref[...], gmix_ref[...]).astype(BF16)
        yield from _proj_stages(h, w_in, src, _seq_conv(cbuf_ref, tail_ref, convw_ref, convb_ref, tb))
        gate_sums = _mix_gate_sums(src, tril_ref, tri_ref)
        yield
        yield from _mix_stages(gate_sums, src, wred_ref, caug_ref, m_ref, sbd_ref, hm_s, hg_s)
        mixed_ref[slot] = _gate_heads(hm_s[...], hg_s[...], mo_s[...], gr_s[...], gmh_ref[...], ggh_ref[...])
        yield

    def back():
        yield from _post_stages(xb_ref[...], pb_ref[...], mixed_ref[1 - slot], w_post, y_ref)

    _interleave("bfff" "bfff" "bfff" "bff" "bff" "bff" "bff", front=front(), back=back())

    @pl.when(jnp.logical_and(s < n_blocks, r == steps_per_seq - 1))
    def _():
        cout_ref[0] = caug_ref[...]
        mout_ref[0] = m_ref[...]
        for g in range(G_HEADS):
            sout_ref[0, g] = sbd_ref[g * G_DK:(g + 1) * G_DK, g * G_DV:(g + 1) * G_DV]


def _seq_call(x2d, p2d, wts, consts, *, n_seq, seq_len):
    tb = SEQ_BLOCK
    steps_per_seq = seq_len // tb
    n_blocks = n_seq * steps_per_seq
    tril, tri, wred = consts
    front_blk = lambda s: jnp.minimum(s, n_blocks - 1)
    back_blk = lambda s: jnp.maximum(s - 1, 0)
    whole = pl.BlockSpec(memory_space=pltpu.VMEM)
    per_seq = lambda shape: pl.BlockSpec((1,) + shape, lambda s: (front_blk(s) // steps_per_seq,) + (0,) * len(shape))
    scratch = [
        pltpu.VMEM((tb + SUBLANES, QK_CONV), F32),
        pltpu.VMEM((M_HEADS, M_DH, 2 * M_DH), F32),
        pltpu.VMEM((SUBLANES, LANES), F32),
        pltpu.VMEM((G_KW, G_VW), F32),
        pltpu.VMEM((2, tb, D_MODEL), BF16),
    ] + [pltpu.VMEM((tb, _PROJ_WIDTH[k]), _PROJ_DTYPE[k]) for k in _PROJ_NAMES] + [
        pltpu.VMEM((tb, M_WIDTH), F32), pltpu.VMEM((tb, G_VW), F32)]
    return pl.pallas_call(
        functools.partial(_seq_kernel, steps_per_seq=steps_per_seq, n_blocks=n_blocks),
        grid=(n_blocks + 1,),
        in_specs=[pl.BlockSpec((tb, D_MODEL), lambda s: (front_blk(s), 0)),
                  pl.BlockSpec((tb, D_MODEL), lambda s: (back_blk(s), 0)),
                  pl.BlockSpec((tb, D_PLE), lambda s: (back_blk(s), 0))] + [whole] * 21,
        out_specs=[pl.BlockSpec((tb, D_MODEL), lambda s: (back_blk(s), 0)),
                   per_seq((SUBLANES, QK_CONV)), per_seq((M_HEADS, M_DH, 2 * M_DH)),
                   per_seq((SUBLANES, LANES)), per_seq((G_HEADS, G_DK, G_DV))],
        out_shape=[jax.ShapeDtypeStruct((n_blocks * tb, D_MODEL), F32),
                   jax.ShapeDtypeStruct((n_seq, SUBLANES, QK_CONV), F32),
                   jax.ShapeDtypeStruct((n_seq, M_HEADS, M_DH, 2 * M_DH), F32),
                   jax.ShapeDtypeStruct((n_seq, SUBLANES, LANES), F32),
                   jax.ShapeDtypeStruct((n_seq, G_HEADS, G_DK, G_DV), F32)],
        scratch_shapes=scratch,
        compiler_params=pltpu.CompilerParams(dimension_semantics=("arbitrary",), vmem_limit_bytes=VMEM_LIMIT),
        name="seq_fused",
    )(x2d, x2d, p2d, wts["g_mix"], wts["w_big"], wts["w_small"], wts["b_small"], wts["w_a2p"], wts["b_a"],
      wts["conv_w"], wts["conv_b"], tril, tri, wred, wts["g_mhead"], wts["g_ghead"],
      *[wts[k] for k in _POST_WEIGHTS])


def _mix_tok_kernel(mq_ref, mk_ref, mv_ref, gates_ref, gq_ref, gk_ref, gv_ref, la_ref,
                    c_ref, n_ref, m_ref, s_ref,
                    hm_ref, hg_ref, cn_ref, nn_ref, mn_ref, sn_ref):
    bb = mq_ref.shape[0]
    wide = bb * LANES
    diag = (lax.broadcasted_iota(jnp.int32, (bb, wide), 0)
            == _div_pow2(lax.broadcasted_iota(jnp.int32, (bb, wide), 1), LANES))
    block_ones = diag.astype(BF16)

    def block_diag(rows):
        return jnp.where(diag, jnp.concatenate([rows] * bb, axis=1), jnp.zeros((), rows.dtype))

    gates = gates_ref[...]
    m_all = m_ref[...]
    for hd in range(M_HEADS):
        cs_, ce_ = hd * M_DH, (hd + 1) * M_DH
        q_bf = mq_ref[:, cs_:ce_]
        q = q_bf.astype(F32)
        k = mk_ref[:, cs_:ce_]
        v_bf = mv_ref[:, cs_:ce_]
        v = v_bf.astype(F32)
        n_prev = n_ref[:, cs_:ce_]
        ig = gates[:, hd:hd + 1]
        lf = gates[:, M_HEADS + hd:M_HEADS + hd + 1]
        m_prev = m_all[:, hd:hd + 1]
        m_new = jnp.maximum(lf + m_prev, ig)
        scale = jnp.exp(lf + m_prev - m_new)
        wk = jnp.exp(ig - m_new)
        s_qk = jnp.sum(q * k, axis=1, keepdims=True) * wk
        den = scale * jnp.sum(q * n_prev, axis=1, keepdims=True) + s_qk
        inv = 1.0 / jnp.maximum(jnp.abs(den), jnp.exp(-m_new))
        nn_ref[:, cs_:ce_] = scale * n_prev + wk * k
        mn_ref[:, hd:hd + 1] = m_new
        q_w = _dot(q.T.astype(BF16), block_ones)
        outer = _dot((k * wk).T.astype(BF16), block_diag(v_bf))
        num_w = s_qk * v
        for b in range(bb):
            blk = slice(b * M_DH, (b + 1) * M_DH)
            c_prev = c_ref[b, hd]
            qc = jnp.sum(q_w[:, blk] * c_prev, axis=0, keepdims=True)
            hm_ref[b:b + 1, cs_:ce_] = (scale[b:b + 1] * qc + num_w[b:b + 1]) * inv[b:b + 1]
            cn_ref[b, hd] = scale[b:b + 1] * c_prev + outer[:, blk]
    la = la_ref[...]
    dec = jnp.exp(la)
    gq = gq_ref[...]
    gk = gk_ref[...]
    for g in range(G_HEADS):
        ks_, ke_ = g * G_DK, (g + 1) * G_DK
        vs_, ve_ = g * G_DV, (g + 1) * G_DV
        q = gq[:, ks_:ke_]
        k = gk[:, ks_:ke_]
        v_bf = gv_ref[:, vs_:ve_]
        av = jnp.sum(q * k, axis=1, keepdims=True) * v_bf.astype(F32)
        qd_w = _dot((q * dec[:, ks_:ke_]).T.astype(BF16), block_ones)
        outer = _dot(k.T.astype(BF16), block_diag(v_bf))
        dec_w = _spread_dot(dec[:, ks_:ke_].T, block_ones)
        for b in range(bb):
            blk = slice(b * G_DV, (b + 1) * G_DV)
            s_prev = s_ref[b, g]
            hg_ref[b:b + 1, vs_:ve_] = jnp.sum(qd_w[:, blk] * s_prev, axis=0, keepdims=True) + av[b:b + 1]
            sn_ref[b, g] = dec_w[:, blk] * s_prev + outer[:, blk]


def _mix_tok_call(p, c0, n0, m0, s0):
    bb = TOK_BATCH
    n = c0.shape[0]
    row = lambda w: pl.BlockSpec((bb, w), lambda i: (i, 0))
    st4 = lambda a, b_: pl.BlockSpec((bb, M_HEADS, a, b_), lambda i: (i, 0, 0, 0))
    return pl.pallas_call(
        _mix_tok_kernel,
        grid=(n // bb,),
        in_specs=[row(M_WIDTH), row(M_WIDTH), row(M_WIDTH), row(LANES), row(G_KW), row(G_KW), row(G_VW),
                  row(G_KW), st4(M_DH, M_DH), row(M_WIDTH), row(M_HEADS), st4(G_DK, G_DV)],
        out_specs=[row(M_WIDTH), row(G_VW), st4(M_DH, M_DH), row(M_WIDTH), row(M_HEADS), st4(G_DK, G_DV)],
        out_shape=[jax.ShapeDtypeStruct((n, M_WIDTH), F32), jax.ShapeDtypeStruct((n, G_VW), F32),
                   jax.ShapeDtypeStruct(c0.shape, F32), jax.ShapeDtypeStruct((n, M_WIDTH), F32),
                   jax.ShapeDtypeStruct((n, M_HEADS), F32), jax.ShapeDtypeStruct(s0.shape, F32)],
        compiler_params=pltpu.CompilerParams(dimension_semantics=("arbitrary",), vmem_limit_bytes=VMEM_LIMIT),
        name="mix_tok",
    )(p["mq"], p["mk"], p["mv"], p["gates"], p["gq"], p["gk"], p["gv"], p["la"], c0, n0, m0, s0)


_IN_OFFS = tuple(sum(IN_SIZES[:i]) for i in range(len(IN_SIZES) + 1))
_GATES_LO, _GATES_HI, _GA_LO = _IN_OFFS[3], _IN_OFFS[5], _IN_OFFS[9]
_N_SMALL = 2 * M_HEADS + G_RANK
_W_BIG_COLS = _GATES_LO + (_GA_LO - _GATES_HI)
_CAST_STEPS = 8


def _cast_kernel(wout_ref, w1_ref, w2_ref, wple_ref, wpg_ref, o_out, o_w1, o_w2, o_ple, o_pg):
    o_out[...] = wout_ref[...].astype(BF16)
    o_w1[...] = w1_ref[...].astype(BF16)
    o_w2[...] = w2_ref[...].astype(BF16)
    o_ple[...] = wple_ref[...].astype(BF16)
    o_pg[...] = wpg_ref[...].astype(BF16)


def _cast_call(*srcs):
    rows = lambda a: pl.BlockSpec((a.shape[0] // _CAST_STEPS, a.shape[1]), lambda i: (i, 0))
    return pl.pallas_call(
        _cast_kernel,
        grid=(_CAST_STEPS,),
        in_specs=[rows(a) for a in srcs],
        out_specs=[rows(a) for a in srcs],
        out_shape=[jax.ShapeDtypeStruct(a.shape, BF16) for a in srcs],
        compiler_params=pltpu.CompilerParams(dimension_semantics=("arbitrary",), vmem_limit_bytes=VMEM_LIMIT),
        name="cast_weights",
    )(*srcs)


def _cast_in_kernel(wt_ref, big_ref, small_ref):
    piece = M_WIDTH
    for n in range(_W_BIG_COLS // piece):
        dst = n * piece
        src = dst if dst < _GATES_LO else dst + (_GATES_HI - _GATES_LO)
        big_ref[:, dst:dst + piece] = wt_ref[src:src + piece, :].T.astype(BF16)
    small = jnp.concatenate([wt_ref[_GATES_LO:_GATES_HI, :], wt_ref[_GA_LO:, :],
                             jnp.zeros((LANES - _N_SMALL, D_MODEL), F32)], axis=0)
    small_ref[...] = small.T.astype(BF16)


def _cast_in_call(w_in_t):
    whole = pl.BlockSpec(memory_space=pltpu.VMEM)
    assert _GATES_LO % M_WIDTH == 0
    return pl.pallas_call(
        _cast_in_kernel,
        in_specs=[whole], out_specs=[whole, whole],
        out_shape=[jax.ShapeDtypeStruct((D_MODEL, _W_BIG_COLS), BF16),
                   jax.ShapeDtypeStruct((D_MODEL, LANES), BF16)],
        compiler_params=pltpu.CompilerParams(vmem_limit_bytes=VMEM_LIMIT),
        name="cast_w_in",
    )(w_in_t)


def _prep_weights(w_in, conv_w, conv_b, b_gate, w_a2, b_a, g_mhead, g_ghead, w_out, g_mix, g_mlp, w1, w2,
                  g_ple, w_ple, w_pg, g_final):
    w_big, w_small = _cast_in_call(jnp.swapaxes(w_in, 0, 1))
    w_out_b, w1_b, w2_b, w_ple_b, w_pg_b = _cast_call(w_out, w1, w2, w_ple, w_pg)
    b_small = jnp.concatenate([b_gate, jnp.zeros((LANES - 2 * M_HEADS,), F32)])[None]
    w_a2p = jnp.concatenate([jnp.zeros((2 * M_HEADS, G_KW), F32), w_a2,
                             jnp.zeros((LANES - _N_SMALL, G_KW), F32)], axis=0).astype(BF16)
    return dict(
        w_big=w_big, w_small=w_small, b_small=b_small, w_a2p=w_a2p, b_a=b_a[None],
        conv_w=conv_w, conv_b=conv_b[None], g_mix=g_mix[None], g_mhead=g_mhead[None], g_ghead=g_ghead[None],
        w_out=w_out_b, g_mlp=g_mlp[None], w1=w1_b, w2=w2_b,
        g_ple=g_ple[None], w_ple=w_ple_b, w_pg=w_pg_b, g_final=g_final[None])


def _mix_constants(ts):
    t = np.arange(ts)
    tril = t[None, :] <= t[:, None]
    tri = (t[:, None] // CHUNK == t[None, :] // CHUNK) & tril
    rr = np.arange(G_KW)
    wred = ((rr[None, :, None] // G_DK == rr[None, None, :] // G_DK)
            & (rr[None, None, :] % SUB == np.arange(SUB)[:, None, None]))
    return tuple(jnp.asarray(np.asarray(a, dtype=BF16)) for a in (tril, tri, wred))


def kernel(x_prompt, x_sample, p_prompt, p_sample, state_mlstm_C, state_mlstm_n, state_mlstm_m, state_conv,
           state_gla_S, w_in, conv_w, conv_b, b_gate, w_a2, b_a, g_mhead, g_ghead, w_out, g_mix, g_mlp, w1,
           w2, g_ple, w_ple, w_pg, g_final):
    assert w_in.shape[0] == 1, "single-layer trunk"
    n_seq, seq_len, _ = x_prompt.shape
    n_tok = x_sample.shape[0]
    assert x_sample.shape[1] == 1 and seq_len % SEQ_BLOCK == 0 and n_tok % TOK_BATCH == 0
    wts = _prep_weights(w_in[0], conv_w[0], conv_b[0], b_gate[0], w_a2[0], b_a[0], g_mhead[0], g_ghead[0],
                        w_out[0], g_mix[0], g_mlp[0], w1[0], w2[0], g_ple[0], w_ple[0], w_pg[0], g_final)

    y_p, tail_p, caug_p, m_p, s_p = _seq_call(
        x_prompt.reshape(n_seq * seq_len, D_MODEL), p_prompt[0].reshape(n_seq * seq_len, D_PLE), wts,
        _mix_constants(SEQ_BLOCK), n_seq=n_seq, seq_len=seq_len)

    xs = x_sample.reshape(n_tok, D_MODEL)
    buf = state_conv[0]
    outs = _proj_tok_call(xs, (buf[:, 0], buf[:, 1], buf[:, 2]), wts)
    ps = dict(zip(_PROJ_NAMES, outs[:-1]))
    raw_s = outs[-1]
    hm_s, hg_s, c_s, n_s, m_s, s_s = _mix_tok_call(
        ps, state_mlstm_C[0], state_mlstm_n[0].reshape(n_tok, M_WIDTH), state_mlstm_m[0], state_gla_S[0])
    y_s = _post_tok_call(xs, p_sample[0].reshape(n_tok, D_PLE), hm_s, hg_s, ps["mo"], ps["gr"], wts)

    return (y_p.reshape(n_seq, seq_len, D_MODEL),
            y_s.reshape(n_tok, 1, D_MODEL),
            caug_p[None, :, :, :, :M_DH],
            caug_p[None, :, :, :, M_DH],
            m_p[None, :, :M_HEADS, 0],
            tail_p[None, :, SUBLANES - (CONV_W - 1):, :],
            s_p[None],
            c_s[None],
            n_s.reshape(1, n_tok, M_HEADS, M_DH),
            m_s[None],
            jnp.stack([buf[:, 1], buf[:, 2], raw_s], axis=1)[None],
            s_s[None])
```

```python
import functools

import jax
import jax.numpy as jnp
import numpy as np
from jax import lax
from jax.experimental import pallas as pl
from jax.experimental.pallas import tpu as pltpu

D_MODEL = 1024
M_HEADS = 4
M_DH = 128
M_WIDTH = M_HEADS * M_DH
G_HEADS = 4
G_DK = 64
G_DV = 128
G_KW = G_HEADS * G_DK
G_VW = G_HEADS * G_DV
G_RANK = 16
G_TAU = 16.0
CONV_W = 4
QK_CONV = 2 * M_WIDTH
D_FF = 4 * D_MODEL
D_PLE = 256
CHUNK = 64
SUB = 16
EPS = 1e-6
IN_SIZES = (QK_CONV, M_WIDTH, M_WIDTH, M_HEADS, M_HEADS, G_KW, G_KW, G_VW, G_VW, G_RANK)

LANES = 128
SUBLANES = 8
VMEM_LIMIT = 60 * 1024 * 1024
SEQ_BLOCK = 256
TOK_BATCH = 8

F32 = jnp.float32
BF16 = jnp.bfloat16
NEG = -1e30


def _rms(x, g):
    return x * lax.rsqrt(jnp.mean(x * x, axis=-1, keepdims=True) + EPS) * g


def _log_sigmoid(x):
    return jnp.minimum(x, 0.0) - jnp.log(1.0 + jnp.exp(-jnp.abs(x)))


def _sigmoid(x):
    return 0.5 * jnp.tanh(0.5 * x) + 0.5


def _div_pow2(idx, d):
    assert d & (d - 1) == 0
    return lax.shift_right_logical(idx, d.bit_length() - 1)


def _dot(a, b):
    return jnp.dot(a, b, preferred_element_type=F32)


def _dot_nt(a, b):
    return lax.dot_general(a, b, (((1,), (1,)), ((), ())), preferred_element_type=F32)


def _split3(x):
    hi = x.astype(BF16)
    r1 = x - hi.astype(F32)
    mid = r1.astype(BF16)
    lo = (r1 - mid.astype(F32)).astype(BF16)
    return hi, mid, lo


def _cumsum_dot(tri, x):
    hi, mid, lo = _split3(x)
    return _dot(tri, hi) + _dot(tri, mid) + _dot(tri, lo)


def _spread_dot(x, onehot):
    hi, mid, lo = _split3(x)
    return _dot(hi, onehot) + _dot(mid, onehot) + _dot(lo, onehot)


def _interleave(pattern, **streams):
    by_letter = {name[0]: gen for name, gen in streams.items()}
    for letter in pattern:
        next(by_letter[letter], None)
    for gen in by_letter.values():
        _run(gen)


def _run(gen):
    for _ in gen:
        pass


def _proj_stages(h, w, sinks, conv_piece):
    width = M_WIDTH

    def put_q(raw):
        sinks["mq"][...] = conv_piece(raw, 0).astype(sinks["mq"].dtype)

    def put_k(raw):
        sinks["mk"][...] = conv_piece(raw, M_WIDTH) * (M_DH ** -0.5)

    def put_mv(raw):
        sinks["mv"][...] = raw.astype(sinks["mv"].dtype)

    def put_mo(raw):
        sinks["mo"][...] = raw

    def put_gqk(raw):
        sinks["gq"][...] = raw[:, :G_KW] * (G_DK ** -0.5)
        sinks["gk"][...] = raw[:, G_KW:]

    def put_gv(raw):
        sinks["gv"][...] = raw.astype(sinks["gv"].dtype)

    def put_gr(raw):
        sinks["gr"][...] = raw

    def put_small(small):
        g = small + w["b_small"][...]
        lane = lax.broadcasted_iota(jnp.int32, g.shape, 1)
        sinks["gates"][...] = jnp.where(lane < M_HEADS, g, _log_sigmoid(g))
        z = _dot(small.astype(BF16), w["w_a2p"][...]) + w["b_a"][...]
        sinks["la"][...] = _log_sigmoid(z) * (1.0 / G_TAU)

    epilogues = [put_small, put_q, put_k, put_mv, put_mo, put_gqk, put_gv, put_gr]
    assert w["w_big"].shape[1] == width * (len(epilogues) - 1)
    pending = (_dot(h, w["w_small"][...]), epilogues[0])
    yield
    for n, epi in enumerate(epilogues[1:]):
        cur = _dot(h, w["w_big"][:, n * width:(n + 1) * width])
        pending[1](pending[0])
        pending = (cur, epi)
        yield
    pending[1](pending[0])
    yield


def _seq_conv(cbuf_ref, tail_ref, convw_ref, convb_ref, tb):
    cw = convw_ref[...]

    def conv_piece(raw, lo):
        cols = slice(lo, lo + raw.shape[1])
        cbuf_ref[SUBLANES:SUBLANES + tb, cols] = raw
        y = convb_ref[:, cols] + raw * cw[3:4, cols]
        for j in range(CONV_W - 1):
            off = SUBLANES - (CONV_W - 1) + j
            y = y + cbuf_ref[off:off + tb, cols] * cw[j:j + 1, cols]
        last = cbuf_ref[tb:tb + SUBLANES, cols]
        tail_ref[0, :, cols] = last
        cbuf_ref[0:SUBLANES, cols] = last
        return y * _sigmoid(y)

    return conv_piece


def _tok_conv(b0_ref, b1_ref, b2_ref, raw_ref, convw_ref, convb_ref):
    cw = convw_ref[...]

    def conv_piece(raw, lo):
        cols = slice(lo, lo + raw.shape[1])
        raw_ref[:, cols] = raw
        y = (convb_ref[:, cols] + b0_ref[:, cols] * cw[0:1, cols] + b1_ref[:, cols] * cw[1:2, cols]
             + b2_ref[:, cols] * cw[2:3, cols] + raw * cw[3:4, cols])
        return y * _sigmoid(y)

    return conv_piece


_PROJ_NAMES = ("mq", "mk", "mv", "mo", "gq", "gk", "gv", "gr", "gates", "la")
_PROJ_WIDTH = dict(mq=M_WIDTH, mk=M_WIDTH, mv=M_WIDTH, mo=M_WIDTH, gq=G_KW, gk=G_KW, gv=G_VW, gr=G_VW,
                   gates=LANES, la=G_KW)
_PROJ_DTYPE = dict(mq=BF16, mk=F32, mv=BF16, mo=F32, gq=F32, gk=F32, gv=BF16, gr=F32, gates=F32, la=F32)
_PROJ_WEIGHTS = ("w_big", "w_small", "b_small", "w_a2p", "b_a")


def _proj_tok_kernel(x_ref, b0_ref, b1_ref, b2_ref, gmix_ref, wbig_ref, wsmall_ref, bsmall_ref, wa2_ref,
                     ba_ref, convw_ref, convb_ref, *out_refs):
    sinks = dict(zip(_PROJ_NAMES, out_refs[:-1]))
    raw_ref = out_refs[-1]
    w = dict(zip(_PROJ_WEIGHTS, (wbig_ref, wsmall_ref, bsmall_ref, wa2_ref, ba_ref)))
    h = _rms(x_ref[...], gmix_ref[...]).astype(BF16)
    _run(_proj_stages(h, w, sinks, _tok_conv(b0_ref, b1_ref, b2_ref, raw_ref, convw_ref, convb_ref)))


def _proj_tok_call(x2d, conv_rows, wts):
    n = x2d.shape[0]
    row = lambda width: pl.BlockSpec((n, width), lambda i: (0, 0))
    whole = pl.BlockSpec(memory_space=pltpu.VMEM)
    names = list(_PROJ_NAMES)
    return pl.pallas_call(
        _proj_tok_kernel,
        grid=(1,),
        in_specs=[row(D_MODEL)] + [row(QK_CONV)] * 3 + [whole] * 8,
        out_specs=[row(_PROJ_WIDTH[k]) for k in names] + [row(QK_CONV)],
        out_shape=[jax.ShapeDtypeStruct((n, _PROJ_WIDTH[k]), _PROJ_DTYPE[k]) for k in names]
        + [jax.ShapeDtypeStruct((n, QK_CONV), F32)],
        compiler_params=pltpu.CompilerParams(dimension_semantics=("arbitrary",), vmem_limit_bytes=VMEM_LIMIT),
        name="proj_tok",
    )(x2d, *conv_rows, wts["g_mix"], wts["w_big"], wts["w_small"], wts["b_small"], wts["w_a2p"], wts["b_a"],
      wts["conv_w"], wts["conv_b"])


def _mix_gate_sums(src, tril_ref, tri_ref):
    gates = src["gates"][...]
    bcum = _cumsum_dot(tril_ref[...], gates)
    bc = _cumsum_dot(tri_ref[...], src["la"][...])
    return gates, bcum, bc, gates.T, bcum.T, bc.T


def _mix_stages(gate_sums, src, wred_ref, caug_ref, m_ref, sbd_ref, hm_ref, hg_ref):
    ts = src["mq"].shape[0]
    n_chunks = ts // CHUNK
    gates, bcum, bc, gates_t, bcum_t, bc_t = gate_sums
    gq = src["gq"][...]
    gk = src["gk"][...]
    gv_ref = src["gv"]

    nb = ts // SUB
    half = SUB // 2
    assert half == SUBLANES
    q3 = gq.reshape(nb, SUB, G_KW)
    k3 = gk.reshape(nb, SUB, G_KW)
    bc3 = bc.reshape(nb, SUB, G_KW)
    tl = lax.broadcasted_iota(jnp.int32, (nb, SUB, G_KW), 1)
    q3u = gq.reshape(nb, 2, half, G_KW)[:, 1]
    bc3u = bc.reshape(nb, 2, half, G_KW)[:, 1]
    tlu = lax.broadcasted_iota(jnp.int32, (nb, half, G_KW), 1) + half

    def exact_pass(j, acc):
        acc_all, acc_upper = acc
        if j < half:
            arg = jnp.where(tl >= j, bc3 - bc3[:, j:j + 1, :], NEG)
            e = (q3 * k3[:, j:j + 1, :] * jnp.exp(arg)).reshape(ts, G_KW)
            return acc_all + _dot(e.astype(BF16), wred_ref[j]), acc_upper
        arg = jnp.where(tlu >= j, bc3u - bc3[:, j:j + 1, :], NEG)
        e = (q3u * k3[:, j:j + 1, :] * jnp.exp(arg)).reshape(ts // 2, G_KW)
        return acc_all, acc_upper + _dot(e.astype(BF16), wred_ref[j])

    krow = lax.broadcasted_iota(jnp.int32, (CHUNK, G_KW), 0)
    same_head_kk = (_div_pow2(lax.broadcasted_iota(jnp.int32, (G_KW, G_KW), 0), G_DK)
                    == _div_pow2(lax.broadcasted_iota(jnp.int32, (G_KW, G_KW), 1), G_DK))
    same_head_kv = (_div_pow2(lax.broadcasted_iota(jnp.int32, (G_KW, G_VW), 0), G_DK)
                    == _div_pow2(lax.broadcasted_iota(jnp.int32, (G_KW, G_VW), 1), G_DV))

    def cross_block_scores(c):
        lo = c * CHUNK
        bc_c = bc[lo:lo + CHUNK]
        k_c = gk[lo:lo + CHUNK]
        out = [jnp.zeros((SUB, G_KW), F32)]
        for i in range(1, CHUNK // SUB):
            r0 = lo + i * SUB
            r_i = bc[r0 - 1:r0, :]
            qi = (gq[r0:r0 + SUB] * jnp.exp(bc[r0:r0 + SUB] - r_i)).astype(BF16)
            ki = (k_c * jnp.exp(jnp.where(krow < i * SUB, r_i - bc_c, NEG))).astype(BF16)
            kbd = jnp.where(same_head_kk, jnp.concatenate([ki] * G_HEADS, axis=0), jnp.zeros((), BF16))
            out.append(_dot_nt(qi, kbd))
        return out

    def state_update_term(c):
        lo, hi = c * CHUNK, (c + 1) * CHUNK
        b_end = bc[hi - 1:hi, :]
        k_out_t = (gk[lo:hi] * jnp.exp(b_end - bc[lo:hi])).T.astype(BF16)
        rows = []
        for g in range(G_HEADS):
            blk = _dot(k_out_t[g * G_DK:(g + 1) * G_DK], gv_ref[lo:hi, g * G_DV:(g + 1) * G_DV])
            zero = jnp.zeros((G_DK, G_DV), F32)
            rows.append(jnp.concatenate([zero] * g + [blk] + [zero] * (G_HEADS - 1 - g), axis=1))
        return jnp.concatenate(rows, axis=0)

    adiag = (jnp.zeros((ts, G_KW), F32), jnp.zeros((ts // 2, G_KW), F32))
    per_stage = SUB // 4
    assert n_chunks == 4
    offs = []
    s_terms = []

    causal = (lax.broadcasted_iota(jnp.int32, (ts, ts), 0) >= lax.broadcasted_iota(jnp.int32, (ts, ts), 1))
    ones_col = (lax.broadcasted_iota(jnp.int32, (ts, LANES), 1) == 0).astype(BF16)
    heads = range(M_HEADS)
    hsl = [slice(hd * M_DH, (hd + 1) * M_DH) for hd in heads]
    b_col = [bcum[:, M_HEADS + hd:M_HEADS + hd + 1] for hd in heads]
    b_row = [bcum_t[M_HEADS + hd:M_HEADS + hd + 1, :] for hd in heads]
    i_row = [gates_t[hd:hd + 1, :] for hd in heads]
    m_prev = [m_ref[hd:hd + 1, 0:1] for hd in heads]
    q = [src["mq"][:, hsl[hd]] for hd in heads]
    i_col = [gates[:, hd:hd + 1] for hd in heads]
    k = [src["mk"][:, hsl[hd]] for hd in heads]
    vaug = [jnp.concatenate([src["mv"][:, hsl[hd]], ones_col], axis=1) for hd in heads]
    caug = [caug_ref[hd] for hd in heads]
    s_qk = [_dot_nt(q[hd], k[hd].astype(BF16)) for hd in heads]
    qc = [_dot(q[hd], caug[hd].astype(BF16)) for hd in heads]
    for j in range(0, per_stage):
        adiag = exact_pass(j, adiag)
    offs += cross_block_scores(0)
    s_terms.append(state_update_term(0))
    yield
    b_last = [b_col[hd][ts - 1:ts, :] for hd in heads]
    dec = [b_last[hd] - b_col[hd] + i_col[hd] for hd in heads]
    m_new = [jnp.maximum(b_last[hd] + m_prev[hd], jnp.max(dec[hd], axis=0, keepdims=True)) for hd in heads]
    kw_t = [(k[hd] * jnp.exp(dec[hd] - m_new[hd])).T.astype(BF16) for hd in heads]
    upd = [_dot(kw_t[hd], vaug[hd]) for hd in heads]
    for j in range(per_stage, 2 * per_stage):
        adiag = exact_pass(j, adiag)
    offs += cross_block_scores(1)
    s_terms.append(state_update_term(1))
    yield
    dmat = [jnp.where(causal, b_col[hd] - b_row[hd] + i_row[hd], -jnp.inf) for hd in heads]
    inter = [b_col[hd] + m_prev[hd] for hd in heads]
    m_tok = [jnp.maximum(inter[hd], jnp.max(dmat[hd], axis=1, keepdims=True)) for hd in heads]
    for j in range(2 * per_stage, 3 * per_stage):
        adiag = exact_pass(j, adiag)
    offs += cross_block_scores(2)
    s_terms.append(state_update_term(2))
    yield
    p = [(s_qk[hd] * jnp.exp(dmat[hd] - m_tok[hd])).astype(BF16) for hd in heads]
    pv = [_dot(p[hd], vaug[hd]) for hd in heads]
    for j in range(3 * per_stage, SUB):
        adiag = exact_pass(j, adiag)
    offs += cross_block_scores(3)
    s_terms.append(state_update_term(3))
    yield
    for hd in heads:
        tot = jnp.exp(inter[hd] - m_tok[hd]) * qc[hd] + pv[hd]
        den = tot[:, M_DH:M_DH + 1]
        hm_ref[:, hsl[hd]] = tot[:, :M_DH] / jnp.maximum(jnp.abs(den), jnp.exp(-m_tok[hd]))
        caug_ref[hd] = jnp.exp(b_last[hd] + m_prev[hd] - m_new[hd]) * caug[hd] + upd[hd]
        m_ref[hd:hd + 1, :] = jnp.broadcast_to(m_new[hd], (1, LANES))
    yield

    sub_of = lambda idx: lax.shift_right_logical(idx & (CHUNK - 1), SUB.bit_length() - 1)
    rowb = sub_of(lax.broadcasted_iota(jnp.int32, (ts, G_KW), 0))
    colb = sub_of(lax.broadcasted_iota(jnp.int32, (ts, G_KW), 1))
    acc_all, acc_upper = adiag
    upper = acc_upper.reshape(nb, half, G_KW)
    acc_all = acc_all + jnp.stack([jnp.zeros_like(upper), upper], axis=1).reshape(ts, G_KW)
    adiag = jnp.where(rowb == colb, acc_all, 0.0)
    intra = (jnp.concatenate(offs, axis=0) + adiag).astype(BF16)
    q_in = (gq * jnp.exp(bc)).astype(BF16)
    o_intra = []
    for c in range(n_chunks):
        lo, hi = c * CHUNK, (c + 1) * CHUNK
        vbd = jnp.where(same_head_kv, jnp.concatenate([gv_ref[lo:hi, :]] * G_HEADS, axis=0),
                        jnp.zeros((), BF16))
        o_intra.append(_dot(intra[lo:hi], vbd))
    sbd = sbd_ref[...]
    for c in range(n_chunks):
        lo, hi = c * CHUNK, (c + 1) * CHUNK
        hg_ref[lo:hi, :] = o_intra[c] + _dot(q_in[lo:hi], sbd.astype(BF16))
        dcol = jnp.exp(bc_t[:, hi - 1:hi])
        sbd = dcol * sbd + s_terms[c]
    sbd_ref[...] = sbd
    yield


def _head_norm(hv, n_heads, width):
    parts = []
    for hd in range(n_heads):
        seg = hv[:, hd * width:(hd + 1) * width]
        parts.append(seg * lax.rsqrt(jnp.mean(seg * seg, axis=-1, keepdims=True) + EPS))
    return jnp.concatenate(parts, axis=1)


def _gate_heads(hm, hg, mo, gr, gmh, ggh):
    hm = _head_norm(hm, M_HEADS, M_DH) * gmh * _sigmoid(mo)
    hg = _head_norm(hg, G_HEADS, G_DV) * ggh * (gr * _sigmoid(gr))
    return jnp.concatenate([hm, hg], axis=1).astype(BF16)


_POST_WEIGHTS = ("w_out", "g_mlp", "w1", "w2", "g_ple", "w_ple", "w_pg", "g_final")


def _post_stages(x, p, mixed, w, y_ref):
    x1 = x + _dot(mixed, w["w_out"][...])
    n1 = _rms(x1, w["g_mlp"][...]).astype(BF16)
    yield
    n_ff = 4
    ff = D_FF // n_ff
    acts = []
    for j in range(n_ff):
        u = _dot(n1, w["w1"][:, j * ff:(j + 1) * ff])
        acts.append(jnp.square(jnp.maximum(u, 0.0)).astype(BF16))
        yield
    act = jnp.concatenate(acts, axis=1)
    half = D_MODEL // 2
    mlp = []
    for j in range(2):
        mlp.append(_dot(act, w["w2"][:, j * half:(j + 1) * half]))
        yield
    x2 = x1 + jnp.concatenate(mlp, axis=1)
    ple = _dot(p.astype(BF16), w["w_ple"][...])
    n2 = _rms(x2, w["g_ple"][...]).astype(BF16)
    yield
    gate = _sigmoid(_dot(n2, w["w_pg"][...]))
    x3 = x2 + ple * gate
    y_ref[...] = _rms(x3, w["g_final"][...])
    yield


def _post_tok_kernel(x_ref, p_ref, hm_ref, hg_ref, mo_ref, gr_ref, gmh_ref, ggh_ref, wout_ref, gmlp_ref,
                     w1_ref, w2_ref, gple_ref, wple_ref, wpg_ref, gfin_ref, y_ref):
    mixed = _gate_heads(hm_ref[...], hg_ref[...], mo_ref[...], gr_ref[...], gmh_ref[...], ggh_ref[...])
    w = dict(zip(_POST_WEIGHTS, (wout_ref, gmlp_ref, w1_ref, w2_ref, gple_ref, wple_ref, wpg_ref, gfin_ref)))
    _run(_post_stages(x_ref[...], p_ref[...], mixed, w, y_ref))


def _post_tok_call(x2d, p2d, hm, hg, mo, gr, wts):
    n = x2d.shape[0]
    row = lambda width: pl.BlockSpec((n, width), lambda i: (0, 0))
    whole = pl.BlockSpec(memory_space=pltpu.VMEM)
    return pl.pallas_call(
        _post_tok_kernel,
        grid=(1,),
        in_specs=[row(D_MODEL), row(D_PLE), row(M_WIDTH), row(G_VW), row(M_WIDTH), row(G_VW)] + [whole] * 10,
        out_specs=row(D_MODEL),
        out_shape=jax.ShapeDtypeStruct((n, D_MODEL), F32),
        compiler_params=pltpu.CompilerParams(dimension_semantics=("arbitrary",), vmem_limit_bytes=VMEM_LIMIT),
        name="post_tok",
    )(x2d, p2d, hm, hg, mo, gr, wts["g_mhead"], wts["g_ghead"], *[wts[k] for k in _POST_WEIGHTS])


def _seq_kernel(xa_ref, xb_ref, pb_ref, gmix_ref, wbig_ref, wsmall_ref, bsmall_ref, wa2_ref, ba_ref,
                convw_ref, convb_ref, tril_ref, tri_ref, wred_ref, gmh_ref, ggh_ref,
                wout_ref, gmlp_ref, w1_ref, w2_ref, gple_ref, wple_ref, wpg_ref, gfin_ref,
                y_ref, tail_ref, cout_ref, mout_ref, sout_ref,
                cbuf_ref, caug_ref, m_ref, sbd_ref, mixed_ref,
                mq_s, mk_s, mv_s, mo_s, gq_s, gk_s, gv_s, gr_s, gates_s, la_s, hm_s, hg_s,
                *, steps_per_seq, n_blocks):
    tb = xa_ref.shape[0]
    s = pl.program_id(0)
    r = lax.rem(jnp.minimum(s, n_blocks - 1), steps_per_seq)

    @pl.when(s == 0)
    def _():
        mixed_ref[...] = jnp.zeros(mixed_ref.shape, BF16)

    @pl.when(r == 0)
    def _():
        cbuf_ref[0:SUBLANES, :] = jnp.zeros((SUBLANES, QK_CONV), F32)
        caug_ref[...] = jnp.zeros(caug_ref.shape, F32)
        m_ref[...] = jnp.zeros(m_ref.shape, F32)
        sbd_ref[...] = jnp.zeros(sbd_ref.shape, F32)

    slot = lax.rem(s, 2)
    src = dict(zip(_PROJ_NAMES, (mq_s, mk_s, mv_s, mo_s, gq_s, gk_s, gv_s, gr_s, gates_s, la_s)))
    w_in = dict(zip(_PROJ_WEIGHTS, (wbig_ref, wsmall_ref, bsmall_ref, wa2_ref, ba_ref)))
    w_post = dict(zip(_POST_WEIGHTS, (wout_ref, gmlp_ref, w1_ref, w2_ref, gple_ref, wple_ref, wpg_ref, gfin_ref)))

    def front():
        h = _rms(xa_ref[...], gmix_ref[...]).astype(BF16)
        yield from _proj_stages(h, w_in, src, _seq_conv(cbuf_ref, tail_ref, convw_ref, convb_ref, tb))
        gate_sums = _mix_gate_sums(src, tril_ref, tri_ref)
        yield
        yield from _mix_stages(gate_sums, src, wred_ref, caug_ref, m_ref, sbd_ref, hm_s, hg_s)
        mixed_ref[slot] = _gate_heads(hm_s[...], hg_s[...], mo_s[...], gr_s[...], gmh_ref[...], ggh_ref[...])
        yield

    def back():
        yield from _post_stages(xb_ref[...], pb_ref[...], mixed_ref[1 - slot], w_post, y_ref)

    _interleave("bfff" "bfff" "bfff" "bff" "bf" "bf" "bf" "bf" "bff", front=front(), back=back())

    @pl.when(jnp.logical_and(s < n_blocks, r == steps_per_seq - 1))
    def _():
        cout_ref[0] = caug_ref[...]
        mout_ref[0] = m_ref[...]
        for g in range(G_HEADS):
            sout_ref[0, g] = sbd_ref[g * G_DK:(g + 1) * G_DK, g * G_DV:(g + 1) * G_DV]


def _seq_call(x2d, p2d, wts, consts, *, n_seq, seq_len):
    tb = SEQ_BLOCK
    steps_per_seq = seq_len // tb
    n_blocks = n_seq * steps_per_seq
    tril, tri, wred = consts
    front_blk = lambda s: jnp.minimum(s, n_blocks - 1)
    back_blk = lambda s: jnp.maximum(s - 1, 0)
    whole = pl.BlockSpec(memory_space=pltpu.VMEM)
    per_seq = lambda shape: pl.BlockSpec((1,) + shape, lambda s: (front_blk(s) // steps_per_seq,) + (0,) * len(shape))
    scratch = [
        pltpu.VMEM((tb + SUBLANES, QK_CONV), F32),
        pltpu.VMEM((M_HEADS, M_DH, 2 * M_DH), F32),
        pltpu.VMEM((SUBLANES, LANES), F32),
        pltpu.VMEM((G_KW, G_VW), F32),
        pltpu.VMEM((2, tb, D_MODEL), BF16),
    ] + [pltpu.VMEM((tb, _PROJ_WIDTH[k]), _PROJ_DTYPE[k]) for k in _PROJ_NAMES] + [
        pltpu.VMEM((tb, M_WIDTH), F32), pltpu.VMEM((tb, G_VW), F32)]
    return pl.pallas_call(
        functools.partial(_seq_kernel, steps_per_seq=steps_per_seq, n_blocks=n_blocks),
        grid=(n_blocks + 1,),
        in_specs=[pl.BlockSpec((tb, D_MODEL), lambda s: (front_blk(s), 0)),
                  pl.BlockSpec((tb, D_MODEL), lambda s: (back_blk(s), 0)),
                  pl.BlockSpec((tb, D_PLE), lambda s: (back_blk(s), 0))] + [whole] * 21,
        out_specs=[pl.BlockSpec((tb, D_MODEL), lambda s: (back_blk(s), 0)),
                   per_seq((SUBLANES, QK_CONV)), per_seq((M_HEADS, M_DH, 2 * M_DH)),
                   per_seq((SUBLANES, LANES)), per_seq((G_HEADS, G_DK, G_DV))],
        out_shape=[jax.ShapeDtypeStruct((n_blocks * tb, D_MODEL), F32),
                   jax.ShapeDtypeStruct((n_seq, SUBLANES, QK_CONV), F32),
                   jax.ShapeDtypeStruct((n_seq, M_HEADS, M_DH, 2 * M_DH), F32),
                   jax.ShapeDtypeStruct((n_seq, SUBLANES, LANES), F32),
                   jax.ShapeDtypeStruct((n_seq, G_HEADS, G_DK, G_DV), F32)],
        scratch_shapes=scratch,
        compiler_params=pltpu.CompilerParams(dimension_semantics=("arbitrary",), vmem_limit_bytes=VMEM_LIMIT),
        name="seq_fused",
    )(x2d, x2d, p2d, wts["g_mix"], wts["w_big"], wts["w_small"], wts["b_small"], wts["w_a2p"], wts["b_a"],
      wts["conv_w"], wts["conv_b"], tril, tri, wred, wts["g_mhead"], wts["g_ghead"],
      *[wts[k] for k in _POST_WEIGHTS])


def _mix_tok_kernel(mq_ref, mk_ref, mv_ref, gates_ref, gq_ref, gk_ref, gv_ref, la_ref,
                    c_ref, n_ref, m_ref, s_ref,
                    hm_ref, hg_ref, cn_ref, nn_ref, mn_ref, sn_ref):
    bb = mq_ref.shape[0]
    wide = bb * LANES
    diag = (lax.broadcasted_iota(jnp.int32, (bb, wide), 0)
            == _div_pow2(lax.broadcasted_iota(jnp.int32, (bb, wide), 1), LANES))
    block_ones = diag.astype(BF16)

    def block_diag(rows):
        return jnp.where(diag, jnp.concatenate([rows] * bb, axis=1), jnp.zeros((), rows.dtype))

    gates = gates_ref[...]
    m_all = m_ref[...]
    for hd in range(M_HEADS):
        cs_, ce_ = hd * M_DH, (hd + 1) * M_DH
        q_bf = mq_ref[:, cs_:ce_]
        q = q_bf.astype(F32)
        k = mk_ref[:, cs_:ce_]
        v_bf = mv_ref[:, cs_:ce_]
        v = v_bf.astype(F32)
        n_prev = n_ref[:, cs_:ce_]
        ig = gates[:, hd:hd + 1]
        lf = gates[:, M_HEADS + hd:M_HEADS + hd + 1]
        m_prev = m_all[:, hd:hd + 1]
        m_new = jnp.maximum(lf + m_prev, ig)
        scale = jnp.exp(lf + m_prev - m_new)
        wk = jnp.exp(ig - m_new)
        s_qk = jnp.sum(q * k, axis=1, keepdims=True) * wk
        den = scale * jnp.sum(q * n_prev, axis=1, keepdims=True) + s_qk
        inv = 1.0 / jnp.maximum(jnp.abs(den), jnp.exp(-m_new))
        nn_ref[:, cs_:ce_] = scale * n_prev + wk * k
        mn_ref[:, hd:hd + 1] = m_new
        q_w = _dot(q.T.astype(BF16), block_ones)
        outer = _dot((k * wk).T.astype(BF16), block_diag(v_bf))
        num_w = s_qk * v
        for b in range(bb):
            blk = slice(b * M_DH, (b + 1) * M_DH)
            c_prev = c_ref[b, hd]
            qc = jnp.sum(q_w[:, blk] * c_prev, axis=0, keepdims=True)
            hm_ref[b:b + 1, cs_:ce_] = (scale[b:b + 1] * qc + num_w[b:b + 1]) * inv[b:b + 1]
            cn_ref[b, hd] = scale[b:b + 1] * c_prev + outer[:, blk]
    la = la_ref[...]
    dec = jnp.exp(la)
    gq = gq_ref[...]
    gk = gk_ref[...]
    for g in range(G_HEADS):
        ks_, ke_ = g * G_DK, (g + 1) * G_DK
        vs_, ve_ = g * G_DV, (g + 1) * G_DV
        q = gq[:, ks_:ke_]
        k = gk[:, ks_:ke_]
        v_bf = gv_ref[:, vs_:ve_]
        av = jnp.sum(q * k, axis=1, keepdims=True) * v_bf.astype(F32)
        qd_w = _dot((q * dec[:, ks_:ke_]).T.astype(BF16), block_ones)
        outer = _dot(k.T.astype(BF16), block_diag(v_bf))
        dec_w = _spread_dot(dec[:, ks_:ke_].T, block_ones)
        for b in range(bb):
            blk = slice(b * G_DV, (b + 1) * G_DV)
            s_prev = s_ref[b, g]
            hg_ref[b:b + 1, vs_:ve_] = jnp.sum(qd_w[:, blk] * s_prev, axis=0, keepdims=True) + av[b:b + 1]
            sn_ref[b, g] = dec_w[:, blk] * s_prev + outer[:, blk]


def _mix_tok_call(p, c0, n0, m0, s0):
    bb = TOK_BATCH
    n = c0.shape[0]
    row = lambda w: pl.BlockSpec((bb, w), lambda i: (i, 0))
    st4 = lambda a, b_: pl.BlockSpec((bb, M_HEADS, a, b_), lambda i: (i, 0, 0, 0))
    return pl.pallas_call(
        _mix_tok_kernel,
        grid=(n // bb,),
        in_specs=[row(M_WIDTH), row(M_WIDTH), row(M_WIDTH), row(LANES), row(G_KW), row(G_KW), row(G_VW),
                  row(G_KW), st4(M_DH, M_DH), row(M_WIDTH), row(M_HEADS), st4(G_DK, G_DV)],
        out_specs=[row(M_WIDTH), row(G_VW), st4(M_DH, M_DH), row(M_WIDTH), row(M_HEADS), st4(G_DK, G_DV)],
        out_shape=[jax.ShapeDtypeStruct((n, M_WIDTH), F32), jax.ShapeDtypeStruct((n, G_VW), F32),
                   jax.ShapeDtypeStruct(c0.shape, F32), jax.ShapeDtypeStruct((n, M_WIDTH), F32),
                   jax.ShapeDtypeStruct((n, M_HEADS), F32), jax.ShapeDtypeStruct(s0.shape, F32)],
        compiler_params=pltpu.CompilerParams(dimension_semantics=("arbitrary",), vmem_limit_bytes=VMEM_LIMIT),
        name="mix_tok",
    )(p["mq"], p["mk"], p["mv"], p["gates"], p["gq"], p["gk"], p["gv"], p["la"], c0, n0, m0, s0)


_IN_OFFS = tuple(sum(IN_SIZES[:i]) for i in range(len(IN_SIZES) + 1))
_GATES_LO, _GATES_HI, _GA_LO = _IN_OFFS[3], _IN_OFFS[5], _IN_OFFS[9]
_N_SMALL = 2 * M_HEADS + G_RANK
_W_BIG_COLS = _GATES_LO + (_GA_LO - _GATES_HI)
_CAST_STEPS = 8


def _cast_kernel(wout_ref, w1_ref, w2_ref, wple_ref, wpg_ref, o_out, o_w1, o_w2, o_ple, o_pg):
    o_out[...] = wout_ref[...].astype(BF16)
    o_w1[...] = w1_ref[...].astype(BF16)
    o_w2[...] = w2_ref[...].astype(BF16)
    o_ple[...] = wple_ref[...].astype(BF16)
    o_pg[...] = wpg_ref[...].astype(BF16)


def _cast_call(*srcs):
    rows = lambda a: pl.BlockSpec((a.shape[0] // _CAST_STEPS, a.shape[1]), lambda i: (i, 0))
    return pl.pallas_call(
        _cast_kernel,
        grid=(_CAST_STEPS,),
        in_specs=[rows(a) for a in srcs],
        out_specs=[rows(a) for a in srcs],
        out_shape=[jax.ShapeDtypeStruct(a.shape, BF16) for a in srcs],
        compiler_params=pltpu.CompilerParams(dimension_semantics=("arbitrary",), vmem_limit_bytes=VMEM_LIMIT),
        name="cast_weights",
    )(*srcs)


def _cast_in_kernel(wt_ref, big_ref, small_ref):
    piece = M_WIDTH
    for n in range(_W_BIG_COLS // piece):
        dst = n * piece
        src = dst if dst < _GATES_LO else dst + (_GATES_HI - _GATES_LO)
        big_ref[:, dst:dst + piece] = wt_ref[src:src + piece, :].T.astype(BF16)
    small = jnp.concatenate([wt_ref[_GATES_LO:_GATES_HI, :], wt_ref[_GA_LO:, :],
                             jnp.zeros((LANES - _N_SMALL, D_MODEL), F32)], axis=0)
    small_ref[...] = small.T.astype(BF16)


def _cast_in_call(w_in_t):
    whole = pl.BlockSpec(memory_space=pltpu.VMEM)
    assert _GATES_LO % M_WIDTH == 0
    return pl.pallas_call(
        _cast_in_kernel,
        in_specs=[whole], out_specs=[whole, whole],
        out_shape=[jax.ShapeDtypeStruct((D_MODEL, _W_BIG_COLS), BF16),
                   jax.ShapeDtypeStruct((D_MODEL, LANES), BF16)],
        compiler_params=pltpu.CompilerParams(vmem_limit_bytes=VMEM_LIMIT),
        name="cast_w_in",
    )(w_in_t)


def _prep_weights(w_in, conv_w, conv_b, b_gate, w_a2, b_a, g_mhead, g_ghead, w_out, g_mix, g_mlp, w1, w2,
                  g_ple, w_ple, w_pg, g_final):
    w_big, w_small = _cast_in_call(jnp.swapaxes(w_in, 0, 1))
    w_out_b, w1_b, w2_b, w_ple_b, w_pg_b = _cast_call(w_out, w1, w2, w_ple, w_pg)
    b_small = jnp.concatenate([b_gate, jnp.zeros((LANES - 2 * M_HEADS,), F32)])[None]
    w_a2p = jnp.concatenate([jnp.zeros((2 * M_HEADS, G_KW), F32), w_a2,
                             jnp.zeros((LANES - _N_SMALL, G_KW), F32)], axis=0).astype(BF16)
    return dict(
        w_big=w_big, w_small=w_small, b_small=b_small, w_a2p=w_a2p, b_a=b_a[None],
        conv_w=conv_w, conv_b=conv_b[None], g_mix=g_mix[None], g_mhead=g_mhead[None], g_ghead=g_ghead[None],
        w_out=w_out_b, g_mlp=g_mlp[None], w1=w1_b, w2=w2_b,
        g_ple=g_ple[None], w_ple=w_ple_b, w_pg=w_pg_b, g_final=g_final[None])


def _mix_constants(ts):
    t = np.arange(ts)
    tril = t[None, :] <= t[:, None]
    tri = (t[:, None] // CHUNK == t[None, :] // CHUNK) & tril
    rr = np.arange(G_KW)
    wred = ((rr[None, :, None] // G_DK == rr[None, None, :] // G_DK)
            & (rr[None, None, :] % SUB == np.arange(SUB)[:, None, None]))
    return tuple(jnp.asarray(np.asarray(a, dtype=BF16)) for a in (tril, tri, wred))


def kernel(x_prompt, x_sample, p_prompt, p_sample, state_mlstm_C, state_mlstm_n, state_mlstm_m, state_conv,
           state_gla_S, w_in, conv_w, conv_b, b_gate, w_a2, b_a, g_mhead, g_ghead, w_out, g_mix, g_mlp, w1,
           w2, g_ple, w_ple, w_pg, g_final):
    assert w_in.shape[0] == 1, "single-layer trunk"
    n_seq, seq_len, _ = x_prompt.shape
    n_tok = x_sample.shape[0]
    assert x_sample.shape[1] == 1 and seq_len % SEQ_BLOCK == 0 and n_tok % TOK_BATCH == 0
    wts = _prep_weights(w_in[0], conv_w[0], conv_b[0], b_gate[0], w_a2[0], b_a[0], g_mhead[0], g_ghead[0],
                        w_out[0], g_mix[0], g_mlp[0], w1[0], w2[0], g_ple[0], w_ple[0], w_pg[0], g_final)

    y_p, tail_p, caug_p, m_p, s_p = _seq_call(
        x_prompt.reshape(n_seq * seq_len, D_MODEL), p_prompt[0].reshape(n_seq * seq_len, D_PLE), wts,
        _mix_constants(SEQ_BLOCK), n_seq=n_seq, seq_len=seq_len)

    xs = x_sample.reshape(n_tok, D_MODEL)
    buf = state_conv[0]
    outs = _proj_tok_call(xs, (buf[:, 0], buf[:, 1], buf[:, 2]), wts)
    ps = dict(zip(_PROJ_NAMES, outs[:-1]))
    raw_s = outs[-1]
    hm_s, hg_s, c_s, n_s, m_s, s_s = _mix_tok_call(
        ps, state_mlstm_C[0], state_mlstm_n[0].reshape(n_tok, M_WIDTH), state_mlstm_m[0], state_gla_S[0])
    y_s = _post_tok_call(xs, p_sample[0].reshape(n_tok, D_PLE), hm_s, hg_s, ps["mo"], ps["gr"], wts)

    return (y_p.reshape(n_seq, seq_len, D_MODEL),
            y_s.reshape(n_tok, 1, D_MODEL),
            caug_p[None, :, :, :, :M_DH],
            caug_p[None, :, :, :, M_DH],
            m_p[None, :, :M_HEADS, 0],
            tail_p[None, :, SUBLANES - (CONV_W - 1):, :],
            s_p[None],
            c_s[None],
            n_s.reshape(1, n_tok, M_HEADS, M_DH),
            m_s[None],
            jnp.stack([buf[:, 1], buf[:, 2], raw_s], axis=1)[None],
            s_s[None])
```

```python
import functools

import jax
import jax.numpy as jnp
import numpy as np
from jax import lax
from jax.experimental import pallas as pl
from jax.experimental.pallas import tpu as pltpu

D_MODEL = 1024
M_HEADS = 4
M_DH = 128
M_WIDTH = M_HEADS * M_DH
G_HEADS = 4
G_DK = 64
G_DV = 128
G_KW = G_HEADS * G_DK
G_VW = G_HEADS * G_DV
G_RANK = 16
G_TAU = 16.0
CONV_W = 4
QK_CONV = 2 * M_WIDTH
D_FF = 4 * D_MODEL
D_PLE = 256
CHUNK = 64
SUB = 16
EPS = 1e-6
IN_SIZES = (QK_CONV, M_WIDTH, M_WIDTH, M_HEADS, M_HEADS, G_KW, G_KW, G_VW, G_VW, G_RANK)

LANES = 128
SUBLANES = 8
VMEM_LIMIT = 60 * 1024 * 1024
SEQ_BLOCK = 256
TOK_BATCH = 16

F32 = jnp.float32
BF16 = jnp.bfloat16
NEG = -1e30


def _rms(x, g):
    return x * lax.rsqrt(jnp.mean(x * x, axis=-1, keepdims=True) + EPS) * g


def _log_sigmoid(x):
    return jnp.minimum(x, 0.0) - jnp.log(1.0 + jnp.exp(-jnp.abs(x)))


def _sigmoid(x):
    return 0.5 * jnp.tanh(0.5 * x) + 0.5


def _div_pow2(idx, d):
    assert d & (d - 1) == 0
    return lax.shift_right_logical(idx, d.bit_length() - 1)


def _dot(a, b):
    return jnp.dot(a, b, preferred_element_type=F32)


def _dot_nt(a, b):
    return lax.dot_general(a, b, (((1,), (1,)), ((), ())), preferred_element_type=F32)


def _split3(x):
    hi = x.astype(BF16)
    r1 = x - hi.astype(F32)
    mid = r1.astype(BF16)
    lo = (r1 - mid.astype(F32)).astype(BF16)
    return hi, mid, lo


def _cumsum_dot(tri, x):
    hi, mid, lo = _split3(x)
    return _dot(tri, hi) + _dot(tri, mid) + _dot(tri, lo)


def _spread_dot(x, onehot):
    hi, mid, lo = _split3(x)
    return _dot(hi, onehot) + _dot(mid, onehot) + _dot(lo, onehot)


def _interleave(pattern, **streams):
    by_letter = {name[0]: gen for name, gen in streams.items()}
    for letter in pattern:
        next(by_letter[letter], None)
    for gen in by_letter.values():
        _run(gen)


def _run(gen):
    for _ in gen:
        pass


def _proj_stages(h, w, sinks, conv_piece):
    width = M_WIDTH

    def put_q(raw):
        sinks["mq"][...] = conv_piece(raw, 0).astype(sinks["mq"].dtype)

    def put_k(raw):
        sinks["mk"][...] = conv_piece(raw, M_WIDTH) * (M_DH ** -0.5)

    def put_mv(raw):
        sinks["mv"][...] = raw.astype(sinks["mv"].dtype)

    def put_mo(raw):
        sinks["mo"][...] = raw

    def put_gqk(raw):
        sinks["gq"][...] = raw[:, :G_KW] * (G_DK ** -0.5)
        sinks["gk"][...] = raw[:, G_KW:]

    def put_gv(raw):
        sinks["gv"][...] = raw.astype(sinks["gv"].dtype)

    def put_gr(raw):
        sinks["gr"][...] = raw

    def put_small(small):
        g = small + w["b_small"][...]
        lane = lax.broadcasted_iota(jnp.int32, g.shape, 1)
        sinks["gates"][...] = jnp.where(lane < M_HEADS, g, _log_sigmoid(g))
        z = _dot(small.astype(BF16), w["w_a2p"][...]) + w["b_a"][...]
        sinks["la"][...] = _log_sigmoid(z) * (1.0 / G_TAU)

    epilogues = [put_small, put_q, put_k, put_mv, put_mo, put_gqk, put_gv, put_gr]
    assert w["w_big"].shape[1] == width * (len(epilogues) - 1)
    pending = (_dot(h, w["w_small"][...]), epilogues[0])
    yield
    for n, epi in enumerate(epilogues[1:]):
        cur = _dot(h, w["w_big"][:, n * width:(n + 1) * width])
        pending[1](pending[0])
        pending = (cur, epi)
        yield
    pending[1](pending[0])
    yield


def _seq_conv(cbuf_ref, tail_ref, convw_ref, convb_ref, tb):
    cw = convw_ref[...]

    def conv_piece(raw, lo):
        cols = slice(lo, lo + raw.shape[1])
        cbuf_ref[SUBLANES:SUBLANES + tb, cols] = raw
        y = convb_ref[:, cols] + raw * cw[3:4, cols]
        for j in range(CONV_W - 1):
            off = SUBLANES - (CONV_W - 1) + j
            y = y + cbuf_ref[off:off + tb, cols] * cw[j:j + 1, cols]
        last = cbuf_ref[tb:tb + SUBLANES, cols]
        tail_ref[0, :, cols] = last
        cbuf_ref[0:SUBLANES, cols] = last
        return y * _sigmoid(y)

    return conv_piece


def _tok_conv(b0_ref, b1_ref, b2_ref, raw_ref, convw_ref, convb_ref):
    cw = convw_ref[...]

    def conv_piece(raw, lo):
        cols = slice(lo, lo + raw.shape[1])
        raw_ref[:, cols] = raw
        y = (convb_ref[:, cols] + b0_ref[:, cols] * cw[0:1, cols] + b1_ref[:, cols] * cw[1:2, cols]
             + b2_ref[:, cols] * cw[2:3, cols] + raw * cw[3:4, cols])
        return y * _sigmoid(y)

    return conv_piece


_PROJ_NAMES = ("mq", "mk", "mv", "mo", "gq", "gk", "gv", "gr", "gates", "la")
_PROJ_WIDTH = dict(mq=M_WIDTH, mk=M_WIDTH, mv=M_WIDTH, mo=M_WIDTH, gq=G_KW, gk=G_KW, gv=G_VW, gr=G_VW,
                   gates=LANES, la=G_KW)
_PROJ_DTYPE = dict(mq=BF16, mk=F32, mv=BF16, mo=F32, gq=F32, gk=F32, gv=BF16, gr=F32, gates=F32, la=F32)
_PROJ_WEIGHTS = ("w_big", "w_small", "b_small", "w_a2p", "b_a")


def _proj_tok_kernel(x_ref, b0_ref, b1_ref, b2_ref, gmix_ref, wbig_ref, wsmall_ref, bsmall_ref, wa2_ref,
                     ba_ref, convw_ref, convb_ref, *out_refs):
    sinks = dict(zip(_PROJ_NAMES, out_refs[:-1]))
    raw_ref = out_refs[-1]
    w = dict(zip(_PROJ_WEIGHTS, (wbig_ref, wsmall_ref, bsmall_ref, wa2_ref, ba_ref)))
    h = _rms(x_ref[...], gmix_ref[...]).astype(BF16)
    _run(_proj_stages(h, w, sinks, _tok_conv(b0_ref, b1_ref, b2_ref, raw_ref, convw_ref, convb_ref)))


def _proj_tok_call(x2d, conv_rows, wts):
    n = x2d.shape[0]
    row = lambda width: pl.BlockSpec((n, width), lambda i: (0, 0))
    whole = pl.BlockSpec(memory_space=pltpu.VMEM)
    names = list(_PROJ_NAMES)
    return pl.pallas_call(
        _proj_tok_kernel,
        grid=(1,),
        in_specs=[row(D_MODEL)] + [row(QK_CONV)] * 3 + [whole] * 8,
        out_specs=[row(_PROJ_WIDTH[k]) for k in names] + [row(QK_CONV)],
        out_shape=[jax.ShapeDtypeStruct((n, _PROJ_WIDTH[k]), _PROJ_DTYPE[k]) for k in names]
        + [jax.ShapeDtypeStruct((n, QK_CONV), F32)],
        compiler_params=pltpu.CompilerParams(dimension_semantics=("arbitrary",), vmem_limit_bytes=VMEM_LIMIT),
        name="proj_tok",
    )(x2d, *conv_rows, wts["g_mix"], wts["w_big"], wts["w_small"], wts["b_small"], wts["w_a2p"], wts["b_a"],
      wts["conv_w"], wts["conv_b"])


def _mix_gate_sums(src, tril_ref, tri_ref):
    gates = src["gates"][...]
    bcum = _cumsum_dot(tril_ref[...], gates)
    bc = _cumsum_dot(tri_ref[...], src["la"][...])
    return gates, bcum, bc, gates.T, bcum.T, bc.T


def _mix_stages(gate_sums, src, wred_ref, caug_ref, m_ref, sbd_ref, hm_ref, hg_ref):
    ts = src["mq"].shape[0]
    n_chunks = ts // CHUNK
    gates, bcum, bc, gates_t, bcum_t, bc_t = gate_sums
    gq = src["gq"][...]
    gk = src["gk"][...]
    gv_ref = src["gv"]

    nb = ts // SUB
    half = SUB // 2
    assert half == SUBLANES
    q3 = gq.reshape(nb, SUB, G_KW)
    k3 = gk.reshape(nb, SUB, G_KW)
    bc3 = bc.reshape(nb, SUB, G_KW)
    tl = lax.broadcasted_iota(jnp.int32, (nb, SUB, G_KW), 1)
    q3u = gq.reshape(nb, 2, half, G_KW)[:, 1]
    bc3u = bc.reshape(nb, 2, half, G_KW)[:, 1]
    tlu = lax.broadcasted_iota(jnp.int32, (nb, half, G_KW), 1) + half

    def exact_pass(j, acc):
        acc_all, acc_upper = acc
        if j < half:
            arg = jnp.where(tl >= j, bc3 - bc3[:, j:j + 1, :], NEG)
            e = (q3 * k3[:, j:j + 1, :] * jnp.exp(arg)).reshape(ts, G_KW)
            return acc_all + _dot(e.astype(BF16), wred_ref[j]), acc_upper
        arg = jnp.where(tlu >= j, bc3u - bc3[:, j:j + 1, :], NEG)
        e = (q3u * k3[:, j:j + 1, :] * jnp.exp(arg)).reshape(ts // 2, G_KW)
        return acc_all, acc_upper + _dot(e.astype(BF16), wred_ref[j])

    krow = lax.broadcasted_iota(jnp.int32, (CHUNK, G_KW), 0)
    same_head_kk = (_div_pow2(lax.broadcasted_iota(jnp.int32, (G_KW, G_KW), 0), G_DK)
                    == _div_pow2(lax.broadcasted_iota(jnp.int32, (G_KW, G_KW), 1), G_DK))
    same_head_kv = (_div_pow2(lax.broadcasted_iota(jnp.int32, (G_KW, G_VW), 0), G_DK)
                    == _div_pow2(lax.broadcasted_iota(jnp.int32, (G_KW, G_VW), 1), G_DV))

    def cross_block_scores(c):
        lo = c * CHUNK
        bc_c = bc[lo:lo + CHUNK]
        k_c = gk[lo:lo + CHUNK]
        out = [jnp.zeros((SUB, G_KW), F32)]
        for i in range(1, CHUNK // SUB):
            r0 = lo + i * SUB
            r_i = bc[r0 - 1:r0, :]
            qi = (gq[r0:r0 + SUB] * jnp.exp(bc[r0:r0 + SUB] - r_i)).astype(BF16)
            ki = (k_c * jnp.exp(jnp.where(krow < i * SUB, r_i - bc_c, NEG))).astype(BF16)
            kbd = jnp.where(same_head_kk, jnp.concatenate([ki] * G_HEADS, axis=0), jnp.zeros((), BF16))
            out.append(_dot_nt(qi, kbd))
        return out

    def state_update_term(c):
        lo, hi = c * CHUNK, (c + 1) * CHUNK
        b_end = bc[hi - 1:hi, :]
        k_out_t = (gk[lo:hi] * jnp.exp(b_end - bc[lo:hi])).T.astype(BF16)
        rows = []
        for g in range(G_HEADS):
            blk = _dot(k_out_t[g * G_DK:(g + 1) * G_DK], gv_ref[lo:hi, g * G_DV:(g + 1) * G_DV])
            zero = jnp.zeros((G_DK, G_DV), F32)
            rows.append(jnp.concatenate([zero] * g + [blk] + [zero] * (G_HEADS - 1 - g), axis=1))
        return jnp.concatenate(rows, axis=0)

    adiag = (jnp.zeros((ts, G_KW), F32), jnp.zeros((ts // 2, G_KW), F32))
    per_stage = SUB // 4
    assert n_chunks == 4
    offs = []
    s_terms = []

    causal = (lax.broadcasted_iota(jnp.int32, (ts, ts), 0) >= lax.broadcasted_iota(jnp.int32, (ts, ts), 1))
    ones_col = (lax.broadcasted_iota(jnp.int32, (ts, LANES), 1) == 0).astype(BF16)
    heads = range(M_HEADS)
    hsl = [slice(hd * M_DH, (hd + 1) * M_DH) for hd in heads]
    b_col = [bcum[:, M_HEADS + hd:M_HEADS + hd + 1] for hd in heads]
    b_row = [bcum_t[M_HEADS + hd:M_HEADS + hd + 1, :] for hd in heads]
    i_row = [gates_t[hd:hd + 1, :] for hd in heads]
    m_prev = [m_ref[hd:hd + 1, 0:1] for hd in heads]
    q = [src["mq"][:, hsl[hd]] for hd in heads]
    i_col = [gates[:, hd:hd + 1] for hd in heads]
    k = [src["mk"][:, hsl[hd]] for hd in heads]
    vaug = [jnp.concatenate([src["mv"][:, hsl[hd]], ones_col], axis=1) for hd in heads]
    caug = [caug_ref[hd] for hd in heads]
    s_qk = [_dot_nt(q[hd], k[hd].astype(BF16)) for hd in heads]
    qc = [_dot(q[hd], caug[hd].astype(BF16)) for hd in heads]
    for j in range(0, per_stage):
        adiag = exact_pass(j, adiag)
    offs += cross_block_scores(0)
    s_terms.append(state_update_term(0))
    yield
    b_last = [b_col[hd][ts - 1:ts, :] for hd in heads]
    dec = [b_last[hd] - b_col[hd] + i_col[hd] for hd in heads]
    m_new = [jnp.maximum(b_last[hd] + m_prev[hd], jnp.max(dec[hd], axis=0, keepdims=True)) for hd in heads]
    kw_t = [(k[hd] * jnp.exp(dec[hd] - m_new[hd])).T.astype(BF16) for hd in heads]
    upd = [_dot(kw_t[hd], vaug[hd]) for hd in heads]
    for j in range(per_stage, 2 * per_stage):
        adiag = exact_pass(j, adiag)
    offs += cross_block_scores(1)
    s_terms.append(state_update_term(1))
    yield
    dmat = [jnp.where(causal, b_col[hd] - b_row[hd] + i_row[hd], -jnp.inf) for hd in heads]
    inter = [b_col[hd] + m_prev[hd] for hd in heads]
    m_tok = [jnp.maximum(inter[hd], jnp.max(dmat[hd], axis=1, keepdims=True)) for hd in heads]
    for j in range(2 * per_stage, 3 * per_stage):
        adiag = exact_pass(j, adiag)
    offs += cross_block_scores(2)
    s_terms.append(state_update_term(2))
    yield
    p = [(s_qk[hd] * jnp.exp(dmat[hd] - m_tok[hd])).astype(BF16) for hd in heads]
    pv = [_dot(p[hd], vaug[hd]) for hd in heads]
    for j in range(3 * per_stage, SUB):
        adiag = exact_pass(j, adiag)
    offs += cross_block_scores(3)
    s_terms.append(state_update_term(3))
    yield
    for hd in heads:
        tot = jnp.exp(inter[hd] - m_tok[hd]) * qc[hd] + pv[hd]
        den = tot[:, M_DH:M_DH + 1]
        hm_ref[:, hsl[hd]] = tot[:, :M_DH] / jnp.maximum(jnp.abs(den), jnp.exp(-m_tok[hd]))
        caug_ref[hd] = jnp.exp(b_last[hd] + m_prev[hd] - m_new[hd]) * caug[hd] + upd[hd]
        m_ref[hd:hd + 1, :] = jnp.broadcast_to(m_new[hd], (1, LANES))
    yield

    sub_of = lambda idx: lax.shift_right_logical(idx & (CHUNK - 1), SUB.bit_length() - 1)
    rowb = sub_of(lax.broadcasted_iota(jnp.int32, (ts, G_KW), 0))
    colb = sub_of(lax.broadcasted_iota(jnp.int32, (ts, G_KW), 1))
    acc_all, acc_upper = adiag
    upper = acc_upper.reshape(nb, half, G_KW)
    acc_all = acc_all + jnp.stack([jnp.zeros_like(upper), upper], axis=1).reshape(ts, G_KW)
    adiag = jnp.where(rowb == colb, acc_all, 0.0)
    intra = (jnp.concatenate(offs, axis=0) + adiag).astype(BF16)
    q_in = (gq * jnp.exp(bc)).astype(BF16)
    o_intra = []
    for c in range(n_chunks):
        lo, hi = c * CHUNK, (c + 1) * CHUNK
        vbd = jnp.where(same_head_kv, jnp.concatenate([gv_ref[lo:hi, :]] * G_HEADS, axis=0),
                        jnp.zeros((), BF16))
        o_intra.append(_dot(intra[lo:hi], vbd))
    sbd = sbd_ref[...]
    for c in range(n_chunks):
        lo, hi = c * CHUNK, (c + 1) * CHUNK
        hg_ref[lo:hi, :] = o_intra[c] + _dot(q_in[lo:hi], sbd.astype(BF16))
        dcol = jnp.exp(bc_t[:, hi - 1:hi])
        sbd = dcol * sbd + s_terms[c]
    sbd_ref[...] = sbd
    yield


def _head_norm(hv, n_heads, width):
    parts = []
    for hd in range(n_heads):
        seg = hv[:, hd * width:(hd + 1) * width]
        parts.append(seg * lax.rsqrt(jnp.mean(seg * seg, axis=-1, keepdims=True) + EPS))
    return jnp.concatenate(parts, axis=1)


def _gate_heads(hm, hg, mo, gr, gmh, ggh):
    hm = _head_norm(hm, M_HEADS, M_DH) * gmh * _sigmoid(mo)
    hg = _head_norm(hg, G_HEADS, G_DV) * ggh * (gr * _sigmoid(gr))
    return jnp.concatenate([hm, hg], axis=1).astype(BF16)


_POST_WEIGHTS = ("w_out", "g_mlp", "w1", "w2", "g_ple", "w_ple", "w_pg", "g_final")


def _post_stages(x, p, mixed, w, y_ref):
    x1 = x + _dot(mixed, w["w_out"][...])
    n1 = _rms(x1, w["g_mlp"][...]).astype(BF16)
    yield
    n_ff = 4
    ff = D_FF // n_ff
    acts = []
    for j in range(n_ff):
        u = _dot(n1, w["w1"][:, j * ff:(j + 1) * ff])
        acts.append(jnp.square(jnp.maximum(u, 0.0)).astype(BF16))
        yield
    act = jnp.concatenate(acts, axis=1)
    half = D_MODEL // 2
    mlp = []
    for j in range(2):
        mlp.append(_dot(act, w["w2"][:, j * half:(j + 1) * half]))
        yield
    x2 = x1 + jnp.concatenate(mlp, axis=1)
    ple = _dot(p.astype(BF16), w["w_ple"][...])
    n2 = _rms(x2, w["g_ple"][...]).astype(BF16)
    yield
    gate = _sigmoid(_dot(n2, w["w_pg"][...]))
    x3 = x2 + ple * gate
    y_ref[...] = _rms(x3, w["g_final"][...])
    yield


def _seq_kernel(xa_ref, xb_ref, pb_ref, xs_ref, ps_ref, ms_ref, gmix_ref, wbig_ref, wsmall_ref, bsmall_ref,
                wa2_ref, ba_ref, convw_ref, convb_ref, tril_ref, tri_ref, wred_ref, gmh_ref, ggh_ref,
                wout_ref, gmlp_ref, w1_ref, w2_ref, gple_ref, wple_ref, wpg_ref, gfin_ref,
                y_ref, ys_ref, tail_ref, cout_ref, mout_ref, sout_ref,
                cbuf_ref, caug_ref, m_ref, sbd_ref, mixed_ref,
                mq_s, mk_s, mv_s, mo_s, gq_s, gk_s, gv_s, gr_s, gates_s, la_s, hm_s, hg_s,
                *, steps_per_seq, n_blocks):
    tb = xa_ref.shape[0]
    s = pl.program_id(0)
    r = lax.rem(jnp.minimum(s, n_blocks - 1), steps_per_seq)

    @pl.when(s == 0)
    def _():
        mixed_ref[1] = ms_ref[...]

    @pl.when(r == 0)
    def _():
        cbuf_ref[0:SUBLANES, :] = jnp.zeros((SUBLANES, QK_CONV), F32)
        caug_ref[...] = jnp.zeros(caug_ref.shape, F32)
        m_ref[...] = jnp.zeros(m_ref.shape, F32)
        sbd_ref[...] = jnp.zeros(sbd_ref.shape, F32)

    slot = lax.rem(s, 2)
    src = dict(zip(_PROJ_NAMES, (mq_s, mk_s, mv_s, mo_s, gq_s, gk_s, gv_s, gr_s, gates_s, la_s)))
    w_in = dict(zip(_PROJ_WEIGHTS, (wbig_ref, wsmall_ref, bsmall_ref, wa2_ref, ba_ref)))
    w_post = dict(zip(_POST_WEIGHTS, (wout_ref, gmlp_ref, w1_ref, w2_ref, gple_ref, wple_ref, wpg_ref, gfin_ref)))

    def front():
        h = _rms(xa_ref[...], gmix_ref[...]).astype(BF16)
        yield from _proj_stages(h, w_in, src, _seq_conv(cbuf_ref, tail_ref, convw_ref, convb_ref, tb))
        gate_sums = _mix_gate_sums(src, tril_ref, tri_ref)
        yield
        yield from _mix_stages(gate_sums, src, wred_ref, caug_ref, m_ref, sbd_ref, hm_s, hg_s)
        mixed_ref[slot] = _gate_heads(hm_s[...], hg_s[...], mo_s[...], gr_s[...], gmh_ref[...], ggh_ref[...])
        yield

    def back():
        first = s == 0
        x = jnp.where(first, xs_ref[...], xb_ref[...])
        p = jnp.where(first, ps_ref[...], pb_ref[...])
        yield from _post_stages(x, p, mixed_ref[1 - slot], w_post, y_ref)

    _interleave("bfff" "bfff" "bfff" "bff" "bf" "bf" "bf" "bf" "bff", front=front(), back=back())

    @pl.when(s == 0)
    def _():
        ys_ref[...] = y_ref[...]

    @pl.when(jnp.logical_and(s < n_blocks, r == steps_per_seq - 1))
    def _():
        cout_ref[0] = caug_ref[...]
        mout_ref[0] = m_ref[...]
        for g in range(G_HEADS):
            sout_ref[0, g] = sbd_ref[g * G_DK:(g + 1) * G_DK, g * G_DV:(g + 1) * G_DV]


def _seq_call(x2d, p2d, sample, wts, consts, *, n_seq, seq_len):
    tb = SEQ_BLOCK
    steps_per_seq = seq_len // tb
    n_blocks = n_seq * steps_per_seq
    tril, tri, wred = consts
    front_blk = lambda s: jnp.minimum(s, n_blocks - 1)
    back_blk = lambda s: jnp.maximum(s - 1, 0)
    whole = pl.BlockSpec(memory_space=pltpu.VMEM)
    per_seq = lambda shape: pl.BlockSpec((1,) + shape, lambda s: (front_blk(s) // steps_per_seq,) + (0,) * len(shape))
    scratch = [
        pltpu.VMEM((tb + SUBLANES, QK_CONV), F32),
        pltpu.VMEM((M_HEADS, M_DH, 2 * M_DH), F32),
        pltpu.VMEM((SUBLANES, LANES), F32),
        pltpu.VMEM((G_KW, G_VW), F32),
        pltpu.VMEM((2, tb, D_MODEL), BF16),
    ] + [pltpu.VMEM((tb, _PROJ_WIDTH[k]), _PROJ_DTYPE[k]) for k in _PROJ_NAMES] + [
        pltpu.VMEM((tb, M_WIDTH), F32), pltpu.VMEM((tb, G_VW), F32)]
    return pl.pallas_call(
        functools.partial(_seq_kernel, steps_per_seq=steps_per_seq, n_blocks=n_blocks),
        grid=(n_blocks + 1,),
        in_specs=[pl.BlockSpec((tb, D_MODEL), lambda s: (front_blk(s), 0)),
                  pl.BlockSpec((tb, D_MODEL), lambda s: (back_blk(s), 0)),
                  pl.BlockSpec((tb, D_PLE), lambda s: (back_blk(s), 0))] + [whole] * 24,
        out_specs=[pl.BlockSpec((tb, D_MODEL), lambda s: (back_blk(s), 0)),
                   pl.BlockSpec((tb, D_MODEL), lambda s: (0, 0)),
                   per_seq((SUBLANES, QK_CONV)), per_seq((M_HEADS, M_DH, 2 * M_DH)),
                   per_seq((SUBLANES, LANES)), per_seq((G_HEADS, G_DK, G_DV))],
        out_shape=[jax.ShapeDtypeStruct((n_blocks * tb, D_MODEL), F32),
                   jax.ShapeDtypeStruct((tb, D_MODEL), F32),
                   jax.ShapeDtypeStruct((n_seq, SUBLANES, QK_CONV), F32),
                   jax.ShapeDtypeStruct((n_seq, M_HEADS, M_DH, 2 * M_DH), F32),
                   jax.ShapeDtypeStruct((n_seq, SUBLANES, LANES), F32),
                   jax.ShapeDtypeStruct((n_seq, G_HEADS, G_DK, G_DV), F32)],
        scratch_shapes=scratch,
        compiler_params=pltpu.CompilerParams(dimension_semantics=("arbitrary",), vmem_limit_bytes=VMEM_LIMIT),
        name="seq_fused",
    )(x2d, x2d, p2d, *sample, wts["g_mix"], wts["w_big"], wts["w_small"], wts["b_small"], wts["w_a2p"], wts["b_a"],
      wts["conv_w"], wts["conv_b"], tril, tri, wred, wts["g_mhead"], wts["g_ghead"],
      *[wts[k] for k in _POST_WEIGHTS])


def _mix_tok_kernel(mq_ref, mk_ref, mv_ref, gates_ref, gq_ref, gk_ref, gv_ref, la_ref, mo_ref, gr_ref,
                    gmh_ref, ggh_ref, c_ref, n_ref, m_ref, s_ref,
                    mixed_ref, cn_ref, nn_ref, mn_ref, sn_ref,
                    hm_ref, hg_ref):
    bb = mq_ref.shape[0]
    wide = bb * LANES
    diag = (lax.broadcasted_iota(jnp.int32, (bb, wide), 0)
            == _div_pow2(lax.broadcasted_iota(jnp.int32, (bb, wide), 1), LANES))
    block_ones = diag.astype(BF16)

    def block_diag(rows):
        return jnp.where(diag, jnp.concatenate([rows] * bb, axis=1), jnp.zeros((), rows.dtype))

    gates = gates_ref[...]
    m_all = m_ref[...]
    for hd in range(M_HEADS):
        cs_, ce_ = hd * M_DH, (hd + 1) * M_DH
        q_bf = mq_ref[:, cs_:ce_]
        q = q_bf.astype(F32)
        k = mk_ref[:, cs_:ce_]
        v_bf = mv_ref[:, cs_:ce_]
        v = v_bf.astype(F32)
        n_prev = n_ref[:, cs_:ce_]
        ig = gates[:, hd:hd + 1]
        lf = gates[:, M_HEADS + hd:M_HEADS + hd + 1]
        m_prev = m_all[:, hd:hd + 1]
        m_new = jnp.maximum(lf + m_prev, ig)
        scale = jnp.exp(lf + m_prev - m_new)
        wk = jnp.exp(ig - m_new)
        s_qk = jnp.sum(q * k, axis=1, keepdims=True) * wk
        den = scale * jnp.sum(q * n_prev, axis=1, keepdims=True) + s_qk
        inv = 1.0 / jnp.maximum(jnp.abs(den), jnp.exp(-m_new))
        nn_ref[:, cs_:ce_] = scale * n_prev + wk * k
        mn_ref[:, hd:hd + 1] = m_new
        q_w = _dot(q.T.astype(BF16), block_ones)
        outer = _dot((k * wk).T.astype(BF16), block_diag(v_bf))
        num_w = s_qk * v
        for b in range(bb):
            blk = slice(b * M_DH, (b + 1) * M_DH)
            c_prev = c_ref[b, hd]
            qc = jnp.sum(q_w[:, blk] * c_prev, axis=0, keepdims=True)
            hm_ref[b:b + 1, cs_:ce_] = (scale[b:b + 1] * qc + num_w[b:b + 1]) * inv[b:b + 1]
            cn_ref[b, hd] = scale[b:b + 1] * c_prev + outer[:, blk]
    la = la_ref[...]
    dec = jnp.exp(la)
    gq = gq_ref[...]
    gk = gk_ref[...]
    for g in range(G_HEADS):
        ks_, ke_ = g * G_DK, (g + 1) * G_DK
        vs_, ve_ = g * G_DV, (g + 1) * G_DV
        q = gq[:, ks_:ke_]
        k = gk[:, ks_:ke_]
        v_bf = gv_ref[:, vs_:ve_]
        av = jnp.sum(q * k, axis=1, keepdims=True) * v_bf.astype(F32)
        qd_w = _dot((q * dec[:, ks_:ke_]).T.astype(BF16), block_ones)
        outer = _dot(k.T.astype(BF16), block_diag(v_bf))
        dec_w = _spread_dot(dec[:, ks_:ke_].T, block_ones)
        for b in range(bb):
            blk = slice(b * G_DV, (b + 1) * G_DV)
            s_prev = s_ref[b, g]
            hg_ref[b:b + 1, vs_:ve_] = jnp.sum(qd_w[:, blk] * s_prev, axis=0, keepdims=True) + av[b:b + 1]
            sn_ref[b, g] = dec_w[:, blk] * s_prev + outer[:, blk]
    mixed_ref[...] = _gate_heads(hm_ref[...], hg_ref[...], mo_ref[...], gr_ref[...], gmh_ref[...], ggh_ref[...])


def _mix_tok_call(p, wts, c0, n0, m0, s0):
    bb = TOK_BATCH
    n = c0.shape[0]
    row = lambda w: pl.BlockSpec((bb, w), lambda i: (i, 0))
    st4 = lambda a, b_: pl.BlockSpec((bb, M_HEADS, a, b_), lambda i: (i, 0, 0, 0))
    whole = pl.BlockSpec(memory_space=pltpu.VMEM)
    return pl.pallas_call(
        _mix_tok_kernel,
        grid=(n // bb,),
        in_specs=[row(M_WIDTH), row(M_WIDTH), row(M_WIDTH), row(LANES), row(G_KW), row(G_KW), row(G_VW),
                  row(G_KW), row(M_WIDTH), row(G_VW), whole, whole,
                  st4(M_DH, M_DH), row(M_WIDTH), row(M_HEADS), st4(G_DK, G_DV)],
        out_specs=[row(D_MODEL), st4(M_DH, M_DH), row(M_WIDTH), row(M_HEADS), st4(G_DK, G_DV)],
        out_shape=[jax.ShapeDtypeStruct((n, D_MODEL), BF16),
                   jax.ShapeDtypeStruct(c0.shape, F32), jax.ShapeDtypeStruct((n, M_WIDTH), F32),
                   jax.ShapeDtypeStruct((n, M_HEADS), F32), jax.ShapeDtypeStruct(s0.shape, F32)],
        scratch_shapes=[pltpu.VMEM((bb, M_WIDTH), F32), pltpu.VMEM((bb, G_VW), F32)],
        compiler_params=pltpu.CompilerParams(dimension_semantics=("arbitrary",), vmem_limit_bytes=VMEM_LIMIT),
        name="mix_tok",
    )(p["mq"], p["mk"], p["mv"], p["gates"], p["gq"], p["gk"], p["gv"], p["la"], p["mo"], p["gr"],
      wts["g_mhead"], wts["g_ghead"], c0, n0, m0, s0)


_IN_OFFS = tuple(sum(IN_SIZES[:i]) for i in range(len(IN_SIZES) + 1))
_GATES_LO, _GATES_HI, _GA_LO = _IN_OFFS[3], _IN_OFFS[5], _IN_OFFS[9]
_N_SMALL = 2 * M_HEADS + G_RANK
_W_BIG_COLS = _GATES_LO + (_GA_LO - _GATES_HI)
_CAST_STEPS = 8


def _cast_kernel(wout_ref, w1_ref, w2_ref, wple_ref, wpg_ref, o_out, o_w1, o_w2, o_ple, o_pg):
    o_out[...] = wout_ref[...].astype(BF16)
    o_w1[...] = w1_ref[...].astype(BF16)
    o_w2[...] = w2_ref[...].astype(BF16)
    o_ple[...] = wple_ref[...].astype(BF16)
    o_pg[...] = wpg_ref[...].astype(BF16)


def _cast_call(*srcs):
    rows = lambda a: pl.BlockSpec((a.shape[0] // _CAST_STEPS, a.shape[1]), lambda i: (i, 0))
    return pl.pallas_call(
        _cast_kernel,
        grid=(_CAST_STEPS,),
        in_specs=[rows(a) for a in srcs],
        out_specs=[rows(a) for a in srcs],
        out_shape=[jax.ShapeDtypeStruct(a.shape, BF16) for a in srcs],
        compiler_params=pltpu.CompilerParams(dimension_semantics=("arbitrary",), vmem_limit_bytes=VMEM_LIMIT),
        name="cast_weights",
    )(*srcs)


def _cast_in_kernel(wt_ref, big_ref, small_ref):
    piece = M_WIDTH
    for n in range(_W_BIG_COLS // piece):
        dst = n * piece
        src = dst if dst < _GATES_LO else dst + (_GATES_HI - _GATES_LO)
        big_ref[:, dst:dst + piece] = wt_ref[src:src + piece, :].T.astype(BF16)
    small = jnp.concatenate([wt_ref[_GATES_LO:_GATES_HI, :], wt_ref[_GA_LO:, :],
                             jnp.zeros((LANES - _N_SMALL, D_MODEL), F32)], axis=0)
    small_ref[...] = small.T.astype(BF16)


def _cast_in_call(w_in_t):
    whole = pl.BlockSpec(memory_space=pltpu.VMEM)
    assert _GATES_LO % M_WIDTH == 0
    return pl.pallas_call(
        _cast_in_kernel,
        in_specs=[whole], out_specs=[whole, whole],
        out_shape=[jax.ShapeDtypeStruct((D_MODEL, _W_BIG_COLS), BF16),
                   jax.ShapeDtypeStruct((D_MODEL, LANES), BF16)],
        compiler_params=pltpu.CompilerParams(vmem_limit_bytes=VMEM_LIMIT),
        name="cast_w_in",
    )(w_in_t)


def _prep_weights(w_in, conv_w, conv_b, b_gate, w_a2, b_a, g_mhead, g_ghead, w_out, g_mix, g_mlp, w1, w2,
                  g_ple, w_ple, w_pg, g_final):
    w_big, w_small = _cast_in_call(jnp.swapaxes(w_in, 0, 1))
    w_out_b, w1_b, w2_b, w_ple_b, w_pg_b = _cast_call(w_out, w1, w2, w_ple, w_pg)
    b_small = jnp.concatenate([b_gate, jnp.zeros((LANES - 2 * M_HEADS,), F32)])[None]
    w_a2p = jnp.concatenate([jnp.zeros((2 * M_HEADS, G_KW), F32), w_a2,
                             jnp.zeros((LANES - _N_SMALL, G_KW), F32)], axis=0).astype(BF16)
    return dict(
        w_big=w_big, w_small=w_small, b_small=b_small, w_a2p=w_a2p, b_a=b_a[None],
        conv_w=conv_w, conv_b=conv_b[None], g_mix=g_mix[None], g_mhead=g_mhead[None], g_ghead=g_ghead[None],
        w_out=w_out_b, g_mlp=g_mlp[None], w1=w1_b, w2=w2_b,
        g_ple=g_ple[None], w_ple=w_ple_b, w_pg=w_pg_b, g_final=g_final[None])


def _mix_constants(ts):
    t = np.arange(ts)
    tril = t[None, :] <= t[:, None]
    tri = (t[:, None] // CHUNK == t[None, :] // CHUNK) & tril
    rr = np.arange(G_KW)
    wred = ((rr[None, :, None] // G_DK == rr[None, None, :] // G_DK)
            & (rr[None, None, :] % SUB == np.arange(SUB)[:, None, None]))
    return tuple(jnp.asarray(np.asarray(a, dtype=BF16)) for a in (tril, tri, wred))


def kernel(x_prompt, x_sample, p_prompt, p_sample, state_mlstm_C, state_mlstm_n, state_mlstm_m, state_conv,
           state_gla_S, w_in, conv_w, conv_b, b_gate, w_a2, b_a, g_mhead, g_ghead, w_out, g_mix, g_mlp, w1,
           w2, g_ple, w_ple, w_pg, g_final):
    assert w_in.shape[0] == 1, "single-layer trunk"
    n_seq, seq_len, _ = x_prompt.shape
    n_tok = x_sample.shape[0]
    assert x_sample.shape[1] == 1 and seq_len % SEQ_BLOCK == 0 and n_tok % TOK_BATCH == 0
    wts = _prep_weights(w_in[0], conv_w[0], conv_b[0], b_gate[0], w_a2[0], b_a[0], g_mhead[0], g_ghead[0],
                        w_out[0], g_mix[0], g_mlp[0], w1[0], w2[0], g_ple[0], w_ple[0], w_pg[0], g_final)

    xs = x_sample.reshape(n_tok, D_MODEL)
    buf = state_conv[0]
    outs = _proj_tok_call(xs, (buf[:, 0], buf[:, 1], buf[:, 2]), wts)
    ps = dict(zip(_PROJ_NAMES, outs[:-1]))
    raw_s = outs[-1]
    mixed_s, c_s, n_s, m_s, s_s = _mix_tok_call(
        ps, wts, state_mlstm_C[0], state_mlstm_n[0].reshape(n_tok, M_WIDTH), state_mlstm_m[0], state_gla_S[0])
    pad = lambda a: jnp.concatenate([a, jnp.zeros((SEQ_BLOCK - n_tok, a.shape[1]), a.dtype)], axis=0)
    sample = (pad(xs), pad(p_sample[0].reshape(n_tok, D_PLE)), pad(mixed_s))

    y_p, y_s, tail_p, caug_p, m_p, s_p = _seq_call(
        x_prompt.reshape(n_seq * seq_len, D_MODEL), p_prompt[0].reshape(n_seq * seq_len, D_PLE), sample, wts,
        _mix_constants(SEQ_BLOCK), n_seq=n_seq, seq_len=seq_len)

    return (y_p.reshape(n_seq, seq_len, D_MODEL),
            y_s[:n_tok].reshape(n_tok, 1, D_MODEL),
            caug_p[None, :, :, :, :M_DH],
            caug_p[None, :, :, :, M_DH],
            m_p[None, :, :M_HEADS, 0],
            tail_p[None, :, SUBLANES - (CONV_W - 1):, :],
            s_p[None],
            c_s[None],
            n_s.reshape(1, n_tok, M_HEADS, M_DH),
            m_s[None],
            jnp.stack([buf[:, 1], buf[:, 2], raw_s], axis=1)[None],
            s_s[None])
```

```python
import functools

import jax
import jax.numpy as jnp
import numpy as np
from jax import lax
from jax.experimental import pallas as pl
from jax.experimental.pallas import tpu as pltpu

D_MODEL = 1024
M_HEADS = 4
M_DH = 128
M_WIDTH = M_HEADS * M_DH
G_HEADS = 4
G_DK = 64
G_DV = 128
G_KW = G_HEADS * G_DK
G_VW = G_HEADS * G_DV
G_RANK = 16
G_TAU = 16.0
CONV_W = 4
QK_CONV = 2 * M_WIDTH
D_FF = 4 * D_MODEL
D_PLE = 256
CHUNK = 64
SUB = 16
EPS = 1e-6
IN_SIZES = (QK_CONV, M_WIDTH, M_WIDTH, M_HEADS, M_HEADS, G_KW, G_KW, G_VW, G_VW, G_RANK)

LANES = 128
SUBLANES = 8
VMEM_LIMIT = 60 * 1024 * 1024
SEQ_BLOCK = 256
TOK_BATCH = 16

F32 = jnp.float32
BF16 = jnp.bfloat16
NEG = -1e30


def _rms(x, g):
    return x * lax.rsqrt(jnp.mean(x * x, axis=-1, keepdims=True) + EPS) * g


def _log_sigmoid(x):
    return jnp.minimum(x, 0.0) - jnp.log(1.0 + jnp.exp(-jnp.abs(x)))


def _sigmoid(x):
    return 0.5 * jnp.tanh(0.5 * x) + 0.5


def _div_pow2(idx, d):
    assert d & (d - 1) == 0
    return lax.shift_right_logical(idx, d.bit_length() - 1)


def _dot(a, b):
    return jnp.dot(a, b, preferred_element_type=F32)


def _dot_nt(a, b):
    return lax.dot_general(a, b, (((1,), (1,)), ((), ())), preferred_element_type=F32)


def _split3(x):
    hi = x.astype(BF16)
    r1 = x - hi.astype(F32)
    mid = r1.astype(BF16)
    lo = (r1 - mid.astype(F32)).astype(BF16)
    return hi, mid, lo


def _cumsum_dot(tri, x):
    hi, mid, lo = _split3(x)
    return _dot(tri, hi) + _dot(tri, mid) + _dot(tri, lo)


def _spread_dot(x, onehot):
    hi, mid, lo = _split3(x)
    return _dot(hi, onehot) + _dot(mid, onehot) + _dot(lo, onehot)


def _interleave(pattern, **streams):
    by_letter = {name[0]: gen for name, gen in streams.items()}
    for letter in pattern:
        next(by_letter[letter], None)
    for gen in by_letter.values():
        _run(gen)


def _run(gen):
    for _ in gen:
        pass


def _proj_stages(h, w, sinks, conv_piece):
    width = M_WIDTH

    def put_q(raw):
        sinks["mq"][...] = conv_piece(raw, 0).astype(sinks["mq"].dtype)

    def put_k(raw):
        sinks["mk"][...] = conv_piece(raw, M_WIDTH) * (M_DH ** -0.5)

    def put_mv(raw):
        sinks["mv"][...] = raw.astype(sinks["mv"].dtype)

    def put_mo(raw):
        sinks["mo"][...] = raw

    def put_gqk(raw):
        sinks["gq"][...] = raw[:, :G_KW] * (G_DK ** -0.5)
        sinks["gk"][...] = raw[:, G_KW:]

    def put_gv(raw):
        sinks["gv"][...] = raw.astype(sinks["gv"].dtype)

    def put_gr(raw):
        sinks["gr"][...] = raw

    def put_small(small):
        g = small + w["b_small"][...]
        lane = lax.broadcasted_iota(jnp.int32, g.shape, 1)
        sinks["gates"][...] = jnp.where(lane < M_HEADS, g, _log_sigmoid(g))
        z = _dot(small.astype(BF16), w["w_a2p"][...]) + w["b_a"][...]
        sinks["la"][...] = _log_sigmoid(z) * (1.0 / G_TAU)

    epilogues = [put_small, put_q, put_k, put_mv, put_mo, put_gqk, put_gv, put_gr]
    assert w["w_big"].shape[1] == width * (len(epilogues) - 1)
    pending = (_dot(h, w["w_small"][...]), epilogues[0])
    yield
    for n, epi in enumerate(epilogues[1:]):
        cur = _dot(h, w["w_big"][:, n * width:(n + 1) * width])
        pending[1](pending[0])
        pending = (cur, epi)
        yield
    pending[1](pending[0])
    yield


def _seq_conv(cbuf_ref, tail_ref, convw_ref, convb_ref, tb):
    cw = convw_ref[...]

    def conv_piece(raw, lo):
        cols = slice(lo, lo + raw.shape[1])
        cbuf_ref[SUBLANES:SUBLANES + tb, cols] = raw
        y = convb_ref[:, cols] + raw * cw[3:4, cols]
        for j in range(CONV_W - 1):
            off = SUBLANES - (CONV_W - 1) + j
            y = y + cbuf_ref[off:off + tb, cols] * cw[j:j + 1, cols]
        last = cbuf_ref[tb:tb + SUBLANES, cols]
        tail_ref[0, :, cols] = last
        cbuf_ref[0:SUBLANES, cols] = last
        return y * _sigmoid(y)

    return conv_piece


def _tok_conv(b0_ref, b1_ref, b2_ref, raw_ref, convw_ref, convb_ref):
    cw = convw_ref[...]

    def conv_piece(raw, lo):
        cols = slice(lo, lo + raw.shape[1])
        raw_ref[:, cols] = raw
        y = (convb_ref[:, cols] + b0_ref[:, cols] * cw[0:1, cols] + b1_ref[:, cols] * cw[1:2, cols]
             + b2_ref[:, cols] * cw[2:3, cols] + raw * cw[3:4, cols])
        return y * _sigmoid(y)

    return conv_piece


_PROJ_NAMES = ("mq", "mk", "mv", "mo", "gq", "gk", "gv", "gr", "gates", "la")
_PROJ_WIDTH = dict(mq=M_WIDTH, mk=M_WIDTH, mv=M_WIDTH, mo=M_WIDTH, gq=G_KW, gk=G_KW, gv=G_VW, gr=G_VW,
                   gates=LANES, la=G_KW)
_PROJ_DTYPE = dict(mq=BF16, mk=F32, mv=BF16, mo=F32, gq=F32, gk=F32, gv=BF16, gr=F32, gates=F32, la=F32)
_PROJ_WEIGHTS = ("w_big", "w_small", "b_small", "w_a2p", "b_a")


def _proj_tok_kernel(x_ref, b0_ref, b1_ref, b2_ref, gmix_ref, wbig_ref, wsmall_ref, bsmall_ref, wa2_ref,
                     ba_ref, convw_ref, convb_ref, *out_refs):
    sinks = dict(zip(_PROJ_NAMES, out_refs[:-1]))
    raw_ref = out_refs[-1]
    w = dict(zip(_PROJ_WEIGHTS, (wbig_ref, wsmall_ref, bsmall_ref, wa2_ref, ba_ref)))
    h = _rms(x_ref[...], gmix_ref[...]).astype(BF16)
    _run(_proj_stages(h, w, sinks, _tok_conv(b0_ref, b1_ref, b2_ref, raw_ref, convw_ref, convb_ref)))


def _proj_tok_call(x2d, conv_rows, wts):
    n = x2d.shape[0]
    row = lambda width: pl.BlockSpec((n, width), lambda i: (0, 0))
    whole = pl.BlockSpec(memory_space=pltpu.VMEM)
    names = list(_PROJ_NAMES)
    return pl.pallas_call(
        _proj_tok_kernel,
        grid=(1,),
        in_specs=[row(D_MODEL)] + [row(QK_CONV)] * 3 + [whole] * 8,
        out_specs=[row(_PROJ_WIDTH[k]) for k in names] + [row(QK_CONV)],
        out_shape=[jax.ShapeDtypeStruct((n, _PROJ_WIDTH[k]), _PROJ_DTYPE[k]) for k in names]
        + [jax.ShapeDtypeStruct((n, QK_CONV), F32)],
        compiler_params=pltpu.CompilerParams(dimension_semantics=("arbitrary",), vmem_limit_bytes=VMEM_LIMIT),
        name="proj_tok",
    )(x2d, *conv_rows, wts["g_mix"], wts["w_big"], wts["w_small"], wts["b_small"], wts["w_a2p"], wts["b_a"],
      wts["conv_w"], wts["conv_b"])


def _mix_gate_sums(src, tril_ref, tri_ref):
    gates = src["gates"][...]
    bcum = _cumsum_dot(tril_ref[...], gates)
    bc = _cumsum_dot(tri_ref[...], src["la"][...])
    return gates, bcum, bc, gates.T, bcum.T, bc.T


def _mix_stages(gate_sums, src, wred_ref, caug_ref, m_ref, sbd_ref, hm_ref, hg_ref):
    ts = src["mq"].shape[0]
    n_chunks = ts // CHUNK
    gates, bcum, bc, gates_t, bcum_t, bc_t = gate_sums
    gq = src["gq"][...]
    gk = src["gk"][...]
    gv_ref = src["gv"]

    nb = ts // SUB
    half = SUB // 2
    assert half == SUBLANES
    q3 = gq.reshape(nb, SUB, G_KW)
    k3 = gk.reshape(nb, SUB, G_KW)
    bc3 = bc.reshape(nb, SUB, G_KW)
    tl = lax.broadcasted_iota(jnp.int32, (nb, SUB, G_KW), 1)
    q3u = gq.reshape(nb, 2, half, G_KW)[:, 1]
    bc3u = bc.reshape(nb, 2, half, G_KW)[:, 1]
    tlu = lax.broadcasted_iota(jnp.int32, (nb, half, G_KW), 1) + half

    def exact_pass(j, acc):
        acc_all, acc_upper = acc
        if j < half:
            arg = jnp.where(tl >= j, bc3 - bc3[:, j:j + 1, :], NEG)
            e = (q3 * k3[:, j:j + 1, :] * jnp.exp(arg)).reshape(ts, G_KW)
            return acc_all + _dot(e.astype(BF16), wred_ref[j]), acc_upper
        arg = jnp.where(tlu >= j, bc3u - bc3[:, j:j + 1, :], NEG)
        e = (q3u * k3[:, j:j + 1, :] * jnp.exp(arg)).reshape(ts // 2, G_KW)
        return acc_all, acc_upper + _dot(e.astype(BF16), wred_ref[j])

    krow = lax.broadcasted_iota(jnp.int32, (CHUNK, G_KW), 0)
    same_head_kk = (_div_pow2(lax.broadcasted_iota(jnp.int32, (G_KW, G_KW), 0), G_DK)
                    == _div_pow2(lax.broadcasted_iota(jnp.int32, (G_KW, G_KW), 1), G_DK))
    same_head_kv = (_div_pow2(lax.broadcasted_iota(jnp.int32, (G_KW, G_VW), 0), G_DK)
                    == _div_pow2(lax.broadcasted_iota(jnp.int32, (G_KW, G_VW), 1), G_DV))

    def cross_block_scores(c):
        lo = c * CHUNK
        bc_c = bc[lo:lo + CHUNK]
        k_c = gk[lo:lo + CHUNK]
        out = [jnp.zeros((SUB, G_KW), F32)]
        for i in range(1, CHUNK // SUB):
            r0 = lo + i * SUB
            r_i = bc[r0 - 1:r0, :]
            qi = (gq[r0:r0 + SUB] * jnp.exp(bc[r0:r0 + SUB] - r_i)).astype(BF16)
            ki = (k_c * jnp.exp(jnp.where(krow < i * SUB, r_i - bc_c, NEG))).astype(BF16)
            kbd = jnp.where(same_head_kk, jnp.concatenate([ki] * G_HEADS, axis=0), jnp.zeros((), BF16))
            out.append(_dot_nt(qi, kbd))
        return out

    def state_update_term(c):
        lo, hi = c * CHUNK, (c + 1) * CHUNK
        b_end = bc[hi - 1:hi, :]
        k_out_t = (gk[lo:hi] * jnp.exp(b_end - bc[lo:hi])).T.astype(BF16)
        rows = []
        for g in range(G_HEADS):
            blk = _dot(k_out_t[g * G_DK:(g + 1) * G_DK], gv_ref[lo:hi, g * G_DV:(g + 1) * G_DV])
            zero = jnp.zeros((G_DK, G_DV), F32)
            rows.append(jnp.concatenate([zero] * g + [blk] + [zero] * (G_HEADS - 1 - g), axis=1))
        return jnp.concatenate(rows, axis=0)

    adiag = (jnp.zeros((ts, G_KW), F32), jnp.zeros((ts // 2, G_KW), F32))
    per_stage = SUB // 4
    assert n_chunks == 4
    offs = []
    s_terms = []

    causal = (lax.broadcasted_iota(jnp.int32, (ts, ts), 0) >= lax.broadcasted_iota(jnp.int32, (ts, ts), 1))
    ones_col = (lax.broadcasted_iota(jnp.int32, (ts, LANES), 1) == 0).astype(BF16)
    heads = range(M_HEADS)
    hsl = [slice(hd * M_DH, (hd + 1) * M_DH) for hd in heads]
    b_col = [bcum[:, M_HEADS + hd:M_HEADS + hd + 1] for hd in heads]
    b_row = [bcum_t[M_HEADS + hd:M_HEADS + hd + 1, :] for hd in heads]
    i_row = [gates_t[hd:hd + 1, :] for hd in heads]
    m_prev = [m_ref[hd:hd + 1, 0:1] for hd in heads]
    q = [src["mq"][:, hsl[hd]] for hd in heads]
    i_col = [gates[:, hd:hd + 1] for hd in heads]
    k = [src["mk"][:, hsl[hd]] for hd in heads]
    vaug = [jnp.concatenate([src["mv"][:, hsl[hd]], ones_col], axis=1) for hd in heads]
    caug = [caug_ref[hd] for hd in heads]
    s_qk = [_dot_nt(q[hd], k[hd].astype(BF16)) for hd in heads]
    qc = [_dot(q[hd], caug[hd].astype(BF16)) for hd in heads]
    for j in range(0, per_stage):
        adiag = exact_pass(j, adiag)
    offs += cross_block_scores(0)
    s_terms.append(state_update_term(0))
    yield
    b_last = [b_col[hd][ts - 1:ts, :] for hd in heads]
    dec = [b_last[hd] - b_col[hd] + i_col[hd] for hd in heads]
    m_new = [jnp.maximum(b_last[hd] + m_prev[hd], jnp.max(dec[hd], axis=0, keepdims=True)) for hd in heads]
    kw_t = [(k[hd] * jnp.exp(dec[hd] - m_new[hd])).T.astype(BF16) for hd in heads]
    upd = [_dot(kw_t[hd], vaug[hd]) for hd in heads]
    for j in range(per_stage, 2 * per_stage):
        adiag = exact_pass(j, adiag)
    offs += cross_block_scores(1)
    s_terms.append(state_update_term(1))
    yield
    dmat = [jnp.where(causal, b_col[hd] - b_row[hd] + i_row[hd], -jnp.inf) for hd in heads]
    inter = [b_col[hd] + m_prev[hd] for hd in heads]
    m_tok = [jnp.maximum(inter[hd], jnp.max(dmat[hd], axis=1, keepdims=True)) for hd in heads]
    for j in range(2 * per_stage, 3 * per_stage):
        adiag = exact_pass(j, adiag)
    offs += cross_block_scores(2)
    s_terms.append(state_update_term(2))
    yield
    p = [(s_qk[hd] * jnp.exp(dmat[hd] - m_tok[hd])).astype(BF16) for hd in heads]
    pv = [_dot(p[hd], vaug[hd]) for hd in heads]
    for j in range(3 * per_stage, SUB):
        adiag = exact_pass(j, adiag)
    offs += cross_block_scores(3)
    s_terms.append(state_update_term(3))
    yield
    for hd in heads:
        tot = jnp.exp(inter[hd] - m_tok[hd]) * qc[hd] + pv[hd]
        den = tot[:, M_DH:M_DH + 1]
        hm_ref[:, hsl[hd]] = tot[:, :M_DH] / jnp.maximum(jnp.abs(den), jnp.exp(-m_tok[hd]))
        caug_ref[hd] = jnp.exp(b_last[hd] + m_prev[hd] - m_new[hd]) * caug[hd] + upd[hd]
        m_ref[hd:hd + 1, :] = jnp.broadcast_to(m_new[hd], (1, LANES))
    yield

    sub_of = lambda idx: lax.shift_right_logical(idx & (CHUNK - 1), SUB.bit_length() - 1)
    rowb = sub_of(lax.broadcasted_iota(jnp.int32, (ts, G_KW), 0))
    colb = sub_of(lax.broadcasted_iota(jnp.int32, (ts, G_KW), 1))
    acc_all, acc_upper = adiag
    upper = acc_upper.reshape(nb, half, G_KW)
    acc_all = acc_all + jnp.stack([jnp.zeros_like(upper), upper], axis=1).reshape(ts, G_KW)
    adiag = jnp.where(rowb == colb, acc_all, 0.0)
    intra = (jnp.concatenate(offs, axis=0) + adiag).astype(BF16)
    q_in = (gq * jnp.exp(bc)).astype(BF16)
    o_intra = []
    for c in range(n_chunks):
        lo, hi = c * CHUNK, (c + 1) * CHUNK
        vbd = jnp.where(same_head_kv, jnp.concatenate([gv_ref[lo:hi, :]] * G_HEADS, axis=0),
                        jnp.zeros((), BF16))
        o_intra.append(_dot(intra[lo:hi], vbd))
    sbd = sbd_ref[...]
    for c in range(n_chunks):
        lo, hi = c * CHUNK, (c + 1) * CHUNK
        hg_ref[lo:hi, :] = o_intra[c] + _dot(q_in[lo:hi], sbd.astype(BF16))
        dcol = jnp.exp(bc_t[:, hi - 1:hi])
        sbd = dcol * sbd + s_terms[c]
    sbd_ref[...] = sbd
    yield


def _head_norm(hv, n_heads, width):
    parts = []
    for hd in range(n_heads):
        seg = hv[:, hd * width:(hd + 1) * width]
        parts.append(seg * lax.rsqrt(jnp.mean(seg * seg, axis=-1, keepdims=True) + EPS))
    return jnp.concatenate(parts, axis=1)


def _gate_heads(hm, hg, mo, gr, gmh, ggh):
    hm = _head_norm(hm, M_HEADS, M_DH) * gmh * _sigmoid(mo)
    hg = _head_norm(hg, G_HEADS, G_DV) * ggh * (gr * _sigmoid(gr))
    return jnp.concatenate([hm, hg], axis=1).astype(BF16)


_POST_WEIGHTS = ("w_out", "g_mlp", "w1", "w2", "g_ple", "w_ple", "w_pg", "g_final")


def _post_stages(x, p, mixed, w, y_ref):
    x1 = x + _dot(mixed, w["w_out"][...])
    n1 = _rms(x1, w["g_mlp"][...]).astype(BF16)
    yield
    n_ff = 4
    ff = D_FF // n_ff
    acts = []
    for j in range(n_ff):
        u = _dot(n1, w["w1"][:, j * ff:(j + 1) * ff])
        acts.append(jnp.square(jnp.maximum(u, 0.0)).astype(BF16))
        yield
    act = jnp.concatenate(acts, axis=1)
    half = D_MODEL // 2
    mlp = []
    for j in range(2):
        mlp.append(_dot(act, w["w2"][:, j * half:(j + 1) * half]))
        yield
    x2 = x1 + jnp.concatenate(mlp, axis=1)
    ple = _dot(p.astype(BF16), w["w_ple"][...])
    n2 = _rms(x2, w["g_ple"][...]).astype(BF16)
    yield
    gate = _sigmoid(_dot(n2, w["w_pg"][...]))
    x3 = x2 + ple * gate
    y_ref[...] = _rms(x3, w["g_final"][...])
    yield


def _post_tok_kernel(x_ref, p_ref, hm_ref, hg_ref, mo_ref, gr_ref, gmh_ref, ggh_ref, wout_ref, gmlp_ref,
                     w1_ref, w2_ref, gple_ref, wple_ref, wpg_ref, gfin_ref, y_ref):
    mixed = _gate_heads(hm_ref[...], hg_ref[...], mo_ref[...], gr_ref[...], gmh_ref[...], ggh_ref[...])
    w = dict(zip(_POST_WEIGHTS, (wout_ref, gmlp_ref, w1_ref, w2_ref, gple_ref, wple_ref, wpg_ref, gfin_ref)))
    _run(_post_stages(x_ref[...], p_ref[...], mixed, w, y_ref))


def _post_tok_call(x2d, p2d, hm, hg, mo, gr, wts):
    n = x2d.shape[0]
    row = lambda width: pl.BlockSpec((n, width), lambda i: (0, 0))
    whole = pl.BlockSpec(memory_space=pltpu.VMEM)
    return pl.pallas_call(
        _post_tok_kernel,
        grid=(1,),
        in_specs=[row(D_MODEL), row(D_PLE), row(M_WIDTH), row(G_VW), row(M_WIDTH), row(G_VW)] + [whole] * 10,
        out_specs=row(D_MODEL),
        out_shape=jax.ShapeDtypeStruct((n, D_MODEL), F32),
        compiler_params=pltpu.CompilerParams(dimension_semantics=("arbitrary",), vmem_limit_bytes=VMEM_LIMIT),
        name="post_tok",
    )(x2d, p2d, hm, hg, mo, gr, wts["g_mhead"], wts["g_ghead"], *[wts[k] for k in _POST_WEIGHTS])


def _seq_kernel(xa_ref, xb_ref, pb_ref, gmix_ref, wbig_ref, wsmall_ref, bsmall_ref, wa2_ref, ba_ref,
                convw_ref, convb_ref, tril_ref, tri_ref, wred_ref, gmh_ref, ggh_ref,
                wout_ref, gmlp_ref, w1_ref, w2_ref, gple_ref, wple_ref, wpg_ref, gfin_ref,
                y_ref, tail_ref, cout_ref, mout_ref, sout_ref,
                cbuf_ref, caug_ref, m_ref, sbd_ref, mixed_ref,
                mq_s, mk_s, mv_s, mo_s, gq_s, gk_s, gv_s, gr_s, gates_s, la_s, hm_s, hg_s,
                *, steps_per_seq, n_blocks):
    tb = xa_ref.shape[0]
    s = pl.program_id(0)
    r = lax.rem(jnp.minimum(s, n_blocks - 1), steps_per_seq)

    @pl.when(s == 0)
    def _():
        mixed_ref[...] = jnp.zeros(mixed_ref.shape, BF16)

    @pl.when(r == 0)
    def _():
        cbuf_ref[0:SUBLANES, :] = jnp.zeros((SUBLANES, QK_CONV), F32)
        caug_ref[...] = jnp.zeros(caug_ref.shape, F32)
        m_ref[...] = jnp.zeros(m_ref.shape, F32)
        sbd_ref[...] = jnp.zeros(sbd_ref.shape, F32)

    slot = lax.rem(s, 2)
    src = dict(zip(_PROJ_NAMES, (mq_s, mk_s, mv_s, mo_s, gq_s, gk_s, gv_s, gr_s, gates_s, la_s)))
    w_in = dict(zip(_PROJ_WEIGHTS, (wbig_ref, wsmall_ref, bsmall_ref, wa2_ref, ba_ref)))
    w_post = dict(zip(_POST_WEIGHTS, (wout_ref, gmlp_ref, w1_ref, w2_ref, gple_ref, wple_ref, wpg_ref, gfin_ref)))

    def front():
        h = _rms(xa_ref[...], gmix_ref[...]).astype(BF16)
        yield from _proj_stages(h, w_in, src, _seq_conv(cbuf_ref, tail_ref, convw_ref, convb_ref, tb))
        gate_sums = _mix_gate_sums(src, tril_ref, tri_ref)
        yield
        yield from _mix_stages(gate_sums, src, wred_ref, caug_ref, m_ref, sbd_ref, hm_s, hg_s)
        mixed_ref[slot] = _gate_heads(hm_s[...], hg_s[...], mo_s[...], gr_s[...], gmh_ref[...], ggh_ref[...])
        yield

    def back():
        yield from _post_stages(xb_ref[...], pb_ref[...], mixed_ref[1 - slot], w_post, y_ref)

    _interleave("bfff" "bfff" "bfff" "bff" "bf" "bf" "bf" "bf" "bff", front=front(), back=back())

    @pl.when(jnp.logical_and(s < n_blocks, r == steps_per_seq - 1))
    def _():
        cout_ref[0] = caug_ref[...]
        mout_ref[0] = m_ref[...]
        for g in range(G_HEADS):
            sout_ref[0, g] = sbd_ref[g * G_DK:(g + 1) * G_DK, g * G_DV:(g + 1) * G_DV]


def _seq_call(x2d, p2d, wts, consts, *, n_seq, seq_len):
    tb = SEQ_BLOCK
    steps_per_seq = seq_len // tb
    n_blocks = n_seq * steps_per_seq
    tril, tri, wred = consts
    front_blk = lambda s: jnp.minimum(s, n_blocks - 1)
    back_blk = lambda s: jnp.maximum(s - 1, 0)
    whole = pl.BlockSpec(memory_space=pltpu.VMEM)
    per_seq = lambda shape: pl.BlockSpec((1,) + shape, lambda s: (front_blk(s) // steps_per_seq,) + (0,) * len(shape))
    scratch = [
        pltpu.VMEM((tb + SUBLANES, QK_CONV), F32),
        pltpu.VMEM((M_HEADS, M_DH, 2 * M_DH), F32),
        pltpu.VMEM((SUBLANES, LANES), F32),
        pltpu.VMEM((G_KW, G_VW), F32),
        pltpu.VMEM((2, tb, D_MODEL), BF16),
    ] + [pltpu.VMEM((tb, _PROJ_WIDTH[k]), _PROJ_DTYPE[k]) for k in _PROJ_NAMES] + [
        pltpu.VMEM((tb, M_WIDTH), F32), pltpu.VMEM((tb, G_VW), F32)]
    return pl.pallas_call(
        functools.partial(_seq_kernel, steps_per_seq=steps_per_seq, n_blocks=n_blocks),
        grid=(n_blocks + 1,),
        in_specs=[pl.BlockSpec((tb, D_MODEL), lambda s: (front_blk(s), 0)),
                  pl.BlockSpec((tb, D_MODEL), lambda s: (back_blk(s), 0)),
                  pl.BlockSpec((tb, D_PLE), lambda s: (back_blk(s), 0))] + [whole] * 21,
        out_specs=[pl.BlockSpec((tb, D_MODEL), lambda s: (back_blk(s), 0)),
                   per_seq((SUBLANES, QK_CONV)), per_seq((M_HEADS, M_DH, 2 * M_DH)),
                   per_seq((SUBLANES, LANES)), per_seq((G_HEADS, G_DK, G_DV))],
        out_shape=[jax.ShapeDtypeStruct((n_blocks * tb, D_MODEL), F32),
                   jax.ShapeDtypeStruct((n_seq, SUBLANES, QK_CONV), F32),
                   jax.ShapeDtypeStruct((n_seq, M_HEADS, M_DH, 2 * M_DH), F32),
                   jax.ShapeDtypeStruct((n_seq, SUBLANES, LANES), F32),
                   jax.ShapeDtypeStruct((n_seq, G_HEADS, G_DK, G_DV), F32)],
        scratch_shapes=scratch,
        compiler_params=pltpu.CompilerParams(dimension_semantics=("arbitrary",), vmem_limit_bytes=VMEM_LIMIT),
        name="seq_fused",
    )(x2d, x2d, p2d, wts["g_mix"], wts["w_big"], wts["w_small"], wts["b_small"], wts["w_a2p"], wts["b_a"],
      wts["conv_w"], wts["conv_b"], tril, tri, wred, wts["g_mhead"], wts["g_ghead"],
      *[wts[k] for k in _POST_WEIGHTS])


def _mix_tok_kernel(mq_ref, mk_ref, mv_ref, gates_ref, gq_ref, gk_ref, gv_ref, la_ref,
                    c_ref, n_ref, m_ref, s_ref,
                    hm_ref, hg_ref, cn_ref, nn_ref, mn_ref, sn_ref):
    bb = mq_ref.shape[0]
    wide = bb * LANES
    diag = (lax.broadcasted_iota(jnp.int32, (bb, wide), 0)
            == _div_pow2(lax.broadcasted_iota(jnp.int32, (bb, wide), 1), LANES))
    block_ones = diag.astype(BF16)

    def block_diag(rows):
        return jnp.where(diag, jnp.concatenate([rows] * bb, axis=1), jnp.zeros((), rows.dtype))

    gates = gates_ref[...]
    m_all = m_ref[...]
    for hd in range(M_HEADS):
        cs_, ce_ = hd * M_DH, (hd + 1) * M_DH
        q_bf = mq_ref[:, cs_:ce_]
        q = q_bf.astype(F32)
        k = mk_ref[:, cs_:ce_]
        v_bf = mv_ref[:, cs_:ce_]
        v = v_bf.astype(F32)
        n_prev = n_ref[:, cs_:ce_]
        ig = gates[:, hd:hd + 1]
        lf = gates[:, M_HEADS + hd:M_HEADS + hd + 1]
        m_prev = m_all[:, hd:hd + 1]
        m_new = jnp.maximum(lf + m_prev, ig)
        scale = jnp.exp(lf + m_prev - m_new)
        wk = jnp.exp(ig - m_new)
        s_qk = jnp.sum(q * k, axis=1, keepdims=True) * wk
        den = scale * jnp.sum(q * n_prev, axis=1, keepdims=True) + s_qk
        inv = 1.0 / jnp.maximum(jnp.abs(den), jnp.exp(-m_new))
        nn_ref[:, cs_:ce_] = scale * n_prev + wk * k
        mn_ref[:, hd:hd + 1] = m_new
        q_w = _dot(q.T.astype(BF16), block_ones)
        outer = _dot((k * wk).T.astype(BF16), block_diag(v_bf))
        num_w = s_qk * v
        for b in range(bb):
            blk = slice(b * M_DH, (b + 1) * M_DH)
            c_prev = c_ref[b, hd]
            qc = jnp.sum(q_w[:, blk] * c_prev, axis=0, keepdims=True)
            hm_ref[b:b + 1, cs_:ce_] = (scale[b:b + 1] * qc + num_w[b:b + 1]) * inv[b:b + 1]
            cn_ref[b, hd] = scale[b:b + 1] * c_prev + outer[:, blk]
    la = la_ref[...]
    dec = jnp.exp(la)
    gq = gq_ref[...]
    gk = gk_ref[...]
    for g in range(G_HEADS):
        ks_, ke_ = g * G_DK, (g + 1) * G_DK
        vs_, ve_ = g * G_DV, (g + 1) * G_DV
        q = gq[:, ks_:ke_]
        k = gk[:, ks_:ke_]
        v_bf = gv_ref[:, vs_:ve_]
        av = jnp.sum(q * k, axis=1, keepdims=True) * v_bf.astype(F32)
        qd_w = _dot((q * dec[:, ks_:ke_]).T.astype(BF16), block_ones)
        outer = _dot(k.T.astype(BF16), block_diag(v_bf))
        dec_w = _spread_dot(dec[:, ks_:ke_].T, block_ones)
        for b in range(bb):
            blk = slice(b * G_DV, (b + 1) * G_DV)
            s_prev = s_ref[b, g]
            hg_ref[b:b + 1, vs_:ve_] = jnp.sum(qd_w[:, blk] * s_prev, axis=0, keepdims=True) + av[b:b + 1]
            sn_ref[b, g] = dec_w[:, blk] * s_prev + outer[:, blk]


def _mix_tok_call(p, c0, n0, m0, s0):
    bb = TOK_BATCH
    n = c0.shape[0]
    row = lambda w: pl.BlockSpec((bb, w), lambda i: (i, 0))
    st4 = lambda a, b_: pl.BlockSpec((bb, M_HEADS, a, b_), lambda i: (i, 0, 0, 0))
    return pl.pallas_call(
        _mix_tok_kernel,
        grid=(n // bb,),
        in_specs=[row(M_WIDTH), row(M_WIDTH), row(M_WIDTH), row(LANES), row(G_KW), row(G_KW), row(G_VW),
                  row(G_KW), st4(M_DH, M_DH), row(M_WIDTH), row(M_HEADS), st4(G_DK, G_DV)],
        out_specs=[row(M_WIDTH), row(G_VW), st4(M_DH, M_DH), row(M_WIDTH), row(M_HEADS), st4(G_DK, G_DV)],
        out_shape=[jax.ShapeDtypeStruct((n, M_WIDTH), F32), jax.ShapeDtypeStruct((n, G_VW), F32),
                   jax.ShapeDtypeStruct(c0.shape, F32), jax.ShapeDtypeStruct((n, M_WIDTH), F32),
                   jax.ShapeDtypeStruct((n, M_HEADS), F32), jax.ShapeDtypeStruct(s0.shape, F32)],
        compiler_params=pltpu.CompilerParams(dimension_semantics=("arbitrary",), vmem_limit_bytes=VMEM_LIMIT),
        name="mix_tok",
    )(p["mq"], p["mk"], p["mv"], p["gates"], p["gq"], p["gk"], p["gv"], p["la"], c0, n0, m0, s0)


_IN_OFFS = tuple(sum(IN_SIZES[:i]) for i in range(len(IN_SIZES) + 1))
_GATES_LO, _GATES_HI, _GA_LO = _IN_OFFS[3], _IN_OFFS[5], _IN_OFFS[9]
_N_SMALL = 2 * M_HEADS + G_RANK
_W_BIG_COLS = _GATES_LO + (_GA_LO - _GATES_HI)
_CAST_STEPS = 8


def _cast_kernel(wout_ref, w1_ref, w2_ref, wple_ref, wpg_ref, o_out, o_w1, o_w2, o_ple, o_pg):
    o_out[...] = wout_ref[...].astype(BF16)
    o_w1[...] = w1_ref[...].astype(BF16)
    o_w2[...] = w2_ref[...].astype(BF16)
    o_ple[...] = wple_ref[...].astype(BF16)
    o_pg[...] = wpg_ref[...].astype(BF16)


def _cast_call(*srcs):
    rows = lambda a: pl.BlockSpec((a.shape[0] // _CAST_STEPS, a.shape[1]), lambda i: (i, 0))
    return pl.pallas_call(
        _cast_kernel,
        grid=(_CAST_STEPS,),
        in_specs=[rows(a) for a in srcs],
        out_specs=[rows(a) for a in srcs],
        out_shape=[jax.ShapeDtypeStruct(a.shape, BF16) for a in srcs],
        compiler_params=pltpu.CompilerParams(dimension_semantics=("arbitrary",), vmem_limit_bytes=VMEM_LIMIT),
        name="cast_weights",
    )(*srcs)


def _cast_in_kernel(wt_ref, big_ref, small_ref):
    piece = M_WIDTH
    for n in range(_W_BIG_COLS // piece):
        dst = n * piece
        src = dst if dst < _GATES_LO else dst + (_GATES_HI - _GATES_LO)
        big_ref[:, dst:dst + piece] = wt_ref[src:src + piece, :].T.astype(BF16)
    small = jnp.concatenate([wt_ref[_GATES_LO:_GATES_HI, :], wt_ref[_GA_LO:, :],
                             jnp.zeros((LANES - _N_SMALL, D_MODEL), F32)], axis=0)
    small_ref[...] = small.T.astype(BF16)


def _cast_in_call(w_in_t):
    whole = pl.BlockSpec(memory_space=pltpu.VMEM)
    assert _GATES_LO % M_WIDTH == 0
    return pl.pallas_call(
        _cast_in_kernel,
        in_specs=[whole], out_specs=[whole, whole],
        out_shape=[jax.ShapeDtypeStruct((D_MODEL, _W_BIG_COLS), BF16),
                   jax.ShapeDtypeStruct((D_MODEL, LANES), BF16)],
        compiler_params=pltpu.CompilerParams(vmem_limit_bytes=VMEM_LIMIT),
        name="cast_w_in",
    )(w_in_t)


def _prep_weights(w_in, conv_w, conv_b, b_gate, w_a2, b_a, g_mhead, g_ghead, w_out, g_mix, g_mlp, w1, w2,
                  g_ple, w_ple, w_pg, g_final):
    w_big, w_small = _cast_in_call(jnp.swapaxes(w_in, 0, 1))
    w_out_b, w1_b, w2_b, w_ple_b, w_pg_b = _cast_call(w_out, w1, w2, w_ple, w_pg)
    b_small = jnp.concatenate([b_gate, jnp.zeros((LANES - 2 * M_HEADS,), F32)])[None]
    w_a2p = jnp.concatenate([jnp.zeros((2 * M_HEADS, G_KW), F32), w_a2,
                             jnp.zeros((LANES - _N_SMALL, G_KW), F32)], axis=0).astype(BF16)
    return dict(
        w_big=w_big, w_small=w_small, b_small=b_small, w_a2p=w_a2p, b_a=b_a[None],
        conv_w=conv_w, conv_b=conv_b[None], g_mix=g_mix[None], g_mhead=g_mhead[None], g_ghead=g_ghead[None],
        w_out=w_out_b, g_mlp=g_mlp[None], w1=w1_b, w2=w2_b,
        g_ple=g_ple[None], w_ple=w_ple_b, w_pg=w_pg_b, g_final=g_final[None])


def _mix_constants(ts):
    t = np.arange(ts)
    tril = t[None, :] <= t[:, None]
    tri = (t[:, None] // CHUNK == t[None, :] // CHUNK) & tril
    rr = np.arange(G_KW)
    wred = ((rr[None, :, None] // G_DK == rr[None, None, :] // G_DK)
            & (rr[None, None, :] % SUB == np.arange(SUB)[:, None, None]))
    return tuple(jnp.asarray(np.asarray(a, dtype=BF16)) for a in (tril, tri, wred))


def kernel(x_prompt, x_sample, p_prompt, p_sample, state_mlstm_C, state_mlstm_n, state_mlstm_m, state_conv,
           state_gla_S, w_in, conv_w, conv_b, b_gate, w_a2, b_a, g_mhead, g_ghead, w_out, g_mix, g_mlp, w1,
           w2, g_ple, w_ple, w_pg, g_final):
    assert w_in.shape[0] == 1, "single-layer trunk"
    n_seq, seq_len, _ = x_prompt.shape
    n_tok = x_sample.shape[0]
    assert x_sample.shape[1] == 1 and seq_len % SEQ_BLOCK == 0 and n_tok % TOK_BATCH == 0
    wts = _prep_weights(w_in[0], conv_w[0], conv_b[0], b_gate[0], w_a2[0], b_a[0], g_mhead[0], g_ghead[0],
                        w_out[0], g_mix[0], g_mlp[0], w1[0], w2[0], g_ple[0], w_ple[0], w_pg[0], g_final)

    y_p, tail_p, caug_p, m_p, s_p = _seq_call(
        x_prompt.reshape(n_seq * seq_len, D_MODEL), p_prompt[0].reshape(n_seq * seq_len, D_PLE), wts,
        _mix_constants(SEQ_BLOCK), n_seq=n_seq, seq_len=seq_len)

    xs = x_sample.reshape(n_tok, D_MODEL)
    buf = state_conv[0]
    outs = _proj_tok_call(xs, (buf[:, 0], buf[:, 1], buf[:, 2]), wts)
    ps = dict(zip(_PROJ_NAMES, outs[:-1]))
    raw_s = outs[-1]
    hm_s, hg_s, c_s, n_s, m_s, s_s = _mix_tok_call(
        ps, state_mlstm_C[0], state_mlstm_n[0].reshape(n_tok, M_WIDTH), state_mlstm_m[0], state_gla_S[0])
    y_s = _post_tok_call(xs, p_sample[0].reshape(n_tok, D_PLE), hm_s, hg_s, ps["mo"], ps["gr"], wts)

    return (y_p.reshape(n_seq, seq_len, D_MODEL),
            y_s.reshape(n_tok, 1, D_MODEL),
            caug_p[None, :, :, :, :M_DH],
            caug_p[None, :, :, :, M_DH],
            m_p[None, :, :M_HEADS, 0],
            tail_p[None, :, SUBLANES - (CONV_W - 1):, :],
            s_p[None],
            c_s[None],
            n_s.reshape(1, n_tok, M_HEADS, M_DH),
            m_s[None],
            jnp.stack([buf[:, 1], buf[:, 2], raw_s], axis=1)[None],
            s_s[None])
```

```python
import functools

import jax
import jax.numpy as jnp
import numpy as np
from jax import lax
from jax.experimental import pallas as pl
from jax.experimental.pallas import tpu as pltpu

D_MODEL = 1024
M_HEADS = 4
M_DH = 128
M_WIDTH = M_HEADS * M_DH
G_HEADS = 4
G_DK = 64
G_DV = 128
G_KW = G_HEADS * G_DK
G_VW = G_HEADS * G_DV
G_RANK = 16
G_TAU = 16.0
CONV_W = 4
QK_CONV = 2 * M_WIDTH
D_FF = 4 * D_MODEL
D_PLE = 256
CHUNK = 64
SUB = 16
EPS = 1e-6
IN_SIZES = (QK_CONV, M_WIDTH, M_WIDTH, M_HEADS, M_HEADS, G_KW, G_KW, G_VW, G_VW, G_RANK)

LANES = 128
SUBLANES = 8
VMEM_LIMIT = 60 * 1024 * 1024
SEQ_BLOCK = 256

F32 = jnp.float32
BF16 = jnp.bfloat16
NEG = -1e30


def _rms(x, g):
    return x * lax.rsqrt(jnp.mean(x * x, axis=-1, keepdims=True) + EPS) * g


def _log_sigmoid(x):
    return jnp.minimum(x, 0.0) - jnp.log(1.0 + jnp.exp(-jnp.abs(x)))


def _sigmoid(x):
    return 0.5 * jnp.tanh(0.5 * x) + 0.5


def _div_pow2(idx, d):
    assert d & (d - 1) == 0
    return lax.shift_right_logical(idx, d.bit_length() - 1)


def _dot(a, b):
    return jnp.dot(a, b, preferred_element_type=F32)


def _dot_nt(a, b):
    return lax.dot_general(a, b, (((1,), (1,)), ((), ())), preferred_element_type=F32)


def _split3(x):
    hi = x.astype(BF16)
    r1 = x - hi.astype(F32)
    mid = r1.astype(BF16)
    lo = (r1 - mid.astype(F32)).astype(BF16)
    return hi, mid, lo


def _cumsum_dot(tri, x):
    hi, mid, lo = _split3(x)
    return _dot(tri, hi) + _dot(tri, mid) + _dot(tri, lo)


def _interleave(pattern, **streams):
    by_letter = {name[0]: gen for name, gen in streams.items()}
    for letter in pattern:
        next(by_letter[letter], None)
    for gen in by_letter.values():
        _run(gen)


def _run(gen):
    for _ in gen:
        pass


def _proj_stages(h, w, sinks, conv_piece):
    width = M_WIDTH

    def put_q(raw):
        sinks["mq"][...] = conv_piece(raw, 0).astype(sinks["mq"].dtype)

    def put_k(raw):
        sinks["mk"][...] = conv_piece(raw, M_WIDTH) * (M_DH ** -0.5)

    def put_mv(raw):
        sinks["mv"][...] = raw.astype(sinks["mv"].dtype)

    def put_mo(raw):
        sinks["mo"][...] = raw

    def put_gqk(raw):
        sinks["gq"][...] = raw[:, :G_KW] * (G_DK ** -0.5)
        sinks["gk"][...] = raw[:, G_KW:]

    def put_gv(raw):
        sinks["gv"][...] = raw.astype(sinks["gv"].dtype)

    def put_gr(raw):
        sinks["gr"][...] = raw

    def put_small(small):
        g = small + w["b_small"][...]
        lane = lax.broadcasted_iota(jnp.int32, g.shape, 1)
        sinks["gates"][...] = jnp.where(lane < M_HEADS, g, _log_sigmoid(g))
        z = _dot(small.astype(BF16), w["w_a2p"][...]) + w["b_a"][...]
        sinks["la"][...] = _log_sigmoid(z) * (1.0 / G_TAU)

    epilogues = [put_small, put_q, put_k, put_mv, put_mo, put_gqk, put_gv, put_gr]
    assert w["w_big"].shape[1] == width * (len(epilogues) - 1)
    pending = (_dot(h, w["w_small"][...]), epilogues[0])
    yield
    for n, epi in enumerate(epilogues[1:]):
        cur = _dot(h, w["w_big"][:, n * width:(n + 1) * width])
        pending[1](pending[0])
        pending = (cur, epi)
        yield
    pending[1](pending[0])
    yield


def _seq_conv(cbuf_ref, tail_ref, convw_ref, convb_ref, tb):
    cw = convw_ref[...]

    def conv_piece(raw, lo):
        cols = slice(lo, lo + raw.shape[1])
        cbuf_ref[SUBLANES:SUBLANES + tb, cols] = raw
        y = convb_ref[:, cols] + raw * cw[3:4, cols]
        for j in range(CONV_W - 1):
            off = SUBLANES - (CONV_W - 1) + j
            y = y + cbuf_ref[off:off + tb, cols] * cw[j:j + 1, cols]
        last = cbuf_ref[tb:tb + SUBLANES, cols]
        tail_ref[0, :, cols] = last
        cbuf_ref[0:SUBLANES, cols] = last
        return y * _sigmoid(y)

    return conv_piece


def _tok_conv(b0_ref, b1_ref, b2_ref, raw_ref, convw_ref, convb_ref):
    cw = convw_ref[...]

    def conv_piece(raw, lo):
        cols = slice(lo, lo + raw.shape[1])
        raw_ref[:, cols] = raw
        y = (convb_ref[:, cols] + b0_ref[:, cols] * cw[0:1, cols] + b1_ref[:, cols] * cw[1:2, cols]
             + b2_ref[:, cols] * cw[2:3, cols] + raw * cw[3:4, cols])
        return y * _sigmoid(y)

    return conv_piece


_PROJ_NAMES = ("mq", "mk", "mv", "mo", "gq", "gk", "gv", "gr", "gates", "la")
_PROJ_WIDTH = dict(mq=M_WIDTH, mk=M_WIDTH, mv=M_WIDTH, mo=M_WIDTH, gq=G_KW, gk=G_KW, gv=G_VW, gr=G_VW,
                   gates=LANES, la=G_KW)
_PROJ_DTYPE = dict(mq=BF16, mk=F32, mv=BF16, mo=F32, gq=F32, gk=F32, gv=BF16, gr=F32, gates=F32, la=F32)
_PROJ_WEIGHTS = ("w_big", "w_small", "b_small", "w_a2p", "b_a")


class _ColumnSink:
    def __init__(self, ref, lo, width, dtype):
        self.ref, self.lo, self.width, self.dtype = ref, lo, width, dtype

    def __setitem__(self, idx, val):
        self.ref[:, self.lo:self.lo + self.width] = val.astype(F32)


def _proj_tok_kernel(x_ref, b0_ref, b1_ref, b2_ref, n_ref, m_ref, gmix_ref, wbig_ref, wsmall_ref, bsmall_ref,
                     wa2_ref, ba_ref, convw_ref, convb_ref, pk_ref, mo_ref, gr_ref, raw_ref):
    sinks = {k: _ColumnSink(pk_ref, lo, width, _PROJ_DTYPE[k]) for k, (lo, width) in _PK_IN.items()
             if k in _PROJ_DTYPE}
    sinks.update(mo=mo_ref, gr=gr_ref)
    n_lo, n_w = _PK_IN["n"]
    m_lo, m_w = _PK_IN["m"]
    pk_ref[:, n_lo:n_lo + n_w] = n_ref[...]
    pk_ref[:, m_lo:m_lo + m_w] = jnp.zeros((pk_ref.shape[0], m_w), F32)
    pk_ref[:, m_lo:m_lo + M_HEADS] = m_ref[...]
    w = dict(zip(_PROJ_WEIGHTS, (wbig_ref, wsmall_ref, bsmall_ref, wa2_ref, ba_ref)))
    h = _rms(x_ref[...], gmix_ref[...]).astype(BF16)
    _run(_proj_stages(h, w, sinks, _tok_conv(b0_ref, b1_ref, b2_ref, raw_ref, convw_ref, convb_ref)))


def _proj_tok_call(x2d, conv_rows, n0, m0, wts):
    n = x2d.shape[0]
    row = lambda width: pl.BlockSpec((n, width), lambda i: (0, 0))
    whole = pl.BlockSpec(memory_space=pltpu.VMEM)
    return pl.pallas_call(
        _proj_tok_kernel,
        grid=(1,),
        in_specs=[row(D_MODEL)] + [row(QK_CONV)] * 3 + [row(M_WIDTH), row(M_HEADS)] + [whole] * 8,
        out_specs=[row(_PK_IN_W), row(M_WIDTH), row(G_VW), row(QK_CONV)],
        out_shape=[jax.ShapeDtypeStruct((n, _PK_IN_W), F32), jax.ShapeDtypeStruct((n, M_WIDTH), F32),
                   jax.ShapeDtypeStruct((n, G_VW), F32), jax.ShapeDtypeStruct((n, QK_CONV), F32)],
        compiler_params=pltpu.CompilerParams(dimension_semantics=("arbitrary",), vmem_limit_bytes=VMEM_LIMIT),
        name="proj_tok",
    )(x2d, *conv_rows, n0, m0, wts["g_mix"], wts["w_big"], wts["w_small"], wts["b_small"], wts["w_a2p"],
      wts["b_a"], wts["conv_w"], wts["conv_b"])


def _mix_gate_sums(src, tril_ref, tri_ref):
    gates = src["gates"][...]
    bcum = _cumsum_dot(tril_ref[...], gates)
    bc = _cumsum_dot(tri_ref[...], src["la"][...])
    return gates, bcum, bc, gates.T, bcum.T, bc.T


def _mix_stages(gate_sums, src, wred_ref, caug_ref, m_ref, sbd_ref, hm_ref, hg_ref):
    ts = src["mq"].shape[0]
    n_chunks = ts // CHUNK
    gates, bcum, bc, gates_t, bcum_t, bc_t = gate_sums
    gq = src["gq"][...]
    gk = src["gk"][...]
    gv_ref = src["gv"]

    nb = ts // SUB
    half = SUB // 2
    assert half == SUBLANES
    q3 = gq.reshape(nb, SUB, G_KW)
    k3 = gk.reshape(nb, SUB, G_KW)
    bc3 = bc.reshape(nb, SUB, G_KW)
    tl = lax.broadcasted_iota(jnp.int32, (nb, SUB, G_KW), 1)
    q3u = gq.reshape(nb, 2, half, G_KW)[:, 1]
    bc3u = bc.reshape(nb, 2, half, G_KW)[:, 1]
    tlu = lax.broadcasted_iota(jnp.int32, (nb, half, G_KW), 1) + half

    def exact_pass(j, acc):
        acc_all, acc_upper = acc
        if j < half:
            arg = jnp.where(tl >= j, bc3 - bc3[:, j:j + 1, :], NEG)
            e = (q3 * k3[:, j:j + 1, :] * jnp.exp(arg)).reshape(ts, G_KW)
            return acc_all + _dot(e.astype(BF16), wred_ref[j]), acc_upper
        arg = jnp.where(tlu >= j, bc3u - bc3[:, j:j + 1, :], NEG)
        e = (q3u * k3[:, j:j + 1, :] * jnp.exp(arg)).reshape(ts // 2, G_KW)
        return acc_all, acc_upper + _dot(e.astype(BF16), wred_ref[j])

    krow = lax.broadcasted_iota(jnp.int32, (CHUNK, G_KW), 0)
    same_head_kk = (_div_pow2(lax.broadcasted_iota(jnp.int32, (G_KW, G_KW), 0), G_DK)
                    == _div_pow2(lax.broadcasted_iota(jnp.int32, (G_KW, G_KW), 1), G_DK))
    same_head_kv = (_div_pow2(lax.broadcasted_iota(jnp.int32, (G_KW, G_VW), 0), G_DK)
                    == _div_pow2(lax.broadcasted_iota(jnp.int32, (G_KW, G_VW), 1), G_DV))

    def cross_block_scores(c):
        lo = c * CHUNK
        bc_c = bc[lo:lo + CHUNK]
        k_c = gk[lo:lo + CHUNK]
        out = [jnp.zeros((SUB, G_KW), F32)]
        for i in range(1, CHUNK // SUB):
            r0 = lo + i * SUB
            r_i = bc[r0 - 1:r0, :]
            qi = (gq[r0:r0 + SUB] * jnp.exp(bc[r0:r0 + SUB] - r_i)).astype(BF16)
            ki = (k_c * jnp.exp(jnp.where(krow < i * SUB, r_i - bc_c, NEG))).astype(BF16)
            kbd = jnp.where(same_head_kk, jnp.concatenate([ki] * G_HEADS, axis=0), jnp.zeros((), BF16))
            out.append(_dot_nt(qi, kbd))
        return out

    def state_update_term(c):
        lo, hi = c * CHUNK, (c + 1) * CHUNK
        b_end = bc[hi - 1:hi, :]
        k_out_t = (gk[lo:hi] * jnp.exp(b_end - bc[lo:hi])).T.astype(BF16)
        rows = []
        for g in range(G_HEADS):
            blk = _dot(k_out_t[g * G_DK:(g + 1) * G_DK], gv_ref[lo:hi, g * G_DV:(g + 1) * G_DV])
            zero = jnp.zeros((G_DK, G_DV), F32)
            rows.append(jnp.concatenate([zero] * g + [blk] + [zero] * (G_HEADS - 1 - g), axis=1))
        return jnp.concatenate(rows, axis=0)

    adiag = (jnp.zeros((ts, G_KW), F32), jnp.zeros((ts // 2, G_KW), F32))
    per_stage = SUB // 4
    assert n_chunks == 4
    offs = []
    s_terms = []

    causal = (lax.broadcasted_iota(jnp.int32, (ts, ts), 0) >= lax.broadcasted_iota(jnp.int32, (ts, ts), 1))
    ones_col = (lax.broadcasted_iota(jnp.int32, (ts, LANES), 1) == 0).astype(BF16)
    heads = range(M_HEADS)
    hsl = [slice(hd * M_DH, (hd + 1) * M_DH) for hd in heads]
    b_col = [bcum[:, M_HEADS + hd:M_HEADS + hd + 1] for hd in heads]
    b_row = [bcum_t[M_HEADS + hd:M_HEADS + hd + 1, :] for hd in heads]
    i_row = [gates_t[hd:hd + 1, :] for hd in heads]
    m_prev = [m_ref[hd:hd + 1, 0:1] for hd in heads]
    q = [src["mq"][:, hsl[hd]] for hd in heads]
    i_col = [gates[:, hd:hd + 1] for hd in heads]
    k = [src["mk"][:, hsl[hd]] for hd in heads]
    vaug = [jnp.concatenate([src["mv"][:, hsl[hd]], ones_col], axis=1) for hd in heads]
    caug = [caug_ref[hd] for hd in heads]
    s_qk = [_dot_nt(q[hd], k[hd].astype(BF16)) for hd in heads]
    qc = [_dot(q[hd], caug[hd].astype(BF16)) for hd in heads]
    for j in range(0, per_stage):
        adiag = exact_pass(j, adiag)
    offs += cross_block_scores(0)
    s_terms.append(state_update_term(0))
    yield
    b_last = [b_col[hd][ts - 1:ts, :] for hd in heads]
    dec = [b_last[hd] - b_col[hd] + i_col[hd] for hd in heads]
    m_new = [jnp.maximum(b_last[hd] + m_prev[hd], jnp.max(dec[hd], axis=0, keepdims=True)) for hd in heads]
    kw_t = [(k[hd] * jnp.exp(dec[hd] - m_new[hd])).T.astype(BF16) for hd in heads]
    upd = [_dot(kw_t[hd], vaug[hd]) for hd in heads]
    for j in range(per_stage, 2 * per_stage):
        adiag = exact_pass(j, adiag)
    offs += cross_block_scores(1)
    s_terms.append(state_update_term(1))
    yield
    dmat = [jnp.where(causal, b_col[hd] - b_row[hd] + i_row[hd], -jnp.inf) for hd in heads]
    inter = [b_col[hd] + m_prev[hd] for hd in heads]
    m_tok = [jnp.maximum(inter[hd], jnp.max(dmat[hd], axis=1, keepdims=True)) for hd in heads]
    for j in range(2 * per_stage, 3 * per_stage):
        adiag = exact_pass(j, adiag)
    offs += cross_block_scores(2)
    s_terms.append(state_update_term(2))
    yield
    p = [(s_qk[hd] * jnp.exp(dmat[hd] - m_tok[hd])).astype(BF16) for hd in heads]
    pv = [_dot(p[hd], vaug[hd]) for hd in heads]
    for j in range(3 * per_stage, SUB):
        adiag = exact_pass(j, adiag)
    offs += cross_block_scores(3)
    s_terms.append(state_update_term(3))
    yield
    for hd in heads:
        tot = jnp.exp(inter[hd] - m_tok[hd]) * qc[hd] + pv[hd]
        den = tot[:, M_DH:M_DH + 1]
        hm_ref[:, hsl[hd]] = tot[:, :M_DH] / jnp.maximum(jnp.abs(den), jnp.exp(-m_tok[hd]))
        caug_ref[hd] = jnp.exp(b_last[hd] + m_prev[hd] - m_new[hd]) * caug[hd] + upd[hd]
        m_ref[hd:hd + 1, :] = jnp.broadcast_to(m_new[hd], (1, LANES))
    yield

    sub_of = lambda idx: lax.shift_right_logical(idx & (CHUNK - 1), SUB.bit_length() - 1)
    rowb = sub_of(lax.broadcasted_iota(jnp.int32, (ts, G_KW), 0))
    colb = sub_of(lax.broadcasted_iota(jnp.int32, (ts, G_KW), 1))
    acc_all, acc_upper = adiag
    upper = acc_upper.reshape(nb, half, G_KW)
    acc_all = acc_all + jnp.stack([jnp.zeros_like(upper), upper], axis=1).reshape(ts, G_KW)
    adiag = jnp.where(rowb == colb, acc_all, 0.0)
    intra = (jnp.concatenate(offs, axis=0) + adiag).astype(BF16)
    q_in = (gq * jnp.exp(bc)).astype(BF16)
    o_intra = []
    for c in range(n_chunks):
        lo, hi = c * CHUNK, (c + 1) * CHUNK
        vbd = jnp.where(same_head_kv, jnp.concatenate([gv_ref[lo:hi, :]] * G_HEADS, axis=0),
                        jnp.zeros((), BF16))
        o_intra.append(_dot(intra[lo:hi], vbd))
    sbd = sbd_ref[...]
    for c in range(n_chunks):
        lo, hi = c * CHUNK, (c + 1) * CHUNK
        hg_ref[lo:hi, :] = o_intra[c] + _dot(q_in[lo:hi], sbd.astype(BF16))
        dcol = jnp.exp(bc_t[:, hi - 1:hi])
        sbd = dcol * sbd + s_terms[c]
    sbd_ref[...] = sbd
    yield


def _head_norm(hv, n_heads, width):
    parts = []
    for hd in range(n_heads):
        seg = hv[:, hd * width:(hd + 1) * width]
        parts.append(seg * lax.rsqrt(jnp.mean(seg * seg, axis=-1, keepdims=True) + EPS))
    return jnp.concatenate(parts, axis=1)


def _gate_heads(hm, hg, mo, gr, gmh, ggh):
    hm = _head_norm(hm, M_HEADS, M_DH) * gmh * _sigmoid(mo)
    hg = _head_norm(hg, G_HEADS, G_DV) * ggh * (gr * _sigmoid(gr))
    return jnp.concatenate([hm, hg], axis=1).astype(BF16)


_POST_WEIGHTS = ("w_out", "g_mlp", "w1", "w2", "g_ple", "w_ple", "w_pg", "g_final")


def _post_stages(x, p, mixed, w, y_ref):
    x1 = x + _dot(mixed, w["w_out"][...])
    n1 = _rms(x1, w["g_mlp"][...]).astype(BF16)
    yield
    n_ff = 4
    ff = D_FF // n_ff
    acts = []
    for j in range(n_ff):
        u = _dot(n1, w["w1"][:, j * ff:(j + 1) * ff])
        acts.append(jnp.square(jnp.maximum(u, 0.0)).astype(BF16))
        yield
    act = jnp.concatenate(acts, axis=1)
    half = D_MODEL // 2
    mlp = []
    for j in range(2):
        mlp.append(_dot(act, w["w2"][:, j * half:(j + 1) * half]))
        yield
    x2 = x1 + jnp.concatenate(mlp, axis=1)
    ple = _dot(p.astype(BF16), w["w_ple"][...])
    n2 = _rms(x2, w["g_ple"][...]).astype(BF16)
    yield
    gate = _sigmoid(_dot(n2, w["w_pg"][...]))
    x3 = x2 + ple * gate
    y_ref[...] = _rms(x3, w["g_final"][...])
    yield


def _post_tok_kernel(x_ref, p_ref, ho_ref, mo_ref, gr_ref, gmh_ref, ggh_ref, wout_ref, gmlp_ref,
                     w1_ref, w2_ref, gple_ref, wple_ref, wpg_ref, gfin_ref, y_ref):
    mixed = _gate_heads(ho_ref[:, :M_WIDTH], ho_ref[:, M_WIDTH:], mo_ref[...], gr_ref[...], gmh_ref[...],
                        ggh_ref[...])
    w = dict(zip(_POST_WEIGHTS, (wout_ref, gmlp_ref, w1_ref, w2_ref, gple_ref, wple_ref, wpg_ref, gfin_ref)))
    _run(_post_stages(x_ref[...], p_ref[...], mixed, w, y_ref))


def _post_tok_call(x2d, p2d, ho, mo, gr, wts):
    n = x2d.shape[0]
    row = lambda width: pl.BlockSpec((n, width), lambda i: (0, 0))
    whole = pl.BlockSpec(memory_space=pltpu.VMEM)
    return pl.pallas_call(
        _post_tok_kernel,
        grid=(1,),
        in_specs=[row(D_MODEL), row(D_PLE), row(D_MODEL), row(M_WIDTH), row(G_VW)] + [whole] * 10,
        out_specs=row(D_MODEL),
        out_shape=jax.ShapeDtypeStruct((n, D_MODEL), F32),
        compiler_params=pltpu.CompilerParams(dimension_semantics=("arbitrary",), vmem_limit_bytes=VMEM_LIMIT),
        name="post_tok",
    )(x2d, p2d, ho, mo, gr, wts["g_mhead"], wts["g_ghead"], *[wts[k] for k in _POST_WEIGHTS])


_PK_IN = dict(mq=(0, M_WIDTH), mk=(512, M_WIDTH), mv=(1024, M_WIDTH), gv=(1536, G_VW), n=(2048, M_WIDTH),
              gq=(2560, G_KW), gk=(2816, G_KW), la=(3072, G_KW), gates=(3328, LANES), m=(3456, LANES))
_PK_IN_W = 3584


def _tok_stages(pk_ref, pk8_ref, c_ref, s_ref, ho_ref, nn_ref, mn_ref, cn_ref, sn_ref):
    nb = pk_ref.shape[1]
    pk8_ref[0:nb, :] = pk_ref[0]
    x = pk8_ref[...]

    def cut(name, lo=0, w=None):
        a = _PK_IN[name][0] + lo
        return x[:, a:a + (w or _PK_IN[name][1])]

    gates = cut("gates")
    m_all = cut("m")
    for hd in range(M_HEADS):
        cs_ = hd * M_DH
        q, k, v, n_prev = cut("mq", cs_, M_DH), cut("mk", cs_, M_DH), cut("mv", cs_, M_DH), cut("n", cs_, M_DH)
        ig = gates[:, hd:hd + 1]
        lf = gates[:, M_HEADS + hd:M_HEADS + hd + 1]
        m_prev = m_all[:, hd:hd + 1]
        m_new = jnp.maximum(lf + m_prev, ig)
        scale = jnp.exp(lf + m_prev - m_new)
        wk = jnp.exp(ig - m_new)
        s_qk = jnp.sum(q * k, axis=1, keepdims=True) * wk
        den = scale * jnp.sum(q * n_prev, axis=1, keepdims=True) + s_qk
        inv = 1.0 / jnp.maximum(jnp.abs(den), jnp.exp(-m_new))
        nn_ref[0, :, cs_:cs_ + M_DH] = (scale * n_prev + wk * k)[0:nb]
        mn_ref[0, :, hd:hd + 1] = m_new[0:nb]
        q_t = q.T
        kw_t = (k * wk).T
        num_w = s_qk * v
        for b in range(nb):
            c_prev = c_ref[b, hd]
            qc = jnp.sum(q_t[:, b:b + 1] * c_prev, axis=0, keepdims=True)
            ho_ref[0, b:b + 1, cs_:cs_ + M_DH] = (scale[b:b + 1] * qc + num_w[b:b + 1]) * inv[b:b + 1]
            cn_ref[b, hd] = scale[b:b + 1] * c_prev + kw_t[:, b:b + 1] * v[b:b + 1]
        yield
    dec = jnp.exp(cut("la"))
    gq, gk = cut("gq"), cut("gk")
    for g in range(G_HEADS):
        ks_, ke_ = g * G_DK, (g + 1) * G_DK
        q, k, d = gq[:, ks_:ke_], gk[:, ks_:ke_], dec[:, ks_:ke_]
        v = cut("gv", g * G_DV, G_DV)
        av = jnp.sum(q * k, axis=1, keepdims=True) * v
        qd_t, k_t, d_t = (q * d).T, k.T, d.T
        for b in range(nb):
            s_prev = s_ref[b, g]
            ho_ref[0, b:b + 1, M_WIDTH + g * G_DV:M_WIDTH + (g + 1) * G_DV] = (
                jnp.sum(qd_t[:, b:b + 1] * s_prev, axis=0, keepdims=True) + av[b:b + 1])
            sn_ref[b, g] = d_t[:, b:b + 1] * s_prev + k_t[:, b:b + 1] * v[b:b + 1]
        yield


def _seq_kernel(xa_ref, xb_ref, pb_ref, pk_ref, cs_ref, ss_ref, gmix_ref, wbig_ref, wsmall_ref, bsmall_ref,
                wa2_ref, ba_ref, convw_ref, convb_ref, tril_ref, tri_ref, wred_ref, gmh_ref, ggh_ref,
                wout_ref, gmlp_ref, w1_ref, w2_ref, gple_ref, wple_ref, wpg_ref, gfin_ref,
                y_ref, tail_ref, cout_ref, mout_ref, sout_ref, ho_ref, nn_ref, mn_ref, cn_ref, sn_ref,
                cbuf_ref, caug_ref, m_ref, sbd_ref, mixed_ref, pk8_ref,
                mq_s, mk_s, mv_s, mo_s, gq_s, gk_s, gv_s, gr_s, gates_s, la_s, hm_s, hg_s,
                *, steps_per_seq, n_blocks):
    tb = xa_ref.shape[0]
    s = pl.program_id(0)
    r = lax.rem(jnp.minimum(s, n_blocks - 1), steps_per_seq)

    @pl.when(s == 0)
    def _():
        mixed_ref[...] = jnp.zeros(mixed_ref.shape, BF16)
        pk8_ref[...] = jnp.zeros(pk8_ref.shape, F32)

    @pl.when(r == 0)
    def _():
        cbuf_ref[0:SUBLANES, :] = jnp.zeros((SUBLANES, QK_CONV), F32)
        caug_ref[...] = jnp.zeros(caug_ref.shape, F32)
        m_ref[...] = jnp.zeros(m_ref.shape, F32)
        sbd_ref[...] = jnp.zeros(sbd_ref.shape, F32)

    slot = lax.rem(s, 2)
    src = dict(zip(_PROJ_NAMES, (mq_s, mk_s, mv_s, mo_s, gq_s, gk_s, gv_s, gr_s, gates_s, la_s)))
    w_in = dict(zip(_PROJ_WEIGHTS, (wbig_ref, wsmall_ref, bsmall_ref, wa2_ref, ba_ref)))
    w_post = dict(zip(_POST_WEIGHTS, (wout_ref, gmlp_ref, w1_ref, w2_ref, gple_ref, wple_ref, wpg_ref, gfin_ref)))

    def front():
        h = _rms(xa_ref[...], gmix_ref[...]).astype(BF16)
        yield from _proj_stages(h, w_in, src, _seq_conv(cbuf_ref, tail_ref, convw_ref, convb_ref, tb))
        gate_sums = _mix_gate_sums(src, tril_ref, tri_ref)
        yield
        yield from _mix_stages(gate_sums, src, wred_ref, caug_ref, m_ref, sbd_ref, hm_s, hg_s)
        mixed_ref[slot] = _gate_heads(hm_s[...], hg_s[...], mo_s[...], gr_s[...], gmh_ref[...], ggh_ref[...])
        yield

    def back():
        yield from _post_stages(xb_ref[...], pb_ref[...], mixed_ref[1 - slot], w_post, y_ref)

    tok = _tok_stages(pk_ref, pk8_ref, cs_ref, ss_ref, ho_ref, nn_ref, mn_ref, cn_ref, sn_ref)
    _interleave("btfff" "btfff" "btfff" "btff" "btf" "btf" "btf" "btf" "bff", front=front(), back=back(), tok=tok)

    @pl.when(jnp.logical_and(s < n_blocks, r == steps_per_seq - 1))
    def _():
        cout_ref[0] = caug_ref[...]
        mout_ref[0] = m_ref[...]
        for g in range(G_HEADS):
            sout_ref[0, g] = sbd_ref[g * G_DK:(g + 1) * G_DK, g * G_DV:(g + 1) * G_DV]


def _seq_call(x2d, p2d, tok, wts, consts, *, n_seq, seq_len):
    tb = SEQ_BLOCK
    steps_per_seq = seq_len // tb
    n_blocks = n_seq * steps_per_seq
    tril, tri, wred = consts
    pk, c0, s0 = tok
    nb = pk.shape[1]
    assert nb * n_blocks == c0.shape[0] and nb <= SUBLANES and pk.shape == (n_blocks, nb, _PK_IN_W)
    front_blk = lambda s: jnp.minimum(s, n_blocks - 1)
    back_blk = lambda s: jnp.maximum(s - 1, 0)
    whole = pl.BlockSpec(memory_space=pltpu.VMEM)
    tok_blk = lambda shape: pl.BlockSpec(shape, lambda s: (front_blk(s),) + (0,) * (len(shape) - 1))
    per_seq = lambda shape: pl.BlockSpec((1,) + shape, lambda s: (front_blk(s) // steps_per_seq,) + (0,) * len(shape))
    scratch = [
        pltpu.VMEM((tb + SUBLANES, QK_CONV), F32),
        pltpu.VMEM((M_HEADS, M_DH, 2 * M_DH), F32),
        pltpu.VMEM((SUBLANES, LANES), F32),
        pltpu.VMEM((G_KW, G_VW), F32),
        pltpu.VMEM((2, tb, D_MODEL), BF16),
        pltpu.VMEM((SUBLANES, _PK_IN_W), F32),
    ] + [pltpu.VMEM((tb, _PROJ_WIDTH[k]), _PROJ_DTYPE[k]) for k in _PROJ_NAMES] + [
        pltpu.VMEM((tb, M_WIDTH), F32), pltpu.VMEM((tb, G_VW), F32)]
    return pl.pallas_call(
        functools.partial(_seq_kernel, steps_per_seq=steps_per_seq, n_blocks=n_blocks),
        grid=(n_blocks + 1,),
        in_specs=[pl.BlockSpec((tb, D_MODEL), lambda s: (front_blk(s), 0)),
                  pl.BlockSpec((tb, D_MODEL), lambda s: (back_blk(s), 0)),
                  pl.BlockSpec((tb, D_PLE), lambda s: (back_blk(s), 0)),
                  tok_blk((1, nb, _PK_IN_W)), tok_blk((nb, M_HEADS, M_DH, M_DH)),
                  tok_blk((nb, G_HEADS, G_DK, G_DV))] + [whole] * 21,
        out_specs=[pl.BlockSpec((tb, D_MODEL), lambda s: (back_blk(s), 0)),
                   per_seq((SUBLANES, QK_CONV)), per_seq((M_HEADS, M_DH, 2 * M_DH)),
                   per_seq((SUBLANES, LANES)), per_seq((G_HEADS, G_DK, G_DV)),
                   tok_blk((1, nb, D_MODEL)), tok_blk((1, nb, M_WIDTH)), tok_blk((1, nb, M_HEADS)),
                   tok_blk((nb, M_HEADS, M_DH, M_DH)), tok_blk((nb, G_HEADS, G_DK, G_DV))],
        out_shape=[jax.ShapeDtypeStruct((n_blocks * tb, D_MODEL), F32),
                   jax.ShapeDtypeStruct((n_seq, SUBLANES, QK_CONV), F32),
                   jax.ShapeDtypeStruct((n_seq, M_HEADS, M_DH, 2 * M_DH), F32),
                   jax.ShapeDtypeStruct((n_seq, SUBLANES, LANES), F32),
                   jax.ShapeDtypeStruct((n_seq, G_HEADS, G_DK, G_DV), F32),
                   jax.ShapeDtypeStruct((n_blocks, nb, D_MODEL), F32),
                   jax.ShapeDtypeStruct((n_blocks, nb, M_WIDTH), F32),
                   jax.ShapeDtypeStruct((n_blocks, nb, M_HEADS), F32),
                   jax.ShapeDtypeStruct(c0.shape, F32), jax.ShapeDtypeStruct(s0.shape, F32)],
        scratch_shapes=scratch,
        compiler_params=pltpu.CompilerParams(dimension_semantics=("arbitrary",), vmem_limit_bytes=VMEM_LIMIT),
        name="seq_fused",
    )(x2d, x2d, p2d, pk, c0, s0, wts["g_mix"], wts["w_big"], wts["w_small"], wts["b_small"], wts["w_a2p"], wts["b_a"],
      wts["conv_w"], wts["conv_b"], tril, tri, wred, wts["g_mhead"], wts["g_ghead"],
      *[wts[k] for k in _POST_WEIGHTS])


_IN_OFFS = tuple(sum(IN_SIZES[:i]) for i in range(len(IN_SIZES) + 1))
_GATES_LO, _GATES_HI, _GA_LO = _IN_OFFS[3], _IN_OFFS[5], _IN_OFFS[9]
_N_SMALL = 2 * M_HEADS + G_RANK
_W_BIG_COLS = _GATES_LO + (_GA_LO - _GATES_HI)
_CAST_STEPS = 8


def _cast_kernel(wout_ref, w1_ref, w2_ref, wple_ref, wpg_ref, o_out, o_w1, o_w2, o_ple, o_pg):
    o_out[...] = wout_ref[...].astype(BF16)
    o_w1[...] = w1_ref[...].astype(BF16)
    o_w2[...] = w2_ref[...].astype(BF16)
    o_ple[...] = wple_ref[...].astype(BF16)
    o_pg[...] = wpg_ref[...].astype(BF16)


def _cast_call(*srcs):
    rows = lambda a: pl.BlockSpec((a.shape[0] // _CAST_STEPS, a.shape[1]), lambda i: (i, 0))
    return pl.pallas_call(
        _cast_kernel,
        grid=(_CAST_STEPS,),
        in_specs=[rows(a) for a in srcs],
        out_specs=[rows(a) for a in srcs],
        out_shape=[jax.ShapeDtypeStruct(a.shape, BF16) for a in srcs],
        compiler_params=pltpu.CompilerParams(dimension_semantics=("arbitrary",), vmem_limit_bytes=VMEM_LIMIT),
        name="cast_weights",
    )(*srcs)


def _cast_in_kernel(wt_ref, big_ref, small_ref):
    piece = M_WIDTH
    for n in range(_W_BIG_COLS // piece):
        dst = n * piece
        src = dst if dst < _GATES_LO else dst + (_GATES_HI - _GATES_LO)
        big_ref[:, dst:dst + piece] = wt_ref[src:src + piece, :].T.astype(BF16)
    small = jnp.concatenate([wt_ref[_GATES_LO:_GATES_HI, :], wt_ref[_GA_LO:, :],
                             jnp.zeros((LANES - _N_SMALL, D_MODEL), F32)], axis=0)
    small_ref[...] = small.T.astype(BF16)


def _cast_in_call(w_in_t):
    whole = pl.BlockSpec(memory_space=pltpu.VMEM)
    assert _GATES_LO % M_WIDTH == 0
    return pl.pallas_call(
        _cast_in_kernel,
        in_specs=[whole], out_specs=[whole, whole],
        out_shape=[jax.ShapeDtypeStruct((D_MODEL, _W_BIG_COLS), BF16),
                   jax.ShapeDtypeStruct((D_MODEL, LANES), BF16)],
        compiler_params=pltpu.CompilerParams(vmem_limit_bytes=VMEM_LIMIT),
        name="cast_w_in",
    )(w_in_t)


def _prep_weights(w_in, conv_w, conv_b, b_gate, w_a2, b_a, g_mhead, g_ghead, w_out, g_mix, g_mlp, w1, w2,
                  g_ple, w_ple, w_pg, g_final):
    w_big, w_small = _cast_in_call(jnp.swapaxes(w_in, 0, 1))
    w_out_b, w1_b, w2_b, w_ple_b, w_pg_b = _cast_call(w_out, w1, w2, w_ple, w_pg)
    b_small = jnp.concatenate([b_gate, jnp.zeros((LANES - 2 * M_HEADS,), F32)])[None]
    w_a2p = jnp.concatenate([jnp.zeros((2 * M_HEADS, G_KW), F32), w_a2,
                             jnp.zeros((LANES - _N_SMALL, G_KW), F32)], axis=0).astype(BF16)
    return dict(
        w_big=w_big, w_small=w_small, b_small=b_small, w_a2p=w_a2p, b_a=b_a[None],
        conv_w=conv_w, conv_b=conv_b[None], g_mix=g_mix[None], g_mhead=g_mhead[None], g_ghead=g_ghead[None],
        w_out=w_out_b, g_mlp=g_mlp[None], w1=w1_b, w2=w2_b,
        g_ple=g_ple[None], w_ple=w_ple_b, w_pg=w_pg_b, g_final=g_final[None])


def _mix_constants(ts):
    t = np.arange(ts)
    tril = t[None, :] <= t[:, None]
    tri = (t[:, None] // CHUNK == t[None, :] // CHUNK) & tril
    rr = np.arange(G_KW)
    wred = ((rr[None, :, None] // G_DK == rr[None, None, :] // G_DK)
            & (rr[None, None, :] % SUB == np.arange(SUB)[:, None, None]))
    return tuple(jnp.asarray(np.asarray(a, dtype=BF16)) for a in (tril, tri, wred))


def kernel(x_prompt, x_sample, p_prompt, p_sample, state_mlstm_C, state_mlstm_n, state_mlstm_m, state_conv,
           state_gla_S, w_in, conv_w, conv_b, b_gate, w_a2, b_a, g_mhead, g_ghead, w_out, g_mix, g_mlp, w1,
           w2, g_ple, w_ple, w_pg, g_final):
    assert w_in.shape[0] == 1, "single-layer trunk"
    n_seq, seq_len, _ = x_prompt.shape
    n_tok = x_sample.shape[0]
    assert x_sample.shape[1] == 1 and seq_len % SEQ_BLOCK == 0
    wts = _prep_weights(w_in[0], conv_w[0], conv_b[0], b_gate[0], w_a2[0], b_a[0], g_mhead[0], g_ghead[0],
                        w_out[0], g_mix[0], g_mlp[0], w1[0], w2[0], g_ple[0], w_ple[0], w_pg[0], g_final)

    xs = x_sample.reshape(n_tok, D_MODEL)
    buf = state_conv[0]
    packed, mo_s, gr_s, raw_s = _proj_tok_call(xs, (buf[:, 0], buf[:, 1], buf[:, 2]),
                                               state_mlstm_n[0].reshape(n_tok, M_WIDTH), state_mlstm_m[0], wts)
    n_blocks = n_seq * (seq_len // SEQ_BLOCK)
    assert n_tok % n_blocks == 0
    packed = packed.reshape(n_blocks, n_tok // n_blocks, _PK_IN_W)

    y_p, tail_p, caug_p, m_p, s_p, ho_s, n_s, m_s, c_s, s_s = _seq_call(
        x_prompt.reshape(n_seq * seq_len, D_MODEL), p_prompt[0].reshape(n_seq * seq_len, D_PLE),
        (packed, state_mlstm_C[0], state_gla_S[0]), wts, _mix_constants(SEQ_BLOCK), n_seq=n_seq, seq_len=seq_len)

    y_s = _post_tok_call(xs, p_sample[0].reshape(n_tok, D_PLE), ho_s.reshape(n_tok, D_MODEL), mo_s, gr_s, wts)

    return (y_p.reshape(n_seq, seq_len, D_MODEL),
            y_s.reshape(n_tok, 1, D_MODEL),
            caug_p[None, :, :, :, :M_DH],
            caug_p[None, :, :, :, M_DH],
            m_p[None, :, :M_HEADS, 0],
            tail_p[None, :, SUBLANES - (CONV_W - 1):, :],
            s_p[None],
            c_s[None],
            n_s.reshape(1, n_tok, M_HEADS, M_DH),
            m_s.reshape(1, n_tok, M_HEADS),
            jnp.stack([buf[:, 1], buf[:, 2], raw_s], axis=1)[None],
            s_s[None])
```

```python
import functools

import jax
import jax.numpy as jnp
import numpy as np
from jax import lax
from jax.experimental import pallas as pl
from jax.experimental.pallas import tpu as pltpu

D_MODEL = 1024
M_HEADS = 4
M_DH = 128
M_WIDTH = M_HEADS * M_DH
G_HEADS = 4
G_DK = 64
G_DV = 128
G_KW = G_HEADS * G_DK
G_VW = G_HEADS * G_DV
G_RANK = 16
G_TAU = 16.0
CONV_W = 4
QK_CONV = 2 * M_WIDTH
D_FF = 4 * D_MODEL
D_PLE = 256
CHUNK = 64
SUB = 16
EPS = 1e-6
IN_SIZES = (QK_CONV, M_WIDTH, M_WIDTH, M_HEADS, M_HEADS, G_KW, G_KW, G_VW, G_VW, G_RANK)

LANES = 128
SUBLANES = 8
VMEM_LIMIT = 60 * 1024 * 1024
SEQ_BLOCK = 256

F32 = jnp.float32
BF16 = jnp.bfloat16
NEG = -1e30


def _rms(x, g):
    return x * lax.rsqrt(jnp.mean(x * x, axis=-1, keepdims=True) + EPS) * g


def _log_sigmoid(x):
    return jnp.minimum(x, 0.0) - jnp.log(1.0 + jnp.exp(-jnp.abs(x)))


def _sigmoid(x):
    return 0.5 * jnp.tanh(0.5 * x) + 0.5


def _div_pow2(idx, d):
    assert d & (d - 1) == 0
    return lax.shift_right_logical(idx, d.bit_length() - 1)


def _dot(a, b):
    return jnp.dot(a, b, preferred_element_type=F32)


def _dot_nt(a, b):
    return lax.dot_general(a, b, (((1,), (1,)), ((), ())), preferred_element_type=F32)


def _split3(x):
    hi = x.astype(BF16)
    r1 = x - hi.astype(F32)
    mid = r1.astype(BF16)
    lo = (r1 - mid.astype(F32)).astype(BF16)
    return hi, mid, lo


def _cumsum_dot(tri, x):
    hi, mid, lo = _split3(x)
    return _dot(tri, hi) + _dot(tri, mid) + _dot(tri, lo)


def _interleave(pattern, **streams):
    by_letter = {name[0]: gen for name, gen in streams.items()}
    for letter in pattern:
        next(by_letter[letter], None)
    for gen in by_letter.values():
        _run(gen)


def _run(gen):
    for _ in gen:
        pass


def _proj_stages(h, w, sinks, conv_piece):
    width = M_WIDTH

    def put_q(raw):
        sinks["mq"][...] = conv_piece(raw, 0).astype(sinks["mq"].dtype)

    def put_k(raw):
        sinks["mk"][...] = conv_piece(raw, M_WIDTH) * (M_DH ** -0.5)

    def put_mv(raw):
        sinks["mv"][...] = raw.astype(sinks["mv"].dtype)

    def put_mo(raw):
        sinks["mo"][...] = raw

    def put_gqk(raw):
        sinks["gq"][...] = raw[:, :G_KW] * (G_DK ** -0.5)
        sinks["gk"][...] = raw[:, G_KW:]

    def put_gv(raw):
        sinks["gv"][...] = raw.astype(sinks["gv"].dtype)

    def put_gr(raw):
        sinks["gr"][...] = raw

    def put_small(small):
        g = small + w["b_small"][...]
        lane = lax.broadcasted_iota(jnp.int32, g.shape, 1)
        sinks["gates"][...] = jnp.where(lane < M_HEADS, g, _log_sigmoid(g))
        z = _dot(small.astype(BF16), w["w_a2p"][...]) + w["b_a"][...]
        sinks["la"][...] = _log_sigmoid(z) * (1.0 / G_TAU)

    epilogues = [put_small, put_q, put_k, put_mv, put_mo, put_gqk, put_gv, put_gr]
    assert w["w_big"].shape[1] == width * (len(epilogues) - 1)
    pending = (_dot(h, w["w_small"][...]), epilogues[0])
    yield
    for n, epi in enumerate(epilogues[1:]):
        cur = _dot(h, w["w_big"][:, n * width:(n + 1) * width])
        pending[1](pending[0])
        pending = (cur, epi)
        yield
    pending[1](pending[0])
    yield


def _seq_conv(cbuf_ref, tail_ref, convw_ref, convb_ref, tb):
    cw = convw_ref[...]

    def conv_piece(raw, lo):
        cols = slice(lo, lo + raw.shape[1])
        cbuf_ref[SUBLANES:SUBLANES + tb, cols] = raw
        y = convb_ref[:, cols] + raw * cw[3:4, cols]
        for j in range(CONV_W - 1):
            off = SUBLANES - (CONV_W - 1) + j
            y = y + cbuf_ref[off:off + tb, cols] * cw[j:j + 1, cols]
        last = cbuf_ref[tb:tb + SUBLANES, cols]
        tail_ref[0, :, cols] = last[SUBLANES - (CONV_W - 1):]
        cbuf_ref[0:SUBLANES, cols] = last
        return y * _sigmoid(y)

    return conv_piece


def _tok_conv(b0_ref, b1_ref, b2_ref, raw_ref, convw_ref, convb_ref):
    cw = convw_ref[...]

    def conv_piece(raw, lo):
        cols = slice(lo, lo + raw.shape[1])
        raw_ref[:, cols] = raw
        y = (convb_ref[:, cols] + b0_ref[:, cols] * cw[0:1, cols] + b1_ref[:, cols] * cw[1:2, cols]
             + b2_ref[:, cols] * cw[2:3, cols] + raw * cw[3:4, cols])
        return y * _sigmoid(y)

    return conv_piece


_PROJ_NAMES = ("mq", "mk", "mv", "mo", "gq", "gk", "gv", "gr", "gates", "la")
_PROJ_WIDTH = dict(mq=M_WIDTH, mk=M_WIDTH, mv=M_WIDTH, mo=M_WIDTH, gq=G_KW, gk=G_KW, gv=G_VW, gr=G_VW,
                   gates=LANES, la=G_KW)
_PROJ_DTYPE = dict(mq=BF16, mk=F32, mv=BF16, mo=F32, gq=F32, gk=F32, gv=BF16, gr=F32, gates=F32, la=F32)
_PROJ_WEIGHTS = ("w_big", "w_small", "b_small", "w_a2p", "b_a")


class _ColumnSink:
    def __init__(self, ref, lo, width, dtype):
        self.ref, self.lo, self.width, self.dtype = ref, lo, width, dtype

    def __setitem__(self, idx, val):
        self.ref[:, self.lo:self.lo + self.width] = val.astype(F32)


def _proj_tok_kernel(x_ref, b0_ref, b1_ref, b2_ref, n_ref, m_ref, gmix_ref, wbig_ref, wsmall_ref, bsmall_ref,
                     wa2_ref, ba_ref, convw_ref, convb_ref, pk_ref, mo_ref, gr_ref, raw_ref):
    sinks = {k: _ColumnSink(pk_ref, lo, width, _PROJ_DTYPE[k]) for k, (lo, width) in _PK_IN.items()
             if k in _PROJ_DTYPE}
    sinks.update(mo=mo_ref, gr=gr_ref)
    n_lo, n_w = _PK_IN["n"]
    m_lo, m_w = _PK_IN["m"]
    pk_ref[:, n_lo:n_lo + n_w] = n_ref[...]
    pk_ref[:, m_lo:m_lo + m_w] = jnp.zeros((pk_ref.shape[0], m_w), F32)
    pk_ref[:, m_lo:m_lo + M_HEADS] = m_ref[...]
    w = dict(zip(_PROJ_WEIGHTS, (wbig_ref, wsmall_ref, bsmall_ref, wa2_ref, ba_ref)))
    h = _rms(x_ref[...], gmix_ref[...]).astype(BF16)
    _run(_proj_stages(h, w, sinks, _tok_conv(b0_ref, b1_ref, b2_ref, raw_ref, convw_ref, convb_ref)))


def _proj_tok_call(x2d, conv_rows, n0, m0, wts):
    n = x2d.shape[0]
    row = lambda width: pl.BlockSpec((n, width), lambda i: (0, 0))
    whole = pl.BlockSpec(memory_space=pltpu.VMEM)
    return pl.pallas_call(
        _proj_tok_kernel,
        grid=(1,),
        in_specs=[row(D_MODEL)] + [row(QK_CONV)] * 3 + [row(M_WIDTH), row(M_HEADS)] + [whole] * 8,
        out_specs=[row(_PK_IN_W), row(M_WIDTH), row(G_VW), row(QK_CONV)],
        out_shape=[jax.ShapeDtypeStruct((n, _PK_IN_W), F32), jax.ShapeDtypeStruct((n, M_WIDTH), F32),
                   jax.ShapeDtypeStruct((n, G_VW), F32), jax.ShapeDtypeStruct((n, QK_CONV), F32)],
        compiler_params=pltpu.CompilerParams(dimension_semantics=("arbitrary",), vmem_limit_bytes=VMEM_LIMIT),
        name="proj_tok",
    )(x2d, *conv_rows, n0, m0, wts["g_mix"], wts["w_big"], wts["w_small"], wts["b_small"], wts["w_a2p"],
      wts["b_a"], wts["conv_w"], wts["conv_b"])


def _mix_gate_sums(src, tril_ref, tri_ref):
    gates = src["gates"][...]
    bcum = _cumsum_dot(tril_ref[...], gates)
    bc = _cumsum_dot(tri_ref[...], src["la"][...])
    return gates, bcum, bc, gates.T, bcum.T, bc.T


def _mix_stages(gate_sums, src, wred_ref, caug_ref, m_ref, sbd_ref, hm_ref, hg_ref):
    ts = src["mq"].shape[0]
    n_chunks = ts // CHUNK
    gates, bcum, bc, gates_t, bcum_t, bc_t = gate_sums
    gq = src["gq"][...]
    gk = src["gk"][...]
    gv_ref = src["gv"]

    nb = ts // SUB
    half = SUB // 2
    assert half == SUBLANES
    q3 = gq.reshape(nb, SUB, G_KW)
    k3 = gk.reshape(nb, SUB, G_KW)
    bc3 = bc.reshape(nb, SUB, G_KW)
    tl = lax.broadcasted_iota(jnp.int32, (nb, SUB, G_KW), 1)
    q3u = gq.reshape(nb, 2, half, G_KW)[:, 1]
    bc3u = bc.reshape(nb, 2, half, G_KW)[:, 1]
    tlu = lax.broadcasted_iota(jnp.int32, (nb, half, G_KW), 1) + half

    def exact_pass(j, acc):
        acc_all, acc_upper = acc
        if j < half:
            arg = jnp.where(tl >= j, bc3 - bc3[:, j:j + 1, :], NEG)
            e = (q3 * k3[:, j:j + 1, :] * jnp.exp(arg)).reshape(ts, G_KW)
            return acc_all + _dot(e.astype(BF16), wred_ref[j]), acc_upper
        arg = jnp.where(tlu >= j, bc3u - bc3[:, j:j + 1, :], NEG)
        e = (q3u * k3[:, j:j + 1, :] * jnp.exp(arg)).reshape(ts // 2, G_KW)
        return acc_all, acc_upper + _dot(e.astype(BF16), wred_ref[j])

    krow = lax.broadcasted_iota(jnp.int32, (CHUNK, G_KW), 0)
    same_head_kk = (_div_pow2(lax.broadcasted_iota(jnp.int32, (G_KW, G_KW), 0), G_DK)
                    == _div_pow2(lax.broadcasted_iota(jnp.int32, (G_KW, G_KW), 1), G_DK))
    same_head_kv = (_div_pow2(lax.broadcasted_iota(jnp.int32, (G_KW, G_VW), 0), G_DK)
                    == _div_pow2(lax.broadcasted_iota(jnp.int32, (G_KW, G_VW), 1), G_DV))

    def cross_block_scores(c):
        lo = c * CHUNK
        bc_c = bc[lo:lo + CHUNK]
        k_c = gk[lo:lo + CHUNK]
        out = [jnp.zeros((SUB, G_KW), F32)]
        for i in range(1, CHUNK // SUB):
            r0 = lo + i * SUB
            r_i = bc[r0 - 1:r0, :]
            qi = (gq[r0:r0 + SUB] * jnp.exp(bc[r0:r0 + SUB] - r_i)).astype(BF16)
            ki = (k_c * jnp.exp(jnp.where(krow < i * SUB, r_i - bc_c, NEG))).astype(BF16)
            kbd = jnp.where(same_head_kk, jnp.concatenate([ki] * G_HEADS, axis=0), jnp.zeros((), BF16))
            out.append(_dot_nt(qi, kbd))
        return out

    def state_update_term(c):
        lo, hi = c * CHUNK, (c + 1) * CHUNK
        b_end = bc[hi - 1:hi, :]
        k_out_t = (gk[lo:hi] * jnp.exp(b_end - bc[lo:hi])).T.astype(BF16)
        rows = []
        for g in range(G_HEADS):
            blk = _dot(k_out_t[g * G_DK:(g + 1) * G_DK], gv_ref[lo:hi, g * G_DV:(g + 1) * G_DV])
            zero = jnp.zeros((G_DK, G_DV), F32)
            rows.append(jnp.concatenate([zero] * g + [blk] + [zero] * (G_HEADS - 1 - g), axis=1))
        return jnp.concatenate(rows, axis=0)

    adiag = (jnp.zeros((ts, G_KW), F32), jnp.zeros((ts // 2, G_KW), F32))
    per_stage = SUB // 4
    assert n_chunks == 4
    offs = []
    s_terms = []

    causal = (lax.broadcasted_iota(jnp.int32, (ts, ts), 0) >= lax.broadcasted_iota(jnp.int32, (ts, ts), 1))
    ones_col = (lax.broadcasted_iota(jnp.int32, (ts, LANES), 1) == 0).astype(BF16)
    heads = range(M_HEADS)
    hsl = [slice(hd * M_DH, (hd + 1) * M_DH) for hd in heads]
    b_col = [bcum[:, M_HEADS + hd:M_HEADS + hd + 1] for hd in heads]
    b_row = [bcum_t[M_HEADS + hd:M_HEADS + hd + 1, :] for hd in heads]
    i_row = [gates_t[hd:hd + 1, :] for hd in heads]
    m_prev = [m_ref[hd:hd + 1, 0:1] for hd in heads]
    q = [src["mq"][:, hsl[hd]] for hd in heads]
    i_col = [gates[:, hd:hd + 1] for hd in heads]
    k = [src["mk"][:, hsl[hd]] for hd in heads]
    vaug = [jnp.concatenate([src["mv"][:, hsl[hd]], ones_col], axis=1) for hd in heads]
    caug = [caug_ref[hd] for hd in heads]
    s_qk = [_dot_nt(q[hd], k[hd].astype(BF16)) for hd in heads]
    qc = [_dot(q[hd], caug[hd].astype(BF16)) for hd in heads]
    for j in range(0, per_stage):
        adiag = exact_pass(j, adiag)
    offs += cross_block_scores(0)
    s_terms.append(state_update_term(0))
    yield
    b_last = [b_col[hd][ts - 1:ts, :] for hd in heads]
    dec = [b_last[hd] - b_col[hd] + i_col[hd] for hd in heads]
    m_new = [jnp.maximum(b_last[hd] + m_prev[hd], jnp.max(dec[hd], axis=0, keepdims=True)) for hd in heads]
    kw_t = [(k[hd] * jnp.exp(dec[hd] - m_new[hd])).T.astype(BF16) for hd in heads]
    upd = [_dot(kw_t[hd], vaug[hd]) for hd in heads]
    for j in range(per_stage, 2 * per_stage):
        adiag = exact_pass(j, adiag)
    offs += cross_block_scores(1)
    s_terms.append(state_update_term(1))
    yield
    dmat = [jnp.where(causal, b_col[hd] - b_row[hd] + i_row[hd], -jnp.inf) for hd in heads]
    inter = [b_col[hd] + m_prev[hd] for hd in heads]
    m_tok = [jnp.maximum(inter[hd], jnp.max(dmat[hd], axis=1, keepdims=True)) for hd in heads]
    for j in range(2 * per_stage, 3 * per_stage):
        adiag = exact_pass(j, adiag)
    offs += cross_block_scores(2)
    s_terms.append(state_update_term(2))
    yield
    p = [(s_qk[hd] * jnp.exp(dmat[hd] - m_tok[hd])).astype(BF16) for hd in heads]
    pv = [_dot(p[hd], vaug[hd]) for hd in heads]
    for j in range(3 * per_stage, SUB):
        adiag = exact_pass(j, adiag)
    offs += cross_block_scores(3)
    s_terms.append(state_update_term(3))
    yield
    for hd in heads:
        tot = jnp.exp(inter[hd] - m_tok[hd]) * qc[hd] + pv[hd]
        den = tot[:, M_DH:M_DH + 1]
        hm_ref[:, hsl[hd]] = tot[:, :M_DH] / jnp.maximum(jnp.abs(den), jnp.exp(-m_tok[hd]))
        caug_ref[hd] = jnp.exp(b_last[hd] + m_prev[hd] - m_new[hd]) * caug[hd] + upd[hd]
        m_ref[hd:hd + 1, :] = jnp.broadcast_to(m_new[hd], (1, LANES))
    yield

    sub_of = lambda idx: lax.shift_right_logical(idx & (CHUNK - 1), SUB.bit_length() - 1)
    rowb = sub_of(lax.broadcasted_iota(jnp.int32, (ts, G_KW), 0))
    colb = sub_of(lax.broadcasted_iota(jnp.int32, (ts, G_KW), 1))
    acc_all, acc_upper = adiag
    upper = acc_upper.reshape(nb, half, G_KW)
    acc_all = acc_all + jnp.stack([jnp.zeros_like(upper), upper], axis=1).reshape(ts, G_KW)
    adiag = jnp.where(rowb == colb, acc_all, 0.0)
    intra = (jnp.concatenate(offs, axis=0) + adiag).astype(BF16)
    q_in = (gq * jnp.exp(bc)).astype(BF16)
    o_intra = []
    for c in range(n_chunks):
        lo, hi = c * CHUNK, (c + 1) * CHUNK
        vbd = jnp.where(same_head_kv, jnp.concatenate([gv_ref[lo:hi, :]] * G_HEADS, axis=0),
                        jnp.zeros((), BF16))
        o_intra.append(_dot(intra[lo:hi], vbd))
    sbd = sbd_ref[...]
    for c in range(n_chunks):
        lo, hi = c * CHUNK, (c + 1) * CHUNK
        hg_ref[lo:hi, :] = o_intra[c] + _dot(q_in[lo:hi], sbd.astype(BF16))
        dcol = jnp.exp(bc_t[:, hi - 1:hi])
        sbd = dcol * sbd + s_terms[c]
    sbd_ref[...] = sbd
    yield


def _head_norm(hv, n_heads, width):
    parts = []
    for hd in range(n_heads):
        seg = hv[:, hd * width:(hd + 1) * width]
        parts.append(seg * lax.rsqrt(jnp.mean(seg * seg, axis=-1, keepdims=True) + EPS))
    return jnp.concatenate(parts, axis=1)


def _gate_heads(hm, hg, mo, gr, gmh, ggh):
    hm = _head_norm(hm, M_HEADS, M_DH) * gmh * _sigmoid(mo)
    hg = _head_norm(hg, G_HEADS, G_DV) * ggh * (gr * _sigmoid(gr))
    return jnp.concatenate([hm, hg], axis=1).astype(BF16)


_POST_WEIGHTS = ("w_out", "g_mlp", "w1", "w2", "g_ple", "w_ple", "w_pg", "g_final")


def _post_stages(x, p, mixed, w, y_ref):
    x1 = x + _dot(mixed, w["w_out"][...])
    n1 = _rms(x1, w["g_mlp"][...]).astype(BF16)
    yield
    n_ff = 4
    ff = D_FF // n_ff
    acts = []
    for j in range(n_ff):
        u = _dot(n1, w["w1"][:, j * ff:(j + 1) * ff])
        acts.append(jnp.square(jnp.maximum(u, 0.0)).astype(BF16))
        yield
    act = jnp.concatenate(acts, axis=1)
    half = D_MODEL // 2
    mlp = []
    for j in range(2):
        mlp.append(_dot(act, w["w2"][:, j * half:(j + 1) * half]))
        yield
    x2 = x1 + jnp.concatenate(mlp, axis=1)
    ple = _dot(p.astype(BF16), w["w_ple"][...])
    n2 = _rms(x2, w["g_ple"][...]).astype(BF16)
    yield
    gate = _sigmoid(_dot(n2, w["w_pg"][...]))
    x3 = x2 + ple * gate
    y_ref[...] = _rms(x3, w["g_final"][...])
    yield


def _post_tok_kernel(x_ref, p_ref, ho_ref, mo_ref, gr_ref, gmh_ref, ggh_ref, wout_ref, gmlp_ref,
                     w1_ref, w2_ref, gple_ref, wple_ref, wpg_ref, gfin_ref, y_ref):
    mixed = _gate_heads(ho_ref[:, :M_WIDTH], ho_ref[:, M_WIDTH:M_WIDTH + G_VW], mo_ref[...], gr_ref[...],
                        gmh_ref[...], ggh_ref[...])
    w = dict(zip(_POST_WEIGHTS, (wout_ref, gmlp_ref, w1_ref, w2_ref, gple_ref, wple_ref, wpg_ref, gfin_ref)))
    _run(_post_stages(x_ref[...], p_ref[...], mixed, w, y_ref))


def _post_tok_call(x2d, p2d, ho, mo, gr, wts):
    n = x2d.shape[0]
    row = lambda width: pl.BlockSpec((n, width), lambda i: (0, 0))
    whole = pl.BlockSpec(memory_space=pltpu.VMEM)
    return pl.pallas_call(
        _post_tok_kernel,
        grid=(1,),
        in_specs=[row(D_MODEL), row(D_PLE), row(ho.shape[1]), row(M_WIDTH), row(G_VW)] + [whole] * 10,
        out_specs=row(D_MODEL),
        out_shape=jax.ShapeDtypeStruct((n, D_MODEL), F32),
        compiler_params=pltpu.CompilerParams(dimension_semantics=("arbitrary",), vmem_limit_bytes=VMEM_LIMIT),
        name="post_tok",
    )(x2d, p2d, ho, mo, gr, wts["g_mhead"], wts["g_ghead"], *[wts[k] for k in _POST_WEIGHTS])


_PK_IN = dict(mq=(0, M_WIDTH), mk=(512, M_WIDTH), mv=(1024, M_WIDTH), gv=(1536, G_VW), n=(2048, M_WIDTH),
              gq=(2560, G_KW), gk=(2816, G_KW), la=(3072, G_KW), gates=(3328, LANES), m=(3456, LANES))
_PK_IN_W = 3584
_PK_OUT = dict(hm=(0, M_WIDTH), hg=(M_WIDTH, G_VW), n=(1024, M_WIDTH), m=(1536, LANES))
_PK_OUT_W = 1664


def _tok_stages(pk_ref, c_ref, s_ref, po_ref, cn_ref, sn_ref):
    nb = c_ref.shape[0]
    x = pk_ref[0]

    def cut(name, lo=0, w=None):
        a = _PK_IN[name][0] + lo
        return x[:, a:a + (w or _PK_IN[name][1])]

    def put(name, lo, val):
        a = _PK_OUT[name][0] + lo
        po_ref[0, :, a:a + val.shape[1]] = val

    po_ref[...] = jnp.zeros(po_ref.shape, F32)
    gates = cut("gates")
    m_all = cut("m")
    for hd in range(M_HEADS):
        cs_ = hd * M_DH
        q, k, v, n_prev = cut("mq", cs_, M_DH), cut("mk", cs_, M_DH), cut("mv", cs_, M_DH), cut("n", cs_, M_DH)
        ig = gates[:, hd:hd + 1]
        lf = gates[:, M_HEADS + hd:M_HEADS + hd + 1]
        m_prev = m_all[:, hd:hd + 1]
        m_new = jnp.maximum(lf + m_prev, ig)
        scale = jnp.exp(lf + m_prev - m_new)
        wk = jnp.exp(ig - m_new)
        s_qk = jnp.sum(q * k, axis=1, keepdims=True) * wk
        den = scale * jnp.sum(q * n_prev, axis=1, keepdims=True) + s_qk
        inv = 1.0 / jnp.maximum(jnp.abs(den), jnp.exp(-m_new))
        put("n", cs_, scale * n_prev + wk * k)
        put("m", hd, m_new)
        q_t = q.T
        kw_t = (k * wk).T
        num_w = s_qk * v
        for b in range(nb):
            c_prev = c_ref[b, hd]
            qc = jnp.sum(q_t[:, b:b + 1] * c_prev, axis=0, keepdims=True)
            po_ref[0, b:b + 1, cs_:cs_ + M_DH] = (scale[b:b + 1] * qc + num_w[b:b + 1]) * inv[b:b + 1]
            cn_ref[b, hd] = scale[b:b + 1] * c_prev + kw_t[:, b:b + 1] * v[b:b + 1]
        yield
    dec = jnp.exp(cut("la"))
    gq, gk = cut("gq"), cut("gk")
    for g in range(G_HEADS):
        ks_, ke_ = g * G_DK, (g + 1) * G_DK
        q, k, d = gq[:, ks_:ke_], gk[:, ks_:ke_], dec[:, ks_:ke_]
        v = cut("gv", g * G_DV, G_DV)
        av = jnp.sum(q * k, axis=1, keepdims=True) * v
        qd_t, k_t, d_t = (q * d).T, k.T, d.T
        for b in range(nb):
            s_prev = s_ref[b, g]
            po_ref[0, b:b + 1, M_WIDTH + g * G_DV:M_WIDTH + (g + 1) * G_DV] = (
                jnp.sum(qd_t[:, b:b + 1] * s_prev, axis=0, keepdims=True) + av[b:b + 1])
            sn_ref[b, g] = d_t[:, b:b + 1] * s_prev + k_t[:, b:b + 1] * v[b:b + 1]
        yield


def _seq_kernel(xa_ref, xb_ref, pb_ref, pk_ref, cs_ref, ss_ref, gmix_ref, wbig_ref, wsmall_ref, bsmall_ref,
                wa2_ref, ba_ref, convw_ref, convb_ref, tril_ref, tri_ref, wred_ref, gmh_ref, ggh_ref,
                wout_ref, gmlp_ref, w1_ref, w2_ref, gple_ref, wple_ref, wpg_ref, gfin_ref,
                y_ref, tail_ref, cout_ref, nout_ref, mout_ref, sout_ref, po_ref, cn_ref, sn_ref,
                cbuf_ref, caug_ref, m_ref, sbd_ref, mixed_ref,
                mq_s, mk_s, mv_s, mo_s, gq_s, gk_s, gv_s, gr_s, gates_s, la_s, hm_s, hg_s,
                *, steps_per_seq, n_blocks):
    tb = xa_ref.shape[0]
    s = pl.program_id(0)
    r = lax.rem(jnp.minimum(s, n_blocks - 1), steps_per_seq)

    @pl.when(s == 0)
    def _():
        mixed_ref[...] = jnp.zeros(mixed_ref.shape, BF16)

    @pl.when(r == 0)
    def _():
        cbuf_ref[0:SUBLANES, :] = jnp.zeros((SUBLANES, QK_CONV), F32)
        caug_ref[...] = jnp.zeros(caug_ref.shape, F32)
        m_ref[...] = jnp.zeros(m_ref.shape, F32)
        sbd_ref[...] = jnp.zeros(sbd_ref.shape, F32)

    slot = lax.rem(s, 2)
    src = dict(zip(_PROJ_NAMES, (mq_s, mk_s, mv_s, mo_s, gq_s, gk_s, gv_s, gr_s, gates_s, la_s)))
    w_in = dict(zip(_PROJ_WEIGHTS, (wbig_ref, wsmall_ref, bsmall_ref, wa2_ref, ba_ref)))
    w_post = dict(zip(_POST_WEIGHTS, (wout_ref, gmlp_ref, w1_ref, w2_ref, gple_ref, wple_ref, wpg_ref, gfin_ref)))

    def front():
        h = _rms(xa_ref[...], gmix_ref[...]).astype(BF16)
        yield from _proj_stages(h, w_in, src, _seq_conv(cbuf_ref, tail_ref, convw_ref, convb_ref, tb))
        gate_sums = _mix_gate_sums(src, tril_ref, tri_ref)
        yield
        yield from _mix_stages(gate_sums, src, wred_ref, caug_ref, m_ref, sbd_ref, hm_s, hg_s)
        mixed_ref[slot] = _gate_heads(hm_s[...], hg_s[...], mo_s[...], gr_s[...], gmh_ref[...], ggh_ref[...])
        yield

    def back():
        yield from _post_stages(xb_ref[...], pb_ref[...], mixed_ref[1 - slot], w_post, y_ref)

    tok = _tok_stages(pk_ref, cs_ref, ss_ref, po_ref, cn_ref, sn_ref)
    _interleave("btfff" "btfff" "btfff" "btff" "btf" "btf" "btf" "btf" "bff", front=front(), back=back(), tok=tok)

    @pl.when(jnp.logical_and(s < n_blocks, r == steps_per_seq - 1))
    def _():
        for hd in range(M_HEADS):
            cout_ref[0, hd] = caug_ref[hd, :, :M_DH]
            nout_ref[0, hd:hd + 1, :] = caug_ref[hd, :, M_DH:].T[0:1, :]
            mout_ref[0, :, hd:hd + 1] = m_ref[hd:hd + 1, 0:1]
        for g in range(G_HEADS):
            sout_ref[0, g] = sbd_ref[g * G_DK:(g + 1) * G_DK, g * G_DV:(g + 1) * G_DV]


def _seq_call(x2d, p2d, tok, wts, consts, *, n_seq, seq_len):
    tb = SEQ_BLOCK
    steps_per_seq = seq_len // tb
    n_blocks = n_seq * steps_per_seq
    tril, tri, wred = consts
    pk, c0, s0 = tok
    nb = c0.shape[0] // n_blocks
    assert nb * n_blocks == c0.shape[0] and nb <= SUBLANES and pk.shape == (n_blocks, SUBLANES, _PK_IN_W)
    front_blk = lambda s: jnp.minimum(s, n_blocks - 1)
    back_blk = lambda s: jnp.maximum(s - 1, 0)
    whole = pl.BlockSpec(memory_space=pltpu.VMEM)
    tok_blk = lambda shape: pl.BlockSpec(shape, lambda s: (front_blk(s),) + (0,) * (len(shape) - 1))
    per_seq = lambda shape: pl.BlockSpec((1,) + shape, lambda s: (front_blk(s) // steps_per_seq,) + (0,) * len(shape))
    scratch = [
        pltpu.VMEM((tb + SUBLANES, QK_CONV), F32),
        pltpu.VMEM((M_HEADS, M_DH, 2 * M_DH), F32),
        pltpu.VMEM((SUBLANES, LANES), F32),
        pltpu.VMEM((G_KW, G_VW), F32),
        pltpu.VMEM((2, tb, D_MODEL), BF16),
    ] + [pltpu.VMEM((tb, _PROJ_WIDTH[k]), _PROJ_DTYPE[k]) for k in _PROJ_NAMES] + [
        pltpu.VMEM((tb, M_WIDTH), F32), pltpu.VMEM((tb, G_VW), F32)]
    return pl.pallas_call(
        functools.partial(_seq_kernel, steps_per_seq=steps_per_seq, n_blocks=n_blocks),
        grid=(n_blocks + 1,),
        in_specs=[pl.BlockSpec((tb, D_MODEL), lambda s: (front_blk(s), 0)),
                  pl.BlockSpec((tb, D_MODEL), lambda s: (back_blk(s), 0)),
                  pl.BlockSpec((tb, D_PLE), lambda s: (back_blk(s), 0)),
                  tok_blk((1, SUBLANES, _PK_IN_W)), tok_blk((nb, M_HEADS, M_DH, M_DH)),
                  tok_blk((nb, G_HEADS, G_DK, G_DV))] + [whole] * 21,
        out_specs=[pl.BlockSpec((tb, D_MODEL), lambda s: (back_blk(s), 0)),
                   per_seq((CONV_W - 1, QK_CONV)), per_seq((M_HEADS, M_DH, M_DH)), per_seq((M_HEADS, M_DH)),
                   per_seq((1, M_HEADS)), per_seq((G_HEADS, G_DK, G_DV)),
                   tok_blk((1, SUBLANES, _PK_OUT_W)), tok_blk((nb, M_HEADS, M_DH, M_DH)),
                   tok_blk((nb, G_HEADS, G_DK, G_DV))],
        out_shape=[jax.ShapeDtypeStruct((n_blocks * tb, D_MODEL), F32),
                   jax.ShapeDtypeStruct((n_seq, CONV_W - 1, QK_CONV), F32),
                   jax.ShapeDtypeStruct((n_seq, M_HEADS, M_DH, M_DH), F32),
                   jax.ShapeDtypeStruct((n_seq, M_HEADS, M_DH), F32),
                   jax.ShapeDtypeStruct((n_seq, 1, M_HEADS), F32),
                   jax.ShapeDtypeStruct((n_seq, G_HEADS, G_DK, G_DV), F32),
                   jax.ShapeDtypeStruct((n_blocks, SUBLANES, _PK_OUT_W), F32),
                   jax.ShapeDtypeStruct(c0.shape, F32), jax.ShapeDtypeStruct(s0.shape, F32)],
        scratch_shapes=scratch,
        compiler_params=pltpu.CompilerParams(dimension_semantics=("arbitrary",), vmem_limit_bytes=VMEM_LIMIT),
        name="seq_fused",
    )(x2d, x2d, p2d, pk, c0, s0, wts["g_mix"], wts["w_big"], wts["w_small"], wts["b_small"], wts["w_a2p"], wts["b_a"],
      wts["conv_w"], wts["conv_b"], tril, tri, wred, wts["g_mhead"], wts["g_ghead"],
      *[wts[k] for k in _POST_WEIGHTS])


_IN_OFFS = tuple(sum(IN_SIZES[:i]) for i in range(len(IN_SIZES) + 1))
_GATES_LO, _GATES_HI, _GA_LO = _IN_OFFS[3], _IN_OFFS[5], _IN_OFFS[9]
_N_SMALL = 2 * M_HEADS + G_RANK
_W_BIG_COLS = _GATES_LO + (_GA_LO - _GATES_HI)
_CAST_STEPS = 8


def _cast_kernel(wout_ref, w1_ref, w2_ref, wple_ref, wpg_ref, o_out, o_w1, o_w2, o_ple, o_pg):
    o_out[...] = wout_ref[...].astype(BF16)
    o_w1[...] = w1_ref[...].astype(BF16)
    o_w2[...] = w2_ref[...].astype(BF16)
    o_ple[...] = wple_ref[...].astype(BF16)
    o_pg[...] = wpg_ref[...].astype(BF16)


def _cast_call(*srcs):
    rows = lambda a: pl.BlockSpec((a.shape[0] // _CAST_STEPS, a.shape[1]), lambda i: (i, 0))
    return pl.pallas_call(
        _cast_kernel,
        grid=(_CAST_STEPS,),
        in_specs=[rows(a) for a in srcs],
        out_specs=[rows(a) for a in srcs],
        out_shape=[jax.ShapeDtypeStruct(a.shape, BF16) for a in srcs],
        compiler_params=pltpu.CompilerParams(dimension_semantics=("arbitrary",), vmem_limit_bytes=VMEM_LIMIT),
        name="cast_weights",
    )(*srcs)


def _cast_in_kernel(wt_ref, big_ref, small_ref):
    piece = M_WIDTH
    for n in range(_W_BIG_COLS // piece):
        dst = n * piece
        src = dst if dst < _GATES_LO else dst + (_GATES_HI - _GATES_LO)
        big_ref[:, dst:dst + piece] = wt_ref[src:src + piece, :].T.astype(BF16)
    small = jnp.concatenate([wt_ref[_GATES_LO:_GATES_HI, :], wt_ref[_GA_LO:, :],
                             jnp.zeros((LANES - _N_SMALL, D_MODEL), F32)], axis=0)
    small_ref[...] = small.T.astype(BF16)


def _cast_in_call(w_in_t):
    whole = pl.BlockSpec(memory_space=pltpu.VMEM)
    assert _GATES_LO % M_WIDTH == 0
    return pl.pallas_call(
        _cast_in_kernel,
        in_specs=[whole], out_specs=[whole, whole],
        out_shape=[jax.ShapeDtypeStruct((D_MODEL, _W_BIG_COLS), BF16),
                   jax.ShapeDtypeStruct((D_MODEL, LANES), BF16)],
        compiler_params=pltpu.CompilerParams(vmem_limit_bytes=VMEM_LIMIT),
        name="cast_w_in",
    )(w_in_t)


def _prep_weights(w_in, conv_w, conv_b, b_gate, w_a2, b_a, g_mhead, g_ghead, w_out, g_mix, g_mlp, w1, w2,
                  g_ple, w_ple, w_pg, g_final):
    w_big, w_small = _cast_in_call(jnp.swapaxes(w_in, 0, 1))
    w_out_b, w1_b, w2_b, w_ple_b, w_pg_b = _cast_call(w_out, w1, w2, w_ple, w_pg)
    b_small = jnp.concatenate([b_gate, jnp.zeros((LANES - 2 * M_HEADS,), F32)])[None]
    w_a2p = jnp.concatenate([jnp.zeros((2 * M_HEADS, G_KW), F32), w_a2,
                             jnp.zeros((LANES - _N_SMALL, G_KW), F32)], axis=0).astype(BF16)
    return dict(
        w_big=w_big, w_small=w_small, b_small=b_small, w_a2p=w_a2p, b_a=b_a[None],
        conv_w=conv_w, conv_b=conv_b[None], g_mix=g_mix[None], g_mhead=g_mhead[None], g_ghead=g_ghead[None],
        w_out=w_out_b, g_mlp=g_mlp[None], w1=w1_b, w2=w2_b,
        g_ple=g_ple[None], w_ple=w_ple_b, w_pg=w_pg_b, g_final=g_final[None])


def _mix_constants(ts):
    t = np.arange(ts)
    tril = t[None, :] <= t[:, None]
    tri = (t[:, None] // CHUNK == t[None, :] // CHUNK) & tril
    rr = np.arange(G_KW)
    wred = ((rr[None, :, None] // G_DK == rr[None, None, :] // G_DK)
            & (rr[None, None, :] % SUB == np.arange(SUB)[:, None, None]))
    return tuple(jnp.asarray(np.asarray(a, dtype=BF16)) for a in (tril, tri, wred))


def kernel(x_prompt, x_sample, p_prompt, p_sample, state_mlstm_C, state_mlstm_n, state_mlstm_m, state_conv,
           state_gla_S, w_in, conv_w, conv_b, b_gate, w_a2, b_a, g_mhead, g_ghead, w_out, g_mix, g_mlp, w1,
           w2, g_ple, w_ple, w_pg, g_final):
    assert w_in.shape[0] == 1, "single-layer trunk"
    n_seq, seq_len, _ = x_prompt.shape
    n_tok = x_sample.shape[0]
    assert x_sample.shape[1] == 1 and seq_len % SEQ_BLOCK == 0
    wts = _prep_weights(w_in[0], conv_w[0], conv_b[0], b_gate[0], w_a2[0], b_a[0], g_mhead[0], g_ghead[0],
                        w_out[0], g_mix[0], g_mlp[0], w1[0], w2[0], g_ple[0], w_ple[0], w_pg[0], g_final)

    xs = x_sample.reshape(n_tok, D_MODEL)
    buf = state_conv[0]
    packed, mo_s, gr_s, raw_s = _proj_tok_call(xs, (buf[:, 0], buf[:, 1], buf[:, 2]),
                                               state_mlstm_n[0].reshape(n_tok, M_WIDTH), state_mlstm_m[0], wts)
    n_blocks = n_seq * (seq_len // SEQ_BLOCK)
    per_step = n_tok // n_blocks
    assert per_step * n_blocks == n_tok
    packed = jnp.pad(packed.reshape(n_blocks, per_step, _PK_IN_W), ((0, 0), (0, SUBLANES - per_step), (0, 0)))

    y_p, conv_p, c_p, n_p, m_p, s_p, po, c_s, s_s = _seq_call(
        x_prompt.reshape(n_seq * seq_len, D_MODEL), p_prompt[0].reshape(n_seq * seq_len, D_PLE),
        (packed, state_mlstm_C[0], state_gla_S[0]), wts, _mix_constants(SEQ_BLOCK), n_seq=n_seq, seq_len=seq_len)
    po = po[:, :per_step].reshape(n_tok, _PK_OUT_W)
    n_s = po[:, _PK_OUT["n"][0]:_PK_OUT["n"][0] + M_WIDTH]
    m_s = po[:, _PK_OUT["m"][0]:_PK_OUT["m"][0] + M_HEADS]

    y_s = _post_tok_call(xs, p_sample[0].reshape(n_tok, D_PLE), po, mo_s, gr_s, wts)

    return (y_p.reshape(n_seq, seq_len, D_MODEL),
            y_s.reshape(n_tok, 1, D_MODEL),
            c_p[None],
            n_p[None],
            m_p.reshape(1, n_seq, M_HEADS),
            conv_p[None],
            s_p[None],
            c_s[None],
            n_s.reshape(1, n_tok, M_HEADS, M_DH),
            m_s.reshape(1, n_tok, M_HEADS),
            jnp.stack([buf[:, 1], buf[:, 2], raw_s], axis=1)[None],
            s_s[None])
```

```python
import functools

import jax
import jax.numpy as jnp
import numpy as np
from jax import lax
from jax.experimental import pallas as pl
from jax.experimental.pallas import tpu as pltpu

D_MODEL = 1024
M_HEADS = 4
M_DH = 128
M_WIDTH = M_HEADS * M_DH
G_HEADS = 4
G_DK = 64
G_DV = 128
G_KW = G_HEADS * G_DK
G_VW = G_HEADS * G_DV
G_RANK = 16
G_TAU = 16.0
CONV_W = 4
QK_CONV = 2 * M_WIDTH
D_FF = 4 * D_MODEL
D_PLE = 256
CHUNK = 64
SUB = 16
EPS = 1e-6
IN_SIZES = (QK_CONV, M_WIDTH, M_WIDTH, M_HEADS, M_HEADS, G_KW, G_KW, G_VW, G_VW, G_RANK)

LANES = 128
SUBLANES = 8
VMEM_LIMIT = 60 * 1024 * 1024
SEQ_BLOCK = 256

F32 = jnp.float32
BF16 = jnp.bfloat16
NEG = -1e30


def _rms(x, g):
    return x * lax.rsqrt(jnp.mean(x * x, axis=-1, keepdims=True) + EPS) * g


def _log_sigmoid(x):
    return jnp.minimum(x, 0.0) - jnp.log(1.0 + jnp.exp(-jnp.abs(x)))


def _sigmoid(x):
    return 0.5 * jnp.tanh(0.5 * x) + 0.5


def _div_pow2(idx, d):
    assert d & (d - 1) == 0
    return lax.shift_right_logical(idx, d.bit_length() - 1)


def _dot(a, b):
    return jnp.dot(a, b, preferred_element_type=F32)


def _dot_nt(a, b):
    return lax.dot_general(a, b, (((1,), (1,)), ((), ())), preferred_element_type=F32)


def _split3(x):
    hi = x.astype(BF16)
    r1 = x - hi.astype(F32)
    mid = r1.astype(BF16)
    lo = (r1 - mid.astype(F32)).astype(BF16)
    return hi, mid, lo


def _cumsum_dot(tri, x):
    hi, mid, lo = _split3(x)
    return _dot(tri, hi) + _dot(tri, mid) + _dot(tri, lo)


def _interleave(pattern, **streams):
    by_letter = {name[0]: gen for name, gen in streams.items()}
    for letter in pattern:
        next(by_letter[letter], None)
    for gen in by_letter.values():
        _run(gen)


def _run(gen):
    for _ in gen:
        pass


def _proj_stages(h, w, sinks, conv_piece):
    width = M_WIDTH

    def put_q(raw):
        sinks["mq"][...] = conv_piece(raw, 0).astype(sinks["mq"].dtype)

    def put_k(raw):
        sinks["mk"][...] = conv_piece(raw, M_WIDTH) * (M_DH ** -0.5)

    def put_mv(raw):
        sinks["mv"][...] = raw.astype(sinks["mv"].dtype)

    def put_mo(raw):
        sinks["mo"][...] = raw

    def put_gqk(raw):
        sinks["gq"][...] = raw[:, :G_KW] * (G_DK ** -0.5)
        sinks["gk"][...] = raw[:, G_KW:]

    def put_gv(raw):
        sinks["gv"][...] = raw.astype(sinks["gv"].dtype)

    def put_gr(raw):
        sinks["gr"][...] = raw

    def put_small(small):
        g = small + w["b_small"][...]
        lane = lax.broadcasted_iota(jnp.int32, g.shape, 1)
        sinks["gates"][...] = jnp.where(lane < M_HEADS, g, _log_sigmoid(g))
        z = _dot(small.astype(BF16), w["w_a2p"][...]) + w["b_a"][...]
        sinks["la"][...] = _log_sigmoid(z) * (1.0 / G_TAU)

    epilogues = [put_small, put_q, put_k, put_mv, put_mo, put_gqk, put_gv, put_gr]
    assert w["w_big"].shape[1] == width * (len(epilogues) - 1)
    pending = (_dot(h, w["w_small"][...]), epilogues[0])
    yield
    for n, epi in enumerate(epilogues[1:]):
        cur = _dot(h, w["w_big"][:, n * width:(n + 1) * width])
        pending[1](pending[0])
        pending = (cur, epi)
        yield
    pending[1](pending[0])
    yield


def _seq_conv(cbuf_ref, convw_ref, convb_ref, tb):
    cw = convw_ref[...]

    def conv_piece(raw, lo):
        cols = slice(lo, lo + raw.shape[1])
        cbuf_ref[SUBLANES:SUBLANES + tb, cols] = raw
        y = convb_ref[:, cols] + raw * cw[3:4, cols]
        for j in range(CONV_W - 1):
            off = SUBLANES - (CONV_W - 1) + j
            y = y + cbuf_ref[off:off + tb, cols] * cw[j:j + 1, cols]
        cbuf_ref[0:SUBLANES, cols] = cbuf_ref[tb:tb + SUBLANES, cols]
        return y * _sigmoid(y)

    return conv_piece


def _tok_conv(b0_ref, b1_ref, b2_ref, raw_ref, convw_ref, convb_ref):
    cw = convw_ref[...]

    def conv_piece(raw, lo):
        cols = slice(lo, lo + raw.shape[1])
        raw_ref[:, cols] = raw
        y = (convb_ref[:, cols] + b0_ref[:, cols] * cw[0:1, cols] + b1_ref[:, cols] * cw[1:2, cols]
             + b2_ref[:, cols] * cw[2:3, cols] + raw * cw[3:4, cols])
        return y * _sigmoid(y)

    return conv_piece


_PROJ_NAMES = ("mq", "mk", "mv", "mo", "gq", "gk", "gv", "gr", "gates", "la")
_PROJ_WIDTH = dict(mq=M_WIDTH, mk=M_WIDTH, mv=M_WIDTH, mo=M_WIDTH, gq=G_KW, gk=G_KW, gv=G_VW, gr=G_VW,
                   gates=LANES, la=G_KW)
_PROJ_DTYPE = dict(mq=BF16, mk=F32, mv=BF16, mo=F32, gq=F32, gk=F32, gv=BF16, gr=F32, gates=F32, la=F32)
_PROJ_WEIGHTS = ("w_big", "w_small", "b_small", "w_a2p", "b_a")


class _ColumnSink:
    def __init__(self, ref, lo, width, dtype):
        self.ref, self.lo, self.width, self.dtype = ref, lo, width, dtype

    def __setitem__(self, idx, val):
        self.ref[:, self.lo:self.lo + self.width] = val.astype(F32)


def _proj_tok_kernel(x_ref, b0_ref, b1_ref, b2_ref, n_ref, m_ref, gmix_ref, wbig_ref, wsmall_ref, bsmall_ref,
                     wa2_ref, ba_ref, convw_ref, convb_ref, pk_ref, mo_ref, gr_ref, raw_ref):
    sinks = {k: _ColumnSink(pk_ref, lo, width, _PROJ_DTYPE[k]) for k, (lo, width) in _PK_IN.items()
             if k in _PROJ_DTYPE}
    sinks.update(mo=mo_ref, gr=gr_ref)
    n_lo, n_w = _PK_IN["n"]
    m_lo, m_w = _PK_IN["m"]
    pk_ref[:, n_lo:n_lo + n_w] = n_ref[...]
    pk_ref[:, m_lo:m_lo + m_w] = jnp.zeros((pk_ref.shape[0], m_w), F32)
    pk_ref[:, m_lo:m_lo + M_HEADS] = m_ref[...]
    w = dict(zip(_PROJ_WEIGHTS, (wbig_ref, wsmall_ref, bsmall_ref, wa2_ref, ba_ref)))
    h = _rms(x_ref[...], gmix_ref[...]).astype(BF16)
    _run(_proj_stages(h, w, sinks, _tok_conv(b0_ref, b1_ref, b2_ref, raw_ref, convw_ref, convb_ref)))


def _proj_tok_call(x2d, conv_rows, n0, m0, wts):
    n = x2d.shape[0]
    row = lambda width: pl.BlockSpec((n, width), lambda i: (0, 0))
    whole = pl.BlockSpec(memory_space=pltpu.VMEM)
    return pl.pallas_call(
        _proj_tok_kernel,
        grid=(1,),
        in_specs=[row(D_MODEL)] + [row(QK_CONV)] * 3 + [row(M_WIDTH), row(M_HEADS)] + [whole] * 8,
        out_specs=[row(_PK_IN_W), row(M_WIDTH), row(G_VW), row(QK_CONV)],
        out_shape=[jax.ShapeDtypeStruct((n, _PK_IN_W), F32), jax.ShapeDtypeStruct((n, M_WIDTH), F32),
                   jax.ShapeDtypeStruct((n, G_VW), F32), jax.ShapeDtypeStruct((n, QK_CONV), F32)],
        compiler_params=pltpu.CompilerParams(dimension_semantics=("arbitrary",), vmem_limit_bytes=VMEM_LIMIT),
        name="proj_tok",
    )(x2d, *conv_rows, n0, m0, wts["g_mix"], wts["w_big"], wts["w_small"], wts["b_small"], wts["w_a2p"],
      wts["b_a"], wts["conv_w"], wts["conv_b"])


def _mix_gate_sums(src, tril_ref, tri_ref):
    gates = src["gates"][...]
    bcum = _cumsum_dot(tril_ref[...], gates)
    bc = _cumsum_dot(tri_ref[...], src["la"][...])
    return gates, bcum, bc, gates.T, bcum.T, bc.T


def _mix_stages(gate_sums, src, wred_ref, caug_ref, m_ref, sbd_ref, hm_ref, hg_ref):
    ts = src["mq"].shape[0]
    n_chunks = ts // CHUNK
    gates, bcum, bc, gates_t, bcum_t, bc_t = gate_sums
    gq = src["gq"][...]
    gk = src["gk"][...]
    gv_ref = src["gv"]

    nb = ts // SUB
    half = SUB // 2
    assert half == SUBLANES
    q3 = gq.reshape(nb, SUB, G_KW)
    k3 = gk.reshape(nb, SUB, G_KW)
    bc3 = bc.reshape(nb, SUB, G_KW)
    tl = lax.broadcasted_iota(jnp.int32, (nb, SUB, G_KW), 1)
    q3u = gq.reshape(nb, 2, half, G_KW)[:, 1]
    bc3u = bc.reshape(nb, 2, half, G_KW)[:, 1]
    tlu = lax.broadcasted_iota(jnp.int32, (nb, half, G_KW), 1) + half

    def exact_pass(j, acc):
        acc_all, acc_upper = acc
        if j < half:
            arg = jnp.where(tl >= j, bc3 - bc3[:, j:j + 1, :], NEG)
            e = (q3 * k3[:, j:j + 1, :] * jnp.exp(arg)).reshape(ts, G_KW)
            return acc_all + _dot(e.astype(BF16), wred_ref[j]), acc_upper
        arg = jnp.where(tlu >= j, bc3u - bc3[:, j:j + 1, :], NEG)
        e = (q3u * k3[:, j:j + 1, :] * jnp.exp(arg)).reshape(ts // 2, G_KW)
        return acc_all, acc_upper + _dot(e.astype(BF16), wred_ref[j])

    krow = lax.broadcasted_iota(jnp.int32, (CHUNK, G_KW), 0)
    same_head_kk = (_div_pow2(lax.broadcasted_iota(jnp.int32, (G_KW, G_KW), 0), G_DK)
                    == _div_pow2(lax.broadcasted_iota(jnp.int32, (G_KW, G_KW), 1), G_DK))
    same_head_kv = (_div_pow2(lax.broadcasted_iota(jnp.int32, (G_KW, G_VW), 0), G_DK)
                    == _div_pow2(lax.broadcasted_iota(jnp.int32, (G_KW, G_VW), 1), G_DV))

    def cross_block_scores(c):
        lo = c * CHUNK
        bc_c = bc[lo:lo + CHUNK]
        k_c = gk[lo:lo + CHUNK]
        out = [jnp.zeros((SUB, G_KW), F32)]
        for i in range(1, CHUNK // SUB):
            r0 = lo + i * SUB
            r_i = bc[r0 - 1:r0, :]
            qi = (gq[r0:r0 + SUB] * jnp.exp(bc[r0:r0 + SUB] - r_i)).astype(BF16)
            ki = (k_c * jnp.exp(jnp.where(krow < i * SUB, r_i - bc_c, NEG))).astype(BF16)
            kbd = jnp.where(same_head_kk, jnp.concatenate([ki] * G_HEADS, axis=0), jnp.zeros((), BF16))
            out.append(_dot_nt(qi, kbd))
        return out

    def state_update_term(c):
        lo, hi = c * CHUNK, (c + 1) * CHUNK
        b_end = bc[hi - 1:hi, :]
        k_out_t = (gk[lo:hi] * jnp.exp(b_end - bc[lo:hi])).T.astype(BF16)
        rows = []
        for g in range(G_HEADS):
            blk = _dot(k_out_t[g * G_DK:(g + 1) * G_DK], gv_ref[lo:hi, g * G_DV:(g + 1) * G_DV])
            zero = jnp.zeros((G_DK, G_DV), F32)
            rows.append(jnp.concatenate([zero] * g + [blk] + [zero] * (G_HEADS - 1 - g), axis=1))
        return jnp.concatenate(rows, axis=0)

    adiag = (jnp.zeros((ts, G_KW), F32), jnp.zeros((ts // 2, G_KW), F32))
    per_stage = SUB // 4
    assert n_chunks == 4
    offs = []
    s_terms = []

    causal = (lax.broadcasted_iota(jnp.int32, (ts, ts), 0) >= lax.broadcasted_iota(jnp.int32, (ts, ts), 1))
    ones_col = (lax.broadcasted_iota(jnp.int32, (ts, LANES), 1) == 0).astype(BF16)
    heads = range(M_HEADS)
    hsl = [slice(hd * M_DH, (hd + 1) * M_DH) for hd in heads]
    b_col = [bcum[:, M_HEADS + hd:M_HEADS + hd + 1] for hd in heads]
    b_row = [bcum_t[M_HEADS + hd:M_HEADS + hd + 1, :] for hd in heads]
    i_row = [gates_t[hd:hd + 1, :] for hd in heads]
    m_prev = [m_ref[hd:hd + 1, 0:1] for hd in heads]
    q = [src["mq"][:, hsl[hd]] for hd in heads]
    i_col = [gates[:, hd:hd + 1] for hd in heads]
    k = [src["mk"][:, hsl[hd]] for hd in heads]
    vaug = [jnp.concatenate([src["mv"][:, hsl[hd]], ones_col], axis=1) for hd in heads]
    caug = [caug_ref[hd] for hd in heads]
    s_qk = [_dot_nt(q[hd], k[hd].astype(BF16)) for hd in heads]
    qc = [_dot(q[hd], caug[hd].astype(BF16)) for hd in heads]
    for j in range(0, per_stage):
        adiag = exact_pass(j, adiag)
    offs += cross_block_scores(0)
    s_terms.append(state_update_term(0))
    yield
    b_last = [b_col[hd][ts - 1:ts, :] for hd in heads]
    dec = [b_last[hd] - b_col[hd] + i_col[hd] for hd in heads]
    m_new = [jnp.maximum(b_last[hd] + m_prev[hd], jnp.max(dec[hd], axis=0, keepdims=True)) for hd in heads]
    kw_t = [(k[hd] * jnp.exp(dec[hd] - m_new[hd])).T.astype(BF16) for hd in heads]
    upd = [_dot(kw_t[hd], vaug[hd]) for hd in heads]
    for j in range(per_stage, 2 * per_stage):
        adiag = exact_pass(j, adiag)
    offs += cross_block_scores(1)
    s_terms.append(state_update_term(1))
    yield
    dmat = [jnp.where(causal, b_col[hd] - b_row[hd] + i_row[hd], -jnp.inf) for hd in heads]
    inter = [b_col[hd] + m_prev[hd] for hd in heads]
    m_tok = [jnp.maximum(inter[hd], jnp.max(dmat[hd], axis=1, keepdims=True)) for hd in heads]
    for j in range(2 * per_stage, 3 * per_stage):
        adiag = exact_pass(j, adiag)
    offs += cross_block_scores(2)
    s_terms.append(state_update_term(2))
    yield
    p = [(s_qk[hd] * jnp.exp(dmat[hd] - m_tok[hd])).astype(BF16) for hd in heads]
    pv = [_dot(p[hd], vaug[hd]) for hd in heads]
    for j in range(3 * per_stage, SUB):
        adiag = exact_pass(j, adiag)
    offs += cross_block_scores(3)
    s_terms.append(state_update_term(3))
    yield
    for hd in heads:
        tot = jnp.exp(inter[hd] - m_tok[hd]) * qc[hd] + pv[hd]
        den = tot[:, M_DH:M_DH + 1]
        hm_ref[:, hsl[hd]] = tot[:, :M_DH] / jnp.maximum(jnp.abs(den), jnp.exp(-m_tok[hd]))
        caug_ref[hd] = jnp.exp(b_last[hd] + m_prev[hd] - m_new[hd]) * caug[hd] + upd[hd]
        m_ref[hd:hd + 1, :] = jnp.broadcast_to(m_new[hd], (1, LANES))
    yield

    sub_of = lambda idx: lax.shift_right_logical(idx & (CHUNK - 1), SUB.bit_length() - 1)
    rowb = sub_of(lax.broadcasted_iota(jnp.int32, (ts, G_KW), 0))
    colb = sub_of(lax.broadcasted_iota(jnp.int32, (ts, G_KW), 1))
    acc_all, acc_upper = adiag
    upper = acc_upper.reshape(nb, half, G_KW)
    acc_all = acc_all + jnp.stack([jnp.zeros_like(upper), upper], axis=1).reshape(ts, G_KW)
    adiag = jnp.where(rowb == colb, acc_all, 0.0)
    intra = (jnp.concatenate(offs, axis=0) + adiag).astype(BF16)
    q_in = (gq * jnp.exp(bc)).astype(BF16)
    o_intra = []
    for c in range(n_chunks):
        lo, hi = c * CHUNK, (c + 1) * CHUNK
        vbd = jnp.where(same_head_kv, jnp.concatenate([gv_ref[lo:hi, :]] * G_HEADS, axis=0),
                        jnp.zeros((), BF16))
        o_intra.append(_dot(intra[lo:hi], vbd))
    sbd = sbd_ref[...]
    for c in range(n_chunks):
        lo, hi = c * CHUNK, (c + 1) * CHUNK
        hg_ref[lo:hi, :] = o_intra[c] + _dot(q_in[lo:hi], sbd.astype(BF16))
        dcol = jnp.exp(bc_t[:, hi - 1:hi])
        sbd = dcol * sbd + s_terms[c]
    sbd_ref[...] = sbd
    yield


def _head_norm(hv, n_heads, width):
    parts = []
    for hd in range(n_heads):
        seg = hv[:, hd * width:(hd + 1) * width]
        parts.append(seg * lax.rsqrt(jnp.mean(seg * seg, axis=-1, keepdims=True) + EPS))
    return jnp.concatenate(parts, axis=1)


def _gate_heads(hm, hg, mo, gr, gmh, ggh):
    hm = _head_norm(hm, M_HEADS, M_DH) * gmh * _sigmoid(mo)
    hg = _head_norm(hg, G_HEADS, G_DV) * ggh * (gr * _sigmoid(gr))
    return jnp.concatenate([hm, hg], axis=1).astype(BF16)


_POST_WEIGHTS = ("w_out", "g_mlp", "w1", "w2", "g_ple", "w_ple", "w_pg", "g_final")


def _post_stages(x, p, mixed, w, y_ref):
    x1 = x + _dot(mixed, w["w_out"][...])
    n1 = _rms(x1, w["g_mlp"][...]).astype(BF16)
    yield
    n_ff = 4
    ff = D_FF // n_ff
    acts = []
    for j in range(n_ff):
        u = _dot(n1, w["w1"][:, j * ff:(j + 1) * ff])
        acts.append(jnp.square(jnp.maximum(u, 0.0)).astype(BF16))
        yield
    act = jnp.concatenate(acts, axis=1)
    half = D_MODEL // 2
    mlp = []
    for j in range(2):
        mlp.append(_dot(act, w["w2"][:, j * half:(j + 1) * half]))
        yield
    x2 = x1 + jnp.concatenate(mlp, axis=1)
    ple = _dot(p.astype(BF16), w["w_ple"][...])
    n2 = _rms(x2, w["g_ple"][...]).astype(BF16)
    yield
    gate = _sigmoid(_dot(n2, w["w_pg"][...]))
    x3 = x2 + ple * gate
    y_ref[...] = _rms(x3, w["g_final"][...])
    yield


def _post_tok_kernel(x_ref, p_ref, ho_ref, mo_ref, gr_ref, gmh_ref, ggh_ref, wout_ref, gmlp_ref,
                     w1_ref, w2_ref, gple_ref, wple_ref, wpg_ref, gfin_ref, y_ref):
    mixed = _gate_heads(ho_ref[:, :M_WIDTH], ho_ref[:, M_WIDTH:M_WIDTH + G_VW], mo_ref[...], gr_ref[...],
                        gmh_ref[...], ggh_ref[...])
    w = dict(zip(_POST_WEIGHTS, (wout_ref, gmlp_ref, w1_ref, w2_ref, gple_ref, wple_ref, wpg_ref, gfin_ref)))
    _run(_post_stages(x_ref[...], p_ref[...], mixed, w, y_ref))


def _post_tok_call(x2d, p2d, ho, mo, gr, wts):
    n = x2d.shape[0]
    row = lambda width: pl.BlockSpec((n, width), lambda i: (0, 0))
    whole = pl.BlockSpec(memory_space=pltpu.VMEM)
    return pl.pallas_call(
        _post_tok_kernel,
        grid=(1,),
        in_specs=[row(D_MODEL), row(D_PLE), row(ho.shape[1]), row(M_WIDTH), row(G_VW)] + [whole] * 10,
        out_specs=row(D_MODEL),
        out_shape=jax.ShapeDtypeStruct((n, D_MODEL), F32),
        compiler_params=pltpu.CompilerParams(dimension_semantics=("arbitrary",), vmem_limit_bytes=VMEM_LIMIT),
        name="post_tok",
    )(x2d, p2d, ho, mo, gr, wts["g_mhead"], wts["g_ghead"], *[wts[k] for k in _POST_WEIGHTS])


_PK_IN = dict(mq=(0, M_WIDTH), mk=(512, M_WIDTH), mv=(1024, M_WIDTH), gv=(1536, G_VW), n=(2048, M_WIDTH),
              gq=(2560, G_KW), gk=(2816, G_KW), la=(3072, G_KW), gates=(3328, LANES), m=(3456, LANES))
_PK_IN_W = 3584
_PK_OUT = dict(hm=(0, M_WIDTH), hg=(M_WIDTH, G_VW), n=(1024, M_WIDTH), m=(1536, LANES))
_PK_OUT_W = 1664


def _tok_stages(pk_ref, c_ref, s_ref, po_ref, cn_ref, sn_ref):
    nb = c_ref.shape[0]
    x = pk_ref[0]

    def cut(name, lo=0, w=None):
        a = _PK_IN[name][0] + lo
        return x[:, a:a + (w or _PK_IN[name][1])]

    def put(name, lo, val):
        a = _PK_OUT[name][0] + lo
        po_ref[0, :, a:a + val.shape[1]] = val

    po_ref[...] = jnp.zeros(po_ref.shape, F32)
    gates = cut("gates")
    m_all = cut("m")
    for hd in range(M_HEADS):
        cs_ = hd * M_DH
        q, k, v, n_prev = cut("mq", cs_, M_DH), cut("mk", cs_, M_DH), cut("mv", cs_, M_DH), cut("n", cs_, M_DH)
        ig = gates[:, hd:hd + 1]
        lf = gates[:, M_HEADS + hd:M_HEADS + hd + 1]
        m_prev = m_all[:, hd:hd + 1]
        m_new = jnp.maximum(lf + m_prev, ig)
        scale = jnp.exp(lf + m_prev - m_new)
        wk = jnp.exp(ig - m_new)
        s_qk = jnp.sum(q * k, axis=1, keepdims=True) * wk
        den = scale * jnp.sum(q * n_prev, axis=1, keepdims=True) + s_qk
        inv = 1.0 / jnp.maximum(jnp.abs(den), jnp.exp(-m_new))
        put("n", cs_, scale * n_prev + wk * k)
        put("m", hd, m_new)
        q_t = q.T
        kw_t = (k * wk).T
        num_w = s_qk * v
        for b in range(nb):
            c_prev = c_ref[b, hd]
            qc = jnp.sum(q_t[:, b:b + 1] * c_prev, axis=0, keepdims=True)
            po_ref[0, b:b + 1, cs_:cs_ + M_DH] = (scale[b:b + 1] * qc + num_w[b:b + 1]) * inv[b:b + 1]
            cn_ref[b, hd] = scale[b:b + 1] * c_prev + kw_t[:, b:b + 1] * v[b:b + 1]
        yield
    dec = jnp.exp(cut("la"))
    gq, gk = cut("gq"), cut("gk")
    for g in range(G_HEADS):
        ks_, ke_ = g * G_DK, (g + 1) * G_DK
        q, k, d = gq[:, ks_:ke_], gk[:, ks_:ke_], dec[:, ks_:ke_]
        v = cut("gv", g * G_DV, G_DV)
        av = jnp.sum(q * k, axis=1, keepdims=True) * v
        qd_t, k_t, d_t = (q * d).T, k.T, d.T
        for b in range(nb):
            s_prev = s_ref[b, g]
            po_ref[0, b:b + 1, M_WIDTH + g * G_DV:M_WIDTH + (g + 1) * G_DV] = (
                jnp.sum(qd_t[:, b:b + 1] * s_prev, axis=0, keepdims=True) + av[b:b + 1])
            sn_ref[b, g] = d_t[:, b:b + 1] * s_prev + k_t[:, b:b + 1] * v[b:b + 1]
        yield


def _seq_kernel(xa_ref, xb_ref, pb_ref, pk_ref, cs_ref, ss_ref, gmix_ref, wbig_ref, wsmall_ref, bsmall_ref,
                wa2_ref, ba_ref, convw_ref, convb_ref, tril_ref, tri_ref, wred_ref, gmh_ref, ggh_ref,
                wout_ref, gmlp_ref, w1_ref, w2_ref, gple_ref, wple_ref, wpg_ref, gfin_ref,
                y_ref, tail_ref, cout_ref, nout_ref, mout_ref, sout_ref, po_ref, cn_ref, sn_ref,
                cbuf_ref, caug_ref, m_ref, sbd_ref, mixed_ref,
                mq_s, mk_s, mv_s, mo_s, gq_s, gk_s, gv_s, gr_s, gates_s, la_s, hm_s, hg_s,
                *, steps_per_seq, n_blocks):
    tb = xa_ref.shape[0]
    s = pl.program_id(0)
    r = lax.rem(jnp.minimum(s, n_blocks - 1), steps_per_seq)

    @pl.when(s == 0)
    def _():
        mixed_ref[...] = jnp.zeros(mixed_ref.shape, BF16)

    @pl.when(r == 0)
    def _():
        cbuf_ref[0:SUBLANES, :] = jnp.zeros((SUBLANES, QK_CONV), F32)
        caug_ref[...] = jnp.zeros(caug_ref.shape, F32)
        m_ref[...] = jnp.zeros(m_ref.shape, F32)
        sbd_ref[...] = jnp.zeros(sbd_ref.shape, F32)

    slot = lax.rem(s, 2)
    src = dict(zip(_PROJ_NAMES, (mq_s, mk_s, mv_s, mo_s, gq_s, gk_s, gv_s, gr_s, gates_s, la_s)))
    w_in = dict(zip(_PROJ_WEIGHTS, (wbig_ref, wsmall_ref, bsmall_ref, wa2_ref, ba_ref)))
    w_post = dict(zip(_POST_WEIGHTS, (wout_ref, gmlp_ref, w1_ref, w2_ref, gple_ref, wple_ref, wpg_ref, gfin_ref)))

    def front():
        h = _rms(xa_ref[...], gmix_ref[...]).astype(BF16)
        yield from _proj_stages(h, w_in, src, _seq_conv(cbuf_ref, convw_ref, convb_ref, tb))
        gate_sums = _mix_gate_sums(src, tril_ref, tri_ref)
        yield
        yield from _mix_stages(gate_sums, src, wred_ref, caug_ref, m_ref, sbd_ref, hm_s, hg_s)
        mixed_ref[slot] = _gate_heads(hm_s[...], hg_s[...], mo_s[...], gr_s[...], gmh_ref[...], ggh_ref[...])
        yield

    def back():
        yield from _post_stages(xb_ref[...], pb_ref[...], mixed_ref[1 - slot], w_post, y_ref)

    tok = _tok_stages(pk_ref, cs_ref, ss_ref, po_ref, cn_ref, sn_ref)
    _interleave("btfff" "btfff" "btfff" "btff" "btf" "btf" "btf" "btf" "bff", front=front(), back=back(), tok=tok)

    @pl.when(jnp.logical_and(s < n_blocks, r == steps_per_seq - 1))
    def _():
        tail_ref[0] = cbuf_ref[SUBLANES - (CONV_W - 1):SUBLANES, :]
        for hd in range(M_HEADS):
            cout_ref[0, hd] = caug_ref[hd, :, :M_DH]
            nout_ref[0, hd:hd + 1, :] = caug_ref[hd, :, M_DH:].T[0:1, :]
            mout_ref[0, :, hd:hd + 1] = m_ref[hd:hd + 1, 0:1]
        for g in range(G_HEADS):
            sout_ref[0, g] = sbd_ref[g * G_DK:(g + 1) * G_DK, g * G_DV:(g + 1) * G_DV]


def _seq_call(x2d, p2d, tok, wts, consts, *, n_seq, seq_len):
    tb = SEQ_BLOCK
    steps_per_seq = seq_len // tb
    n_blocks = n_seq * steps_per_seq
    tril, tri, wred = consts
    pk, c0, s0 = tok
    nb = c0.shape[0] // n_blocks
    assert nb * n_blocks == c0.shape[0] and nb <= SUBLANES and pk.shape == (n_blocks, SUBLANES, _PK_IN_W)
    front_blk = lambda s: jnp.minimum(s, n_blocks - 1)
    back_blk = lambda s: jnp.maximum(s - 1, 0)
    whole = pl.BlockSpec(memory_space=pltpu.VMEM)
    tok_blk = lambda shape: pl.BlockSpec(shape, lambda s: (front_blk(s),) + (0,) * (len(shape) - 1))
    per_seq = lambda shape: pl.BlockSpec((1,) + shape, lambda s: (front_blk(s) // steps_per_seq,) + (0,) * len(shape))
    scratch = [
        pltpu.VMEM((tb + SUBLANES, QK_CONV), F32),
        pltpu.VMEM((M_HEADS, M_DH, 2 * M_DH), F32),
        pltpu.VMEM((SUBLANES, LANES), F32),
        pltpu.VMEM((G_KW, G_VW), F32),
        pltpu.VMEM((2, tb, D_MODEL), BF16),
    ] + [pltpu.VMEM((tb, _PROJ_WIDTH[k]), _PROJ_DTYPE[k]) for k in _PROJ_NAMES] + [
        pltpu.VMEM((tb, M_WIDTH), F32), pltpu.VMEM((tb, G_VW), F32)]
    return pl.pallas_call(
        functools.partial(_seq_kernel, steps_per_seq=steps_per_seq, n_blocks=n_blocks),
        grid=(n_blocks + 1,),
        in_specs=[pl.BlockSpec((tb, D_MODEL), lambda s: (front_blk(s), 0)),
                  pl.BlockSpec((tb, D_MODEL), lambda s: (back_blk(s), 0)),
                  pl.BlockSpec((tb, D_PLE), lambda s: (back_blk(s), 0)),
                  tok_blk((1, SUBLANES, _PK_IN_W)), tok_blk((nb, M_HEADS, M_DH, M_DH)),
                  tok_blk((nb, G_HEADS, G_DK, G_DV))] + [whole] * 21,
        out_specs=[pl.BlockSpec((tb, D_MODEL), lambda s: (back_blk(s), 0)),
                   per_seq((CONV_W - 1, QK_CONV)), per_seq((M_HEADS, M_DH, M_DH)), per_seq((M_HEADS, M_DH)),
                   per_seq((1, M_HEADS)), per_seq((G_HEADS, G_DK, G_DV)),
                   tok_blk((1, SUBLANES, _PK_OUT_W)), tok_blk((nb, M_HEADS, M_DH, M_DH)),
                   tok_blk((nb, G_HEADS, G_DK, G_DV))],
        out_shape=[jax.ShapeDtypeStruct((n_blocks * tb, D_MODEL), F32),
                   jax.ShapeDtypeStruct((n_seq, CONV_W - 1, QK_CONV), F32),
                   jax.ShapeDtypeStruct((n_seq, M_HEADS, M_DH, M_DH), F32),
                   jax.ShapeDtypeStruct((n_seq, M_HEADS, M_DH), F32),
                   jax.ShapeDtypeStruct((n_seq, 1, M_HEADS), F32),
                   jax.ShapeDtypeStruct((n_seq, G_HEADS, G_DK, G_DV), F32),
                   jax.ShapeDtypeStruct((n_blocks, SUBLANES, _PK_OUT_W), F32),
                   jax.ShapeDtypeStruct(c0.shape, F32), jax.ShapeDtypeStruct(s0.shape, F32)],
        scratch_shapes=scratch,
        compiler_params=pltpu.CompilerParams(dimension_semantics=("arbitrary",), vmem_limit_bytes=VMEM_LIMIT),
        name="seq_fused",
    )(x2d, x2d, p2d, pk, c0, s0, wts["g_mix"], wts["w_big"], wts["w_small"], wts["b_small"], wts["w_a2p"], wts["b_a"],
      wts["conv_w"], wts["conv_b"], tril, tri, wred, wts["g_mhead"], wts["g_ghead"],
      *[wts[k] for k in _POST_WEIGHTS])


_IN_OFFS = tuple(sum(IN_SIZES[:i]) for i in range(len(IN_SIZES) + 1))
_GATES_LO, _GATES_HI, _GA_LO = _IN_OFFS[3], _IN_OFFS[5], _IN_OFFS[9]
_N_SMALL = 2 * M_HEADS + G_RANK
_W_BIG_COLS = _GATES_LO + (_GA_LO - _GATES_HI)
_CAST_STEPS = 8


def _cast_kernel(wout_ref, w1_ref, w2_ref, wple_ref, wpg_ref, o_out, o_w1, o_w2, o_ple, o_pg):
    o_out[...] = wout_ref[...].astype(BF16)
    o_w1[...] = w1_ref[...].astype(BF16)
    o_w2[...] = w2_ref[...].astype(BF16)
    o_ple[...] = wple_ref[...].astype(BF16)
    o_pg[...] = wpg_ref[...].astype(BF16)


def _cast_call(*srcs):
    rows = lambda a: pl.BlockSpec((a.shape[0] // _CAST_STEPS, a.shape[1]), lambda i: (i, 0))
    return pl.pallas_call(
        _cast_kernel,
        grid=(_CAST_STEPS,),
        in_specs=[rows(a) for a in srcs],
        out_specs=[rows(a) for a in srcs],
        out_shape=[jax.ShapeDtypeStruct(a.shape, BF16) for a in srcs],
        compiler_params=pltpu.CompilerParams(dimension_semantics=("arbitrary",), vmem_limit_bytes=VMEM_LIMIT),
        name="cast_weights",
    )(*srcs)


def _cast_in_kernel(wt_ref, big_ref, small_ref):
    piece = M_WIDTH
    for n in range(_W_BIG_COLS // piece):
        dst = n * piece
        src = dst if dst < _GATES_LO else dst + (_GATES_HI - _GATES_LO)
        big_ref[:, dst:dst + piece] = wt_ref[src:src + piece, :].T.astype(BF16)
    small = jnp.concatenate([wt_ref[_GATES_LO:_GATES_HI, :], wt_ref[_GA_LO:, :],
                             jnp.zeros((LANES - _N_SMALL, D_MODEL), F32)], axis=0)
    small_ref[...] = small.T.astype(BF16)


def _cast_in_call(w_in_t):
    whole = pl.BlockSpec(memory_space=pltpu.VMEM)
    assert _GATES_LO % M_WIDTH == 0
    return pl.pallas_call(
        _cast_in_kernel,
        in_specs=[whole], out_specs=[whole, whole],
        out_shape=[jax.ShapeDtypeStruct((D_MODEL, _W_BIG_COLS), BF16),
                   jax.ShapeDtypeStruct((D_MODEL, LANES), BF16)],
        compiler_params=pltpu.CompilerParams(vmem_limit_bytes=VMEM_LIMIT),
        name="cast_w_in",
    )(w_in_t)


def _prep_weights(w_in, conv_w, conv_b, b_gate, w_a2, b_a, g_mhead, g_ghead, w_out, g_mix, g_mlp, w1, w2,
                  g_ple, w_ple, w_pg, g_final):
    w_big, w_small = _cast_in_call(jnp.swapaxes(w_in, 0, 1))
    w_out_b, w1_b, w2_b, w_ple_b, w_pg_b = _cast_call(w_out, w1, w2, w_ple, w_pg)
    b_small = jnp.concatenate([b_gate, jnp.zeros((LANES - 2 * M_HEADS,), F32)])[None]
    w_a2p = jnp.concatenate([jnp.zeros((2 * M_HEADS, G_KW), F32), w_a2,
                             jnp.zeros((LANES - _N_SMALL, G_KW), F32)], axis=0).astype(BF16)
    return dict(
        w_big=w_big, w_small=w_small, b_small=b_small, w_a2p=w_a2p, b_a=b_a[None],
        conv_w=conv_w, conv_b=conv_b[None], g_mix=g_mix[None], g_mhead=g_mhead[None], g_ghead=g_ghead[None],
        w_out=w_out_b, g_mlp=g_mlp[None], w1=w1_b, w2=w2_b,
        g_ple=g_ple[None], w_ple=w_ple_b, w_pg=w_pg_b, g_final=g_final[None])


def _mix_constants(ts):
    t = np.arange(ts)
    tril = t[None, :] <= t[:, None]
    tri = (t[:, None] // CHUNK == t[None, :] // CHUNK) & tril
    rr = np.arange(G_KW)
    wred = ((rr[None, :, None] // G_DK == rr[None, None, :] // G_DK)
            & (rr[None, None, :] % SUB == np.arange(SUB)[:, None, None]))
    return tuple(jnp.asarray(np.asarray(a, dtype=BF16)) for a in (tril, tri, wred))


def kernel(x_prompt, x_sample, p_prompt, p_sample, state_mlstm_C, state_mlstm_n, state_mlstm_m, state_conv,
           state_gla_S, w_in, conv_w, conv_b, b_gate, w_a2, b_a, g_mhead, g_ghead, w_out, g_mix, g_mlp, w1,
           w2, g_ple, w_ple, w_pg, g_final):
    assert w_in.shape[0] == 1, "single-layer trunk"
    n_seq, seq_len, _ = x_prompt.shape
    n_tok = x_sample.shape[0]
    assert x_sample.shape[1] == 1 and seq_len % SEQ_BLOCK == 0
    wts = _prep_weights(w_in[0], conv_w[0], conv_b[0], b_gate[0], w_a2[0], b_a[0], g_mhead[0], g_ghead[0],
                        w_out[0], g_mix[0], g_mlp[0], w1[0], w2[0], g_ple[0], w_ple[0], w_pg[0], g_final)

    xs = x_sample.reshape(n_tok, D_MODEL)
    buf = state_conv[0]
    packed, mo_s, gr_s, raw_s = _proj_tok_call(xs, (buf[:, 0], buf[:, 1], buf[:, 2]),
                                               state_mlstm_n[0].reshape(n_tok, M_WIDTH), state_mlstm_m[0], wts)
    n_blocks = n_seq * (seq_len // SEQ_BLOCK)
    per_step = n_tok // n_blocks
    assert per_step * n_blocks == n_tok
    packed = jnp.pad(packed.reshape(n_blocks, per_step, _PK_IN_W), ((0, 0), (0, SUBLANES - per_step), (0, 0)))

    y_p, conv_p, c_p, n_p, m_p, s_p, po, c_s, s_s = _seq_call(
        x_prompt.reshape(n_seq * seq_len, D_MODEL), p_prompt[0].reshape(n_seq * seq_len, D_PLE),
        (packed, state_mlstm_C[0], state_gla_S[0]), wts, _mix_constants(SEQ_BLOCK), n_seq=n_seq, seq_len=seq_len)
    po = po[:, :per_step].reshape(n_tok, _PK_OUT_W)
    n_s = po[:, _PK_OUT["n"][0]:_PK_OUT["n"][0] + M_WIDTH]
    m_s = po[:, _PK_OUT["m"][0]:_PK_OUT["m"][0] + M_HEADS]

    y_s = _post_tok_call(xs, p_sample[0].reshape(n_tok, D_PLE), po, mo_s, gr_s, wts)

    return (y_p.reshape(n_seq, seq_len, D_MODEL),
            y_s.reshape(n_tok, 1, D_MODEL),
            c_p[None],
            n_p[None],
            m_p.reshape(1, n_seq, M_HEADS),
            conv_p[None],
            s_p[None],
            c_s[None],
            n_s.reshape(1, n_tok, M_HEADS, M_DH),
            m_s.reshape(1, n_tok, M_HEADS),
            jnp.stack([buf[:, 1], buf[:, 2], raw_s], axis=1)[None],
            s_s[None])
```

```python
import functools

import jax
import jax.numpy as jnp
import numpy as np
from jax import lax
from jax.experimental import pallas as pl
from jax.experimental.pallas import tpu as pltpu

D_MODEL = 1024
M_HEADS = 4
M_DH = 128
M_WIDTH = M_HEADS * M_DH
G_HEADS = 4
G_DK = 64
G_DV = 128
G_KW = G_HEADS * G_DK
G_VW = G_HEADS * G_DV
G_RANK = 16
G_TAU = 16.0
CONV_W = 4
QK_CONV = 2 * M_WIDTH
D_FF = 4 * D_MODEL
D_PLE = 256
CHUNK = 64
SUB = 16
EPS = 1e-6
IN_SIZES = (QK_CONV, M_WIDTH, M_WIDTH, M_HEADS, M_HEADS, G_KW, G_KW, G_VW, G_VW, G_RANK)

LANES = 128
SUBLANES = 8
VMEM_LIMIT = 60 * 1024 * 1024
SEQ_BLOCK = 256

F32 = jnp.float32
BF16 = jnp.bfloat16
NEG = -1e30


def _rms(x, g):
    return x * lax.rsqrt(jnp.mean(x * x, axis=-1, keepdims=True) + EPS) * g


def _log_sigmoid(x):
    return jnp.minimum(x, 0.0) - jnp.log(1.0 + jnp.exp(-jnp.abs(x)))


def _sigmoid(x):
    return 0.5 * jnp.tanh(0.5 * x) + 0.5


def _div_pow2(idx, d):
    assert d & (d - 1) == 0
    return lax.shift_right_logical(idx, d.bit_length() - 1)


def _dot(a, b):
    return jnp.dot(a, b, preferred_element_type=F32)


def _dot_nt(a, b):
    return lax.dot_general(a, b, (((1,), (1,)), ((), ())), preferred_element_type=F32)


def _split3(x):
    hi = x.astype(BF16)
    r1 = x - hi.astype(F32)
    mid = r1.astype(BF16)
    lo = (r1 - mid.astype(F32)).astype(BF16)
    return hi, mid, lo


def _cumsum_dot(tri, x):
    hi, mid, lo = _split3(x)
    return _dot(tri, hi) + _dot(tri, mid) + _dot(tri, lo)


def _interleave(pattern, **streams):
    by_letter = {name[0]: gen for name, gen in streams.items()}
    for letter in pattern:
        next(by_letter[letter], None)
    for gen in by_letter.values():
        _run(gen)


def _run(gen):
    for _ in gen:
        pass


def _proj_stages(h, w, sinks, conv_piece):
    width = M_WIDTH

    def put_q(raw):
        sinks["mq"][...] = conv_piece(raw, 0).astype(sinks["mq"].dtype)

    def put_k(raw):
        sinks["mk"][...] = conv_piece(raw, M_WIDTH) * (M_DH ** -0.5)

    def put_mv(raw):
        sinks["mv"][...] = raw.astype(sinks["mv"].dtype)

    def put_mo(raw):
        sinks["mo"][...] = raw

    def put_gqk(raw):
        sinks["gq"][...] = raw[:, :G_KW] * (G_DK ** -0.5)
        sinks["gk"][...] = raw[:, G_KW:]

    def put_gv(raw):
        sinks["gv"][...] = raw.astype(sinks["gv"].dtype)

    def put_gr(raw):
        sinks["gr"][...] = raw

    def put_small(small):
        g = small + w["b_small"][...]
        lane = lax.broadcasted_iota(jnp.int32, g.shape, 1)
        sinks["gates"][...] = jnp.where(lane < M_HEADS, g, _log_sigmoid(g))
        z = _dot(small.astype(BF16), w["w_a2p"][...]) + w["b_a"][...]
        sinks["la"][...] = _log_sigmoid(z) * (1.0 / G_TAU)

    epilogues = [put_small, put_q, put_k, put_mv, put_mo, put_gqk, put_gv, put_gr]
    assert w["w_big"].shape[1] == width * (len(epilogues) - 1)
    pending = (_dot(h, w["w_small"][...]), epilogues[0])
    yield
    for n, epi in enumerate(epilogues[1:]):
        cur = _dot(h, w["w_big"][:, n * width:(n + 1) * width])
        pending[1](pending[0])
        pending = (cur, epi)
        yield
    pending[1](pending[0])
    yield


def _seq_conv(cbuf_ref, convw_ref, convb_ref, tb):
    cw = convw_ref[...]

    def conv_piece(raw, lo):
        cols = slice(lo, lo + raw.shape[1])
        cbuf_ref[SUBLANES:SUBLANES + tb, cols] = raw
        y = convb_ref[:, cols] + raw * cw[3:4, cols]
        for j in range(CONV_W - 1):
            off = SUBLANES - (CONV_W - 1) + j
            y = y + cbuf_ref[off:off + tb, cols] * cw[j:j + 1, cols]
        cbuf_ref[0:SUBLANES, cols] = cbuf_ref[tb:tb + SUBLANES, cols]
        return y * _sigmoid(y)

    return conv_piece


def _tok_conv(b0_ref, b1_ref, b2_ref, raw_ref, convw_ref, convb_ref):
    cw = convw_ref[...]

    def conv_piece(raw, lo):
        cols = slice(lo, lo + raw.shape[1])
        raw_ref[:, cols] = raw
        y = (convb_ref[:, cols] + b0_ref[:, cols] * cw[0:1, cols] + b1_ref[:, cols] * cw[1:2, cols]
             + b2_ref[:, cols] * cw[2:3, cols] + raw * cw[3:4, cols])
        return y * _sigmoid(y)

    return conv_piece


_PROJ_NAMES = ("mq", "mk", "mv", "mo", "gq", "gk", "gv", "gr", "gates", "la")
_PROJ_WIDTH = dict(mq=M_WIDTH, mk=M_WIDTH, mv=M_WIDTH, mo=M_WIDTH, gq=G_KW, gk=G_KW, gv=G_VW, gr=G_VW,
                   gates=LANES, la=G_KW)
_PROJ_DTYPE = dict(mq=BF16, mk=F32, mv=BF16, mo=F32, gq=F32, gk=F32, gv=BF16, gr=F32, gates=F32, la=F32)
_PROJ_WEIGHTS = ("w_big", "w_small", "b_small", "w_a2p", "b_a")


class _ColumnSink:
    def __init__(self, ref, lo, width, dtype):
        self.ref, self.lo, self.width, self.dtype = ref, lo, width, dtype

    def __setitem__(self, idx, val):
        self.ref[:, self.lo:self.lo + self.width] = val.astype(F32)


def _proj_tok_kernel(x_ref, b0_ref, b1_ref, b2_ref, n_ref, m_ref, gmix_ref, wbig_ref, wsmall_ref, bsmall_ref,
                     wa2_ref, ba_ref, convw_ref, convb_ref, pk_ref, mo_ref, gr_ref, raw_ref):
    sinks = {k: _ColumnSink(pk_ref, lo, width, _PROJ_DTYPE[k]) for k, (lo, width) in _PK_IN.items()
             if k in _PROJ_DTYPE}
    sinks.update(mo=mo_ref, gr=gr_ref)
    n_lo, n_w = _PK_IN["n"]
    m_lo, m_w = _PK_IN["m"]
    pk_ref[:, n_lo:n_lo + n_w] = n_ref[...]
    pk_ref[:, m_lo:m_lo + m_w] = jnp.zeros((pk_ref.shape[0], m_w), F32)
    pk_ref[:, m_lo:m_lo + M_HEADS] = m_ref[...]
    w = dict(zip(_PROJ_WEIGHTS, (wbig_ref, wsmall_ref, bsmall_ref, wa2_ref, ba_ref)))
    h = _rms(x_ref[...], gmix_ref[...]).astype(BF16)
    _run(_proj_stages(h, w, sinks, _tok_conv(b0_ref, b1_ref, b2_ref, raw_ref, convw_ref, convb_ref)))


def _proj_tok_call(x2d, conv_rows, n0, m0, wts):
    n = x2d.shape[0]
    row = lambda width: pl.BlockSpec((n, width), lambda i: (0, 0))
    whole = pl.BlockSpec(memory_space=pltpu.VMEM)
    return pl.pallas_call(
        _proj_tok_kernel,
        grid=(1,),
        in_specs=[row(D_MODEL)] + [row(QK_CONV)] * 3 + [row(M_WIDTH), row(M_HEADS)] + [whole] * 8,
        out_specs=[row(_PK_IN_W), row(M_WIDTH), row(G_VW), row(QK_CONV)],
        out_shape=[jax.ShapeDtypeStruct((n, _PK_IN_W), F32), jax.ShapeDtypeStruct((n, M_WIDTH), F32),
                   jax.ShapeDtypeStruct((n, G_VW), F32), jax.ShapeDtypeStruct((n, QK_CONV), F32)],
        compiler_params=pltpu.CompilerParams(dimension_semantics=("arbitrary",), vmem_limit_bytes=VMEM_LIMIT),
        name="proj_tok",
    )(x2d, *conv_rows, n0, m0, wts["g_mix"], wts["w_big"], wts["w_small"], wts["b_small"], wts["w_a2p"],
      wts["b_a"], wts["conv_w"], wts["conv_b"])


def _mix_gate_sums(src, tril_ref, tri_ref):
    gates = src["gates"][...]
    bcum = _cumsum_dot(tril_ref[...], gates)
    bc = _cumsum_dot(tri_ref[...], src["la"][...])
    return gates, bcum, bc, gates.T, bcum.T, bc.T


def _mix_stages(gate_sums, src, wred_ref, caug_ref, m_ref, sbd_ref, hm_ref, hg_ref):
    ts = src["mq"].shape[0]
    n_chunks = ts // CHUNK
    gates, bcum, bc, gates_t, bcum_t, bc_t = gate_sums
    gq = src["gq"][...]
    gk = src["gk"][...]
    gv_ref = src["gv"]

    nb = ts // SUB
    half = SUB // 2
    assert half == SUBLANES
    q3 = gq.reshape(nb, SUB, G_KW)
    k3 = gk.reshape(nb, SUB, G_KW)
    bc3 = bc.reshape(nb, SUB, G_KW)
    tl = lax.broadcasted_iota(jnp.int32, (nb, SUB, G_KW), 1)
    q3u = gq.reshape(nb, 2, half, G_KW)[:, 1]
    bc3u = bc.reshape(nb, 2, half, G_KW)[:, 1]
    tlu = lax.broadcasted_iota(jnp.int32, (nb, half, G_KW), 1) + half

    def exact_pass(j, acc):
        acc_all, acc_upper = acc
        if j < half:
            arg = jnp.where(tl >= j, bc3 - bc3[:, j:j + 1, :], NEG)
            e = (q3 * k3[:, j:j + 1, :] * jnp.exp(arg)).reshape(ts, G_KW)
            return acc_all + _dot(e.astype(BF16), wred_ref[j]), acc_upper
        arg = jnp.where(tlu >= j, bc3u - bc3[:, j:j + 1, :], NEG)
        e = (q3u * k3[:, j:j + 1, :] * jnp.exp(arg)).reshape(ts // 2, G_KW)
        return acc_all, acc_upper + _dot(e.astype(BF16), wred_ref[j])

    krow = lax.broadcasted_iota(jnp.int32, (CHUNK, G_KW), 0)
    same_head_kk = (_div_pow2(lax.broadcasted_iota(jnp.int32, (G_KW, G_KW), 0), G_DK)
                    == _div_pow2(lax.broadcasted_iota(jnp.int32, (G_KW, G_KW), 1), G_DK))
    same_head_kv = (_div_pow2(lax.broadcasted_iota(jnp.int32, (G_KW, G_VW), 0), G_DK)
                    == _div_pow2(lax.broadcasted_iota(jnp.int32, (G_KW, G_VW), 1), G_DV))

    def cross_block_scores(c):
        lo = c * CHUNK
        bc_c = bc[lo:lo + CHUNK]
        k_c = gk[lo:lo + CHUNK]
        out = [jnp.zeros((SUB, G_KW), F32)]
        for i in range(1, CHUNK // SUB):
            r0 = lo + i * SUB
            r_i = bc[r0 - 1:r0, :]
            qi = (gq[r0:r0 + SUB] * jnp.exp(bc[r0:r0 + SUB] - r_i)).astype(BF16)
            ki = (k_c * jnp.exp(jnp.where(krow < i * SUB, r_i - bc_c, NEG))).astype(BF16)
            kbd = jnp.where(same_head_kk, jnp.concatenate([ki] * G_HEADS, axis=0), jnp.zeros((), BF16))
            out.append(_dot_nt(qi, kbd))
        return out

    def state_update_term(c):
        lo, hi = c * CHUNK, (c + 1) * CHUNK
        b_end = bc[hi - 1:hi, :]
        k_out_t = (gk[lo:hi] * jnp.exp(b_end - bc[lo:hi])).T.astype(BF16)
        rows = []
        for g in range(G_HEADS):
            blk = _dot(k_out_t[g * G_DK:(g + 1) * G_DK], gv_ref[lo:hi, g * G_DV:(g + 1) * G_DV])
            zero = jnp.zeros((G_DK, G_DV), F32)
            rows.append(jnp.concatenate([zero] * g + [blk] + [zero] * (G_HEADS - 1 - g), axis=1))
        return jnp.concatenate(rows, axis=0)

    adiag = (jnp.zeros((ts, G_KW), F32), jnp.zeros((ts // 2, G_KW), F32))
    per_stage = SUB // 4
    assert n_chunks == 4
    offs = []
    s_terms = []

    causal = (lax.broadcasted_iota(jnp.int32, (ts, ts), 0) >= lax.broadcasted_iota(jnp.int32, (ts, ts), 1))
    ones_col = (lax.broadcasted_iota(jnp.int32, (ts, LANES), 1) == 0).astype(BF16)
    heads = range(M_HEADS)
    hsl = [slice(hd * M_DH, (hd + 1) * M_DH) for hd in heads]
    b_col = [bcum[:, M_HEADS + hd:M_HEADS + hd + 1] for hd in heads]
    b_row = [bcum_t[M_HEADS + hd:M_HEADS + hd + 1, :] for hd in heads]
    i_row = [gates_t[hd:hd + 1, :] for hd in heads]
    m_prev = [m_ref[hd:hd + 1, 0:1] for hd in heads]
    q = [src["mq"][:, hsl[hd]] for hd in heads]
    i_col = [gates[:, hd:hd + 1] for hd in heads]
    k = [src["mk"][:, hsl[hd]] for hd in heads]
    vaug = [jnp.concatenate([src["mv"][:, hsl[hd]], ones_col], axis=1) for hd in heads]
    caug = [caug_ref[hd] for hd in heads]
    hr = ts // 2
    row_halves = ((slice(0, hr), slice(0, hr)), (slice(hr, ts), slice(0, ts)))
    s_qk = [[_dot_nt(q[hd][rows], k[hd][keys].astype(BF16)) for rows, keys in row_halves] for hd in heads]
    qc = [_dot(q[hd], caug[hd].astype(BF16)) for hd in heads]
    for j in range(0, per_stage):
        adiag = exact_pass(j, adiag)
    offs += cross_block_scores(0)
    s_terms.append(state_update_term(0))
    yield
    b_last = [b_col[hd][ts - 1:ts, :] for hd in heads]
    dec = [b_last[hd] - b_col[hd] + i_col[hd] for hd in heads]
    m_new = [jnp.maximum(b_last[hd] + m_prev[hd], jnp.max(dec[hd], axis=0, keepdims=True)) for hd in heads]
    kw_t = [(k[hd] * jnp.exp(dec[hd] - m_new[hd])).T.astype(BF16) for hd in heads]
    upd = [_dot(kw_t[hd], vaug[hd]) for hd in heads]
    for j in range(per_stage, 2 * per_stage):
        adiag = exact_pass(j, adiag)
    offs += cross_block_scores(1)
    s_terms.append(state_update_term(1))
    yield
    dmat = [[jnp.where(causal[rows, keys], b_col[hd][rows] - b_row[hd][:, keys] + i_row[hd][:, keys], -jnp.inf)
             for rows, keys in row_halves] for hd in heads]
    inter = [b_col[hd] + m_prev[hd] for hd in heads]
    m_tok = [jnp.maximum(inter[hd], jnp.concatenate(
        [jnp.max(d, axis=1, keepdims=True) for d in dmat[hd]], axis=0)) for hd in heads]
    for j in range(2 * per_stage, 3 * per_stage):
        adiag = exact_pass(j, adiag)
    offs += cross_block_scores(2)
    s_terms.append(state_update_term(2))
    yield
    pv = []
    for hd in heads:
        parts = []
        for (rows, keys), s_h, d_h in zip(row_halves, s_qk[hd], dmat[hd]):
            p_h = (s_h * jnp.exp(d_h - m_tok[hd][rows])).astype(BF16)
            parts.append(_dot(p_h, vaug[hd][keys]))
        pv.append(jnp.concatenate(parts, axis=0))
    for j in range(3 * per_stage, SUB):
        adiag = exact_pass(j, adiag)
    offs += cross_block_scores(3)
    s_terms.append(state_update_term(3))
    yield
    for hd in heads:
        tot = jnp.exp(inter[hd] - m_tok[hd]) * qc[hd] + pv[hd]
        den = tot[:, M_DH:M_DH + 1]
        hm_ref[:, hsl[hd]] = tot[:, :M_DH] / jnp.maximum(jnp.abs(den), jnp.exp(-m_tok[hd]))
        caug_ref[hd] = jnp.exp(b_last[hd] + m_prev[hd] - m_new[hd]) * caug[hd] + upd[hd]
        m_ref[hd:hd + 1, :] = jnp.broadcast_to(m_new[hd], (1, LANES))
    yield

    sub_of = lambda idx: lax.shift_right_logical(idx & (CHUNK - 1), SUB.bit_length() - 1)
    rowb = sub_of(lax.broadcasted_iota(jnp.int32, (ts, G_KW), 0))
    colb = sub_of(lax.broadcasted_iota(jnp.int32, (ts, G_KW), 1))
    acc_all, acc_upper = adiag
    upper = acc_upper.reshape(nb, half, G_KW)
    acc_all = acc_all + jnp.stack([jnp.zeros_like(upper), upper], axis=1).reshape(ts, G_KW)
    adiag = jnp.where(rowb == colb, acc_all, 0.0)
    intra = (jnp.concatenate(offs, axis=0) + adiag).astype(BF16)
    q_in = (gq * jnp.exp(bc)).astype(BF16)
    o_intra = []
    for c in range(n_chunks):
        lo, hi = c * CHUNK, (c + 1) * CHUNK
        vbd = jnp.where(same_head_kv, jnp.concatenate([gv_ref[lo:hi, :]] * G_HEADS, axis=0),
                        jnp.zeros((), BF16))
        o_intra.append(_dot(intra[lo:hi], vbd))
    sbd = sbd_ref[...]
    for c in range(n_chunks):
        lo, hi = c * CHUNK, (c + 1) * CHUNK
        hg_ref[lo:hi, :] = o_intra[c] + _dot(q_in[lo:hi], sbd.astype(BF16))
        dcol = jnp.exp(bc_t[:, hi - 1:hi])
        sbd = dcol * sbd + s_terms[c]
    sbd_ref[...] = sbd
    yield


def _head_norm(hv, n_heads, width):
    parts = []
    for hd in range(n_heads):
        seg = hv[:, hd * width:(hd + 1) * width]
        parts.append(seg * lax.rsqrt(jnp.mean(seg * seg, axis=-1, keepdims=True) + EPS))
    return jnp.concatenate(parts, axis=1)


def _gate_heads(hm, hg, mo, gr, gmh, ggh):
    hm = _head_norm(hm, M_HEADS, M_DH) * gmh * _sigmoid(mo)
    hg = _head_norm(hg, G_HEADS, G_DV) * ggh * (gr * _sigmoid(gr))
    return jnp.concatenate([hm, hg], axis=1).astype(BF16)


_POST_WEIGHTS = ("w_out", "g_mlp", "w1", "w2", "g_ple", "w_ple", "w_pg", "g_final")


def _post_stages(x, p, mixed, w, y_ref):
    x1 = x + _dot(mixed, w["w_out"][...])
    n1 = _rms(x1, w["g_mlp"][...]).astype(BF16)
    yield
    n_ff = 4
    ff = D_FF // n_ff
    acts = []
    for j in range(n_ff):
        u = _dot(n1, w["w1"][:, j * ff:(j + 1) * ff])
        acts.append(jnp.square(jnp.maximum(u, 0.0)).astype(BF16))
        yield
    act = jnp.concatenate(acts, axis=1)
    half = D_MODEL // 2
    mlp = []
    for j in range(2):
        mlp.append(_dot(act, w["w2"][:, j * half:(j + 1) * half]))
        yield
    x2 = x1 + jnp.concatenate(mlp, axis=1)
    ple = _dot(p.astype(BF16), w["w_ple"][...])
    n2 = _rms(x2, w["g_ple"][...]).astype(BF16)
    yield
    gate = _sigmoid(_dot(n2, w["w_pg"][...]))
    x3 = x2 + ple * gate
    y_ref[...] = _rms(x3, w["g_final"][...])
    yield


def _post_tok_kernel(x_ref, p_ref, ho_ref, mo_ref, gr_ref, gmh_ref, ggh_ref, wout_ref, gmlp_ref,
                     w1_ref, w2_ref, gple_ref, wple_ref, wpg_ref, gfin_ref, y_ref):
    mixed = _gate_heads(ho_ref[:, :M_WIDTH], ho_ref[:, M_WIDTH:M_WIDTH + G_VW], mo_ref[...], gr_ref[...],
                        gmh_ref[...], ggh_ref[...])
    w = dict(zip(_POST_WEIGHTS, (wout_ref, gmlp_ref, w1_ref, w2_ref, gple_ref, wple_ref, wpg_ref, gfin_ref)))
    _run(_post_stages(x_ref[...], p_ref[...], mixed, w, y_ref))


def _post_tok_call(x2d, p2d, ho, mo, gr, wts):
    n = x2d.shape[0]
    row = lambda width: pl.BlockSpec((n, width), lambda i: (0, 0))
    whole = pl.BlockSpec(memory_space=pltpu.VMEM)
    return pl.pallas_call(
        _post_tok_kernel,
        grid=(1,),
        in_specs=[row(D_MODEL), row(D_PLE), row(ho.shape[1]), row(M_WIDTH), row(G_VW)] + [whole] * 10,
        out_specs=row(D_MODEL),
        out_shape=jax.ShapeDtypeStruct((n, D_MODEL), F32),
        compiler_params=pltpu.CompilerParams(dimension_semantics=("arbitrary",), vmem_limit_bytes=VMEM_LIMIT),
        name="post_tok",
    )(x2d, p2d, ho, mo, gr, wts["g_mhead"], wts["g_ghead"], *[wts[k] for k in _POST_WEIGHTS])


_PK_IN = dict(mq=(0, M_WIDTH), mk=(512, M_WIDTH), mv=(1024, M_WIDTH), gv=(1536, G_VW), n=(2048, M_WIDTH),
              gq=(2560, G_KW), gk=(2816, G_KW), la=(3072, G_KW), gates=(3328, LANES), m=(3456, LANES))
_PK_IN_W = 3584
_PK_OUT = dict(hm=(0, M_WIDTH), hg=(M_WIDTH, G_VW), n=(1024, M_WIDTH), m=(1536, LANES))
_PK_OUT_W = 1664


def _tok_stages(pk_ref, c_ref, s_ref, po_ref, cn_ref, sn_ref):
    nb = c_ref.shape[0]
    x = pk_ref[0]

    def cut(name, lo=0, w=None):
        a = _PK_IN[name][0] + lo
        return x[:, a:a + (w or _PK_IN[name][1])]

    def put(name, lo, val):
        a = _PK_OUT[name][0] + lo
        po_ref[0, :, a:a + val.shape[1]] = val

    po_ref[...] = jnp.zeros(po_ref.shape, F32)
    gates = cut("gates")
    m_all = cut("m")
    for hd in range(M_HEADS):
        cs_ = hd * M_DH
        q, k, v, n_prev = cut("mq", cs_, M_DH), cut("mk", cs_, M_DH), cut("mv", cs_, M_DH), cut("n", cs_, M_DH)
        ig = gates[:, hd:hd + 1]
        lf = gates[:, M_HEADS + hd:M_HEADS + hd + 1]
        m_prev = m_all[:, hd:hd + 1]
        m_new = jnp.maximum(lf + m_prev, ig)
        scale = jnp.exp(lf + m_prev - m_new)
        wk = jnp.exp(ig - m_new)
        s_qk = jnp.sum(q * k, axis=1, keepdims=True) * wk
        den = scale * jnp.sum(q * n_prev, axis=1, keepdims=True) + s_qk
        inv = 1.0 / jnp.maximum(jnp.abs(den), jnp.exp(-m_new))
        put("n", cs_, scale * n_prev + wk * k)
        put("m", hd, m_new)
        q_t = q.T
        kw_t = (k * wk).T
        num_w = s_qk * v
        for b in range(nb):
            c_prev = c_ref[b, hd]
            qc = jnp.sum(q_t[:, b:b + 1] * c_prev, axis=0, keepdims=True)
            po_ref[0, b:b + 1, cs_:cs_ + M_DH] = (scale[b:b + 1] * qc + num_w[b:b + 1]) * inv[b:b + 1]
            cn_ref[b, hd] = scale[b:b + 1] * c_prev + kw_t[:, b:b + 1] * v[b:b + 1]
        yield
    dec = jnp.exp(cut("la"))
    gq, gk = cut("gq"), cut("gk")
    for g in range(G_HEADS):
        ks_, ke_ = g * G_DK, (g + 1) * G_DK
        q, k, d = gq[:, ks_:ke_], gk[:, ks_:ke_], dec[:, ks_:ke_]
        v = cut("gv", g * G_DV, G_DV)
        av = jnp.sum(q * k, axis=1, keepdims=True) * v
        qd_t, k_t, d_t = (q * d).T, k.T, d.T
        for b in range(nb):
            s_prev = s_ref[b, g]
            po_ref[0, b:b + 1, M_WIDTH + g * G_DV:M_WIDTH + (g + 1) * G_DV] = (
                jnp.sum(qd_t[:, b:b + 1] * s_prev, axis=0, keepdims=True) + av[b:b + 1])
            sn_ref[b, g] = d_t[:, b:b + 1] * s_prev + k_t[:, b:b + 1] * v[b:b + 1]
        yield


def _seq_kernel(xa_ref, xb_ref, pb_ref, pk_ref, cs_ref, ss_ref, gmix_ref, wbig_ref, wsmall_ref, bsmall_ref,
                wa2_ref, ba_ref, convw_ref, convb_ref, tril_ref, tri_ref, wred_ref, gmh_ref, ggh_ref,
                wout_ref, gmlp_ref, w1_ref, w2_ref, gple_ref, wple_ref, wpg_ref, gfin_ref,
                y_ref, tail_ref, cout_ref, nout_ref, mout_ref, sout_ref, po_ref, cn_ref, sn_ref,
                cbuf_ref, caug_ref, m_ref, sbd_ref, mixed_ref,
                mq_s, mk_s, mv_s, mo_s, gq_s, gk_s, gv_s, gr_s, gates_s, la_s, hm_s, hg_s,
                *, steps_per_seq, n_blocks):
    tb = xa_ref.shape[0]
    s = pl.program_id(0)
    r = lax.rem(jnp.minimum(s, n_blocks - 1), steps_per_seq)

    @pl.when(s == 0)
    def _():
        mixed_ref[...] = jnp.zeros(mixed_ref.shape, BF16)

    @pl.when(r == 0)
    def _():
        cbuf_ref[0:SUBLANES, :] = jnp.zeros((SUBLANES, QK_CONV), F32)
        caug_ref[...] = jnp.zeros(caug_ref.shape, F32)
        m_ref[...] = jnp.zeros(m_ref.shape, F32)
        sbd_ref[...] = jnp.zeros(sbd_ref.shape, F32)

    slot = lax.rem(s, 2)
    src = dict(zip(_PROJ_NAMES, (mq_s, mk_s, mv_s, mo_s, gq_s, gk_s, gv_s, gr_s, gates_s, la_s)))
    w_in = dict(zip(_PROJ_WEIGHTS, (wbig_ref, wsmall_ref, bsmall_ref, wa2_ref, ba_ref)))
    w_post = dict(zip(_POST_WEIGHTS, (wout_ref, gmlp_ref, w1_ref, w2_ref, gple_ref, wple_ref, wpg_ref, gfin_ref)))

    def front():
        h = _rms(xa_ref[...], gmix_ref[...]).astype(BF16)
        yield from _proj_stages(h, w_in, src, _seq_conv(cbuf_ref, convw_ref, convb_ref, tb))
        gate_sums = _mix_gate_sums(src, tril_ref, tri_ref)
        yield
        yield from _mix_stages(gate_sums, src, wred_ref, caug_ref, m_ref, sbd_ref, hm_s, hg_s)
        mixed_ref[slot] = _gate_heads(hm_s[...], hg_s[...], mo_s[...], gr_s[...], gmh_ref[...], ggh_ref[...])
        yield

    def back():
        yield from _post_stages(xb_ref[...], pb_ref[...], mixed_ref[1 - slot], w_post, y_ref)

    tok = _tok_stages(pk_ref, cs_ref, ss_ref, po_ref, cn_ref, sn_ref)
    _interleave("btfff" "btfff" "btfff" "btff" "btf" "btf" "btf" "btf" "bff", front=front(), back=back(), tok=tok)

    @pl.when(jnp.logical_and(s < n_blocks, r == steps_per_seq - 1))
    def _():
        tail_ref[0] = cbuf_ref[SUBLANES - (CONV_W - 1):SUBLANES, :]
        for hd in range(M_HEADS):
            cout_ref[0, hd] = caug_ref[hd, :, :M_DH]
            nout_ref[0, hd:hd + 1, :] = caug_ref[hd, :, M_DH:].T[0:1, :]
            mout_ref[0, :, hd:hd + 1] = m_ref[hd:hd + 1, 0:1]
        for g in range(G_HEADS):
            sout_ref[0, g] = sbd_ref[g * G_DK:(g + 1) * G_DK, g * G_DV:(g + 1) * G_DV]


def _seq_call(x2d, p2d, tok, wts, consts, *, n_seq, seq_len):
    tb = SEQ_BLOCK
    steps_per_seq = seq_len // tb
    n_blocks = n_seq * steps_per_seq
    tril, tri, wred = consts
    pk, c0, s0 = tok
    nb = c0.shape[0] // n_blocks
    assert nb * n_blocks == c0.shape[0] and nb <= SUBLANES and pk.shape == (n_blocks, SUBLANES, _PK_IN_W)
    front_blk = lambda s: jnp.minimum(s, n_blocks - 1)
    back_blk = lambda s: jnp.maximum(s - 1, 0)
    whole = pl.BlockSpec(memory_space=pltpu.VMEM)
    tok_blk = lambda shape: pl.BlockSpec(shape, lambda s: (front_blk(s),) + (0,) * (len(shape) - 1))
    per_seq = lambda shape: pl.BlockSpec((1,) + shape, lambda s: (front_blk(s) // steps_per_seq,) + (0,) * len(shape))
    scratch = [
        pltpu.VMEM((tb + SUBLANES, QK_CONV), F32),
        pltpu.VMEM((M_HEADS, M_DH, 2 * M_DH), F32),
        pltpu.VMEM((SUBLANES, LANES), F32),
        pltpu.VMEM((G_KW, G_VW), F32),
        pltpu.VMEM((2, tb, D_MODEL), BF16),
    ] + [pltpu.VMEM((tb, _PROJ_WIDTH[k]), _PROJ_DTYPE[k]) for k in _PROJ_NAMES] + [
        pltpu.VMEM((tb, M_WIDTH), F32), pltpu.VMEM((tb, G_VW), F32)]
    return pl.pallas_call(
        functools.partial(_seq_kernel, steps_per_seq=steps_per_seq, n_blocks=n_blocks),
        grid=(n_blocks + 1,),
        in_specs=[pl.BlockSpec((tb, D_MODEL), lambda s: (front_blk(s), 0)),
                  pl.BlockSpec((tb, D_MODEL), lambda s: (back_blk(s), 0)),
                  pl.BlockSpec((tb, D_PLE), lambda s: (back_blk(s), 0)),
                  tok_blk((1, SUBLANES, _PK_IN_W)), tok_blk((nb, M_HEADS, M_DH, M_DH)),
                  tok_blk((nb, G_HEADS, G_DK, G_DV))] + [whole] * 21,
        out_specs=[pl.BlockSpec((tb, D_MODEL), lambda s: (back_blk(s), 0)),
                   per_seq((CONV_W - 1, QK_CONV)), per_seq((M_HEADS, M_DH, M_DH)), per_seq((M_HEADS, M_DH)),
                   per_seq((1, M_HEADS)), per_seq((G_HEADS, G_DK, G_DV)),
                   tok_blk((1, SUBLANES, _PK_OUT_W)), tok_blk((nb, M_HEADS, M_DH, M_DH)),
                   tok_blk((nb, G_HEADS, G_DK, G_DV))],
        out_shape=[jax.ShapeDtypeStruct((n_blocks * tb, D_MODEL), F32),
                   jax.ShapeDtypeStruct((n_seq, CONV_W - 1, QK_CONV), F32),
                   jax.ShapeDtypeStruct((n_seq, M_HEADS, M_DH, M_DH), F32),
                   jax.ShapeDtypeStruct((n_seq, M_HEADS, M_DH), F32),
                   jax.ShapeDtypeStruct((n_seq, 1, M_HEADS), F32),
                   jax.ShapeDtypeStruct((n_seq, G_HEADS, G_DK, G_DV), F32),
                   jax.ShapeDtypeStruct((n_blocks, SUBLANES, _PK_OUT_W), F32),
                   jax.ShapeDtypeStruct(c0.shape, F32), jax.ShapeDtypeStruct(s0.shape, F32)],
        scratch_shapes=scratch,
        compiler_params=pltpu.CompilerParams(dimension_semantics=("arbitrary",), vmem_limit_bytes=VMEM_LIMIT),
        name="seq_fused",
    )(x2d, x2d, p2d, pk, c0, s0, wts["g_mix"], wts["w_big"], wts["w_small"], wts["b_small"], wts["w_a2p"], wts["b_a"],
      wts["conv_w"], wts["conv_b"], tril, tri, wred, wts["g_mhead"], wts["g_ghead"],
      *[wts[k] for k in _POST_WEIGHTS])


_IN_OFFS = tuple(sum(IN_SIZES[:i]) for i in range(len(IN_SIZES) + 1))
_GATES_LO, _GATES_HI, _GA_LO = _IN_OFFS[3], _IN_OFFS[5], _IN_OFFS[9]
_N_SMALL = 2 * M_HEADS + G_RANK
_W_BIG_COLS = _GATES_LO + (_GA_LO - _GATES_HI)
_CAST_STEPS = 8


def _cast_kernel(wout_ref, w1_ref, w2_ref, wple_ref, wpg_ref, o_out, o_w1, o_w2, o_ple, o_pg):
    o_out[...] = wout_ref[...].astype(BF16)
    o_w1[...] = w1_ref[...].astype(BF16)
    o_w2[...] = w2_ref[...].astype(BF16)
    o_ple[...] = wple_ref[...].astype(BF16)
    o_pg[...] = wpg_ref[...].astype(BF16)


def _cast_call(*srcs):
    rows = lambda a: pl.BlockSpec((a.shape[0] // _CAST_STEPS, a.shape[1]), lambda i: (i, 0))
    return pl.pallas_call(
        _cast_kernel,
        grid=(_CAST_STEPS,),
        in_specs=[rows(a) for a in srcs],
        out_specs=[rows(a) for a in srcs],
        out_shape=[jax.ShapeDtypeStruct(a.shape, BF16) for a in srcs],
        compiler_params=pltpu.CompilerParams(dimension_semantics=("arbitrary",), vmem_limit_bytes=VMEM_LIMIT),
        name="cast_weights",
    )(*srcs)


def _cast_in_kernel(wt_ref, big_ref, small_ref):
    piece = M_WIDTH
    for n in range(_W_BIG_COLS // piece):
        dst = n * piece
        src = dst if dst < _GATES_LO else dst + (_GATES_HI - _GATES_LO)
        big_ref[:, dst:dst + piece] = wt_ref[src:src + piece, :].T.astype(BF16)
    small = jnp.concatenate([wt_ref[_GATES_LO:_GATES_HI, :], wt_ref[_GA_LO:, :],
                             jnp.zeros((LANES - _N_SMALL, D_MODEL), F32)], axis=0)
    small_ref[...] = small.T.astype(BF16)


def _cast_in_call(w_in_t):
    whole = pl.BlockSpec(memory_space=pltpu.VMEM)
    assert _GATES_LO % M_WIDTH == 0
    return pl.pallas_call(
        _cast_in_kernel,
        in_specs=[whole], out_specs=[whole, whole],
        out_shape=[jax.ShapeDtypeStruct((D_MODEL, _W_BIG_COLS), BF16),
                   jax.ShapeDtypeStruct((D_MODEL, LANES), BF16)],
        compiler_params=pltpu.CompilerParams(vmem_limit_bytes=VMEM_LIMIT),
        name="cast_w_in",
    )(w_in_t)


def _prep_weights(w_in, conv_w, conv_b, b_gate, w_a2, b_a, g_mhead, g_ghead, w_out, g_mix, g_mlp, w1, w2,
                  g_ple, w_ple, w_pg, g_final):
    w_big, w_small = _cast_in_call(jnp.swapaxes(w_in, 0, 1))
    w_out_b, w1_b, w2_b, w_ple_b, w_pg_b = _cast_call(w_out, w1, w2, w_ple, w_pg)
    b_small = jnp.concatenate([b_gate, jnp.zeros((LANES - 2 * M_HEADS,), F32)])[None]
    w_a2p = jnp.concatenate([jnp.zeros((2 * M_HEADS, G_KW), F32), w_a2,
                             jnp.zeros((LANES - _N_SMALL, G_KW), F32)], axis=0).astype(BF16)
    return dict(
        w_big=w_big, w_small=w_small, b_small=b_small, w_a2p=w_a2p, b_a=b_a[None],
        conv_w=conv_w, conv_b=conv_b[None], g_mix=g_mix[None], g_mhead=g_mhead[None], g_ghead=g_ghead[None],
        w_out=w_out_b, g_mlp=g_mlp[None], w1=w1_b, w2=w2_b,
        g_ple=g_ple[None], w_ple=w_ple_b, w_pg=w_pg_b, g_final=g_final[None])


def _mix_constants(ts):
    t = np.arange(ts)
    tril = t[None, :] <= t[:, None]
    tri = (t[:, None] // CHUNK == t[None, :] // CHUNK) & tril
    rr = np.arange(G_KW)
    wred = ((rr[None, :, None] // G_DK == rr[None, None, :] // G_DK)
            & (rr[None, None, :] % SUB == np.arange(SUB)[:, None, None]))
    return tuple(jnp.asarray(np.asarray(a, dtype=BF16)) for a in (tril, tri, wred))


def kernel(x_prompt, x_sample, p_prompt, p_sample, state_mlstm_C, state_mlstm_n, state_mlstm_m, state_conv,
           state_gla_S, w_in, conv_w, conv_b, b_gate, w_a2, b_a, g_mhead, g_ghead, w_out, g_mix, g_mlp, w1,
           w2, g_ple, w_ple, w_pg, g_final):
    assert w_in.shape[0] == 1, "single-layer trunk"
    n_seq, seq_len, _ = x_prompt.shape
    n_tok = x_sample.shape[0]
    assert x_sample.shape[1] == 1 and seq_len % SEQ_BLOCK == 0
    wts = _prep_weights(w_in[0], conv_w[0], conv_b[0], b_gate[0], w_a2[0], b_a[0], g_mhead[0], g_ghead[0],
                        w_out[0], g_mix[0], g_mlp[0], w1[0], w2[0], g_ple[0], w_ple[0], w_pg[0], g_final)

    xs = x_sample.reshape(n_tok, D_MODEL)
    buf = state_conv[0]
    packed, mo_s, gr_s, raw_s = _proj_tok_call(xs, (buf[:, 0], buf[:, 1], buf[:, 2]),
                                               state_mlstm_n[0].reshape(n_tok, M_WIDTH), state_mlstm_m[0], wts)
    n_blocks = n_seq * (seq_len // SEQ_BLOCK)
    per_step = n_tok // n_blocks
    assert per_step * n_blocks == n_tok
    packed = jnp.pad(packed.reshape(n_blocks, per_step, _PK_IN_W), ((0, 0), (0, SUBLANES - per_step), (0, 0)))

    y_p, conv_p, c_p, n_p, m_p, s_p, po, c_s, s_s = _seq_call(
        x_prompt.reshape(n_seq * seq_len, D_MODEL), p_prompt[0].reshape(n_seq * seq_len, D_PLE),
        (packed, state_mlstm_C[0], state_gla_S[0]), wts, _mix_constants(SEQ_BLOCK), n_seq=n_seq, seq_len=seq_len)
    po = po[:, :per_step].reshape(n_tok, _PK_OUT_W)
    n_s = po[:, _PK_OUT["n"][0]:_PK_OUT["n"][0] + M_WIDTH]
    m_s = po[:, _PK_OUT["m"][0]:_PK_OUT["m"][0] + M_HEADS]

    y_s = _post_tok_call(xs, p_sample[0].reshape(n_tok, D_PLE), po, mo_s, gr_s, wts)

    return (y_p.reshape(n_seq, seq_len, D_MODEL),
            y_s.reshape(n_tok, 1, D_MODEL),
            c_p[None],
            n_p[None],
            m_p.reshape(1, n_seq, M_HEADS),
            conv_p[None],
            s_p[None],
            c_s[None],
            n_s.reshape(1, n_tok, M_HEADS, M_DH),
            m_s.reshape(1, n_tok, M_HEADS),
            jnp.stack([buf[:, 1], buf[:, 2], raw_s], axis=1)[None],
            s_s[None])
```
